```python
import jax, jax.numpy as jnp
from jax import lax
import numpy as np

D_MODEL = 1024
BATCH = 32
SEQ = 2048
DEPTH = 1

CTX_LEN = 256
GRID_W = 64
MIX_WIDTH = D_MODEL
HG_WIDTH = MIX_WIDTH // 2
HG_HEADS = 4
HG_DK = HG_WIDTH // HG_HEADS
HG_CHUNK = 32
RG_WIDTH = MIX_WIDTH - HG_WIDTH
RG_HEADS = 8
RG_HD = RG_WIDTH // RG_HEADS
RG_CONV = 4
RG_PAD_L = 2
RG_PAD_R = RG_CONV - 1 - RG_PAD_L
RG_C = 8.0
IN_WIDTH = 5 * HG_WIDTH + 2 * RG_WIDTH
N_EXPERTS = 32
TOP_K = 4
D_FF = D_MODEL
SWIGLU_LIMIT = 7.0
SWIGLU_ALPHA = 1.702
MOE_BLOCK = 128
EPS = 1e-6

kernel_name = 'hymba_hgrn2_rglru_moe_dit_layer'


def rmsnorm(x, w):
    xf = x.astype(jnp.float32)
    y = xf * lax.rsqrt(jnp.mean(xf * xf, axis=-1, keepdims=True) + EPS)
    return (y * w.astype(jnp.float32)).astype(x.dtype)


def modulate(h, shift, scale):
    return h * (1.0 + scale) + shift


def to_colmajor(a, rows):
    b, t, ch = a.shape
    return a.reshape(b, rows, GRID_W, ch).transpose(0, 2, 1, 3).reshape(b, t, ch)


def from_colmajor(a, rows):
    b, t, ch = a.shape
    return a.reshape(b, GRID_W, rows, ch).transpose(0, 2, 1, 3).reshape(b, t, ch)


def gla_chunked(q, k, v, log_f, s0):
    bn, t, h, _ = q.shape
    dv = v.shape[-1]
    n_chunks = t // HG_CHUNK

    def chunks(a):
        a = a.astype(jnp.float32).reshape(bn, n_chunks, HG_CHUNK, h, a.shape[-1])
        return jnp.moveaxis(a, 1, 0)

    mask = jnp.tril(jnp.ones((HG_CHUNK, HG_CHUNK), dtype=bool))

    def step(s, xs):
        qc, kc, vc, gc = xs
        bcum = jnp.cumsum(gc, axis=1)
        blast = bcum[:, -1]
        q_dec = qc * jnp.exp(bcum)
        k_dec = kc * jnp.exp(-bcum)
        scores = jnp.einsum('bchk,bjhk->bhcj', q_dec, k_dec)
        scores = jnp.where(mask[None, None], scores, 0.0)
        o = jnp.einsum('bhcj,bjhv->bchv', scores, vc) + jnp.einsum('bchk,bhkv->bchv', q_dec, s)
        k_end = kc * jnp.exp(blast[:, None] - bcum)
        s = jnp.exp(blast)[..., None] * s + jnp.einsum('bjhk,bjhv->bhkv', k_end, vc)
        return s, o

    s_final, o = lax.scan(step, s0, (chunks(q), chunks(k), chunks(v), chunks(log_f)))
    return jnp.moveaxis(o, 0, 1).reshape(bn, t, h, dv), s_final


def hgrn2_direction(q, v, f_logit, lb, s0, reverse):
    if reverse:
        q, v, f_logit = q[:, ::-1], v[:, ::-1], f_logit[:, ::-1]
    b, t = f_logit.shape[:2]
    f = lb + (1.0 - lb) * jax.nn.sigmoid(f_logit.astype(jnp.float32))
    k = (1.0 - f).reshape(b, t, HG_HEADS, HG_DK)
    log_f = jnp.log(f).reshape(b, t, HG_HEADS, HG_DK)
    o, s = gla_chunked(q, k, v, log_f, s0)
    if reverse:
        o = o[:, ::-1]
    return o, s


def hgrn2_group(hq, hi, hf_fwd, hf_bwd, hg, lb, norm_w, s0_fwd, s0_bwd):
    b, t, _ = hq.shape
    q = jax.nn.silu(hq).reshape(b, t, HG_HEADS, HG_DK)
    v = hi.reshape(b, t, HG_HEADS, HG_DK)
    o_f, s_f = hgrn2_direction(q, v, hf_fwd, lb[0], s0_fwd, False)
    o_b, s_b = hgrn2_direction(q, v, hf_bwd, lb[1], s0_bwd, True)
    o = o_f + o_b
    o = o * lax.rsqrt(jnp.mean(o * o, axis=-1, keepdims=True) + EPS)
    o = o * norm_w.astype(jnp.float32).reshape(HG_HEADS, HG_DK)
    o = o.reshape(b, t, HG_WIDTH).astype(hq.dtype) * jax.nn.silu(hg)
    return o, s_f, s_b


def centred_dwconv(x, w, bias):
    t = x.shape[1]
    xp = jnp.pad(x, ((0, 0), (RG_PAD_L, RG_PAD_R), (0, 0)))
    y = bias
    for j in range(RG_CONV):
        y = y + xp[:, j:j + t] * w[j]
    return y


def linear_scan(a, bx, h0):
    def combine(e1, e2):
        a1, b1 = e1
        a2, b2 = e2
        return a1 * a2, a2 * b1 + b2
    a_cum, b_cum = lax.associative_scan(combine, (a, bx), axis=1)
    return a_cum * h0[:, None] + b_cum


def rglru_direction(xc, wa, ba, wx, bx, lam, h0, reverse):
    if reverse:
        xc = xc[:, ::-1]
    b, t, ch = xc.shape
    xh = xc.reshape(b, t, RG_HEADS, RG_HD)
    r = jax.nn.sigmoid((jnp.einsum('bthi,hij->bthj', xh, wa).reshape(b, t, ch) + ba).astype(jnp.float32))
    ig = jax.nn.sigmoid((jnp.einsum('bthi,hij->bthj', xh, wx).reshape(b, t, ch) + bx).astype(jnp.float32))
    log_a = -RG_C * jax.nn.softplus(-lam.astype(jnp.float32)) * r
    a = jnp.exp(log_a)
    mult = jnp.sqrt(-jnp.expm1(2.0 * log_a))
    h = linear_scan(a, mult * ig * xc.astype(jnp.float32), h0)
    h_last = h[:, -1]
    if reverse:
        h = h[:, ::-1]
    return h, h_last


def rglru_group(rx, rgate, conv_w, conv_b, wa, ba, wx, bx, lam, h0_fwd, h0_bwd):
    xc = centred_dwconv(rx, conv_w, conv_b)
    h_f, s_f = rglru_direction(xc, wa[0], ba[0], wx[0], bx[0], lam[0], h0_fwd, False)
    h_b, s_b = rglru_direction(xc, wa[1], ba[1], wx[1], bx[1], lam[1], h0_bwd, True)
    y = jax.nn.gelu(rgate) * (h_f + h_b).astype(rx.dtype)
    return y, s_f, s_b


def hybrid_mixer(h_lat, h_ctx, rows, need_ctx_out, w_in, lb, hg_norm_w, conv_w, conv_b,
                 wa, ba, wx, bx, lam, w_out):
    cuts = [HG_WIDTH, 2 * HG_WIDTH, 3 * HG_WIDTH, 4 * HG_WIDTH, 5 * HG_WIDTH, 5 * HG_WIDTH + RG_WIDTH]
    cq, ci, cff, cfb, cg, crx, crg = jnp.split(h_ctx @ w_in, cuts, axis=-1)
    lq, li, lff, lfb, lg, lrx, lrg = jnp.split(h_lat @ w_in, cuts, axis=-1)
    b = h_ctx.shape[0]
    s_zero = jnp.zeros((b, HG_HEADS, HG_DK, HG_DK), jnp.float32)
    h_zero = jnp.zeros((b, RG_WIDTH), jnp.float32)
    hg_ctx, s_ctx_f, s_ctx_b = hgrn2_group(cq, ci, cff, cfb, cg, lb, hg_norm_w, s_zero, s_zero)
    rg_ctx, h_ctx_f, h_ctx_b = rglru_group(crx, crg, conv_w, conv_b, wa, ba, wx, bx, lam, h_zero, h_zero)
    hg_lat, _, _ = hgrn2_group(lq, li, lff, lfb, lg, lb, hg_norm_w, s_ctx_f, s_ctx_b)
    rg_lat, _, _ = rglru_group(to_colmajor(lrx, rows), to_colmajor(lrg, rows), conv_w, conv_b,
                               wa, ba, wx, bx, lam, h_ctx_f, h_ctx_b)
    rg_lat = from_colmajor(rg_lat, rows)
    o_lat = jnp.concatenate([hg_lat, rg_lat], axis=-1) @ w_out
    o_ctx = jnp.concatenate([hg_ctx, rg_ctx], axis=-1) @ w_out if need_ctx_out else None
    return o_lat, o_ctx


def expert_swiglu(xb, wgu, bgu, wd, bd):
    gu = xb @ wgu + bgu
    gate, up = gu[:, :D_FF], gu[:, D_FF:]
    gate = jnp.minimum(gate, SWIGLU_LIMIT)
    up = jnp.clip(up, -SWIGLU_LIMIT, SWIGLU_LIMIT)
    glu = gate * jax.nn.sigmoid(SWIGLU_ALPHA * gate)
    return (glu * (up + 1.0)) @ wd + bd


def moe_ffn(h, router_w, router_b, w_gate_up, b_gate_up, w_down, b_down):
    bn, t, d = h.shape
    xf = h.reshape(bn * t, d)
    n_tok = bn * t
    n_assign = n_tok * TOP_K
    logits = (xf @ router_w + router_b).astype(jnp.float32)
    top_val, top_idx = lax.top_k(logits, TOP_K)
    gates = jax.nn.softmax(top_val, axis=-1)
    e_flat = top_idx.reshape(-1)
    order = jnp.argsort(e_flat)
    e_sorted = e_flat[order]
    tok_sorted = (order // TOP_K).astype(jnp.int32)
    w_sorted = gates.reshape(-1)[order].astype(h.dtype)
    counts = jnp.bincount(e_flat, length=N_EXPERTS)
    padded = (counts + MOE_BLOCK - 1) // MOE_BLOCK * MOE_BLOCK
    pad_end = jnp.cumsum(padded)
    pad_start = pad_end - padded
    cnt_start = jnp.cumsum(counts) - counts
    dest = pad_start[e_sorted] + jnp.arange(n_assign) - cnt_start[e_sorted]
    n_blocks = -(-n_assign // MOE_BLOCK) + N_EXPERTS
    n_slots = n_blocks * MOE_BLOCK
    slot_tok = jnp.full((n_slots,), n_tok, jnp.int32).at[dest].set(tok_sorted)
    slot_w = jnp.zeros((n_slots,), h.dtype).at[dest].set(w_sorted)
    block_expert = jnp.minimum(
        jnp.searchsorted(pad_end, jnp.arange(n_blocks) * MOE_BLOCK, side='right'), N_EXPERTS - 1)
    x_src = jnp.concatenate([xf, jnp.zeros((1, d), xf.dtype)], axis=0)

    def block_step(acc, blk):
        start = blk * MOE_BLOCK
        tok = lax.dynamic_slice(slot_tok, (start,), (MOE_BLOCK,))
        wgt = lax.dynamic_slice(slot_w, (start,), (MOE_BLOCK,))
        e = block_expert[blk]
        y = expert_swiglu(x_src[tok], w_gate_up[e], b_gate_up[e], w_down[e], b_down[e])
        return acc.at[tok].add(wgt[:, None] * y.astype(acc.dtype)), None

    acc, _ = lax.scan(block_step, jnp.zeros((n_tok + 1, d), xf.dtype), jnp.arange(n_blocks))
    return acc[:n_tok].reshape(bn, t, d)


def setup_inputs(seed: int = 0) -> dict:
    key = jax.random.key(seed)
    ks = jax.random.split(key, 26)
    nrm = lambda k, shape, s: jax.random.normal(k, shape, jnp.float32) * s
    a0 = jax.random.uniform(ks[17], (DEPTH, 2, RG_WIDTH), jnp.float32, 0.9, 0.999)
    s_lam = a0 ** (1.0 / RG_C)
    return {
        'x': nrm(ks[0], (BATCH, SEQ, D_MODEL), 1.0),
        'c': nrm(ks[1], (BATCH, D_MODEL), 1.0),
        'ctx': nrm(ks[2], (BATCH, CTX_LEN, D_MODEL), 1.0),
        'c_ctx': nrm(ks[3], (D_MODEL,), 1.0),
        'norm1_w': 1.0 + nrm(ks[4], (DEPTH, D_MODEL), 0.01),
        'norm2_w': 1.0 + nrm(ks[5], (DEPTH, D_MODEL), 0.01),
        'w_ada': nrm(ks[6], (DEPTH, D_MODEL, 6 * D_MODEL), D_MODEL ** -0.5),
        'b_ada': nrm(ks[7], (DEPTH, 6 * D_MODEL), 0.01),
        'w_in': nrm(ks[8], (DEPTH, D_MODEL, IN_WIDTH), D_MODEL ** -0.5),
        'hg_lb_logits': nrm(ks[9], (DEPTH + 1, 2, HG_WIDTH), 0.1),
        'hg_norm_w': 1.0 + nrm(ks[10], (DEPTH, HG_WIDTH), 0.01),
        'rg_conv_w': nrm(ks[11], (DEPTH, RG_CONV, RG_WIDTH), RG_CONV ** -0.5),
        'rg_conv_b': nrm(ks[12], (DEPTH, RG_WIDTH), 0.01),
        'rg_wa': nrm(ks[13], (DEPTH, 2, RG_HEADS, RG_HD, RG_HD), RG_HD ** -0.5),
        'rg_ba': nrm(ks[14], (DEPTH, 2, RG_WIDTH), 0.01),
        'rg_wx': nrm(ks[15], (DEPTH, 2, RG_HEADS, RG_HD, RG_HD), RG_HD ** -0.5),
        'rg_bx': nrm(ks[16], (DEPTH, 2, RG_WIDTH), 0.01),
        'rg_lambda': jnp.log(s_lam) - jnp.log1p(-s_lam),
        'w_out': nrm(ks[18], (DEPTH, MIX_WIDTH, D_MODEL), MIX_WIDTH ** -0.5),
        'router_w': nrm(ks[19], (DEPTH, D_MODEL, N_EXPERTS), D_MODEL ** -0.5),
        'router_b': nrm(ks[20], (DEPTH, N_EXPERTS), 0.01),
        'w_gate_up': nrm(ks[21], (DEPTH, N_EXPERTS, D_MODEL, 2 * D_FF), D_MODEL ** -0.5),
        'b_gate_up': nrm(ks[22], (DEPTH, N_EXPERTS, 2 * D_FF), 0.01),
        'w_down': nrm(ks[23], (DEPTH, N_EXPERTS, D_FF, D_MODEL), D_FF ** -0.5),
        'b_down': nrm(ks[24], (DEPTH, N_EXPERTS, D_MODEL), 0.01),
        'final_norm_w': 1.0 + nrm(ks[25], (D_MODEL,), 0.01),
    }


def reference(x, c, ctx, c_ctx, norm1_w, norm2_w, w_ada, b_ada, w_in, hg_lb_logits, hg_norm_w,
              rg_conv_w, rg_conv_b, rg_wa, rg_ba, rg_wx, rg_bx, rg_lambda, w_out, router_w, router_b,
              w_gate_up, b_gate_up, w_down, b_down, final_norm_w):
    rows = x.shape[1] // GRID_W
    lb_all = jnp.cumsum(jax.nn.softmax(hg_lb_logits.astype(jnp.float32), axis=0), axis=0)
    silu_c = jax.nn.silu(c)
    silu_cc = jax.nn.silu(c_ctx)
    for l in range(DEPTH):
        mod = silu_c @ w_ada[l] + b_ada[l]
        mod_c = silu_cc @ w_ada[l] + b_ada[l]
        sh1, sc1, g1, sh2, sc2, g2 = [m[:, None] for m in jnp.split(mod, 6, axis=-1)]
        csh1, csc1, cg1, csh2, csc2, cg2 = jnp.split(mod_c, 6, axis=-1)
        last = l == DEPTH - 1
        h_lat = modulate(rmsnorm(x, norm1_w[l]), sh1, sc1)
        h_ctx = modulate(rmsnorm(ctx, norm1_w[l]), csh1, csc1)
        o_lat, o_ctx = hybrid_mixer(h_lat, h_ctx, rows, not last, w_in[l], lb_all[l], hg_norm_w[l],
                                    rg_conv_w[l], rg_conv_b[l], rg_wa[l], rg_ba[l], rg_wx[l], rg_bx[l],
                                    rg_lambda[l], w_out[l])
        x = x + g1 * o_lat
        h2 = modulate(rmsnorm(x, norm2_w[l]), sh2, sc2)
        x = x + g2 * moe_ffn(h2, router_w[l], router_b[l], w_gate_up[l], b_gate_up[l], w_down[l], b_down[l])
        if not last:
            ctx = ctx + cg1 * o_ctx
            hc2 = modulate(rmsnorm(ctx, norm2_w[l]), csh2, csc2)
            ctx = ctx + cg2 * moe_ffn(hc2, router_w[l], router_b[l], w_gate_up[l], b_gate_up[l],
                                      w_down[l], b_down[l])
    return rmsnorm(x, final_norm_w)
```

```python
import functools

import jax
import jax.numpy as jnp
from jax import lax
from jax.experimental import pallas as pl
from jax.experimental.pallas import tpu as pltpu

GRID_W = 64
HG_HEADS = 4
HG_CHUNK = 32
RG_HEADS = 8
RG_CONV = 4
RG_C = 8.0
TOP_K = 4
SWIGLU_LIMIT = 7.0
SWIGLU_ALPHA = 1.702
EPS = 1e-6

LANES = 128
VMEM_LIMIT = 56 * 1024 * 1024

F32 = jnp.float32
BF16 = jnp.bfloat16
HIGHEST = lax.Precision.HIGHEST


def _params(*sem):
    return pltpu.CompilerParams(dimension_semantics=sem, vmem_limit_bytes=VMEM_LIMIT)


def _silu(x):
    return x * jax.nn.sigmoid(x)


def _rms(x, w):
    return x * lax.rsqrt(jnp.mean(x * x, axis=-1, keepdims=True) + EPS) * w


def _dot(a, b):
    return jnp.dot(a, b, preferred_element_type=F32)


def _dot_nt(a, b):
    return lax.dot_general(a, b, (((1,), (1,)), ((), ())), preferred_element_type=F32)


def _mod_kernel(c_ref, w_ref, b_ref, o_ref):
    o_ref[...] = jnp.dot(_silu(c_ref[...]), w_ref[...], preferred_element_type=F32,
                         precision=HIGHEST) + b_ref[...]


def _mod(c_all, w_ada, b_ada):
    r, d = c_all.shape
    n = w_ada.shape[1]
    tn = n // 4
    return pl.pallas_call(
        _mod_kernel, grid=(n // tn,),
        in_specs=[pl.BlockSpec((r, d), lambda j: (0, 0)),
                  pl.BlockSpec((d, tn), lambda j: (0, j)),
                  pl.BlockSpec((1, tn), lambda j: (0, j))],
        out_specs=pl.BlockSpec((r, tn), lambda j: (0, j)),
        out_shape=jax.ShapeDtypeStruct((r, n), F32),
        compiler_params=_params("arbitrary"), name="adaln_mod",
    )(c_all, w_ada, b_ada.reshape(1, n))


def _inproj_kernel(x_ref, sh_ref, sc_ref, nw_ref, w_ref, o_ref):
    h = _rms(x_ref[0], nw_ref[...]) * (1.0 + sc_ref[0]) + sh_ref[0]
    o_ref[0] = _dot(h.astype(BF16), w_ref[...])


def _inproj(x, shift, scale, norm_w, w_bf16, tm):
    b, t, d = x.shape
    n = w_bf16.shape[1]
    per_batch = shift.shape[0] == b
    mod_map = (lambda i, j: (i, 0, 0)) if per_batch else (lambda i, j: (0, 0, 0))
    return pl.pallas_call(
        _inproj_kernel, grid=(b, t // tm),
        in_specs=[pl.BlockSpec((1, tm, d), lambda i, j: (i, j, 0)),
                  pl.BlockSpec((1, 1, d), mod_map),
                  pl.BlockSpec((1, 1, d), mod_map),
                  pl.BlockSpec((1, d), lambda i, j: (0, 0)),
                  pl.BlockSpec((d, n), lambda i, j: (0, 0))],
        out_specs=pl.BlockSpec((1, tm, n), lambda i, j: (i, j, 0)),
        out_shape=jax.ShapeDtypeStruct((b, t, n), F32),
        compiler_params=_params("arbitrary", "arbitrary"), name="norm_inproj",
    )(x, shift, scale, norm_w, w_bf16)


def _hg_kernel(q_ref, v_ref, zf_ref, zb_ref, g_ref, cv_ref, czf_ref, czb_ref, lb_ref, nw_ref,
               o_ref, of_s, ob_s, *, t_lat, t_ctx):
    c = HG_CHUNK
    dk = q_ref.shape[-1]
    ri = lax.broadcasted_iota(jnp.int32, (c, c), 0)
    ci = lax.broadcasted_iota(jnp.int32, (c, c), 1)
    lower = ci <= ri
    upper = ci >= ri
    tri_f = lower.astype(F32)
    tri_b = upper.astype(F32)
    lb = lb_ref[...]
    lbf, lbb = lb[0:1], lb[1:2]

    def step(qr, vr, zr, r0, lbv, tri, mask, last_row, st, want_o):
        z = zr[0, pl.ds(r0, c), :]
        f = lbv + (1.0 - lbv) * jax.nn.sigmoid(z)
        kk = 1.0 - f
        bc = jnp.dot(tri, jnp.log(f), preferred_element_type=F32, precision=HIGHEST)
        bl = bc[last_row:last_row + 1]
        v = vr[0, pl.ds(r0, c), :]
        k_end = kk * jnp.exp(bl - bc)
        o = None
        if want_o:
            hq = qr[0, pl.ds(r0, c), :]
            qd = (_silu(hq) * jnp.exp(bc)).astype(BF16)
            kd = (kk * jnp.exp(-bc)).astype(BF16)
            sc = jnp.where(mask, _dot_nt(qd, kd), 0.0)
            o = _dot(sc.astype(BF16), v.astype(BF16)) + _dot_nt(qd, st.astype(BF16))
        st = st * jnp.exp(bl) + _dot(v.T.astype(BF16), k_end.astype(BF16))
        return st, o

    n_ctx = t_ctx // c
    n_lat = t_lat // c

    def ctx_body(i, carry):
        sf, sb = carry
        r0 = pl.multiple_of(i * c, c)
        sf, _ = step(None, cv_ref, czf_ref, r0, lbf, tri_f, lower, c - 1, sf, False)
        j0 = pl.multiple_of((n_ctx - 1 - i) * c, c)
        sb, _ = step(None, cv_ref, czb_ref, j0, lbb, tri_b, upper, 0, sb, False)
        return sf, sb

    zero = jnp.zeros((dk, dk), F32)
    carry = lax.fori_loop(0, n_ctx, ctx_body, (zero, zero))

    def lat_body(i, carry):
        sf, sb = carry
        r0 = pl.multiple_of(i * c, c)
        sf, o_f = step(q_ref, v_ref, zf_ref, r0, lbf, tri_f, lower, c - 1, sf, True)
        of_s[pl.ds(r0, c), :] = o_f
        j0 = pl.multiple_of((n_lat - 1 - i) * c, c)
        sb, o_b = step(q_ref, v_ref, zb_ref, j0, lbb, tri_b, upper, 0, sb, True)
        ob_s[pl.ds(j0, c), :] = o_b
        return sf, sb

    lax.fori_loop(0, n_lat, lat_body, carry)

    eb = min(256, t_lat)

    def epi(i, _):
        r0 = pl.multiple_of(i * eb, eb)
        o = of_s[pl.ds(r0, eb), :] + ob_s[pl.ds(r0, eb), :]
        o_ref[0, pl.ds(r0, eb), :] = _rms(o, nw_ref[...]) * _silu(g_ref[0, pl.ds(r0, eb), :])
        return 0

    lax.fori_loop(0, t_lat // eb, epi, 0)


def _hgrn2(p_lat, p_ctx, lb, norm_w):
    b, t, _ = p_lat.shape
    tc = p_ctx.shape[1]
    hw = lb.shape[1]
    dk = hw // HG_HEADS
    nh = HG_HEADS

    def col(k, tt):
        return pl.BlockSpec((1, tt, dk), lambda i, h, k=k: (i, 0, k * nh + h))

    return pl.pallas_call(
        functools.partial(_hg_kernel, t_lat=t, t_ctx=tc), grid=(b, nh),
        in_specs=[col(0, t), col(1, t), col(2, t), col(3, t), col(4, t),
                  col(1, tc), col(2, tc), col(3, tc),
                  pl.BlockSpec((2, dk), lambda i, h: (0, h)),
                  pl.BlockSpec((1, dk), lambda i, h: (0, h))],
        out_specs=pl.BlockSpec((1, t, dk), lambda i, h: (i, 0, h)),
        out_shape=jax.ShapeDtypeStruct((b, t, hw), F32),
        scratch_shapes=[pltpu.VMEM((t, dk), F32), pltpu.VMEM((t, dk), F32)],
        compiler_params=_params("arbitrary", "arbitrary"), name="hgrn2",
    )(p_lat, p_lat, p_lat, p_lat, p_lat, p_ctx, p_ctx, p_ctx, lb, norm_w)


def _shift_rows(x, k):
    n = x.shape[0]
    y = pltpu.roll(x, k % n, 0)
    r = lax.broadcasted_iota(jnp.int32, x.shape, 0)
    return jnp.where((r >= k) & (r < n + k), y, 0.0)


def _rg_kernel(rx_ref, rgate_ref, crx_ref, cw_ref, cb_ref, wg_ref, bg_ref, lam_ref, o_ref,
               xc_s, af_s, bf_s, ab_s, bb_s, hf_s, hb_s, caf_s, cbf_s, cab_s, cbb_s, *, t_lat, t_ctx):
    w = GRID_W
    rows = t_lat // w
    ch = rx_ref.shape[-1]
    half = ch // 2
    cw = cw_ref[...]
    cb = cb_ref[...]
    bg = bg_ref[...]
    nl = -lam_ref[...]
    cdec = -RG_C * (jnp.maximum(nl, 0.0) + jnp.log1p(jnp.exp(-jnp.abs(nl))))

    def conv(xm2, xm1, x0, xp1):
        return cb + cw[0:1] * xm2 + cw[1:2] * xm1 + cw[2:3] * x0 + cw[3:4] * xp1

    def gates(xc):
        xb = xc.astype(BF16)
        g0 = _dot(xb[:, :half], wg_ref[0])
        g1 = _dot(xb[:, half:], wg_ref[1])
        outs = []
        for d in range(2):
            pre = []
            for s in (2 * d, 2 * d + 1):
                pre.append(jnp.concatenate([g0[:, s * half:(s + 1) * half], g1[:, s * half:(s + 1) * half]],
                                           axis=1) + bg[:, s * ch:(s + 1) * ch])
            log_a = cdec[d:d + 1] * jax.nn.sigmoid(pre[0])
            a = jnp.exp(log_a)
            mult = jnp.sqrt(-jnp.tanh(log_a) * (a * a + 1.0))
            outs += [a, mult * jax.nn.sigmoid(pre[1]) * xc]
        return outs

    xctx = crx_ref[0]
    xcc = conv(_shift_rows(xctx, 2), _shift_rows(xctx, 1), xctx, _shift_rows(xctx, -1))
    caf_s[...], cbf_s[...], cab_s[...], cbb_s[...] = gates(xcc)

    def cstep(i, carry):
        hf, hb = carry
        hf = caf_s[pl.ds(i, 1), :] * hf + cbf_s[pl.ds(i, 1), :]
        j = t_ctx - 1 - i
        hb = cab_s[pl.ds(j, 1), :] * hb + cbb_s[pl.ds(j, 1), :]
        return hf, hb

    zrow = jnp.zeros((1, ch), F32)
    hf0, hb0 = lax.fori_loop(0, t_ctx, cstep, (zrow, zrow))

    def slab(rr):
        if 0 <= rr < rows:
            return rx_ref[0, rr * w:(rr + 1) * w, :]
        if rr < 0:
            return _shift_rows(rx_ref[0, (rr + rows) * w:(rr + rows + 1) * w, :], 1)
        return _shift_rows(rx_ref[0, (rr - rows) * w:(rr - rows + 1) * w, :], -1)

    for r in range(rows):
        xc_s[r * w:(r + 1) * w, :] = conv(slab(r - 2), slab(r - 1), slab(r), slab(r + 1))

    mb = min(256, t_lat)

    def gbody(i, _):
        r0 = pl.multiple_of(i * mb, mb)
        a_f, b_f, a_b, b_b = gates(xc_s[pl.ds(r0, mb), :])
        af_s[pl.ds(r0, mb), :] = a_f
        bf_s[pl.ds(r0, mb), :] = b_f
        ab_s[pl.ds(r0, mb), :] = a_b
        bb_s[pl.ds(r0, mb), :] = b_b
        return 0

    lax.fori_loop(0, t_lat // mb, gbody, 0)

    def l1(i, _):
        pf = pl.multiple_of(i * w, w)
        qf = pl.multiple_of((i - 1) * w, w)
        a = af_s[pl.ds(pf, w), :]
        af_s[pl.ds(pf, w), :] = a * af_s[pl.ds(qf, w), :]
        bf_s[pl.ds(pf, w), :] = a * bf_s[pl.ds(qf, w), :] + bf_s[pl.ds(pf, w), :]
        pb = pl.multiple_of((rows - 1 - i) * w, w)
        qb = pl.multiple_of((rows - i) * w, w)
        a = ab_s[pl.ds(pb, w), :]
        ab_s[pl.ds(pb, w), :] = a * ab_s[pl.ds(qb, w), :]
        bb_s[pl.ds(pb, w), :] = a * bb_s[pl.ds(qb, w), :] + bb_s[pl.ds(pb, w), :]
        return 0

    lax.fori_loop(1, rows, l1, 0)

    last = (rows - 1) * w

    def l2(i, carry):
        hf, hb = carry
        hf_s[pl.ds(i, 1), :] = hf
        hf = af_s[pl.ds(last + i, 1), :] * hf + bf_s[pl.ds(last + i, 1), :]
        j = w - 1 - i
        hb_s[pl.ds(j, 1), :] = hb
        hb = ab_s[pl.ds(j, 1), :] * hb + bb_s[pl.ds(j, 1), :]
        return hf, hb

    lax.fori_loop(0, w, l2, (hf0, hb0))

    def l3(i, _):
        p = pl.multiple_of(i * w, w)
        h = (af_s[pl.ds(p, w), :] * hf_s[...] + bf_s[pl.ds(p, w), :]
             + ab_s[pl.ds(p, w), :] * hb_s[...] + bb_s[pl.ds(p, w), :])
        o_ref[0, pl.ds(p, w), :] = jax.nn.gelu(rgate_ref[0, pl.ds(p, w), :]) * h
        return 0

    lax.fori_loop(0, rows, l3, 0)


def _rglru(p_lat, p_ctx, conv_w, conv_b, wg, bg, lam):
    b, t, _ = p_lat.shape
    tc = p_ctx.shape[1]
    ch = conv_w.shape[1]
    rx_blk = (p_lat.shape[2] - 2 * ch) // ch
    full = lambda shape: pl.BlockSpec(shape, lambda i: (0,) * len(shape))
    big = lambda: pltpu.VMEM((t, ch), F32)
    small = lambda: pltpu.VMEM((tc, ch), F32)
    return pl.pallas_call(
        functools.partial(_rg_kernel, t_lat=t, t_ctx=tc), grid=(b,),
        in_specs=[pl.BlockSpec((1, t, ch), lambda i: (i, 0, rx_blk)),
                  pl.BlockSpec((1, t, ch), lambda i: (i, 0, rx_blk + 1)),
                  pl.BlockSpec((1, tc, ch), lambda i: (i, 0, rx_blk)),
                  full(conv_w.shape), full(conv_b.shape), full(wg.shape), full(bg.shape), full(lam.shape)],
        out_specs=pl.BlockSpec((1, t, ch), lambda i: (i, 0, 0)),
        out_shape=jax.ShapeDtypeStruct((b, t, ch), F32),
        scratch_shapes=[big(), big(), big(), big(), big(),
                        pltpu.VMEM((GRID_W, ch), F32), pltpu.VMEM((GRID_W, ch), F32),
                        small(), small(), small(), small()],
        compiler_params=_params("arbitrary"), name="rglru",
    )(p_lat, p_lat, p_ctx, conv_w, conv_b, wg, bg, lam)


def _mix_kernel(hg_ref, rg_ref, x_ref, g1_ref, sh_ref, sc_ref, nw_ref, wo_ref, rw_ref, rb_ref,
                x1_ref, h2_ref, meta_ref, cnt_ref, base_s, *, n_exp):
    tm = x_ref.shape[1]

    @pl.when((pl.program_id(0) == 0) & (pl.program_id(1) == 0))
    def _():
        base_s[...] = jnp.zeros_like(base_s)

    hcat = jnp.concatenate([hg_ref[0], rg_ref[0]], axis=1).astype(BF16)
    x1 = x_ref[0] + g1_ref[0] * _dot(hcat, wo_ref[...])
    x1_ref[0] = x1
    h2 = _rms(x1, nw_ref[...]) * (1.0 + sc_ref[0]) + sh_ref[0]
    h2_ref[0] = h2
    logits = jnp.dot(h2, rw_ref[...], preferred_element_type=F32, precision=HIGHEST) + rb_ref[...]

    lane_e = lax.broadcasted_iota(jnp.int32, (tm, n_exp), 1)
    vals, idxs = [], []
    cur = logits
    for _ in range(TOP_K):
        m = jnp.max(cur, axis=1, keepdims=True)
        ix = jnp.min(jnp.where(cur == m, lane_e, n_exp), axis=1, keepdims=True)
        vals.append(m)
        idxs.append(ix)
        cur = jnp.where(lane_e == ix, -jnp.inf, cur)
    ex = [jnp.exp(v - vals[0]) for v in vals]
    den = ex[0] + ex[1] + ex[2] + ex[3]

    lane = lax.broadcasted_iota(jnp.int32, (tm, LANES), 1)
    onehot = jnp.zeros((tm, LANES), F32)
    for k in range(TOP_K):
        onehot = jnp.where(lane == idxs[k] + k * n_exp, 1.0, onehot)
    ri = lax.broadcasted_iota(jnp.int32, (tm, tm), 0)
    ci = lax.broadcasted_iota(jnp.int32, (tm, tm), 1)
    prefix = _dot((ci < ri).astype(BF16), onehot.astype(BF16))
    tot = jnp.broadcast_to(prefix[tm - 1:tm] + onehot[tm - 1:tm], (8, LANES))
    lane8 = lax.broadcasted_iota(jnp.int32, (8, LANES), 1)
    off = base_s[...]
    tot_all = tot
    for j in range(1, TOP_K):
        rolled = pltpu.roll(tot, j * n_exp, 1)
        off = off + jnp.where(lane8 >= j * n_exp, rolled, 0.0)
        tot_all = tot_all + rolled
    pos = onehot * (prefix + off[0:1])
    meta = jnp.zeros((tm, LANES), F32)
    for k in range(TOP_K):
        in_k = (lane >= k * n_exp) & (lane < (k + 1) * n_exp)
        rank = jnp.sum(jnp.where(in_k, pos, 0.0), axis=1, keepdims=True)
        meta = jnp.where(lane == k, idxs[k].astype(F32), meta)
        meta = jnp.where(lane == TOP_K + k, ex[k] / den, meta)
        meta = jnp.where(lane == 2 * TOP_K + k, rank, meta)
    meta_ref[0] = meta
    base_s[...] = base_s[...] + tot_all
    cnt_ref[...] = base_s[...]


def _mix(hg, rg, x, g1, sh2, sc2, norm_w, wo_bf16, router_w, router_b, tm):
    b, t, d = x.shape
    hw = hg.shape[2]
    n_exp = router_w.shape[1]
    assert TOP_K * n_exp == LANES
    tok = lambda last: pl.BlockSpec((1, tm, last), lambda i, j: (i, j, 0))
    per_b = pl.BlockSpec((1, 1, d), lambda i, j: (i, 0, 0))
    full = lambda shape: pl.BlockSpec(shape, lambda i, j: (0,) * len(shape))
    return pl.pallas_call(
        functools.partial(_mix_kernel, n_exp=n_exp), grid=(b, t // tm),
        in_specs=[tok(hw), tok(hw), tok(d), per_b, per_b, per_b, full((1, d)),
                  full(wo_bf16.shape), full(router_w.shape), full((1, n_exp))],
        out_specs=[tok(d), tok(d), tok(LANES), pl.BlockSpec((8, LANES), lambda i, j: (0, 0))],
        out_shape=[jax.ShapeDtypeStruct((b, t, d), F32), jax.ShapeDtypeStruct((b, t, d), F32),
                   jax.ShapeDtypeStruct((b, t, LANES), F32), jax.ShapeDtypeStruct((8, LANES), F32)],
        scratch_shapes=[pltpu.VMEM((8, LANES), F32)],
        compiler_params=_params("arbitrary", "arbitrary"), name="outproj_router",
    )(hg, rg, x, g1, sh2, sc2, norm_w, wo_bf16, router_w, router_b)


def _row_copy(src, s, dst, d, sem):
    return pltpu.make_async_copy(src.at[pl.ds(s, 1), :], dst.at[pl.ds(d, 1), :], sem)


def _dispatch_kernel(h_ref, dest_hbm, xs_in, xs_out, idx_s, sem_i, sem_d):
    del xs_in
    td = h_ref.shape[0]
    n_idx = td * TOP_K
    i = pl.program_id(0)
    cp = pltpu.make_async_copy(dest_hbm.at[pl.ds(i * n_idx, n_idx)], idx_s, sem_i)
    cp.start()
    cp.wait()

    def issue(t, _):
        for k in range(TOP_K):
            _row_copy(h_ref, t, xs_out, idx_s[t * TOP_K + k], sem_d).start()
        return 0

    lax.fori_loop(0, td, issue, 0)

    def drain(t, _):
        for k in range(TOP_K):
            _row_copy(h_ref, 0, xs_out, 0, sem_d).wait()
        return 0

    lax.fori_loop(0, td, drain, 0)


def _dispatch(h2, dest_flat, n_slots, td):
    n, d = h2.shape
    xs0 = jnp.zeros((n_slots, d), F32)
    return pl.pallas_call(
        _dispatch_kernel, grid=(n // td,),
        in_specs=[pl.BlockSpec((td, d), lambda i: (i, 0)),
                  pl.BlockSpec(memory_space=pl.ANY),
                  pl.BlockSpec(memory_space=pl.ANY)],
        out_specs=pl.BlockSpec(memory_space=pl.ANY),
        out_shape=jax.ShapeDtypeStruct((n_slots, d), F32),
        scratch_shapes=[pltpu.SMEM((td * TOP_K,), jnp.int32), pltpu.SemaphoreType.DMA, pltpu.SemaphoreType.DMA],
        input_output_aliases={2: 0},
        compiler_params=_params("arbitrary"), name="moe_dispatch",
    )(h2, dest_flat, xs0)


def _expert_kernel(be_ref, nu_ref, x_ref, wgu_ref, bgu_ref, wd_ref, bd_ref, y_ref):
    del be_ref
    d_ff = wd_ref.shape[1]

    @pl.when(pl.program_id(0) < nu_ref[0])
    def _():
        gu = _dot(x_ref[...].astype(BF16), wgu_ref[0]) + bgu_ref[0]
        gate = jnp.minimum(gu[:, :d_ff], SWIGLU_LIMIT)
        up = jnp.clip(gu[:, d_ff:], -SWIGLU_LIMIT, SWIGLU_LIMIT)
        act = gate * jax.nn.sigmoid(SWIGLU_ALPHA * gate) * (up + 1.0)
        y_ref[...] = _dot(act.astype(BF16), wd_ref[0]) + bd_ref[0]


def _experts(xs, blk_expert, n_used, wgu, bgu, wd, bd, bm):
    n_slots, d = xs.shape
    n_exp, _, f2 = wgu.shape
    d_ff = wd.shape[1]
    n_blocks = n_slots // bm
    row = lambda i, be, nu: (jnp.minimum(i, nu[0] - 1), 0)
    grid_spec = pltpu.PrefetchScalarGridSpec(
        num_scalar_prefetch=2, grid=(n_blocks,),
        in_specs=[pl.BlockSpec((bm, d), row),
                  pl.BlockSpec((1, d, f2), lambda i, be, nu: (be[i], 0, 0)),
                  pl.BlockSpec((1, 1, f2), lambda i, be, nu: (be[i], 0, 0)),
                  pl.BlockSpec((1, d_ff, d), lambda i, be, nu: (be[i], 0, 0)),
                  pl.BlockSpec((1, 1, d), lambda i, be, nu: (be[i], 0, 0))],
        out_specs=pl.BlockSpec((bm, d), row))
    return pl.pallas_call(
        _expert_kernel, grid_spec=grid_spec,
        out_shape=jax.ShapeDtypeStruct((n_slots, d), F32),
        compiler_params=_params("arbitrary"), name="moe_experts",
    )(blk_expert, n_used, xs, wgu, bgu.reshape(n_exp, 1, f2), wd, bd.reshape(n_exp, 1, d))


def _combine_kernel(x1_ref, meta_ref, g2_ref, fw_ref, dest_hbm, y_hbm, o_ref, rows_s, idx_s, sem_i, sem_d):
    tc = x1_ref.shape[0]
    n_idx = tc * TOP_K
    i = pl.program_id(0)
    cp = pltpu.make_async_copy(dest_hbm.at[pl.ds(i * n_idx, n_idx)], idx_s, sem_i)
    cp.start()
    cp.wait()

    def issue(t, _):
        for k in range(TOP_K):
            _row_copy(y_hbm, idx_s[t * TOP_K + k], rows_s.at[k], t, sem_d).start()
        return 0

    lax.fori_loop(0, tc, issue, 0)

    def drain(t, _):
        for k in range(TOP_K):
            _row_copy(y_hbm, 0, rows_s.at[k], 0, sem_d).wait()
        return 0

    lax.fori_loop(0, tc, drain, 0)

    meta = meta_ref[...]
    moe = meta[:, TOP_K:TOP_K + 1] * rows_s[0]
    for k in range(1, TOP_K):
        moe = moe + meta[:, TOP_K + k:TOP_K + k + 1] * rows_s[k]
    o_ref[...] = _rms(x1_ref[...] + g2_ref[0] * moe, fw_ref[...])


def _combine(x1, meta, g2, final_w, dest_flat, y, t_seq, tc):
    n, d = x1.shape
    return pl.pallas_call(
        _combine_kernel, grid=(n // tc,),
        in_specs=[pl.BlockSpec((tc, d), lambda i: (i, 0)),
                  pl.BlockSpec((tc, LANES), lambda i: (i, 0)),
                  pl.BlockSpec((1, 1, d), lambda i: (i * tc // t_seq, 0, 0)),
                  pl.BlockSpec((1, d), lambda i: (0, 0)),
                  pl.BlockSpec(memory_space=pl.ANY),
                  pl.BlockSpec(memory_space=pl.ANY)],
        out_specs=pl.BlockSpec((tc, d), lambda i: (i, 0)),
        out_shape=jax.ShapeDtypeStruct((n, d), F32),
        scratch_shapes=[pltpu.VMEM((TOP_K, tc, d), F32), pltpu.SMEM((tc * TOP_K,), jnp.int32),
                        pltpu.SemaphoreType.DMA, pltpu.SemaphoreType.DMA],
        compiler_params=_params("arbitrary"), name="moe_combine",
    )(x1, meta, g2, final_w, dest_flat, y)


def _gate_weights(wa, wx):
    _, heads, hd, _ = wa.shape
    hh = heads // 2
    eye = jnp.eye(hh, dtype=wa.dtype)

    def blockdiag(wsel):
        return jnp.einsum('hij,hg->higj', wsel, eye).reshape(hh * hd, hh * hd)

    halves = []
    for s in range(2):
        sl = slice(s * hh, (s + 1) * hh)
        halves.append(jnp.concatenate([blockdiag(wa[0, sl]), blockdiag(wx[0, sl]),
                                       blockdiag(wa[1, sl]), blockdiag(wx[1, sl])], axis=1))
    return jnp.stack(halves).astype(BF16)


def kernel(x, c, ctx, c_ctx, norm1_w, norm2_w, w_ada, b_ada, w_in, hg_lb_logits, hg_norm_w, rg_conv_w, rg_conv_b,
           rg_wa, rg_ba, rg_wx, rg_bx, rg_lambda, w_out, router_w, router_b, w_gate_up, b_gate_up, w_down,
           b_down, final_norm_w):
    b, t, d = x.shape
    tcx = ctx.shape[1]
    n_exp = router_w.shape[-1]
    n_tok = b * t
    depth = w_in.shape[0]
    lb_all = jnp.cumsum(jax.nn.softmax(hg_lb_logits.astype(F32), axis=0), axis=0)

    for l in range(depth):
        assert l == depth - 1, "context stream update of non-final layers is not implemented"
        pad = (-(b + 1)) % 8
        c_all = jnp.concatenate([c, c_ctx[None], jnp.zeros((pad, d), F32)], axis=0)
        mod = _mod(c_all, w_ada[l], b_ada[l])
        sh1, sc1, g1, sh2, sc2, g2 = [m[:b, None, :] for m in jnp.split(mod, 6, axis=-1)]
        csh1, csc1 = [m[b:b + 1, None, :] for m in jnp.split(mod, 6, axis=-1)[:2]]

        w_in_b = w_in[l].astype(BF16)
        nw1 = norm1_w[l].reshape(1, d)
        tm = min(512, t)
        p_lat = _inproj(x, sh1, sc1, nw1, w_in_b, tm)
        p_ctx = _inproj(ctx, csh1, csc1, nw1, w_in_b, min(256, tcx))

        hg = _hgrn2(p_lat, p_ctx, lb_all[l], hg_norm_w[l].reshape(1, -1))
        ch = rg_conv_w.shape[-1]
        wg = _gate_weights(rg_wa[l], rg_wx[l])
        bg = jnp.concatenate([rg_ba[l, 0], rg_bx[l, 0], rg_ba[l, 1], rg_bx[l, 1]]).reshape(1, 4 * ch)
        rg = _rglru(p_lat, p_ctx, rg_conv_w[l], rg_conv_b[l].reshape(1, ch), wg, bg, rg_lambda[l])

        x1, h2, meta, cnt = _mix(hg, rg, x, g1, sh2, sc2, norm2_w[l].reshape(1, d), w_out[l].astype(BF16),
                                 router_w[l], router_b[l].reshape(1, n_exp), tm)

        bm = 512
        meta2 = meta.reshape(n_tok, LANES)
        idx = meta2[:, 0:TOP_K].astype(jnp.int32)
        rank = meta2[:, 2 * TOP_K:3 * TOP_K].astype(jnp.int32)
        counts = cnt[0, :n_exp].astype(jnp.int32)
        padded = (counts + bm - 1) // bm * bm
        pad_end = jnp.cumsum(padded)
        dest = ((pad_end - padded)[idx] + rank).reshape(-1)
        n_blocks = -(-n_tok * TOP_K // bm) + n_exp
        blk_expert = jnp.minimum(jnp.searchsorted(pad_end, jnp.arange(n_blocks) * bm, side='right'),
                                 n_exp - 1).astype(jnp.int32)
        n_used = (pad_end[-1:] // bm).astype(jnp.int32)

        tdc = min(256, t)
        xs = _dispatch(h2.reshape(n_tok, d), dest, n_blocks * bm, tdc)
        y = _experts(xs, blk_expert, n_used, w_gate_up[l].astype(BF16), b_gate_up[l], w_down[l].astype(BF16),
                     b_down[l], bm)
        out = _combine(x1.reshape(n_tok, d), meta2, g2, final_norm_w.reshape(1, d), dest, y, t, tdc)
        return out.reshape(b, t, d)
```

```python
import functools

import jax
import jax.numpy as jnp
from jax import lax
from jax.experimental import pallas as pl
from jax.experimental.pallas import tpu as pltpu

GRID_W = 64
HG_HEADS = 4
HG_CHUNK = 32
RG_HEADS = 8
RG_CONV = 4
RG_C = 8.0
TOP_K = 4
SWIGLU_LIMIT = 7.0
SWIGLU_ALPHA = 1.702
EPS = 1e-6

LANES = 128
SUBLANES = 8
VMEM_LIMIT = 56 * 1024 * 1024

F32 = jnp.float32
BF16 = jnp.bfloat16
HIGHEST = lax.Precision.HIGHEST


def _params(*sem):
    return pltpu.CompilerParams(dimension_semantics=sem, vmem_limit_bytes=VMEM_LIMIT)


def _silu(x):
    return x * jax.nn.sigmoid(x)


def _rms(x, w):
    return x * lax.rsqrt(jnp.mean(x * x, axis=-1, keepdims=True) + EPS) * w


def _dot(a, b):
    return jnp.dot(a, b, preferred_element_type=F32)


def _dot_nt(a, b):
    return lax.dot_general(a, b, (((1,), (1,)), ((), ())), preferred_element_type=F32)


def _mod_kernel(c_ref, w_ref, b_ref, o_ref):
    o_ref[...] = jnp.dot(_silu(c_ref[...]), w_ref[...], preferred_element_type=F32,
                         precision=HIGHEST) + b_ref[...]


def _mod(c_all, w_ada, b_ada):
    r, d = c_all.shape
    n = w_ada.shape[1]
    tn = n // 4
    return pl.pallas_call(
        _mod_kernel, grid=(n // tn,),
        in_specs=[pl.BlockSpec((r, d), lambda j: (0, 0)),
                  pl.BlockSpec((d, tn), lambda j: (0, j)),
                  pl.BlockSpec((1, tn), lambda j: (0, j))],
        out_specs=pl.BlockSpec((r, tn), lambda j: (0, j)),
        out_shape=jax.ShapeDtypeStruct((r, n), F32),
        compiler_params=_params("arbitrary"), name="adaln_mod",
    )(c_all, w_ada, b_ada.reshape(1, n))


def _inproj_kernel(x_ref, sh_ref, sc_ref, nw_ref, w_ref, o_ref):
    h = _rms(x_ref[0], nw_ref[...]) * (1.0 + sc_ref[0]) + sh_ref[0]
    o_ref[0] = _dot(h.astype(BF16), w_ref[...])


def _inproj(x, shift, scale, norm_w, w_bf16, tm):
    b, t, d = x.shape
    n = w_bf16.shape[1]
    per_batch = shift.shape[0] == b
    mod_map = (lambda i, j: (i, 0, 0)) if per_batch else (lambda i, j: (0, 0, 0))
    return pl.pallas_call(
        _inproj_kernel, grid=(b, t // tm),
        in_specs=[pl.BlockSpec((1, tm, d), lambda i, j: (i, j, 0)),
                  pl.BlockSpec((1, 1, d), mod_map),
                  pl.BlockSpec((1, 1, d), mod_map),
                  pl.BlockSpec((1, d), lambda i, j: (0, 0)),
                  pl.BlockSpec((d, n), lambda i, j: (0, 0))],
        out_specs=pl.BlockSpec((1, tm, n), lambda i, j: (i, j, 0)),
        out_shape=jax.ShapeDtypeStruct((b, t, n), F32),
        compiler_params=_params("arbitrary", "arbitrary"), name="norm_inproj",
    )(x, shift, scale, norm_w, w_bf16)


def _split_bf16(x):
    hi = x.astype(BF16)
    return hi, (x - hi.astype(F32)).astype(BF16)


def _hg_kernel(q_ref, v_ref, zf_ref, zb_ref, g_ref, cv_ref, czf_ref, czb_ref, lb_ref, nw_ref, o_ref,
               o_s, qd_s, ke_s, dec_s, cke_s, cdec_s, tot_s, *, t_lat, t_ctx):
    c = HG_CHUNK
    dk = q_ref.shape[-1]
    lb = lb_ref[...]
    lbf, lbb = lb[0:1], lb[1:2]

    def prep(rb, zf, zb, q, v, ke_out, dec_out, r0, blk):
        nc = rb // c
        ri = lax.broadcasted_iota(jnp.int32, (rb, rb), 0)
        ci = lax.broadcasted_iota(jnp.int32, (rb, rb), 1)
        same = (ri // c) == (ci // c)
        low = same & (ci <= ri)
        upp = same & (ci >= ri)
        ff = lbf + (1.0 - lbf) * jax.nn.sigmoid(zf)
        fb = lbb + (1.0 - lbb) * jax.nn.sigmoid(zb)
        lgf, lgb = jnp.log(ff), jnp.log(fb)
        rhs = jnp.concatenate([*_split_bf16(lgf), *_split_bf16(lgb)], axis=1)
        pre = _dot(low.astype(BF16), rhs)
        suf = _dot(upp.astype(BF16), rhs)
        bcf = pre[:, :dk] + pre[:, dk:2 * dk]
        remf = suf[:, :dk] + suf[:, dk:2 * dk] - lgf
        bcb = suf[:, 2 * dk:3 * dk] + suf[:, 3 * dk:]
        remb = pre[:, 2 * dk:3 * dk] + pre[:, 3 * dk:] - lgb
        kkf, kkb = 1.0 - ff, 1.0 - fb
        ke_out[0, pl.ds(r0, rb), :] = (kkf * jnp.exp(remf)).astype(BF16)
        ke_out[1, pl.ds(r0, rb), :] = (kkb * jnp.exp(remb)).astype(BF16)
        half = tot_s.shape[0] // 2
        tot_s[0:rb, :] = bcf + remf
        tot_s[half:half + rb, :] = bcb + remb
        c0 = pl.multiple_of(blk * nc, nc)
        dec_out[0, pl.ds(c0, nc), :] = jnp.exp(tot_s[pl.ds(0, nc, stride=c), :])
        dec_out[1, pl.ds(c0, nc), :] = jnp.exp(tot_s[pl.ds(half, nc, stride=c), :])
        if q is None:
            return
        sq = _silu(q)
        qdf = (sq * jnp.exp(bcf)).astype(BF16)
        qdb = (sq * jnp.exp(bcb)).astype(BF16)
        kdf = (kkf * jnp.exp(-bcf)).astype(BF16)
        kdb = (kkb * jnp.exp(-bcb)).astype(BF16)
        p = jnp.where(low, _dot_nt(qdf, kdf), 0.0) + jnp.where(upp, _dot_nt(qdb, kdb), 0.0)
        o_s[pl.ds(r0, rb), :] = _dot(p.astype(BF16), v.astype(BF16))
        qd_s[0, pl.ds(r0, rb), :] = qdf
        qd_s[1, pl.ds(r0, rb), :] = qdb

    def seq(n, vr, ker, decr, with_o, carry):
        def one(d, ch, st):
            r0 = pl.multiple_of(ch * c, c)
            if with_o:
                o_s[pl.ds(r0, c), :] += _dot_nt(qd_s[d, pl.ds(r0, c), :], st.astype(BF16))
            upd = lax.dot_general(vr[0, pl.ds(r0, c), :].astype(BF16), ker[d, pl.ds(r0, c), :],
                                  (((0,), (0,)), ((), ())), preferred_element_type=F32)
            return st * decr[d, pl.ds(ch, 1), :] + upd

        def body(i, carry):
            sf, sb = carry
            return one(0, i, sf), one(1, n - 1 - i, sb)

        return lax.fori_loop(0, n, body, carry, unroll=2)

    rbc = min(256, t_ctx)
    for blk in range(t_ctx // rbc):
        r0 = blk * rbc
        prep(rbc, czf_ref[0, r0:r0 + rbc, :], czb_ref[0, r0:r0 + rbc, :], None, None, cke_s, cdec_s, r0, blk)
    zero = jnp.zeros((dk, dk), F32)
    carry = seq(t_ctx // c, cv_ref, cke_s, cdec_s, False, (zero, zero))

    rbl = min(256, t_lat)

    def lat_prep(blk, _):
        r0 = pl.multiple_of(blk * rbl, rbl)
        rows = pl.ds(r0, rbl)
        prep(rbl, zf_ref[0, rows, :], zb_ref[0, rows, :], q_ref[0, rows, :], v_ref[0, rows, :], ke_s, dec_s, r0, blk)
        return 0

    lax.fori_loop(0, t_lat // rbl, lat_prep, 0)
    seq(t_lat // c, v_ref, ke_s, dec_s, True, carry)

    def epi(i, _):
        rows = pl.ds(pl.multiple_of(i * rbl, rbl), rbl)
        o_ref[0, rows, :] = _rms(o_s[rows, :], nw_ref[...]) * _silu(g_ref[0, rows, :])
        return 0

    lax.fori_loop(0, t_lat // rbl, epi, 0)


def _hgrn2(p_lat, p_ctx, lb, norm_w):
    b, t, _ = p_lat.shape
    tc = p_ctx.shape[1]
    hw = lb.shape[1]
    dk = hw // HG_HEADS
    nh = HG_HEADS
    c = HG_CHUNK

    def col(k, tt):
        return pl.BlockSpec((1, tt, dk), lambda i, h, k=k: (i, 0, k * nh + h))

    return pl.pallas_call(
        functools.partial(_hg_kernel, t_lat=t, t_ctx=tc), grid=(b, nh),
        in_specs=[col(0, t), col(1, t), col(2, t), col(3, t), col(4, t),
                  col(1, tc), col(2, tc), col(3, tc),
                  pl.BlockSpec((2, dk), lambda i, h: (0, h)),
                  pl.BlockSpec((1, dk), lambda i, h: (0, h))],
        out_specs=pl.BlockSpec((1, t, dk), lambda i, h: (i, 0, h)),
        out_shape=jax.ShapeDtypeStruct((b, t, hw), F32),
        scratch_shapes=[pltpu.VMEM((t, dk), F32),
                        pltpu.VMEM((2, t, dk), BF16),
                        pltpu.VMEM((2, t, dk), BF16),
                        pltpu.VMEM((2, t // c, dk), F32),
                        pltpu.VMEM((2, tc, dk), BF16),
                        pltpu.VMEM((2, tc // c, dk), F32),
                        pltpu.VMEM((2 * min(256, max(t, tc)), dk), F32)],
        compiler_params=_params("arbitrary", "arbitrary"), name="hgrn2",
    )(p_lat, p_lat, p_lat, p_lat, p_lat, p_ctx, p_ctx, p_ctx, lb, norm_w)


def _shift_rows(x, k):
    n = x.shape[0]
    y = pltpu.roll(x, k % n, 0)
    r = lax.broadcasted_iota(jnp.int32, x.shape, 0)
    return jnp.where((r >= k) & (r < n + k), y, 0.0)


def _rg_kernel(rx_ref, rgate_ref, crx_ref, cw_ref, cb_ref, wg_ref, bg_ref, lam_ref, o_ref,
               xc_s, af_s, bf_s, ab_s, bb_s, hf_s, hb_s, caf_s, cbf_s, cab_s, cbb_s, *, t_lat, t_ctx):
    w = GRID_W
    rows = t_lat // w
    ch = rx_ref.shape[-1]
    half = ch // 2
    cw = cw_ref[...]
    cb = cb_ref[...]
    bg = bg_ref[...]
    nl = -lam_ref[...]
    cdec = -RG_C * (jnp.maximum(nl, 0.0) + jnp.log1p(jnp.exp(-jnp.abs(nl))))

    def conv(xm2, xm1, x0, xp1):
        return cb + cw[0:1] * xm2 + cw[1:2] * xm1 + cw[2:3] * x0 + cw[3:4] * xp1

    def gates(xc):
        xb = xc.astype(BF16)
        g0 = _dot(xb[:, :half], wg_ref[0])
        g1 = _dot(xb[:, half:], wg_ref[1])
        outs = []
        for d in range(2):
            pre = []
            for s in (2 * d, 2 * d + 1):
                pre.append(jnp.concatenate([g0[:, s * half:(s + 1) * half], g1[:, s * half:(s + 1) * half]],
                                           axis=1) + bg[:, s * ch:(s + 1) * ch])
            log_a = cdec[d:d + 1] * jax.nn.sigmoid(pre[0])
            a = jnp.exp(log_a)
            mult = jnp.sqrt(-jnp.tanh(log_a) * (a * a + 1.0))
            outs += [a, mult * jax.nn.sigmoid(pre[1]) * xc]
        return outs

    xctx = crx_ref[0]
    xcc = conv(_shift_rows(xctx, 2), _shift_rows(xctx, 1), xctx, _shift_rows(xctx, -1))
    caf_s[...], cbf_s[...], cab_s[...], cbb_s[...] = gates(xcc)

    def cstep(i, carry):
        hf, hb = carry
        hf = caf_s[pl.ds(i, 1), :] * hf + cbf_s[pl.ds(i, 1), :]
        j = t_ctx - 1 - i
        hb = cab_s[pl.ds(j, 1), :] * hb + cbb_s[pl.ds(j, 1), :]
        return hf, hb

    zrow = jnp.zeros((1, ch), F32)
    hf0, hb0 = lax.fori_loop(0, t_ctx, cstep, (zrow, zrow))

    def slab(rr):
        if 0 <= rr < rows:
            return rx_ref[0, rr * w:(rr + 1) * w, :]
        if rr < 0:
            return _shift_rows(rx_ref[0, (rr + rows) * w:(rr + rows + 1) * w, :], 1)
        return _shift_rows(rx_ref[0, (rr - rows) * w:(rr - rows + 1) * w, :], -1)

    for r in range(rows):
        xc_s[r * w:(r + 1) * w, :] = conv(slab(r - 2), slab(r - 1), slab(r), slab(r + 1))

    mb = min(256, t_lat)

    def gbody(i, _):
        r0 = pl.multiple_of(i * mb, mb)
        a_f, b_f, a_b, b_b = gates(xc_s[pl.ds(r0, mb), :])
        af_s[pl.ds(r0, mb), :] = a_f
        bf_s[pl.ds(r0, mb), :] = b_f
        ab_s[pl.ds(r0, mb), :] = a_b
        bb_s[pl.ds(r0, mb), :] = b_b
        return 0

    lax.fori_loop(0, t_lat // mb, gbody, 0)

    def l1(i, _):
        pf = pl.multiple_of(i * w, w)
        qf = pl.multiple_of((i - 1) * w, w)
        a = af_s[pl.ds(pf, w), :]
        af_s[pl.ds(pf, w), :] = a * af_s[pl.ds(qf, w), :]
        bf_s[pl.ds(pf, w), :] = a * bf_s[pl.ds(qf, w), :] + bf_s[pl.ds(pf, w), :]
        pb = pl.multiple_of((rows - 1 - i) * w, w)
        qb = pl.multiple_of((rows - i) * w, w)
        a = ab_s[pl.ds(pb, w), :]
        ab_s[pl.ds(pb, w), :] = a * ab_s[pl.ds(qb, w), :]
        bb_s[pl.ds(pb, w), :] = a * bb_s[pl.ds(qb, w), :] + bb_s[pl.ds(pb, w), :]
        return 0

    lax.fori_loop(1, rows, l1, 0)

    last = (rows - 1) * w

    def l2(i, carry):
        hf, hb = carry
        hf_s[pl.ds(i, 1), :] = hf
        hf = af_s[pl.ds(last + i, 1), :] * hf + bf_s[pl.ds(last + i, 1), :]
        j = w - 1 - i
        hb_s[pl.ds(j, 1), :] = hb
        hb = ab_s[pl.ds(j, 1), :] * hb + bb_s[pl.ds(j, 1), :]
        return hf, hb

    lax.fori_loop(0, w, l2, (hf0, hb0))

    def l3(i, _):
        p = pl.multiple_of(i * w, w)
        h = (af_s[pl.ds(p, w), :] * hf_s[...] + bf_s[pl.ds(p, w), :]
             + ab_s[pl.ds(p, w), :] * hb_s[...] + bb_s[pl.ds(p, w), :])
        o_ref[0, pl.ds(p, w), :] = jax.nn.gelu(rgate_ref[0, pl.ds(p, w), :]) * h
        return 0

    lax.fori_loop(0, rows, l3, 0)


def _rglru(p_lat, p_ctx, conv_w, conv_b, wg, bg, lam):
    b, t, _ = p_lat.shape
    tc = p_ctx.shape[1]
    ch = conv_w.shape[1]
    rx_blk = (p_lat.shape[2] - 2 * ch) // ch
    full = lambda shape: pl.BlockSpec(shape, lambda i: (0,) * len(shape))
    big = lambda: pltpu.VMEM((t, ch), F32)
    small = lambda: pltpu.VMEM((tc, ch), F32)
    return pl.pallas_call(
        functools.partial(_rg_kernel, t_lat=t, t_ctx=tc), grid=(b,),
        in_specs=[pl.BlockSpec((1, t, ch), lambda i: (i, 0, rx_blk)),
                  pl.BlockSpec((1, t, ch), lambda i: (i, 0, rx_blk + 1)),
                  pl.BlockSpec((1, tc, ch), lambda i: (i, 0, rx_blk)),
                  full(conv_w.shape), full(conv_b.shape), full(wg.shape), full(bg.shape), full(lam.shape)],
        out_specs=pl.BlockSpec((1, t, ch), lambda i: (i, 0, 0)),
        out_shape=jax.ShapeDtypeStruct((b, t, ch), F32),
        scratch_shapes=[big(), big(), big(), big(), big(),
                        pltpu.VMEM((GRID_W, ch), F32), pltpu.VMEM((GRID_W, ch), F32),
                        small(), small(), small(), small()],
        compiler_params=_params("arbitrary"), name="rglru",
    )(p_lat, p_lat, p_ctx, conv_w, conv_b, wg, bg, lam)


def _mix_kernel(hg_ref, rg_ref, x_ref, g1_ref, sh_ref, sc_ref, nw_ref, wo_ref, rw_ref, rb_ref,
                x1_ref, h2_ref, meta_ref, cnt_ref, base_s, *, n_exp):
    tm = x_ref.shape[1]

    @pl.when((pl.program_id(0) == 0) & (pl.program_id(1) == 0))
    def _():
        base_s[...] = jnp.zeros_like(base_s)

    hcat = jnp.concatenate([hg_ref[0], rg_ref[0]], axis=1).astype(BF16)
    x1 = x_ref[0] + g1_ref[0] * _dot(hcat, wo_ref[...])
    x1_ref[0] = x1
    h2 = _rms(x1, nw_ref[...]) * (1.0 + sc_ref[0]) + sh_ref[0]
    h2_ref[0] = h2
    logits = jnp.dot(h2, rw_ref[...], preferred_element_type=F32, precision=HIGHEST) + rb_ref[...]

    lane_e = lax.broadcasted_iota(jnp.int32, (tm, n_exp), 1)
    vals, idxs = [], []
    cur = logits
    for _ in range(TOP_K):
        m = jnp.max(cur, axis=1, keepdims=True)
        ix = jnp.min(jnp.where(cur == m, lane_e, n_exp), axis=1, keepdims=True)
        vals.append(m)
        idxs.append(ix)
        cur = jnp.where(lane_e == ix, -jnp.inf, cur)
    ex = [jnp.exp(v - vals[0]) for v in vals]
    den = ex[0] + ex[1] + ex[2] + ex[3]

    lane = lax.broadcasted_iota(jnp.int32, (tm, LANES), 1)
    onehot = jnp.zeros((tm, LANES), F32)
    for k in range(TOP_K):
        onehot = jnp.where(lane == idxs[k] + k * n_exp, 1.0, onehot)
    ri = lax.broadcasted_iota(jnp.int32, (tm, tm), 0)
    ci = lax.broadcasted_iota(jnp.int32, (tm, tm), 1)
    prefix = _dot((ci < ri).astype(BF16), onehot.astype(BF16))
    tot = jnp.broadcast_to(prefix[tm - 1:tm] + onehot[tm - 1:tm], (8, LANES))
    lane8 = lax.broadcasted_iota(jnp.int32, (8, LANES), 1)
    off = base_s[...]
    tot_all = tot
    for j in range(1, TOP_K):
        rolled = pltpu.roll(tot, j * n_exp, 1)
        off = off + jnp.where(lane8 >= j * n_exp, rolled, 0.0)
        tot_all = tot_all + rolled
    pos = onehot * (prefix + off[0:1])
    meta = jnp.zeros((tm, LANES), F32)
    for k in range(TOP_K):
        in_k = (lane >= k * n_exp) & (lane < (k + 1) * n_exp)
        rank = jnp.sum(jnp.where(in_k, pos, 0.0), axis=1, keepdims=True)
        meta = jnp.where(lane == k, idxs[k].astype(F32), meta)
        meta = jnp.where(lane == TOP_K + k, ex[k] / den, meta)
        meta = jnp.where(lane == 2 * TOP_K + k, rank, meta)
    meta_ref[0] = meta
    base_s[...] = base_s[...] + tot_all
    cnt_ref[...] = base_s[...]


def _mix(hg, rg, x, g1, sh2, sc2, norm_w, wo_bf16, router_w, router_b, tm):
    b, t, d = x.shape
    hw = hg.shape[2]
    n_exp = router_w.shape[1]
    assert TOP_K * n_exp == LANES
    tok = lambda last: pl.BlockSpec((1, tm, last), lambda i, j: (i, j, 0))
    per_b = pl.BlockSpec((1, 1, d), lambda i, j: (i, 0, 0))
    full = lambda shape: pl.BlockSpec(shape, lambda i, j: (0,) * len(shape))
    return pl.pallas_call(
        functools.partial(_mix_kernel, n_exp=n_exp), grid=(b, t // tm),
        in_specs=[tok(hw), tok(hw), tok(d), per_b, per_b, per_b, full((1, d)),
                  full(wo_bf16.shape), full(router_w.shape), full((1, n_exp))],
        out_specs=[tok(d), tok(d), tok(LANES), pl.BlockSpec((8, LANES), lambda i, j: (0, 0))],
        out_shape=[jax.ShapeDtypeStruct((b, t, d), F32), jax.ShapeDtypeStruct((b, t, d), F32),
                   jax.ShapeDtypeStruct((b, t, LANES), F32), jax.ShapeDtypeStruct((8, LANES), F32)],
        scratch_shapes=[pltpu.VMEM((8, LANES), F32)],
        compiler_params=_params("arbitrary", "arbitrary"), name="outproj_router",
    )(hg, rg, x, g1, sh2, sc2, norm_w, wo_bf16, router_w, router_b)


def _row_copy(src, s, dst, d, sem):
    return pltpu.make_async_copy(src.at[pl.ds(s, 1), :], dst.at[pl.ds(d, 1), :], sem)


def _dispatch_kernel(fill_off, fill_n, tail, h_ref, dest_hbm, xs_out, idx_s, zero_s, sem_i, sem_d, sem_z, *, pad_bits):
    td = h_ref.shape[0]
    n_idx = td * TOP_K
    n_exp = fill_n.shape[0]
    zrows = zero_s.shape[0]
    i = pl.program_id(0)
    cp = pltpu.make_async_copy(dest_hbm.at[pl.ds(i * n_idx, n_idx)], idx_s, sem_i)
    cp.start()

    def fill(wait):
        def go(copy, cond):
            @pl.when(cond)
            def _():
                copy.wait() if wait else copy.start()

        def per_expert(e, _):
            off = fill_off[e]
            npad = fill_n[e]
            n_single = npad & (SUBLANES - 1)
            for r in range(SUBLANES - 1):
                go(pltpu.make_async_copy(zero_s.at[pl.ds(0, 1), :], xs_out.at[pl.ds(off + r, 1), :], sem_z),
                   r < n_single)
            off = pl.multiple_of(off + n_single, SUBLANES)
            for bit in reversed(range(SUBLANES.bit_length() - 1, pad_bits)):
                size = 1 << bit
                go(pltpu.make_async_copy(zero_s.at[pl.ds(0, size), :], xs_out.at[pl.ds(off, size), :], sem_z),
                   (npad & size) != 0)
                off = pl.multiple_of(off + (npad & size), SUBLANES)
            return 0

        lax.fori_loop(0, n_exp, per_expert, 0)

        def per_tail_chunk(j, _):
            off = pl.multiple_of(tail[0] + j * zrows, zrows)
            copy = pltpu.make_async_copy(zero_s, xs_out.at[pl.ds(off, zrows), :], sem_z)
            copy.wait() if wait else copy.start()
            return 0

        lax.fori_loop(0, tail[1], per_tail_chunk, 0)

    @pl.when(i == 0)
    def _():
        zero_s[...] = jnp.zeros_like(zero_s)
        fill(False)

    cp.wait()

    def issue(t, _):
        for k in range(TOP_K):
            _row_copy(h_ref, t, xs_out, idx_s[t * TOP_K + k], sem_d).start()
        return 0

    lax.fori_loop(0, td, issue, 0, unroll=8)
    for k in range(TOP_K):
        pltpu.make_async_copy(h_ref, xs_out.at[pl.ds(0, td), :], sem_d).wait()

    @pl.when(i == 0)
    def _():
        fill(True)


def _dispatch(h2, dest_flat, fill_off, fill_n, tail, n_slots, td, bm):
    n, d = h2.shape
    pad_bits = (bm - 1).bit_length()
    grid_spec = pltpu.PrefetchScalarGridSpec(
        num_scalar_prefetch=3, grid=(n // td,),
        in_specs=[pl.BlockSpec((td, d), lambda i, fo, fn, tl: (i, 0)),
                  pl.BlockSpec(memory_space=pl.ANY)],
        out_specs=pl.BlockSpec(memory_space=pl.ANY),
        scratch_shapes=[pltpu.SMEM((td * TOP_K,), jnp.int32), pltpu.VMEM((bm // 2, d), F32),
                        pltpu.SemaphoreType.DMA, pltpu.SemaphoreType.DMA, pltpu.SemaphoreType.DMA])
    return pl.pallas_call(
        functools.partial(_dispatch_kernel, pad_bits=pad_bits), grid_spec=grid_spec,
        out_shape=jax.ShapeDtypeStruct((n_slots, d), F32),
        compiler_params=_params("arbitrary"), name="moe_dispatch",
    )(fill_off, fill_n, tail, h2, dest_flat)


def _expert_kernel(be_ref, nu_ref, x_ref, wgu_ref, bgu_ref, wd_ref, bd_ref, y_ref):
    del be_ref
    d_ff = wd_ref.shape[1]

    @pl.when(pl.program_id(0) >= nu_ref[0])
    def _():
        y_ref[...] = jnp.zeros_like(y_ref)

    @pl.when(pl.program_id(0) < nu_ref[0])
    def _():
        gu = _dot(x_ref[...].astype(BF16), wgu_ref[0]) + bgu_ref[0]
        gate = jnp.minimum(gu[:, :d_ff], SWIGLU_LIMIT)
        up = jnp.clip(gu[:, d_ff:], -SWIGLU_LIMIT, SWIGLU_LIMIT)
        act = gate * jax.nn.sigmoid(SWIGLU_ALPHA * gate) * (up + 1.0)
        y_ref[...] = _dot(act.astype(BF16), wd_ref[0]) + bd_ref[0]


def _experts(xs, blk_expert, n_used, wgu, bgu, wd, bd, bm):
    n_slots, d = xs.shape
    n_exp, _, f2 = wgu.shape
    d_ff = wd.shape[1]
    n_blocks = n_slots // bm
    row = lambda i, be, nu: (jnp.minimum(i, nu[0] - 1), 0)
    grid_spec = pltpu.PrefetchScalarGridSpec(
        num_scalar_prefetch=2, grid=(n_blocks,),
        in_specs=[pl.BlockSpec((bm, d), row),
                  pl.BlockSpec((1, d, f2), lambda i, be, nu: (be[i], 0, 0)),
                  pl.BlockSpec((1, 1, f2), lambda i, be, nu: (be[i], 0, 0)),
                  pl.BlockSpec((1, d_ff, d), lambda i, be, nu: (be[i], 0, 0)),
                  pl.BlockSpec((1, 1, d), lambda i, be, nu: (be[i], 0, 0))],
        out_specs=pl.BlockSpec((bm, d), lambda i, be, nu: (i, 0)))
    return pl.pallas_call(
        _expert_kernel, grid_spec=grid_spec,
        out_shape=jax.ShapeDtypeStruct((n_slots, d), F32),
        compiler_params=_params("arbitrary"), name="moe_experts",
    )(blk_expert, n_used, xs, wgu, bgu.reshape(n_exp, 1, f2), wd, bd.reshape(n_exp, 1, d))


def _combine_kernel(x1_ref, meta_ref, g2_ref, fw_ref, dest_hbm, y_hbm, o_ref, rows_s, idx_s, sem_i, sem_d):
    tc = x1_ref.shape[0]
    n_idx = tc * TOP_K
    i = pl.program_id(0)
    n = pl.num_programs(0)
    slot = i % 2

    def gather(j, sl):
        cp = pltpu.make_async_copy(dest_hbm.at[pl.ds(j * n_idx, n_idx)], idx_s, sem_i)
        cp.start()
        cp.wait()

        def issue(t, _):
            for k in range(TOP_K):
                _row_copy(y_hbm, idx_s[t * TOP_K + k], rows_s.at[sl, k], t, sem_d.at[sl]).start()
            return 0

        lax.fori_loop(0, tc, issue, 0, unroll=8)

    @pl.when(i == 0)
    def _():
        gather(0, 0)

    @pl.when(i + 1 < n)
    def _():
        gather(i + 1, 1 - slot)

    for k in range(TOP_K):
        pltpu.make_async_copy(y_hbm.at[pl.ds(0, tc), :], rows_s.at[slot, k], sem_d.at[slot]).wait()

    meta = meta_ref[...]
    moe = meta[:, TOP_K:TOP_K + 1] * rows_s[slot, 0]
    for k in range(1, TOP_K):
        moe = moe + meta[:, TOP_K + k:TOP_K + k + 1] * rows_s[slot, k]
    o_ref[...] = _rms(x1_ref[...] + g2_ref[0] * moe, fw_ref[...])


def _combine(x1, meta, g2, final_w, dest_flat, y, t_seq, tc):
    n, d = x1.shape
    return pl.pallas_call(
        _combine_kernel, grid=(n // tc,),
        in_specs=[pl.BlockSpec((tc, d), lambda i: (i, 0)),
                  pl.BlockSpec((tc, LANES), lambda i: (i, 0)),
                  pl.BlockSpec((1, 1, d), lambda i: (i * tc // t_seq, 0, 0)),
                  pl.BlockSpec((1, d), lambda i: (0, 0)),
                  pl.BlockSpec(memory_space=pl.ANY),
                  pl.BlockSpec(memory_space=pl.ANY)],
        out_specs=pl.BlockSpec((tc, d), lambda i: (i, 0)),
        out_shape=jax.ShapeDtypeStruct((n, d), F32),
        scratch_shapes=[pltpu.VMEM((2, TOP_K, tc, d), F32), pltpu.SMEM((tc * TOP_K,), jnp.int32),
                        pltpu.SemaphoreType.DMA, pltpu.SemaphoreType.DMA((2,))],
        compiler_params=_params("arbitrary"), name="moe_combine",
    )(x1, meta, g2, final_w, dest_flat, y)


def _gate_weights(wa, wx):
    _, heads, hd, _ = wa.shape
    hh = heads // 2
    eye = jnp.eye(hh, dtype=wa.dtype)

    def blockdiag(wsel):
        return jnp.einsum('hij,hg->higj', wsel, eye).reshape(hh * hd, hh * hd)

    halves = []
    for s in range(2):
        sl = slice(s * hh, (s + 1) * hh)
        halves.append(jnp.concatenate([blockdiag(wa[0, sl]), blockdiag(wx[0, sl]),
                                       blockdiag(wa[1, sl]), blockdiag(wx[1, sl])], axis=1))
    return jnp.stack(halves).astype(BF16)


def kernel(x, c, ctx, c_ctx, norm1_w, norm2_w, w_ada, b_ada, w_in, hg_lb_logits, hg_norm_w, rg_conv_w, rg_conv_b,
           rg_wa, rg_ba, rg_wx, rg_bx, rg_lambda, w_out, router_w, router_b, w_gate_up, b_gate_up, w_down,
           b_down, final_norm_w):
    b, t, d = x.shape
    tcx = ctx.shape[1]
    n_exp = router_w.shape[-1]
    n_tok = b * t
    depth = w_in.shape[0]
    lb_all = jnp.cumsum(jax.nn.softmax(hg_lb_logits.astype(F32), axis=0), axis=0)

    for l in range(depth):
        assert l == depth - 1, "context stream update of non-final layers is not implemented"
        pad = (-(b + 1)) % 8
        c_all = jnp.concatenate([c, c_ctx[None], jnp.zeros((pad, d), F32)], axis=0)
        mod = _mod(c_all, w_ada[l], b_ada[l])
        sh1, sc1, g1, sh2, sc2, g2 = [m[:b, None, :] for m in jnp.split(mod, 6, axis=-1)]
        csh1, csc1 = [m[b:b + 1, None, :] for m in jnp.split(mod, 6, axis=-1)[:2]]

        w_in_b = w_in[l].astype(BF16)
        nw1 = norm1_w[l].reshape(1, d)
        tm = min(512, t)
        p_lat = _inproj(x, sh1, sc1, nw1, w_in_b, tm)
        p_ctx = _inproj(ctx, csh1, csc1, nw1, w_in_b, min(256, tcx))

        hg = _hgrn2(p_lat, p_ctx, lb_all[l], hg_norm_w[l].reshape(1, -1))
        ch = rg_conv_w.shape[-1]
        wg = _gate_weights(rg_wa[l], rg_wx[l])
        bg = jnp.concatenate([rg_ba[l, 0], rg_bx[l, 0], rg_ba[l, 1], rg_bx[l, 1]]).reshape(1, 4 * ch)
        rg = _rglru(p_lat, p_ctx, rg_conv_w[l], rg_conv_b[l].reshape(1, ch), wg, bg, rg_lambda[l])

        x1, h2, meta, cnt = _mix(hg, rg, x, g1, sh2, sc2, norm2_w[l].reshape(1, d), w_out[l].astype(BF16),
                                 router_w[l], router_b[l].reshape(1, n_exp), tm)

        bm = 512
        meta2 = meta.reshape(n_tok, LANES)
        idx = meta2[:, 0:TOP_K].astype(jnp.int32)
        rank = meta2[:, 2 * TOP_K:3 * TOP_K].astype(jnp.int32)
        counts = cnt[0, :n_exp].astype(jnp.int32)
        padded = (counts + bm - 1) // bm * bm
        pad_end = jnp.cumsum(padded)
        pad_start = pad_end - padded
        dest = (pad_start[idx] + rank).reshape(-1)
        n_blocks = -(-n_tok * TOP_K // bm) + n_exp
        blk_start = jnp.arange(n_blocks, dtype=jnp.int32) * bm
        blk_expert = jnp.minimum(jnp.sum(blk_start[:, None] >= pad_end[None, :], axis=1), n_exp - 1).astype(jnp.int32)
        n_used = (pad_end[-1:] // bm).astype(jnp.int32)

        tdc = min(256, t)
        n_slots = n_blocks * bm
        tail = jnp.stack([pad_end[-1], (n_slots - pad_end[-1]) // (bm // 2)]).astype(jnp.int32)
        xs = _dispatch(h2.reshape(n_tok, d), dest, pad_start + counts, padded - counts, tail, n_slots,
                       min(512, t), bm)
        y = _experts(xs, blk_expert, n_used, w_gate_up[l].astype(BF16), b_gate_up[l], w_down[l].astype(BF16),
                     b_down[l], bm)
        out = _combine(x1.reshape(n_tok, d), meta2, g2, final_norm_w.reshape(1, d), dest, y, t, tdc)
        return out.reshape(b, t, d)
```

```python
import functools

import jax
import jax.numpy as jnp
from jax import lax
from jax.experimental import pallas as pl
from jax.experimental.pallas import tpu as pltpu

GRID_W = 64
HG_HEADS = 4
HG_CHUNK = 32
RG_HEADS = 8
RG_CONV = 4
RG_C = 8.0
TOP_K = 4
SWIGLU_LIMIT = 7.0
SWIGLU_ALPHA = 1.702
EPS = 1e-6

LANES = 128
SUBLANES = 8
VMEM_LIMIT = 56 * 1024 * 1024

F32 = jnp.float32
BF16 = jnp.bfloat16
HIGHEST = lax.Precision.HIGHEST


def _params(*sem):
    return pltpu.CompilerParams(dimension_semantics=sem, vmem_limit_bytes=VMEM_LIMIT)


def _silu(x):
    return x * jax.nn.sigmoid(x)


def _rms(x, w):
    return x * lax.rsqrt(jnp.mean(x * x, axis=-1, keepdims=True) + EPS) * w


def _dot(a, b):
    return jnp.dot(a, b, preferred_element_type=F32)


def _dot_nt(a, b):
    return lax.dot_general(a, b, (((1,), (1,)), ((), ())), preferred_element_type=F32)


def _mod_kernel(c_ref, w_ref, b_ref, o_ref):
    o_ref[...] = jnp.dot(_silu(c_ref[...]), w_ref[...], preferred_element_type=F32,
                         precision=HIGHEST) + b_ref[...]


def _mod(c_all, w_ada, b_ada):
    r, d = c_all.shape
    n = w_ada.shape[1]
    tn = n // 4
    return pl.pallas_call(
        _mod_kernel, grid=(n // tn,),
        in_specs=[pl.BlockSpec((r, d), lambda j: (0, 0)),
                  pl.BlockSpec((d, tn), lambda j: (0, j)),
                  pl.BlockSpec((1, tn), lambda j: (0, j))],
        out_specs=pl.BlockSpec((r, tn), lambda j: (0, j)),
        out_shape=jax.ShapeDtypeStruct((r, n), F32),
        compiler_params=_params("arbitrary"), name="adaln_mod",
    )(c_all, w_ada, b_ada.reshape(1, n))


def _inproj_kernel(x_ref, sh_ref, sc_ref, nw_ref, w_ref, o_ref):
    h = _rms(x_ref[0], nw_ref[...]) * (1.0 + sc_ref[0]) + sh_ref[0]
    o_ref[0] = _dot(h.astype(BF16), w_ref[...])


def _inproj(x, shift, scale, norm_w, w_bf16, tm):
    b, t, d = x.shape
    n = w_bf16.shape[1]
    per_batch = shift.shape[0] == b
    mod_map = (lambda i, j: (i, 0, 0)) if per_batch else (lambda i, j: (0, 0, 0))
    return pl.pallas_call(
        _inproj_kernel, grid=(b, t // tm),
        in_specs=[pl.BlockSpec((1, tm, d), lambda i, j: (i, j, 0)),
                  pl.BlockSpec((1, 1, d), mod_map),
                  pl.BlockSpec((1, 1, d), mod_map),
                  pl.BlockSpec((1, d), lambda i, j: (0, 0)),
                  pl.BlockSpec((d, n), lambda i, j: (0, 0))],
        out_specs=pl.BlockSpec((1, tm, n), lambda i, j: (i, j, 0)),
        out_shape=jax.ShapeDtypeStruct((b, t, n), F32),
        compiler_params=_params("arbitrary", "arbitrary"), name="norm_inproj",
    )(x, shift, scale, norm_w, w_bf16)


def _split_bf16(x):
    hi = x.astype(BF16)
    return hi, (x - hi.astype(F32)).astype(BF16)


def _hg_kernel(q_ref, v_ref, zf_ref, zb_ref, g_ref, cv_ref, czf_ref, czb_ref, lb_ref, nw_ref, o_ref,
               o_s, qd_s, u_s, dec_s, st_s, cu_s, cdec_s, tot_s, *, t_lat, t_ctx):
    c = HG_CHUNK
    dk = q_ref.shape[-1]
    lb = lb_ref[...]
    lbf, lbb = lb[0:1], lb[1:2]

    def prep(rb, zf, zb, q, v, u_out, dec_out, blk, r0):
        nc = rb // c
        ri = lax.broadcasted_iota(jnp.int32, (rb, rb), 0)
        ci = lax.broadcasted_iota(jnp.int32, (rb, rb), 1)
        same = (ri // c) == (ci // c)
        low = same & (ci <= ri)
        upp = same & (ci >= ri)
        ff = lbf + (1.0 - lbf) * jax.nn.sigmoid(zf)
        fb = lbb + (1.0 - lbb) * jax.nn.sigmoid(zb)
        lgf, lgb = jnp.log(ff), jnp.log(fb)
        rhs = jnp.concatenate([*_split_bf16(lgf), *_split_bf16(lgb)], axis=1)
        pre = _dot(low.astype(BF16), rhs)
        suf = _dot(upp.astype(BF16), rhs)
        bcf = pre[:, :dk] + pre[:, dk:2 * dk]
        remf = suf[:, :dk] + suf[:, dk:2 * dk] - lgf
        bcb = suf[:, 2 * dk:3 * dk] + suf[:, 3 * dk:]
        remb = pre[:, 2 * dk:3 * dk] + pre[:, 3 * dk:] - lgb
        kkf, kkb = 1.0 - ff, 1.0 - fb
        kef = (kkf * jnp.exp(remf)).astype(BF16)
        keb = (kkb * jnp.exp(remb)).astype(BF16)
        chunk_of_row = lax.broadcasted_iota(jnp.int32, (rb, dk), 0) // c
        zero = jnp.zeros((rb, dk), BF16)
        keys = jnp.concatenate([jnp.where(chunk_of_row == j, ke, zero) for ke in (kef, keb) for j in range(nc)],
                               axis=1)
        u_all = _dot(v.T.astype(BF16), keys)
        c0 = blk * nc
        for d in range(2):
            for j in range(nc):
                u_out[d, c0 + j] = u_all[:, (d * nc + j) * dk:(d * nc + j + 1) * dk]
        half = tot_s.shape[0] // 2
        tot_s[0:rb, :] = bcf + remf
        tot_s[half:half + rb, :] = bcb + remb
        dec_out[0, pl.ds(pl.multiple_of(c0, nc), nc), :] = jnp.exp(tot_s[pl.ds(0, nc, stride=c), :])
        dec_out[1, pl.ds(pl.multiple_of(c0, nc), nc), :] = jnp.exp(tot_s[pl.ds(half, nc, stride=c), :])
        if q is None:
            return
        sq = _silu(q)
        qdf = (sq * jnp.exp(bcf)).astype(BF16)
        qdb = (sq * jnp.exp(bcb)).astype(BF16)
        kdf = (kkf * jnp.exp(-bcf)).astype(BF16)
        kdb = (kkb * jnp.exp(-bcb)).astype(BF16)
        p = jnp.where(low, _dot_nt(qdf, kdf), 0.0) + jnp.where(upp, _dot_nt(qdb, kdb), 0.0)
        o_s[pl.ds(r0, rb), :] = _dot(p.astype(BF16), v.astype(BF16))
        qd_s[pl.ds(r0, rb), 0:dk] = qdf
        qd_s[pl.ds(r0, rb), dk:2 * dk] = qdb

    def scan(n, ur, decr, keep, carry):
        def body(i, carry):
            sf, sb = carry
            j = n - 1 - i
            if keep:
                st_s[i, :, 0:dk] = sf.astype(BF16)
                st_s[j, :, dk:2 * dk] = sb.astype(BF16)
            return sf * decr[0, pl.ds(i, 1), :] + ur[0, i], sb * decr[1, pl.ds(j, 1), :] + ur[1, j]

        return lax.fori_loop(0, n, body, carry, unroll=2)

    rbc = min(256, t_ctx)
    for blk in range(t_ctx // rbc):
        r0 = blk * rbc
        prep(rbc, czf_ref[0, r0:r0 + rbc, :], czb_ref[0, r0:r0 + rbc, :], None, cv_ref[0, r0:r0 + rbc, :],
             cu_s, cdec_s, blk, r0)
    zero = jnp.zeros((dk, dk), F32)
    carry = scan(t_ctx // c, cu_s, cdec_s, False, (zero, zero))

    rbl = min(256, t_lat)

    def lat_prep(blk, _):
        r0 = pl.multiple_of(blk * rbl, rbl)
        rows = pl.ds(r0, rbl)
        prep(rbl, zf_ref[0, rows, :], zb_ref[0, rows, :], q_ref[0, rows, :], v_ref[0, rows, :], u_s, dec_s, blk, r0)
        return 0

    lax.fori_loop(0, t_lat // rbl, lat_prep, 0)
    scan(t_lat // c, u_s, dec_s, True, carry)

    def finish(blk, _):
        r0 = pl.multiple_of(blk * rbl, rbl)
        inter = [_dot_nt(qd_s[pl.ds(r0 + j * c, c), :], st_s[blk * (rbl // c) + j]) for j in range(rbl // c)]
        o = o_s[pl.ds(r0, rbl), :] + jnp.concatenate(inter, axis=0)
        o_ref[0, pl.ds(r0, rbl), :] = _rms(o, nw_ref[...]) * _silu(g_ref[0, pl.ds(r0, rbl), :])
        return 0

    lax.fori_loop(0, t_lat // rbl, finish, 0, unroll=2)


def _hgrn2(p_lat, p_ctx, lb, norm_w):
    b, t, _ = p_lat.shape
    tc = p_ctx.shape[1]
    hw = lb.shape[1]
    dk = hw // HG_HEADS
    nh = HG_HEADS
    c = HG_CHUNK

    def col(k, tt):
        return pl.BlockSpec((1, tt, dk), lambda i, h, k=k: (i, 0, k * nh + h))

    return pl.pallas_call(
        functools.partial(_hg_kernel, t_lat=t, t_ctx=tc), grid=(b, nh),
        in_specs=[col(0, t), col(1, t), col(2, t), col(3, t), col(4, t),
                  col(1, tc), col(2, tc), col(3, tc),
                  pl.BlockSpec((2, dk), lambda i, h: (0, h)),
                  pl.BlockSpec((1, dk), lambda i, h: (0, h))],
        out_specs=pl.BlockSpec((1, t, dk), lambda i, h: (i, 0, h)),
        out_shape=jax.ShapeDtypeStruct((b, t, hw), F32),
        scratch_shapes=[pltpu.VMEM((t, dk), F32),
                        pltpu.VMEM((t, 2 * dk), BF16),
                        pltpu.VMEM((2, t // c, dk, dk), F32),
                        pltpu.VMEM((2, t // c, dk), F32),
                        pltpu.VMEM((t // c, dk, 2 * dk), BF16),
                        pltpu.VMEM((2, tc // c, dk, dk), F32),
                        pltpu.VMEM((2, tc // c, dk), F32),
                        pltpu.VMEM((2 * min(256, max(t, tc)), dk), F32)],
        compiler_params=_params("arbitrary", "arbitrary"), name="hgrn2",
    )(p_lat, p_lat, p_lat, p_lat, p_lat, p_ctx, p_ctx, p_ctx, lb, norm_w)


def _shift_rows(x, k):
    n = x.shape[0]
    y = pltpu.roll(x, k % n, 0)
    r = lax.broadcasted_iota(jnp.int32, x.shape, 0)
    return jnp.where((r >= k) & (r < n + k), y, 0.0)


def _rg_kernel(rx_ref, rgate_ref, crx_ref, cw_ref, cb_ref, wg_ref, bg_ref, lam_ref, o_ref,
               xc_s, af_s, bf_s, ab_s, bb_s, hf_s, hb_s, caf_s, cbf_s, cab_s, cbb_s, *, t_lat, t_ctx):
    w = GRID_W
    rows = t_lat // w
    ch = rx_ref.shape[-1]
    half = ch // 2
    cw = cw_ref[...]
    cb = cb_ref[...]
    bg = bg_ref[...]
    nl = -lam_ref[...]
    cdec = -RG_C * (jnp.maximum(nl, 0.0) + jnp.log1p(jnp.exp(-jnp.abs(nl))))

    def conv(xm2, xm1, x0, xp1):
        return cb + cw[0:1] * xm2 + cw[1:2] * xm1 + cw[2:3] * x0 + cw[3:4] * xp1

    def gates(xc):
        xb = xc.astype(BF16)
        g0 = _dot(xb[:, :half], wg_ref[0])
        g1 = _dot(xb[:, half:], wg_ref[1])
        outs = []
        for d in range(2):
            pre = []
            for s in (2 * d, 2 * d + 1):
                pre.append(jnp.concatenate([g0[:, s * half:(s + 1) * half], g1[:, s * half:(s + 1) * half]],
                                           axis=1) + bg[:, s * ch:(s + 1) * ch])
            log_a = cdec[d:d + 1] * jax.nn.sigmoid(pre[0])
            a = jnp.exp(log_a)
            mult = jnp.sqrt(-jnp.tanh(log_a) * (a * a + 1.0))
            outs += [a, mult * jax.nn.sigmoid(pre[1]) * xc]
        return outs

    xctx = crx_ref[0]
    xcc = conv(_shift_rows(xctx, 2), _shift_rows(xctx, 1), xctx, _shift_rows(xctx, -1))
    caf_s[...], cbf_s[...], cab_s[...], cbb_s[...] = gates(xcc)

    def cstep(i, carry):
        hf, hb = carry
        hf = caf_s[pl.ds(i, 1), :] * hf + cbf_s[pl.ds(i, 1), :]
        j = t_ctx - 1 - i
        hb = cab_s[pl.ds(j, 1), :] * hb + cbb_s[pl.ds(j, 1), :]
        return hf, hb

    zrow = jnp.zeros((1, ch), F32)
    hf0, hb0 = lax.fori_loop(0, t_ctx, cstep, (zrow, zrow))

    def slab(rr):
        if 0 <= rr < rows:
            return rx_ref[0, rr * w:(rr + 1) * w, :]
        if rr < 0:
            return _shift_rows(rx_ref[0, (rr + rows) * w:(rr + rows + 1) * w, :], 1)
        return _shift_rows(rx_ref[0, (rr - rows) * w:(rr - rows + 1) * w, :], -1)

    for r in range(rows):
        xc_s[r * w:(r + 1) * w, :] = conv(slab(r - 2), slab(r - 1), slab(r), slab(r + 1))

    mb = min(256, t_lat)

    def gbody(i, _):
        r0 = pl.multiple_of(i * mb, mb)
        a_f, b_f, a_b, b_b = gates(xc_s[pl.ds(r0, mb), :])
        af_s[pl.ds(r0, mb), :] = a_f
        bf_s[pl.ds(r0, mb), :] = b_f
        ab_s[pl.ds(r0, mb), :] = a_b
        bb_s[pl.ds(r0, mb), :] = b_b
        return 0

    lax.fori_loop(0, t_lat // mb, gbody, 0)

    def l1(i, _):
        pf = pl.multiple_of(i * w, w)
        qf = pl.multiple_of((i - 1) * w, w)
        a = af_s[pl.ds(pf, w), :]
        af_s[pl.ds(pf, w), :] = a * af_s[pl.ds(qf, w), :]
        bf_s[pl.ds(pf, w), :] = a * bf_s[pl.ds(qf, w), :] + bf_s[pl.ds(pf, w), :]
        pb = pl.multiple_of((rows - 1 - i) * w, w)
        qb = pl.multiple_of((rows - i) * w, w)
        a = ab_s[pl.ds(pb, w), :]
        ab_s[pl.ds(pb, w), :] = a * ab_s[pl.ds(qb, w), :]
        bb_s[pl.ds(pb, w), :] = a * bb_s[pl.ds(qb, w), :] + bb_s[pl.ds(pb, w), :]
        return 0

    lax.fori_loop(1, rows, l1, 0)

    last = (rows - 1) * w

    def l2(i, carry):
        hf, hb = carry
        hf_s[pl.ds(i, 1), :] = hf
        hf = af_s[pl.ds(last + i, 1), :] * hf + bf_s[pl.ds(last + i, 1), :]
        j = w - 1 - i
        hb_s[pl.ds(j, 1), :] = hb
        hb = ab_s[pl.ds(j, 1), :] * hb + bb_s[pl.ds(j, 1), :]
        return hf, hb

    lax.fori_loop(0, w, l2, (hf0, hb0))

    def l3(i, _):
        p = pl.multiple_of(i * w, w)
        h = (af_s[pl.ds(p, w), :] * hf_s[...] + bf_s[pl.ds(p, w), :]
             + ab_s[pl.ds(p, w), :] * hb_s[...] + bb_s[pl.ds(p, w), :])
        o_ref[0, pl.ds(p, w), :] = jax.nn.gelu(rgate_ref[0, pl.ds(p, w), :]) * h
        return 0

    lax.fori_loop(0, rows, l3, 0)


def _rglru(p_lat, p_ctx, conv_w, conv_b, wg, bg, lam):
    b, t, _ = p_lat.shape
    tc = p_ctx.shape[1]
    ch = conv_w.shape[1]
    rx_blk = (p_lat.shape[2] - 2 * ch) // ch
    full = lambda shape: pl.BlockSpec(shape, lambda i: (0,) * len(shape))
    big = lambda: pltpu.VMEM((t, ch), F32)
    small = lambda: pltpu.VMEM((tc, ch), F32)
    return pl.pallas_call(
        functools.partial(_rg_kernel, t_lat=t, t_ctx=tc), grid=(b,),
        in_specs=[pl.BlockSpec((1, t, ch), lambda i: (i, 0, rx_blk)),
                  pl.BlockSpec((1, t, ch), lambda i: (i, 0, rx_blk + 1)),
                  pl.BlockSpec((1, tc, ch), lambda i: (i, 0, rx_blk)),
                  full(conv_w.shape), full(conv_b.shape), full(wg.shape), full(bg.shape), full(lam.shape)],
        out_specs=pl.BlockSpec((1, t, ch), lambda i: (i, 0, 0)),
        out_shape=jax.ShapeDtypeStruct((b, t, ch), F32),
        scratch_shapes=[big(), big(), big(), big(), big(),
                        pltpu.VMEM((GRID_W, ch), F32), pltpu.VMEM((GRID_W, ch), F32),
                        small(), small(), small(), small()],
        compiler_params=_params("arbitrary"), name="rglru",
    )(p_lat, p_lat, p_ctx, conv_w, conv_b, wg, bg, lam)


def _mix_kernel(hg_ref, rg_ref, x_ref, g1_ref, sh_ref, sc_ref, nw_ref, wo_ref, rw_ref, rb_ref,
                x1_ref, h2_ref, meta_ref, cnt_ref, base_s, *, n_exp):
    tm = x_ref.shape[1]

    @pl.when((pl.program_id(0) == 0) & (pl.program_id(1) == 0))
    def _():
        base_s[...] = jnp.zeros_like(base_s)

    hcat = jnp.concatenate([hg_ref[0], rg_ref[0]], axis=1).astype(BF16)
    x1 = x_ref[0] + g1_ref[0] * _dot(hcat, wo_ref[...])
    x1_ref[0] = x1
    h2 = _rms(x1, nw_ref[...]) * (1.0 + sc_ref[0]) + sh_ref[0]
    h2_ref[0] = h2
    logits = _dot(h2.astype(BF16), rw_ref[...]) + rb_ref[...]

    lane_e = lax.broadcasted_iota(jnp.int32, (tm, n_exp), 1)
    vals, idxs = [], []
    cur = logits
    for _ in range(TOP_K):
        m = jnp.max(cur, axis=1, keepdims=True)
        ix = jnp.min(jnp.where(cur == m, lane_e, n_exp), axis=1, keepdims=True)
        vals.append(m)
        idxs.append(ix)
        cur = jnp.where(lane_e == ix, -jnp.inf, cur)
    ex = [jnp.exp(v - vals[0]) for v in vals]
    den = ex[0] + ex[1] + ex[2] + ex[3]

    lane = lax.broadcasted_iota(jnp.int32, (tm, LANES), 1)
    onehot = jnp.zeros((tm, LANES), F32)
    for k in range(TOP_K):
        onehot = jnp.where(lane == idxs[k] + k * n_exp, 1.0, onehot)
    ri = lax.broadcasted_iota(jnp.int32, (tm, tm), 0)
    ci = lax.broadcasted_iota(jnp.int32, (tm, tm), 1)
    prefix = _dot((ci < ri).astype(BF16), onehot.astype(BF16))
    tot = jnp.broadcast_to(prefix[tm - 1:tm] + onehot[tm - 1:tm], (8, LANES))
    lane8 = lax.broadcasted_iota(jnp.int32, (8, LANES), 1)
    off = base_s[...]
    tot_all = tot
    for j in range(1, TOP_K):
        rolled = pltpu.roll(tot, j * n_exp, 1)
        off = off + jnp.where(lane8 >= j * n_exp, rolled, 0.0)
        tot_all = tot_all + rolled
    pos = onehot * (prefix + off[0:1])
    meta = jnp.zeros((tm, LANES), F32)
    for k in range(TOP_K):
        in_k = (lane >= k * n_exp) & (lane < (k + 1) * n_exp)
        rank = jnp.sum(jnp.where(in_k, pos, 0.0), axis=1, keepdims=True)
        meta = jnp.where(lane == k, idxs[k].astype(F32), meta)
        meta = jnp.where(lane == TOP_K + k, ex[k] / den, meta)
        meta = jnp.where(lane == 2 * TOP_K + k, rank, meta)
    meta_ref[0] = meta
    base_s[...] = base_s[...] + tot_all
    cnt_ref[...] = base_s[...]


def _mix(hg, rg, x, g1, sh2, sc2, norm_w, wo_bf16, router_w, router_b, tm):
    b, t, d = x.shape
    hw = hg.shape[2]
    n_exp = router_w.shape[1]
    assert TOP_K * n_exp == LANES
    tok = lambda last: pl.BlockSpec((1, tm, last), lambda i, j: (i, j, 0))
    per_b = pl.BlockSpec((1, 1, d), lambda i, j: (i, 0, 0))
    full = lambda shape: pl.BlockSpec(shape, lambda i, j: (0,) * len(shape))
    return pl.pallas_call(
        functools.partial(_mix_kernel, n_exp=n_exp), grid=(b, t // tm),
        in_specs=[tok(hw), tok(hw), tok(d), per_b, per_b, per_b, full((1, d)),
                  full(wo_bf16.shape), full(router_w.shape), full((1, n_exp))],
        out_specs=[tok(d), tok(d), tok(LANES), pl.BlockSpec((8, LANES), lambda i, j: (0, 0))],
        out_shape=[jax.ShapeDtypeStruct((b, t, d), F32), jax.ShapeDtypeStruct((b, t, d), F32),
                   jax.ShapeDtypeStruct((b, t, LANES), F32), jax.ShapeDtypeStruct((8, LANES), F32)],
        scratch_shapes=[pltpu.VMEM((8, LANES), F32)],
        compiler_params=_params("arbitrary", "arbitrary"), name="outproj_router",
    )(hg, rg, x, g1, sh2, sc2, norm_w, wo_bf16, router_w, router_b)


def _row_copy(src, s, dst, d, sem):
    return pltpu.make_async_copy(src.at[pl.ds(s, 1), :], dst.at[pl.ds(d, 1), :], sem)


def _dispatch_kernel(fill_off, fill_n, tail, h_ref, dest_hbm, xs_out, idx_s, zero_s, sem_i, sem_d, sem_z, *, pad_bits):
    td = h_ref.shape[0]
    n_idx = td * TOP_K
    n_exp = fill_n.shape[0]
    zrows = zero_s.shape[0]
    i = pl.program_id(0)
    cp = pltpu.make_async_copy(dest_hbm.at[pl.ds(i * n_idx, n_idx)], idx_s, sem_i)
    cp.start()

    def fill(wait):
        def go(copy, cond):
            @pl.when(cond)
            def _():
                copy.wait() if wait else copy.start()

        def per_expert(e, _):
            off = fill_off[e]
            npad = fill_n[e]
            n_single = npad & (SUBLANES - 1)
            for r in range(SUBLANES - 1):
                go(pltpu.make_async_copy(zero_s.at[pl.ds(0, 1), :], xs_out.at[pl.ds(off + r, 1), :], sem_z),
                   r < n_single)
            off = pl.multiple_of(off + n_single, SUBLANES)
            for bit in reversed(range(SUBLANES.bit_length() - 1, pad_bits)):
                size = 1 << bit
                go(pltpu.make_async_copy(zero_s.at[pl.ds(0, size), :], xs_out.at[pl.ds(off, size), :], sem_z),
                   (npad & size) != 0)
                off = pl.multiple_of(off + (npad & size), SUBLANES)
            return 0

        lax.fori_loop(0, n_exp, per_expert, 0)

        def per_tail_chunk(j, _):
            off = pl.multiple_of(tail[0] + j * zrows, zrows)
            copy = pltpu.make_async_copy(zero_s, xs_out.at[pl.ds(off, zrows), :], sem_z)
            copy.wait() if wait else copy.start()
            return 0

        lax.fori_loop(0, tail[1], per_tail_chunk, 0)

    @pl.when(i == 0)
    def _():
        zero_s[...] = jnp.zeros_like(zero_s)
        fill(False)

    cp.wait()

    def issue(t, _):
        for k in range(TOP_K):
            _row_copy(h_ref, t, xs_out, idx_s[t * TOP_K + k], sem_d).start()
        return 0

    lax.fori_loop(0, td, issue, 0, unroll=8)
    for k in range(TOP_K):
        pltpu.make_async_copy(h_ref, xs_out.at[pl.ds(0, td), :], sem_d).wait()

    @pl.when(i == 0)
    def _():
        fill(True)


def _dispatch(h2, dest_flat, fill_off, fill_n, tail, n_slots, td, bm):
    n, d = h2.shape
    pad_bits = (bm - 1).bit_length()
    grid_spec = pltpu.PrefetchScalarGridSpec(
        num_scalar_prefetch=3, grid=(n // td,),
        in_specs=[pl.BlockSpec((td, d), lambda i, fo, fn, tl: (i, 0)),
                  pl.BlockSpec(memory_space=pl.ANY)],
        out_specs=pl.BlockSpec(memory_space=pl.ANY),
        scratch_shapes=[pltpu.SMEM((td * TOP_K,), jnp.int32), pltpu.VMEM((bm // 2, d), F32),
                        pltpu.SemaphoreType.DMA, pltpu.SemaphoreType.DMA, pltpu.SemaphoreType.DMA])
    return pl.pallas_call(
        functools.partial(_dispatch_kernel, pad_bits=pad_bits), grid_spec=grid_spec,
        out_shape=jax.ShapeDtypeStruct((n_slots, d), F32),
        compiler_params=_params("arbitrary"), name="moe_dispatch",
    )(fill_off, fill_n, tail, h2, dest_flat)


def _expert_kernel(be_ref, nu_ref, x_ref, wgu_ref, bgu_ref, wd_ref, bd_ref, y_ref):
    del be_ref
    d_ff = wd_ref.shape[1]

    @pl.when(pl.program_id(0) >= nu_ref[0])
    def _():
        y_ref[...] = jnp.zeros_like(y_ref)

    @pl.when(pl.program_id(0) < nu_ref[0])
    def _():
        gu = _dot(x_ref[...].astype(BF16), wgu_ref[0]) + bgu_ref[0]
        gate = jnp.minimum(gu[:, :d_ff], SWIGLU_LIMIT)
        up = jnp.clip(gu[:, d_ff:], -SWIGLU_LIMIT, SWIGLU_LIMIT)
        act = gate * jax.nn.sigmoid(SWIGLU_ALPHA * gate) * (up + 1.0)
        y_ref[...] = _dot(act.astype(BF16), wd_ref[0]) + bd_ref[0]


def _experts(xs, blk_expert, n_used, wgu, bgu, wd, bd, bm):
    n_slots, d = xs.shape
    n_exp, _, f2 = wgu.shape
    d_ff = wd.shape[1]
    n_blocks = n_slots // bm
    row = lambda i, be, nu: (jnp.minimum(i, nu[0] - 1), 0)
    grid_spec = pltpu.PrefetchScalarGridSpec(
        num_scalar_prefetch=2, grid=(n_blocks,),
        in_specs=[pl.BlockSpec((bm, d), row),
                  pl.BlockSpec((1, d, f2), lambda i, be, nu: (be[i], 0, 0)),
                  pl.BlockSpec((1, 1, f2), lambda i, be, nu: (be[i], 0, 0)),
                  pl.BlockSpec((1, d_ff, d), lambda i, be, nu: (be[i], 0, 0)),
                  pl.BlockSpec((1, 1, d), lambda i, be, nu: (be[i], 0, 0))],
        out_specs=pl.BlockSpec((bm, d), lambda i, be, nu: (i, 0)))
    return pl.pallas_call(
        _expert_kernel, grid_spec=grid_spec,
        out_shape=jax.ShapeDtypeStruct((n_slots, d), F32),
        compiler_params=_params("arbitrary"), name="moe_experts",
    )(blk_expert, n_used, xs, wgu, bgu.reshape(n_exp, 1, f2), wd, bd.reshape(n_exp, 1, d))


def _combine_kernel(x1_ref, meta_ref, g2_ref, fw_ref, dest_hbm, y_hbm, o_ref, rows_s, idx_s, sem_i, sem_d):
    tc = x1_ref.shape[0]
    n_idx = tc * TOP_K
    i = pl.program_id(0)
    n = pl.num_programs(0)
    slot = i % 2

    def gather(j, sl):
        cp = pltpu.make_async_copy(dest_hbm.at[pl.ds(j * n_idx, n_idx)], idx_s, sem_i)
        cp.start()
        cp.wait()

        def issue(t, _):
            for k in range(TOP_K):
                _row_copy(y_hbm, idx_s[t * TOP_K + k], rows_s.at[sl, k], t, sem_d.at[sl]).start()
            return 0

        lax.fori_loop(0, tc, issue, 0, unroll=8)

    @pl.when(i == 0)
    def _():
        gather(0, 0)

    @pl.when(i + 1 < n)
    def _():
        gather(i + 1, 1 - slot)

    for k in range(TOP_K):
        pltpu.make_async_copy(y_hbm.at[pl.ds(0, tc), :], rows_s.at[slot, k], sem_d.at[slot]).wait()

    meta = meta_ref[...]
    moe = meta[:, TOP_K:TOP_K + 1] * rows_s[slot, 0]
    for k in range(1, TOP_K):
        moe = moe + meta[:, TOP_K + k:TOP_K + k + 1] * rows_s[slot, k]
    o_ref[...] = _rms(x1_ref[...] + g2_ref[0] * moe, fw_ref[...])


def _combine(x1, meta, g2, final_w, dest_flat, y, t_seq, tc):
    n, d = x1.shape
    return pl.pallas_call(
        _combine_kernel, grid=(n // tc,),
        in_specs=[pl.BlockSpec((tc, d), lambda i: (i, 0)),
                  pl.BlockSpec((tc, LANES), lambda i: (i, 0)),
                  pl.BlockSpec((1, 1, d), lambda i: (i * tc // t_seq, 0, 0)),
                  pl.BlockSpec((1, d), lambda i: (0, 0)),
                  pl.BlockSpec(memory_space=pl.ANY),
                  pl.BlockSpec(memory_space=pl.ANY)],
        out_specs=pl.BlockSpec((tc, d), lambda i: (i, 0)),
        out_shape=jax.ShapeDtypeStruct((n, d), F32),
        scratch_shapes=[pltpu.VMEM((2, TOP_K, tc, d), F32), pltpu.SMEM((tc * TOP_K,), jnp.int32),
                        pltpu.SemaphoreType.DMA, pltpu.SemaphoreType.DMA((2,))],
        compiler_params=_params("arbitrary"), name="moe_combine",
    )(x1, meta, g2, final_w, dest_flat, y)


def _gate_weights(wa, wx):
    _, heads, hd, _ = wa.shape
    hh = heads // 2
    eye = jnp.eye(hh, dtype=wa.dtype)

    def blockdiag(wsel):
        return jnp.einsum('hij,hg->higj', wsel, eye).reshape(hh * hd, hh * hd)

    halves = []
    for s in range(2):
        sl = slice(s * hh, (s + 1) * hh)
        halves.append(jnp.concatenate([blockdiag(wa[0, sl]), blockdiag(wx[0, sl]),
                                       blockdiag(wa[1, sl]), blockdiag(wx[1, sl])], axis=1))
    return jnp.stack(halves).astype(BF16)


def kernel(x, c, ctx, c_ctx, norm1_w, norm2_w, w_ada, b_ada, w_in, hg_lb_logits, hg_norm_w, rg_conv_w, rg_conv_b,
           rg_wa, rg_ba, rg_wx, rg_bx, rg_lambda, w_out, router_w, router_b, w_gate_up, b_gate_up, w_down,
           b_down, final_norm_w):
    b, t, d = x.shape
    tcx = ctx.shape[1]
    n_exp = router_w.shape[-1]
    n_tok = b * t
    depth = w_in.shape[0]
    lb_all = jnp.cumsum(jax.nn.softmax(hg_lb_logits.astype(F32), axis=0), axis=0)

    for l in range(depth):
        assert l == depth - 1, "context stream update of non-final layers is not implemented"
        pad = (-(b + 1)) % 8
        c_all = jnp.concatenate([c, c_ctx[None], jnp.zeros((pad, d), F32)], axis=0)
        mod = _mod(c_all, w_ada[l], b_ada[l])
        sh1, sc1, g1, sh2, sc2, g2 = [m[:b, None, :] for m in jnp.split(mod, 6, axis=-1)]
        csh1, csc1 = [m[b:b + 1, None, :] for m in jnp.split(mod, 6, axis=-1)[:2]]

        w_in_b = w_in[l].astype(BF16)
        nw1 = norm1_w[l].reshape(1, d)
        tm = min(512, t)
        p_lat = _inproj(x, sh1, sc1, nw1, w_in_b, tm)
        p_ctx = _inproj(ctx, csh1, csc1, nw1, w_in_b, min(256, tcx))

        hg = _hgrn2(p_lat, p_ctx, lb_all[l], hg_norm_w[l].reshape(1, -1))
        ch = rg_conv_w.shape[-1]
        wg = _gate_weights(rg_wa[l], rg_wx[l])
        bg = jnp.concatenate([rg_ba[l, 0], rg_bx[l, 0], rg_ba[l, 1], rg_bx[l, 1]]).reshape(1, 4 * ch)
        rg = _rglru(p_lat, p_ctx, rg_conv_w[l], rg_conv_b[l].reshape(1, ch), wg, bg, rg_lambda[l])

        x1, h2, meta, cnt = _mix(hg, rg, x, g1, sh2, sc2, norm2_w[l].reshape(1, d), w_out[l].astype(BF16),
                                 router_w[l].astype(BF16), router_b[l].reshape(1, n_exp), tm)

        bm = 512
        meta2 = meta.reshape(n_tok, LANES)
        idx = meta2[:, 0:TOP_K].astype(jnp.int32)
        rank = meta2[:, 2 * TOP_K:3 * TOP_K].astype(jnp.int32)
        counts = cnt[0, :n_exp].astype(jnp.int32)
        padded = (counts + bm - 1) // bm * bm
        pad_end = jnp.cumsum(padded)
        pad_start = pad_end - padded
        dest = (pad_start[idx] + rank).reshape(-1)
        n_blocks = -(-n_tok * TOP_K // bm) + n_exp
        blk_start = jnp.arange(n_blocks, dtype=jnp.int32) * bm
        blk_expert = jnp.minimum(jnp.sum(blk_start[:, None] >= pad_end[None, :], axis=1), n_exp - 1).astype(jnp.int32)
        n_used = (pad_end[-1:] // bm).astype(jnp.int32)

        tdc = min(256, t)
        n_slots = n_blocks * bm
        tail = jnp.stack([pad_end[-1], (n_slots - pad_end[-1]) // (bm // 2)]).astype(jnp.int32)
        xs = _dispatch(h2.reshape(n_tok, d), dest, pad_start + counts, padded - counts, tail, n_slots,
                       min(512, t), bm)
        y = _experts(xs, blk_expert, n_used, w_gate_up[l].astype(BF16), b_gate_up[l], w_down[l].astype(BF16),
                     b_down[l], bm)
        out = _combine(x1.reshape(n_tok, d), meta2, g2, final_norm_w.reshape(1, d), dest, y, t, tdc)
        return out.reshape(b, t, d)
```

```python
import functools

import jax
import jax.numpy as jnp
from jax import lax
from jax.experimental import pallas as pl
from jax.experimental.pallas import tpu as pltpu

GRID_W = 64
HG_HEADS = 4
HG_CHUNK = 32
RG_HEADS = 8
RG_CONV = 4
RG_C = 8.0
TOP_K = 4
SWIGLU_LIMIT = 7.0
SWIGLU_ALPHA = 1.702
EPS = 1e-6

LANES = 128
SUBLANES = 8
VMEM_LIMIT = 56 * 1024 * 1024

F32 = jnp.float32
BF16 = jnp.bfloat16
HIGHEST = lax.Precision.HIGHEST


def _params(*sem):
    return pltpu.CompilerParams(dimension_semantics=sem, vmem_limit_bytes=VMEM_LIMIT)


def _silu(x):
    return x * jax.nn.sigmoid(x)


def _rms(x, w):
    return x * lax.rsqrt(jnp.mean(x * x, axis=-1, keepdims=True) + EPS) * w


def _dot(a, b):
    return jnp.dot(a, b, preferred_element_type=F32)


def _dot_nt(a, b):
    return lax.dot_general(a, b, (((1,), (1,)), ((), ())), preferred_element_type=F32)


def _mod_kernel(c_ref, w_ref, b_ref, o_ref):
    o_ref[...] = jnp.dot(_silu(c_ref[...]), w_ref[...], preferred_element_type=F32,
                         precision=HIGHEST) + b_ref[...]


def _mod(c_all, w_ada, b_ada):
    r, d = c_all.shape
    n = w_ada.shape[1]
    tn = n // 4
    return pl.pallas_call(
        _mod_kernel, grid=(n // tn,),
        in_specs=[pl.BlockSpec((r, d), lambda j: (0, 0)),
                  pl.BlockSpec((d, tn), lambda j: (0, j)),
                  pl.BlockSpec((1, tn), lambda j: (0, j))],
        out_specs=pl.BlockSpec((r, tn), lambda j: (0, j)),
        out_shape=jax.ShapeDtypeStruct((r, n), F32),
        compiler_params=_params("arbitrary"), name="adaln_mod",
    )(c_all, w_ada, b_ada.reshape(1, n))


def _inproj_kernel(x_ref, sh_ref, sc_ref, nw_ref, w_ref, o_ref):
    h = _rms(x_ref[0], nw_ref[...]) * (1.0 + sc_ref[0]) + sh_ref[0]
    o_ref[0] = _dot(h.astype(BF16), w_ref[...])


def _inproj(x, shift, scale, norm_w, w_bf16, tm):
    b, t, d = x.shape
    n = w_bf16.shape[1]
    per_batch = shift.shape[0] == b
    mod_map = (lambda i, j: (i, 0, 0)) if per_batch else (lambda i, j: (0, 0, 0))
    return pl.pallas_call(
        _inproj_kernel, grid=(b, t // tm),
        in_specs=[pl.BlockSpec((1, tm, d), lambda i, j: (i, j, 0)),
                  pl.BlockSpec((1, 1, d), mod_map),
                  pl.BlockSpec((1, 1, d), mod_map),
                  pl.BlockSpec((1, d), lambda i, j: (0, 0)),
                  pl.BlockSpec((d, n), lambda i, j: (0, 0))],
        out_specs=pl.BlockSpec((1, tm, n), lambda i, j: (i, j, 0)),
        out_shape=jax.ShapeDtypeStruct((b, t, n), F32),
        compiler_params=_params("arbitrary", "arbitrary"), name="norm_inproj",
    )(x, shift, scale, norm_w, w_bf16)


def _split_bf16(x):
    hi = x.astype(BF16)
    return hi, (x - hi.astype(F32)).astype(BF16)


def _hg_kernel(q_ref, v_ref, zf_ref, zb_ref, g_ref, cv_ref, czf_ref, czb_ref, lb_ref, nw_ref, o_ref,
               o_s, qd_s, u_s, dec_s, st_s, cu_s, cdec_s, tot_s, *, t_lat, t_ctx):
    c = HG_CHUNK
    dk = q_ref.shape[-1]
    lb = lb_ref[...]
    lbf, lbb = lb[0:1], lb[1:2]

    def prep(rb, zf, zb, q, v, u_out, dec_out, blk, r0):
        nc = rb // c
        ri = lax.broadcasted_iota(jnp.int32, (rb, rb), 0)
        ci = lax.broadcasted_iota(jnp.int32, (rb, rb), 1)
        same = (ri // c) == (ci // c)
        low = same & (ci <= ri)
        upp = same & (ci >= ri)
        ff = lbf + (1.0 - lbf) * jax.nn.sigmoid(zf)
        fb = lbb + (1.0 - lbb) * jax.nn.sigmoid(zb)
        lgf, lgb = jnp.log(ff), jnp.log(fb)
        rhs = jnp.concatenate([*_split_bf16(lgf), *_split_bf16(lgb)], axis=1)
        pre = _dot(low.astype(BF16), rhs)
        suf = _dot(upp.astype(BF16), rhs)
        bcf = pre[:, :dk] + pre[:, dk:2 * dk]
        remf = suf[:, :dk] + suf[:, dk:2 * dk] - lgf
        bcb = suf[:, 2 * dk:3 * dk] + suf[:, 3 * dk:]
        remb = pre[:, 2 * dk:3 * dk] + pre[:, 3 * dk:] - lgb
        kkf, kkb = 1.0 - ff, 1.0 - fb
        kef = (kkf * jnp.exp(remf)).astype(BF16)
        keb = (kkb * jnp.exp(remb)).astype(BF16)
        chunk_of_row = lax.broadcasted_iota(jnp.int32, (rb, dk), 0) // c
        zero = jnp.zeros((rb, dk), BF16)
        keys = jnp.concatenate([jnp.where(chunk_of_row == j, ke, zero) for ke in (kef, keb) for j in range(nc)],
                               axis=1)
        u_all = _dot(v.T.astype(BF16), keys)
        c0 = blk * nc
        for d in range(2):
            for j in range(nc):
                u_out[d, c0 + j] = u_all[:, (d * nc + j) * dk:(d * nc + j + 1) * dk]
        half = tot_s.shape[0] // 2
        tot_s[0:rb, :] = bcf + remf
        tot_s[half:half + rb, :] = bcb + remb
        dec_out[0, pl.ds(pl.multiple_of(c0, nc), nc), :] = jnp.exp(tot_s[pl.ds(0, nc, stride=c), :])
        dec_out[1, pl.ds(pl.multiple_of(c0, nc), nc), :] = jnp.exp(tot_s[pl.ds(half, nc, stride=c), :])
        if q is None:
            return
        sq = _silu(q)
        qdf = (sq * jnp.exp(bcf)).astype(BF16)
        qdb = (sq * jnp.exp(bcb)).astype(BF16)
        kdf = (kkf * jnp.exp(-bcf)).astype(BF16)
        kdb = (kkb * jnp.exp(-bcb)).astype(BF16)
        p = jnp.where(low, _dot_nt(qdf, kdf), 0.0) + jnp.where(upp, _dot_nt(qdb, kdb), 0.0)
        o_s[pl.ds(r0, rb), :] = _dot(p.astype(BF16), v.astype(BF16))
        qd_s[pl.ds(r0, rb), 0:dk] = qdf
        qd_s[pl.ds(r0, rb), dk:2 * dk] = qdb

    def scan(n, ur, decr, keep, carry):
        def body(i, carry):
            sf, sb = carry
            j = n - 1 - i
            if keep:
                st_s[i, :, 0:dk] = sf.astype(BF16)
                st_s[j, :, dk:2 * dk] = sb.astype(BF16)
            return sf * decr[0, pl.ds(i, 1), :] + ur[0, i], sb * decr[1, pl.ds(j, 1), :] + ur[1, j]

        return lax.fori_loop(0, n, body, carry, unroll=2)

    rbc = min(256, t_ctx)
    for blk in range(t_ctx // rbc):
        r0 = blk * rbc
        prep(rbc, czf_ref[0, r0:r0 + rbc, :], czb_ref[0, r0:r0 + rbc, :], None, cv_ref[0, r0:r0 + rbc, :],
             cu_s, cdec_s, blk, r0)
    zero = jnp.zeros((dk, dk), F32)
    carry = scan(t_ctx // c, cu_s, cdec_s, False, (zero, zero))

    rbl = min(256, t_lat)

    def lat_prep(blk, _):
        r0 = pl.multiple_of(blk * rbl, rbl)
        rows = pl.ds(r0, rbl)
        prep(rbl, zf_ref[0, rows, :], zb_ref[0, rows, :], q_ref[0, rows, :], v_ref[0, rows, :], u_s, dec_s, blk, r0)
        return 0

    lax.fori_loop(0, t_lat // rbl, lat_prep, 0)
    scan(t_lat // c, u_s, dec_s, True, carry)

    def finish(blk, _):
        r0 = pl.multiple_of(blk * rbl, rbl)
        inter = [_dot_nt(qd_s[pl.ds(r0 + j * c, c), :], st_s[blk * (rbl // c) + j]) for j in range(rbl // c)]
        o = o_s[pl.ds(r0, rbl), :] + jnp.concatenate(inter, axis=0)
        o_ref[0, pl.ds(r0, rbl), :] = _rms(o, nw_ref[...]) * _silu(g_ref[0, pl.ds(r0, rbl), :])
        return 0

    lax.fori_loop(0, t_lat // rbl, finish, 0, unroll=2)


def _hgrn2(p_lat, p_ctx, lb, norm_w):
    b, t, _ = p_lat.shape
    tc = p_ctx.shape[1]
    hw = lb.shape[1]
    dk = hw // HG_HEADS
    nh = HG_HEADS
    c = HG_CHUNK

    def col(k, tt):
        return pl.BlockSpec((1, tt, dk), lambda i, h, k=k: (i, 0, k * nh + h))

    return pl.pallas_call(
        functools.partial(_hg_kernel, t_lat=t, t_ctx=tc), grid=(b, nh),
        in_specs=[col(0, t), col(1, t), col(2, t), col(3, t), col(4, t),
                  col(1, tc), col(2, tc), col(3, tc),
                  pl.BlockSpec((2, dk), lambda i, h: (0, h)),
                  pl.BlockSpec((1, dk), lambda i, h: (0, h))],
        out_specs=pl.BlockSpec((1, t, dk), lambda i, h: (i, 0, h)),
        out_shape=jax.ShapeDtypeStruct((b, t, hw), F32),
        scratch_shapes=[pltpu.VMEM((t, dk), F32),
                        pltpu.VMEM((t, 2 * dk), BF16),
                        pltpu.VMEM((2, t // c, dk, dk), F32),
                        pltpu.VMEM((2, t // c, dk), F32),
                        pltpu.VMEM((t // c, dk, 2 * dk), BF16),
                        pltpu.VMEM((2, tc // c, dk, dk), F32),
                        pltpu.VMEM((2, tc // c, dk), F32),
                        pltpu.VMEM((2 * min(256, max(t, tc)), dk), F32)],
        compiler_params=_params("arbitrary", "arbitrary"), name="hgrn2",
    )(p_lat, p_lat, p_lat, p_lat, p_lat, p_ctx, p_ctx, p_ctx, lb, norm_w)


def _shift_rows(x, k):
    n = x.shape[0]
    y = pltpu.roll(x, k % n, 0)
    r = lax.broadcasted_iota(jnp.int32, x.shape, 0)
    return jnp.where((r >= k) & (r < n + k), y, 0.0)


def _rg_kernel(rx_ref, rgate_ref, crx_ref, cw_ref, cb_ref, wg_ref, bg_ref, lam_ref, o_ref,
               xc_s, af_s, bf_s, ab_s, bb_s, hf_s, hb_s, caf_s, cbf_s, cab_s, cbb_s, *, t_lat, t_ctx):
    w = GRID_W
    rows = t_lat // w
    ch = rx_ref.shape[-1]
    half = ch // 2
    cw = cw_ref[...]
    cb = cb_ref[...]
    bg = bg_ref[...]
    nl = -lam_ref[...]
    cdec = -RG_C * (jnp.maximum(nl, 0.0) + jnp.log1p(jnp.exp(-jnp.abs(nl))))

    def conv(xm2, xm1, x0, xp1):
        return cb + cw[0:1] * xm2 + cw[1:2] * xm1 + cw[2:3] * x0 + cw[3:4] * xp1

    def gates(xc):
        xb = xc.astype(BF16)
        g0 = _dot(xb[:, :half], wg_ref[0])
        g1 = _dot(xb[:, half:], wg_ref[1])
        outs = []
        for d in range(2):
            pre = []
            for s in (2 * d, 2 * d + 1):
                pre.append(jnp.concatenate([g0[:, s * half:(s + 1) * half], g1[:, s * half:(s + 1) * half]],
                                           axis=1) + bg[:, s * ch:(s + 1) * ch])
            log_a = cdec[d:d + 1] * jax.nn.sigmoid(pre[0])
            a = jnp.exp(log_a)
            mult = jnp.sqrt(-jnp.tanh(log_a) * (a * a + 1.0))
            outs += [a, mult * jax.nn.sigmoid(pre[1]) * xc]
        return outs

    xctx = crx_ref[0]
    xcc = conv(_shift_rows(xctx, 2), _shift_rows(xctx, 1), xctx, _shift_rows(xctx, -1))
    caf_s[...], cbf_s[...], cab_s[...], cbb_s[...] = gates(xcc)

    def cstep(i, carry):
        hf, hb = carry
        hf = caf_s[pl.ds(i, 1), :] * hf + cbf_s[pl.ds(i, 1), :]
        j = t_ctx - 1 - i
        hb = cab_s[pl.ds(j, 1), :] * hb + cbb_s[pl.ds(j, 1), :]
        return hf, hb

    zrow = jnp.zeros((1, ch), F32)
    hf0, hb0 = lax.fori_loop(0, t_ctx, cstep, (zrow, zrow))

    def slab(rr):
        if 0 <= rr < rows:
            return rx_ref[0, rr * w:(rr + 1) * w, :]
        if rr < 0:
            return _shift_rows(rx_ref[0, (rr + rows) * w:(rr + rows + 1) * w, :], 1)
        return _shift_rows(rx_ref[0, (rr - rows) * w:(rr - rows + 1) * w, :], -1)

    for r in range(rows):
        xc_s[r * w:(r + 1) * w, :] = conv(slab(r - 2), slab(r - 1), slab(r), slab(r + 1))

    mb = min(256, t_lat)

    def gbody(i, _):
        r0 = pl.multiple_of(i * mb, mb)
        a_f, b_f, a_b, b_b = gates(xc_s[pl.ds(r0, mb), :])
        af_s[pl.ds(r0, mb), :] = a_f
        bf_s[pl.ds(r0, mb), :] = b_f
        ab_s[pl.ds(r0, mb), :] = a_b
        bb_s[pl.ds(r0, mb), :] = b_b
        return 0

    lax.fori_loop(0, t_lat // mb, gbody, 0)

    def l1(i, _):
        pf = pl.multiple_of(i * w, w)
        qf = pl.multiple_of((i - 1) * w, w)
        a = af_s[pl.ds(pf, w), :]
        af_s[pl.ds(pf, w), :] = a * af_s[pl.ds(qf, w), :]
        bf_s[pl.ds(pf, w), :] = a * bf_s[pl.ds(qf, w), :] + bf_s[pl.ds(pf, w), :]
        pb = pl.multiple_of((rows - 1 - i) * w, w)
        qb = pl.multiple_of((rows - i) * w, w)
        a = ab_s[pl.ds(pb, w), :]
        ab_s[pl.ds(pb, w), :] = a * ab_s[pl.ds(qb, w), :]
        bb_s[pl.ds(pb, w), :] = a * bb_s[pl.ds(qb, w), :] + bb_s[pl.ds(pb, w), :]
        return 0

    lax.fori_loop(1, rows, l1, 0)

    last = (rows - 1) * w

    def l2(i, carry):
        hf, hb = carry
        hf_s[pl.ds(i, 1), :] = hf
        hf = af_s[pl.ds(last + i, 1), :] * hf + bf_s[pl.ds(last + i, 1), :]
        j = w - 1 - i
        hb_s[pl.ds(j, 1), :] = hb
        hb = ab_s[pl.ds(j, 1), :] * hb + bb_s[pl.ds(j, 1), :]
        return hf, hb

    lax.fori_loop(0, w, l2, (hf0, hb0))

    def l3(i, _):
        p = pl.multiple_of(i * w, w)
        h = (af_s[pl.ds(p, w), :] * hf_s[...] + bf_s[pl.ds(p, w), :]
             + ab_s[pl.ds(p, w), :] * hb_s[...] + bb_s[pl.ds(p, w), :])
        o_ref[0, pl.ds(p, w), :] = jax.nn.gelu(rgate_ref[0, pl.ds(p, w), :]) * h
        return 0

    lax.fori_loop(0, rows, l3, 0)


def _rglru(p_lat, p_ctx, conv_w, conv_b, wg, bg, lam):
    b, t, _ = p_lat.shape
    tc = p_ctx.shape[1]
    ch = conv_w.shape[1]
    rx_blk = (p_lat.shape[2] - 2 * ch) // ch
    full = lambda shape: pl.BlockSpec(shape, lambda i: (0,) * len(shape))
    big = lambda: pltpu.VMEM((t, ch), F32)
    small = lambda: pltpu.VMEM((tc, ch), F32)
    return pl.pallas_call(
        functools.partial(_rg_kernel, t_lat=t, t_ctx=tc), grid=(b,),
        in_specs=[pl.BlockSpec((1, t, ch), lambda i: (i, 0, rx_blk)),
                  pl.BlockSpec((1, t, ch), lambda i: (i, 0, rx_blk + 1)),
                  pl.BlockSpec((1, tc, ch), lambda i: (i, 0, rx_blk)),
                  full(conv_w.shape), full(conv_b.shape), full(wg.shape), full(bg.shape), full(lam.shape)],
        out_specs=pl.BlockSpec((1, t, ch), lambda i: (i, 0, 0)),
        out_shape=jax.ShapeDtypeStruct((b, t, ch), F32),
        scratch_shapes=[big(), big(), big(), big(), big(),
                        pltpu.VMEM((GRID_W, ch), F32), pltpu.VMEM((GRID_W, ch), F32),
                        small(), small(), small(), small()],
        compiler_params=_params("arbitrary"), name="rglru",
    )(p_lat, p_lat, p_ctx, conv_w, conv_b, wg, bg, lam)


def _mix_kernel(hg_ref, rg_ref, x_ref, g1_ref, sh_ref, sc_ref, nw_ref, wo_ref, rw_ref, rb_ref,
                x1_ref, h2_ref, meta_ref, cnt_ref, base_s, *, n_exp):
    tm = x_ref.shape[1]

    @pl.when((pl.program_id(0) == 0) & (pl.program_id(1) == 0))
    def _():
        base_s[...] = jnp.zeros_like(base_s)

    hcat = jnp.concatenate([hg_ref[0], rg_ref[0]], axis=1).astype(BF16)
    x1 = x_ref[0] + g1_ref[0] * _dot(hcat, wo_ref[...])
    x1_ref[0] = x1
    h2 = _rms(x1, nw_ref[...]) * (1.0 + sc_ref[0]) + sh_ref[0]
    h2_ref[0] = h2
    logits = _dot(h2.astype(BF16), rw_ref[...]) + rb_ref[...]

    lane_e = lax.broadcasted_iota(jnp.int32, (tm, n_exp), 1)
    vals, idxs = [], []
    cur = logits
    for _ in range(TOP_K):
        m = jnp.max(cur, axis=1, keepdims=True)
        ix = jnp.min(jnp.where(cur == m, lane_e, n_exp), axis=1, keepdims=True)
        vals.append(m)
        idxs.append(ix)
        cur = jnp.where(lane_e == ix, -jnp.inf, cur)
    ex = [jnp.exp(v - vals[0]) for v in vals]
    den = ex[0] + ex[1] + ex[2] + ex[3]

    lane = lax.broadcasted_iota(jnp.int32, (tm, LANES), 1)
    onehot = jnp.zeros((tm, LANES), F32)
    for k in range(TOP_K):
        onehot = jnp.where(lane == idxs[k] + k * n_exp, 1.0, onehot)
    ri = lax.broadcasted_iota(jnp.int32, (tm, tm), 0)
    ci = lax.broadcasted_iota(jnp.int32, (tm, tm), 1)
    prefix = _dot((ci < ri).astype(BF16), onehot.astype(BF16))
    tot = jnp.broadcast_to(prefix[tm - 1:tm] + onehot[tm - 1:tm], (8, LANES))
    lane8 = lax.broadcasted_iota(jnp.int32, (8, LANES), 1)
    off = base_s[...]
    tot_all = tot
    for j in range(1, TOP_K):
        rolled = pltpu.roll(tot, j * n_exp, 1)
        off = off + jnp.where(lane8 >= j * n_exp, rolled, 0.0)
        tot_all = tot_all + rolled
    pos = onehot * (prefix + off[0:1])
    meta = jnp.zeros((tm, LANES), F32)
    for k in range(TOP_K):
        in_k = (lane >= k * n_exp) & (lane < (k + 1) * n_exp)
        rank = jnp.sum(jnp.where(in_k, pos, 0.0), axis=1, keepdims=True)
        meta = jnp.where(lane == k, idxs[k].astype(F32), meta)
        meta = jnp.where(lane == TOP_K + k, ex[k] / den, meta)
        meta = jnp.where(lane == 2 * TOP_K + k, rank, meta)
    meta_ref[0] = meta
    base_s[...] = base_s[...] + tot_all
    cnt_ref[...] = base_s[...]


def _mix(hg, rg, x, g1, sh2, sc2, norm_w, wo_bf16, router_w, router_b, tm):
    b, t, d = x.shape
    hw = hg.shape[2]
    n_exp = router_w.shape[1]
    assert TOP_K * n_exp == LANES
    tok = lambda last: pl.BlockSpec((1, tm, last), lambda i, j: (i, j, 0))
    per_b = pl.BlockSpec((1, 1, d), lambda i, j: (i, 0, 0))
    full = lambda shape: pl.BlockSpec(shape, lambda i, j: (0,) * len(shape))
    return pl.pallas_call(
        functools.partial(_mix_kernel, n_exp=n_exp), grid=(b, t // tm),
        in_specs=[tok(hw), tok(hw), tok(d), per_b, per_b, per_b, full((1, d)),
                  full(wo_bf16.shape), full(router_w.shape), full((1, n_exp))],
        out_specs=[tok(d), tok(d), tok(LANES), pl.BlockSpec((8, LANES), lambda i, j: (0, 0))],
        out_shape=[jax.ShapeDtypeStruct((b, t, d), F32), jax.ShapeDtypeStruct((b, t, d), F32),
                   jax.ShapeDtypeStruct((b, t, LANES), F32), jax.ShapeDtypeStruct((8, LANES), F32)],
        scratch_shapes=[pltpu.VMEM((8, LANES), F32)],
        compiler_params=_params("arbitrary", "arbitrary"), name="outproj_router",
    )(hg, rg, x, g1, sh2, sc2, norm_w, wo_bf16, router_w, router_b)


def _row_copy(src, s, dst, d, sem):
    return pltpu.make_async_copy(src.at[pl.ds(s, 1), :], dst.at[pl.ds(d, 1), :], sem)


def _dispatch_kernel(fill_off, fill_n, tail, h_ref, dest_hbm, xs_out, idx_s, zero_s, sem_i, sem_d, sem_z, *, pad_bits):
    td = h_ref.shape[0]
    n_idx = td * TOP_K
    n_exp = fill_n.shape[0]
    zrows = zero_s.shape[0]
    i = pl.program_id(0)
    cp = pltpu.make_async_copy(dest_hbm.at[pl.ds(i * n_idx, n_idx)], idx_s, sem_i)
    cp.start()

    def fill(wait):
        def go(copy, cond):
            @pl.when(cond)
            def _():
                copy.wait() if wait else copy.start()

        def per_expert(e, _):
            off = fill_off[e]
            npad = fill_n[e]
            n_single = npad & (SUBLANES - 1)
            for r in range(SUBLANES - 1):
                go(pltpu.make_async_copy(zero_s.at[pl.ds(0, 1), :], xs_out.at[pl.ds(off + r, 1), :], sem_z),
                   r < n_single)
            off = pl.multiple_of(off + n_single, SUBLANES)
            for bit in reversed(range(SUBLANES.bit_length() - 1, pad_bits)):
                size = 1 << bit
                go(pltpu.make_async_copy(zero_s.at[pl.ds(0, size), :], xs_out.at[pl.ds(off, size), :], sem_z),
                   (npad & size) != 0)
                off = pl.multiple_of(off + (npad & size), SUBLANES)
            return 0

        lax.fori_loop(0, n_exp, per_expert, 0)

        def per_tail_chunk(j, _):
            off = pl.multiple_of(tail[0] + j * zrows, zrows)
            copy = pltpu.make_async_copy(zero_s, xs_out.at[pl.ds(off, zrows), :], sem_z)
            copy.wait() if wait else copy.start()
            return 0

        lax.fori_loop(0, tail[1], per_tail_chunk, 0)

    @pl.when(i == 0)
    def _():
        zero_s[...] = jnp.zeros_like(zero_s)
        fill(False)

    cp.wait()

    def issue(t, _):
        for k in range(TOP_K):
            _row_copy(h_ref, t, xs_out, idx_s[t * TOP_K + k], sem_d).start(priority=k % 2)
        return 0

    lax.fori_loop(0, td, issue, 0, unroll=8)
    for k in range(TOP_K):
        pltpu.make_async_copy(h_ref, xs_out.at[pl.ds(0, td), :], sem_d).wait()

    @pl.when(i == 0)
    def _():
        fill(True)


def _dispatch(h2, dest_flat, fill_off, fill_n, tail, n_slots, td, bm):
    n, d = h2.shape
    pad_bits = (bm - 1).bit_length()
    grid_spec = pltpu.PrefetchScalarGridSpec(
        num_scalar_prefetch=3, grid=(n // td,),
        in_specs=[pl.BlockSpec((td, d), lambda i, fo, fn, tl: (i, 0)),
                  pl.BlockSpec(memory_space=pl.ANY)],
        out_specs=pl.BlockSpec(memory_space=pl.ANY),
        scratch_shapes=[pltpu.SMEM((td * TOP_K,), jnp.int32), pltpu.VMEM((bm // 2, d), F32),
                        pltpu.SemaphoreType.DMA, pltpu.SemaphoreType.DMA, pltpu.SemaphoreType.DMA])
    return pl.pallas_call(
        functools.partial(_dispatch_kernel, pad_bits=pad_bits), grid_spec=grid_spec,
        out_shape=jax.ShapeDtypeStruct((n_slots, d), F32),
        compiler_params=_params("arbitrary"), name="moe_dispatch",
    )(fill_off, fill_n, tail, h2, dest_flat)


def _expert_kernel(be_ref, nu_ref, x_ref, wgu_ref, bgu_ref, wd_ref, bd_ref, y_ref):
    del be_ref
    d_ff = wd_ref.shape[1]

    @pl.when(pl.program_id(0) >= nu_ref[0])
    def _():
        y_ref[...] = jnp.zeros_like(y_ref)

    @pl.when(pl.program_id(0) < nu_ref[0])
    def _():
        gu = _dot(x_ref[...].astype(BF16), wgu_ref[0]) + bgu_ref[0]
        gate = jnp.minimum(gu[:, :d_ff], SWIGLU_LIMIT)
        up = jnp.clip(gu[:, d_ff:], -SWIGLU_LIMIT, SWIGLU_LIMIT)
        act = gate * jax.nn.sigmoid(SWIGLU_ALPHA * gate) * (up + 1.0)
        y_ref[...] = _dot(act.astype(BF16), wd_ref[0]) + bd_ref[0]


def _experts(xs, blk_expert, n_used, wgu, bgu, wd, bd, bm):
    n_slots, d = xs.shape
    n_exp, _, f2 = wgu.shape
    d_ff = wd.shape[1]
    n_blocks = n_slots // bm
    row = lambda i, be, nu: (jnp.minimum(i, nu[0] - 1), 0)
    grid_spec = pltpu.PrefetchScalarGridSpec(
        num_scalar_prefetch=2, grid=(n_blocks,),
        in_specs=[pl.BlockSpec((bm, d), row),
                  pl.BlockSpec((1, d, f2), lambda i, be, nu: (be[i], 0, 0)),
                  pl.BlockSpec((1, 1, f2), lambda i, be, nu: (be[i], 0, 0)),
                  pl.BlockSpec((1, d_ff, d), lambda i, be, nu: (be[i], 0, 0)),
                  pl.BlockSpec((1, 1, d), lambda i, be, nu: (be[i], 0, 0))],
        out_specs=pl.BlockSpec((bm, d), lambda i, be, nu: (i, 0)))
    return pl.pallas_call(
        _expert_kernel, grid_spec=grid_spec,
        out_shape=jax.ShapeDtypeStruct((n_slots, d), F32),
        compiler_params=_params("arbitrary"), name="moe_experts",
    )(blk_expert, n_used, xs, wgu, bgu.reshape(n_exp, 1, f2), wd, bd.reshape(n_exp, 1, d))


def _combine_kernel(x1_ref, meta_ref, g2_ref, fw_ref, dest_hbm, y_hbm, o_ref, rows_s, idx_s, sem_i, sem_d):
    tc = x1_ref.shape[0]
    n_idx = tc * TOP_K
    i = pl.program_id(0)
    n = pl.num_programs(0)
    slot = i % 2

    def gather(j, sl):
        cp = pltpu.make_async_copy(dest_hbm.at[pl.ds(j * n_idx, n_idx)], idx_s, sem_i)
        cp.start()
        cp.wait()

        def issue(t, _):
            for k in range(TOP_K):
                _row_copy(y_hbm, idx_s[t * TOP_K + k], rows_s.at[sl, k], t, sem_d.at[sl]).start(priority=k % 2)
            return 0

        lax.fori_loop(0, tc, issue, 0, unroll=8)

    @pl.when(i == 0)
    def _():
        gather(0, 0)

    @pl.when(i + 1 < n)
    def _():
        gather(i + 1, 1 - slot)

    for k in range(TOP_K):
        pltpu.make_async_copy(y_hbm.at[pl.ds(0, tc), :], rows_s.at[slot, k], sem_d.at[slot]).wait()

    meta = meta_ref[...]
    moe = meta[:, TOP_K:TOP_K + 1] * rows_s[slot, 0]
    for k in range(1, TOP_K):
        moe = moe + meta[:, TOP_K + k:TOP_K + k + 1] * rows_s[slot, k]
    o_ref[...] = _rms(x1_ref[...] + g2_ref[0] * moe, fw_ref[...])


def _combine(x1, meta, g2, final_w, dest_flat, y, t_seq, tc):
    n, d = x1.shape
    return pl.pallas_call(
        _combine_kernel, grid=(n // tc,),
        in_specs=[pl.BlockSpec((tc, d), lambda i: (i, 0)),
                  pl.BlockSpec((tc, LANES), lambda i: (i, 0)),
                  pl.BlockSpec((1, 1, d), lambda i: (i * tc // t_seq, 0, 0)),
                  pl.BlockSpec((1, d), lambda i: (0, 0)),
                  pl.BlockSpec(memory_space=pl.ANY),
                  pl.BlockSpec(memory_space=pl.ANY)],
        out_specs=pl.BlockSpec((tc, d), lambda i: (i, 0)),
        out_shape=jax.ShapeDtypeStruct((n, d), F32),
        scratch_shapes=[pltpu.VMEM((2, TOP_K, tc, d), F32), pltpu.SMEM((tc * TOP_K,), jnp.int32),
                        pltpu.SemaphoreType.DMA, pltpu.SemaphoreType.DMA((2,))],
        compiler_params=_params("arbitrary"), name="moe_combine",
    )(x1, meta, g2, final_w, dest_flat, y)


def _gate_weights(wa, wx):
    _, heads, hd, _ = wa.shape
    hh = heads // 2
    eye = jnp.eye(hh, dtype=wa.dtype)

    def blockdiag(wsel):
        return jnp.einsum('hij,hg->higj', wsel, eye).reshape(hh * hd, hh * hd)

    halves = []
    for s in range(2):
        sl = slice(s * hh, (s + 1) * hh)
        halves.append(jnp.concatenate([blockdiag(wa[0, sl]), blockdiag(wx[0, sl]),
                                       blockdiag(wa[1, sl]), blockdiag(wx[1, sl])], axis=1))
    return jnp.stack(halves).astype(BF16)


def kernel(x, c, ctx, c_ctx, norm1_w, norm2_w, w_ada, b_ada, w_in, hg_lb_logits, hg_norm_w, rg_conv_w, rg_conv_b,
           rg_wa, rg_ba, rg_wx, rg_bx, rg_lambda, w_out, router_w, router_b, w_gate_up, b_gate_up, w_down,
           b_down, final_norm_w):
    b, t, d = x.shape
    tcx = ctx.shape[1]
    n_exp = router_w.shape[-1]
    n_tok = b * t
    depth = w_in.shape[0]
    lb_all = jnp.cumsum(jax.nn.softmax(hg_lb_logits.astype(F32), axis=0), axis=0)

    for l in range(depth):
        assert l == depth - 1, "context stream update of non-final layers is not implemented"
        pad = (-(b + 1)) % 8
        c_all = jnp.concatenate([c, c_ctx[None], jnp.zeros((pad, d), F32)], axis=0)
        mod = _mod(c_all, w_ada[l], b_ada[l])
        sh1, sc1, g1, sh2, sc2, g2 = [m[:b, None, :] for m in jnp.split(mod, 6, axis=-1)]
        csh1, csc1 = [m[b:b + 1, None, :] for m in jnp.split(mod, 6, axis=-1)[:2]]

        w_in_b = w_in[l].astype(BF16)
        nw1 = norm1_w[l].reshape(1, d)
        tm = min(512, t)
        p_lat = _inproj(x, sh1, sc1, nw1, w_in_b, tm)
        p_ctx = _inproj(ctx, csh1, csc1, nw1, w_in_b, min(256, tcx))

        hg = _hgrn2(p_lat, p_ctx, lb_all[l], hg_norm_w[l].reshape(1, -1))
        ch = rg_conv_w.shape[-1]
        wg = _gate_weights(rg_wa[l], rg_wx[l])
        bg = jnp.concatenate([rg_ba[l, 0], rg_bx[l, 0], rg_ba[l, 1], rg_bx[l, 1]]).reshape(1, 4 * ch)
        rg = _rglru(p_lat, p_ctx, rg_conv_w[l], rg_conv_b[l].reshape(1, ch), wg, bg, rg_lambda[l])

        x1, h2, meta, cnt = _mix(hg, rg, x, g1, sh2, sc2, norm2_w[l].reshape(1, d), w_out[l].astype(BF16),
                                 router_w[l].astype(BF16), router_b[l].reshape(1, n_exp), tm)

        bm = 512
        meta2 = meta.reshape(n_tok, LANES)
        idx = meta2[:, 0:TOP_K].astype(jnp.int32)
        rank = meta2[:, 2 * TOP_K:3 * TOP_K].astype(jnp.int32)
        counts = cnt[0, :n_exp].astype(jnp.int32)
        padded = (counts + bm - 1) // bm * bm
        pad_end = jnp.cumsum(padded)
        pad_start = pad_end - padded
        dest = (pad_start[idx] + rank).reshape(-1)
        n_blocks = -(-n_tok * TOP_K // bm) + n_exp
        blk_start = jnp.arange(n_blocks, dtype=jnp.int32) * bm
        blk_expert = jnp.minimum(jnp.sum(blk_start[:, None] >= pad_end[None, :], axis=1), n_exp - 1).astype(jnp.int32)
        n_used = (pad_end[-1:] // bm).astype(jnp.int32)

        tdc = min(256, t)
        n_slots = n_blocks * bm
        tail = jnp.stack([pad_end[-1], (n_slots - pad_end[-1]) // (bm // 2)]).astype(jnp.int32)
        xs = _dispatch(h2.reshape(n_tok, d), dest, pad_start + counts, padded - counts, tail, n_slots,
                       min(512, t), bm)
        y = _experts(xs, blk_expert, n_used, w_gate_up[l].astype(BF16), b_gate_up[l], w_down[l].astype(BF16),
                     b_down[l], bm)
        out = _combine(x1.reshape(n_tok, d), meta2, g2, final_norm_w.reshape(1, d), dest, y, t, tdc)
        return out.reshape(b, t, d)
```

```python
import functools

import jax
import jax.numpy as jnp
from jax import lax
from jax.experimental import pallas as pl
from jax.experimental.pallas import tpu as pltpu

GRID_W = 64
HG_HEADS = 4
HG_CHUNK = 32
RG_HEADS = 8
RG_CONV = 4
RG_C = 8.0
TOP_K = 4
SWIGLU_LIMIT = 7.0
SWIGLU_ALPHA = 1.702
EPS = 1e-6

LANES = 128
SUBLANES = 8
VMEM_LIMIT = 56 * 1024 * 1024

F32 = jnp.float32
BF16 = jnp.bfloat16
HIGHEST = lax.Precision.HIGHEST


def _params(*sem):
    return pltpu.CompilerParams(dimension_semantics=sem, vmem_limit_bytes=VMEM_LIMIT)


def _silu(x):
    return x * jax.nn.sigmoid(x)


def _rms(x, w):
    return x * lax.rsqrt(jnp.mean(x * x, axis=-1, keepdims=True) + EPS) * w


def _dot(a, b):
    return jnp.dot(a, b, preferred_element_type=F32)


def _dot_nt(a, b):
    return lax.dot_general(a, b, (((1,), (1,)), ((), ())), preferred_element_type=F32)


def _mod_kernel(c_ref, w_ref, b_ref, o_ref):
    o_ref[...] = jnp.dot(_silu(c_ref[...]), w_ref[...], preferred_element_type=F32,
                         precision=HIGHEST) + b_ref[...]


def _mod(c_all, w_ada, b_ada):
    r, d = c_all.shape
    n = w_ada.shape[1]
    tn = n // 4
    return pl.pallas_call(
        _mod_kernel, grid=(n // tn,),
        in_specs=[pl.BlockSpec((r, d), lambda j: (0, 0)),
                  pl.BlockSpec((d, tn), lambda j: (0, j)),
                  pl.BlockSpec((1, tn), lambda j: (0, j))],
        out_specs=pl.BlockSpec((r, tn), lambda j: (0, j)),
        out_shape=jax.ShapeDtypeStruct((r, n), F32),
        compiler_params=_params("arbitrary"), name="adaln_mod",
    )(c_all, w_ada, b_ada.reshape(1, n))


def _inproj_kernel(x_ref, sh_ref, sc_ref, nw_ref, w_ref, o_ref):
    h = _rms(x_ref[0], nw_ref[...]) * (1.0 + sc_ref[0]) + sh_ref[0]
    o_ref[0] = _dot(h.astype(BF16), w_ref[...])


def _inproj(x, shift, scale, norm_w, w_bf16, tm):
    b, t, d = x.shape
    n = w_bf16.shape[1]
    per_batch = shift.shape[0] == b
    mod_map = (lambda i, j: (i, 0, 0)) if per_batch else (lambda i, j: (0, 0, 0))
    return pl.pallas_call(
        _inproj_kernel, grid=(b, t // tm),
        in_specs=[pl.BlockSpec((1, tm, d), lambda i, j: (i, j, 0)),
                  pl.BlockSpec((1, 1, d), mod_map),
                  pl.BlockSpec((1, 1, d), mod_map),
                  pl.BlockSpec((1, d), lambda i, j: (0, 0)),
                  pl.BlockSpec((d, n), lambda i, j: (0, 0))],
        out_specs=pl.BlockSpec((1, tm, n), lambda i, j: (i, j, 0)),
        out_shape=jax.ShapeDtypeStruct((b, t, n), F32),
        compiler_params=_params("arbitrary", "arbitrary"), name="norm_inproj",
    )(x, shift, scale, norm_w, w_bf16)


def _split_bf16(x):
    hi = x.astype(BF16)
    return hi, (x - hi.astype(F32)).astype(BF16)


def _hg_kernel(q_ref, v_ref, zf_ref, zb_ref, g_ref, cv_ref, czf_ref, czb_ref, lb_ref, nw_ref, o_ref,
               o_s, qd_s, u_s, dec_s, st_s, cu_s, cdec_s, tot_s, *, t_lat, t_ctx):
    c = HG_CHUNK
    dk = q_ref.shape[-1]
    lb = lb_ref[...]
    lbf, lbb = lb[0:1], lb[1:2]

    def prep(rb, zf, zb, q, v, u_out, dec_out, blk, r0):
        nc = rb // c
        ri = lax.broadcasted_iota(jnp.int32, (rb, rb), 0)
        ci = lax.broadcasted_iota(jnp.int32, (rb, rb), 1)
        same = (ri // c) == (ci // c)
        low = same & (ci <= ri)
        upp = same & (ci >= ri)
        ff = lbf + (1.0 - lbf) * jax.nn.sigmoid(zf)
        fb = lbb + (1.0 - lbb) * jax.nn.sigmoid(zb)
        lgf, lgb = jnp.log(ff), jnp.log(fb)
        rhs = jnp.concatenate([*_split_bf16(lgf), *_split_bf16(lgb)], axis=1)
        pre = _dot(low.astype(BF16), rhs)
        suf = _dot(upp.astype(BF16), rhs)
        bcf = pre[:, :dk] + pre[:, dk:2 * dk]
        remf = suf[:, :dk] + suf[:, dk:2 * dk] - lgf
        bcb = suf[:, 2 * dk:3 * dk] + suf[:, 3 * dk:]
        remb = pre[:, 2 * dk:3 * dk] + pre[:, 3 * dk:] - lgb
        kkf, kkb = 1.0 - ff, 1.0 - fb
        kef = (kkf * jnp.exp(remf)).astype(BF16)
        keb = (kkb * jnp.exp(remb)).astype(BF16)
        chunk_of_row = lax.broadcasted_iota(jnp.int32, (rb, dk), 0) // c
        zero = jnp.zeros((rb, dk), BF16)
        keys = jnp.concatenate([jnp.where(chunk_of_row == j, ke, zero) for ke in (kef, keb) for j in range(nc)],
                               axis=1)
        u_all = _dot(v.T.astype(BF16), keys)
        c0 = blk * nc
        for d in range(2):
            for j in range(nc):
                u_out[d, c0 + j] = u_all[:, (d * nc + j) * dk:(d * nc + j + 1) * dk]
        half = tot_s.shape[0] // 2
        tot_s[0:rb, :] = bcf + remf
        tot_s[half:half + rb, :] = bcb + remb
        dec_out[0, pl.ds(pl.multiple_of(c0, nc), nc), :] = jnp.exp(tot_s[pl.ds(0, nc, stride=c), :])
        dec_out[1, pl.ds(pl.multiple_of(c0, nc), nc), :] = jnp.exp(tot_s[pl.ds(half, nc, stride=c), :])
        if q is None:
            return
        sq = _silu(q)
        qdf = (sq * jnp.exp(bcf)).astype(BF16)
        qdb = (sq * jnp.exp(bcb)).astype(BF16)
        kdf = (kkf * jnp.exp(-bcf)).astype(BF16)
        kdb = (kkb * jnp.exp(-bcb)).astype(BF16)
        p = jnp.where(low, _dot_nt(qdf, kdf), 0.0) + jnp.where(upp, _dot_nt(qdb, kdb), 0.0)
        o_s[pl.ds(r0, rb), :] = _dot(p.astype(BF16), v.astype(BF16))
        qd_s[pl.ds(r0, rb), 0:dk] = qdf
        qd_s[pl.ds(r0, rb), dk:2 * dk] = qdb

    def scan(n, ur, decr, keep, carry):
        def body(i, carry):
            sf, sb = carry
            j = n - 1 - i
            if keep:
                st_s[i, :, 0:dk] = sf.astype(BF16)
                st_s[j, :, dk:2 * dk] = sb.astype(BF16)
            return sf * decr[0, pl.ds(i, 1), :] + ur[0, i], sb * decr[1, pl.ds(j, 1), :] + ur[1, j]

        return lax.fori_loop(0, n, body, carry, unroll=2)

    rbc = min(256, t_ctx)
    for blk in range(t_ctx // rbc):
        r0 = blk * rbc
        prep(rbc, czf_ref[0, r0:r0 + rbc, :], czb_ref[0, r0:r0 + rbc, :], None, cv_ref[0, r0:r0 + rbc, :],
             cu_s, cdec_s, blk, r0)
    zero = jnp.zeros((dk, dk), F32)
    carry = scan(t_ctx // c, cu_s, cdec_s, False, (zero, zero))

    rbl = min(256, t_lat)

    def lat_prep(blk, _):
        r0 = pl.multiple_of(blk * rbl, rbl)
        rows = pl.ds(r0, rbl)
        prep(rbl, zf_ref[0, rows, :], zb_ref[0, rows, :], q_ref[0, rows, :], v_ref[0, rows, :], u_s, dec_s, blk, r0)
        return 0

    lax.fori_loop(0, t_lat // rbl, lat_prep, 0)
    scan(t_lat // c, u_s, dec_s, True, carry)

    def finish(blk, _):
        r0 = pl.multiple_of(blk * rbl, rbl)
        inter = [_dot_nt(qd_s[pl.ds(r0 + j * c, c), :], st_s[blk * (rbl // c) + j]) for j in range(rbl // c)]
        o = o_s[pl.ds(r0, rbl), :] + jnp.concatenate(inter, axis=0)
        o_ref[0, pl.ds(r0, rbl), :] = _rms(o, nw_ref[...]) * _silu(g_ref[0, pl.ds(r0, rbl), :])
        return 0

    lax.fori_loop(0, t_lat // rbl, finish, 0, unroll=2)


def _hgrn2(p_lat, p_ctx, lb, norm_w):
    b, t, _ = p_lat.shape
    tc = p_ctx.shape[1]
    hw = lb.shape[1]
    dk = hw // HG_HEADS
    nh = HG_HEADS
    c = HG_CHUNK

    def col(k, tt):
        return pl.BlockSpec((1, tt, dk), lambda i, h, k=k: (i, 0, k * nh + h))

    return pl.pallas_call(
        functools.partial(_hg_kernel, t_lat=t, t_ctx=tc), grid=(b, nh),
        in_specs=[col(0, t), col(1, t), col(2, t), col(3, t), col(4, t),
                  col(1, tc), col(2, tc), col(3, tc),
                  pl.BlockSpec((2, dk), lambda i, h: (0, h)),
                  pl.BlockSpec((1, dk), lambda i, h: (0, h))],
        out_specs=pl.BlockSpec((1, t, dk), lambda i, h: (i, 0, h)),
        out_shape=jax.ShapeDtypeStruct((b, t, hw), F32),
        scratch_shapes=[pltpu.VMEM((t, dk), F32),
                        pltpu.VMEM((t, 2 * dk), BF16),
                        pltpu.VMEM((2, t // c, dk, dk), F32),
                        pltpu.VMEM((2, t // c, dk), F32),
                        pltpu.VMEM((t // c, dk, 2 * dk), BF16),
                        pltpu.VMEM((2, tc // c, dk, dk), F32),
                        pltpu.VMEM((2, tc // c, dk), F32),
                        pltpu.VMEM((2 * min(256, max(t, tc)), dk), F32)],
        compiler_params=_params("arbitrary", "arbitrary"), name="hgrn2",
    )(p_lat, p_lat, p_lat, p_lat, p_lat, p_ctx, p_ctx, p_ctx, lb, norm_w)


def _shift_rows(x, k):
    n = x.shape[0]
    y = pltpu.roll(x, k % n, 0)
    r = lax.broadcasted_iota(jnp.int32, x.shape, 0)
    return jnp.where((r >= k) & (r < n + k), y, 0.0)


def _rg_kernel(rx_ref, rgate_ref, crx_ref, cw_ref, cb_ref, wg_ref, bg_ref, lam_ref, o_ref,
               xc_s, af_s, bf_s, ab_s, bb_s, hf_s, hb_s, caf_s, cbf_s, cab_s, cbb_s, *, t_lat, t_ctx):
    w = GRID_W
    rows = t_lat // w
    ch = rx_ref.shape[-1]
    half = ch // 2
    cw = cw_ref[...]
    cb = cb_ref[...]
    bg = bg_ref[...]
    nl = -lam_ref[...]
    cdec = -RG_C * (jnp.maximum(nl, 0.0) + jnp.log1p(jnp.exp(-jnp.abs(nl))))

    def conv(xm2, xm1, x0, xp1):
        return cb + cw[0:1] * xm2 + cw[1:2] * xm1 + cw[2:3] * x0 + cw[3:4] * xp1

    def gates(xc):
        xb = xc.astype(BF16)
        g0 = _dot(xb[:, :half], wg_ref[0])
        g1 = _dot(xb[:, half:], wg_ref[1])
        outs = []
        for d in range(2):
            pre = []
            for s in (2 * d, 2 * d + 1):
                pre.append(jnp.concatenate([g0[:, s * half:(s + 1) * half], g1[:, s * half:(s + 1) * half]],
                                           axis=1) + bg[:, s * ch:(s + 1) * ch])
            log_a = cdec[d:d + 1] * jax.nn.sigmoid(pre[0])
            a = jnp.exp(log_a)
            mult = jnp.sqrt(-jnp.tanh(log_a) * (a * a + 1.0))
            outs += [a, mult * jax.nn.sigmoid(pre[1]) * xc]
        return outs

    xctx = crx_ref[0]
    xcc = conv(_shift_rows(xctx, 2), _shift_rows(xctx, 1), xctx, _shift_rows(xctx, -1))
    caf_s[...], cbf_s[...], cab_s[...], cbb_s[...] = gates(xcc)

    def cstep(i, carry):
        hf, hb = carry
        hf = caf_s[pl.ds(i, 1), :] * hf + cbf_s[pl.ds(i, 1), :]
        j = t_ctx - 1 - i
        hb = cab_s[pl.ds(j, 1), :] * hb + cbb_s[pl.ds(j, 1), :]
        return hf, hb

    zrow = jnp.zeros((1, ch), F32)
    hf0, hb0 = lax.fori_loop(0, t_ctx, cstep, (zrow, zrow))

    def slab(rr):
        if 0 <= rr < rows:
            return rx_ref[0, rr * w:(rr + 1) * w, :]
        if rr < 0:
            return _shift_rows(rx_ref[0, (rr + rows) * w:(rr + rows + 1) * w, :], 1)
        return _shift_rows(rx_ref[0, (rr - rows) * w:(rr - rows + 1) * w, :], -1)

    for r in range(rows):
        xc_s[r * w:(r + 1) * w, :] = conv(slab(r - 2), slab(r - 1), slab(r), slab(r + 1))

    mb = min(256, t_lat)

    def gbody(i, _):
        r0 = pl.multiple_of(i * mb, mb)
        a_f, b_f, a_b, b_b = gates(xc_s[pl.ds(r0, mb), :])
        af_s[pl.ds(r0, mb), :] = a_f
        bf_s[pl.ds(r0, mb), :] = b_f
        ab_s[pl.ds(r0, mb), :] = a_b
        bb_s[pl.ds(r0, mb), :] = b_b
        return 0

    lax.fori_loop(0, t_lat // mb, gbody, 0)

    def l1(i, _):
        pf = pl.multiple_of(i * w, w)
        qf = pl.multiple_of((i - 1) * w, w)
        a = af_s[pl.ds(pf, w), :]
        af_s[pl.ds(pf, w), :] = a * af_s[pl.ds(qf, w), :]
        bf_s[pl.ds(pf, w), :] = a * bf_s[pl.ds(qf, w), :] + bf_s[pl.ds(pf, w), :]
        pb = pl.multiple_of((rows - 1 - i) * w, w)
        qb = pl.multiple_of((rows - i) * w, w)
        a = ab_s[pl.ds(pb, w), :]
        ab_s[pl.ds(pb, w), :] = a * ab_s[pl.ds(qb, w), :]
        bb_s[pl.ds(pb, w), :] = a * bb_s[pl.ds(qb, w), :] + bb_s[pl.ds(pb, w), :]
        return 0

    lax.fori_loop(1, rows, l1, 0)

    last = (rows - 1) * w

    def l2(i, carry):
        hf, hb = carry
        hf_s[pl.ds(i, 1), :] = hf
        hf = af_s[pl.ds(last + i, 1), :] * hf + bf_s[pl.ds(last + i, 1), :]
        j = w - 1 - i
        hb_s[pl.ds(j, 1), :] = hb
        hb = ab_s[pl.ds(j, 1), :] * hb + bb_s[pl.ds(j, 1), :]
        return hf, hb

    lax.fori_loop(0, w, l2, (hf0, hb0))

    def l3(i, _):
        p = pl.multiple_of(i * w, w)
        h = (af_s[pl.ds(p, w), :] * hf_s[...] + bf_s[pl.ds(p, w), :]
             + ab_s[pl.ds(p, w), :] * hb_s[...] + bb_s[pl.ds(p, w), :])
        o_ref[0, pl.ds(p, w), :] = jax.nn.gelu(rgate_ref[0, pl.ds(p, w), :]) * h
        return 0

    lax.fori_loop(0, rows, l3, 0)


def _rglru(p_lat, p_ctx, conv_w, conv_b, wg, bg, lam):
    b, t, _ = p_lat.shape
    tc = p_ctx.shape[1]
    ch = conv_w.shape[1]
    rx_blk = (p_lat.shape[2] - 2 * ch) // ch
    full = lambda shape: pl.BlockSpec(shape, lambda i: (0,) * len(shape))
    big = lambda: pltpu.VMEM((t, ch), F32)
    small = lambda: pltpu.VMEM((tc, ch), F32)
    return pl.pallas_call(
        functools.partial(_rg_kernel, t_lat=t, t_ctx=tc), grid=(b,),
        in_specs=[pl.BlockSpec((1, t, ch), lambda i: (i, 0, rx_blk)),
                  pl.BlockSpec((1, t, ch), lambda i: (i, 0, rx_blk + 1)),
                  pl.BlockSpec((1, tc, ch), lambda i: (i, 0, rx_blk)),
                  full(conv_w.shape), full(conv_b.shape), full(wg.shape), full(bg.shape), full(lam.shape)],
        out_specs=pl.BlockSpec((1, t, ch), lambda i: (i, 0, 0)),
        out_shape=jax.ShapeDtypeStruct((b, t, ch), F32),
        scratch_shapes=[big(), big(), big(), big(), big(),
                        pltpu.VMEM((GRID_W, ch), F32), pltpu.VMEM((GRID_W, ch), F32),
                        small(), small(), small(), small()],
        compiler_params=_params("arbitrary"), name="rglru",
    )(p_lat, p_lat, p_ctx, conv_w, conv_b, wg, bg, lam)


def _mix_kernel(hg_ref, rg_ref, x_ref, g1_ref, sh_ref, sc_ref, nw_ref, wo_ref, rw_ref, rb_ref,
                x1_ref, h2_ref, meta_ref, cnt_ref, base_s, *, n_exp):
    tm = x_ref.shape[1]

    @pl.when((pl.program_id(0) == 0) & (pl.program_id(1) == 0))
    def _():
        base_s[...] = jnp.zeros_like(base_s)

    hcat = jnp.concatenate([hg_ref[0], rg_ref[0]], axis=1).astype(BF16)
    x1 = x_ref[0] + g1_ref[0] * _dot(hcat, wo_ref[...])
    x1_ref[0] = x1
    h2 = _rms(x1, nw_ref[...]) * (1.0 + sc_ref[0]) + sh_ref[0]
    for sl in range(SUBLANES):
        h2_ref[0, :, sl, :] = h2[:, sl * LANES:(sl + 1) * LANES]
    logits = _dot(h2.astype(BF16), rw_ref[...]) + rb_ref[...]

    lane_e = lax.broadcasted_iota(jnp.int32, (tm, n_exp), 1)
    vals, idxs = [], []
    cur = logits
    for _ in range(TOP_K):
        m = jnp.max(cur, axis=1, keepdims=True)
        ix = jnp.min(jnp.where(cur == m, lane_e, n_exp), axis=1, keepdims=True)
        vals.append(m)
        idxs.append(ix)
        cur = jnp.where(lane_e == ix, -jnp.inf, cur)
    ex = [jnp.exp(v - vals[0]) for v in vals]
    den = ex[0] + ex[1] + ex[2] + ex[3]

    lane = lax.broadcasted_iota(jnp.int32, (tm, LANES), 1)
    onehot = jnp.zeros((tm, LANES), F32)
    for k in range(TOP_K):
        onehot = jnp.where(lane == idxs[k] + k * n_exp, 1.0, onehot)
    ri = lax.broadcasted_iota(jnp.int32, (tm, tm), 0)
    ci = lax.broadcasted_iota(jnp.int32, (tm, tm), 1)
    prefix = _dot((ci < ri).astype(BF16), onehot.astype(BF16))
    tot = jnp.broadcast_to(prefix[tm - 1:tm] + onehot[tm - 1:tm], (8, LANES))
    lane8 = lax.broadcasted_iota(jnp.int32, (8, LANES), 1)
    off = base_s[...]
    tot_all = tot
    for j in range(1, TOP_K):
        rolled = pltpu.roll(tot, j * n_exp, 1)
        off = off + jnp.where(lane8 >= j * n_exp, rolled, 0.0)
        tot_all = tot_all + rolled
    pos = onehot * (prefix + off[0:1])
    meta = jnp.zeros((tm, LANES), F32)
    for k in range(TOP_K):
        in_k = (lane >= k * n_exp) & (lane < (k + 1) * n_exp)
        rank = jnp.sum(jnp.where(in_k, pos, 0.0), axis=1, keepdims=True)
        meta = jnp.where(lane == k, idxs[k].astype(F32), meta)
        meta = jnp.where(lane == TOP_K + k, ex[k] / den, meta)
        meta = jnp.where(lane == 2 * TOP_K + k, rank, meta)
    meta_ref[0] = meta
    base_s[...] = base_s[...] + tot_all
    cnt_ref[...] = base_s[...]


def _mix(hg, rg, x, g1, sh2, sc2, norm_w, wo_bf16, router_w, router_b, tm):
    b, t, d = x.shape
    hw = hg.shape[2]
    n_exp = router_w.shape[1]
    assert TOP_K * n_exp == LANES
    tok = lambda last: pl.BlockSpec((1, tm, last), lambda i, j: (i, j, 0))
    per_b = pl.BlockSpec((1, 1, d), lambda i, j: (i, 0, 0))
    full = lambda shape: pl.BlockSpec(shape, lambda i, j: (0,) * len(shape))
    return pl.pallas_call(
        functools.partial(_mix_kernel, n_exp=n_exp), grid=(b, t // tm),
        in_specs=[tok(hw), tok(hw), tok(d), per_b, per_b, per_b, full((1, d)),
                  full(wo_bf16.shape), full(router_w.shape), full((1, n_exp))],
        out_specs=[tok(d), pl.BlockSpec((1, tm, SUBLANES, d // SUBLANES), lambda i, j: (i, j, 0, 0)), tok(LANES),
                   pl.BlockSpec((8, LANES), lambda i, j: (0, 0))],
        out_shape=[jax.ShapeDtypeStruct((b, t, d), F32), jax.ShapeDtypeStruct((b, t, SUBLANES, d // SUBLANES), F32),
                   jax.ShapeDtypeStruct((b, t, LANES), F32), jax.ShapeDtypeStruct((8, LANES), F32)],
        scratch_shapes=[pltpu.VMEM((8, LANES), F32)],
        compiler_params=_params("arbitrary", "arbitrary"), name="outproj_router",
    )(hg, rg, x, g1, sh2, sc2, norm_w, wo_bf16, router_w, router_b)


def _row_copy(src, s, dst, d, sem):
    return pltpu.make_async_copy(src.at[pl.ds(s, 1)], dst.at[pl.ds(d, 1)], sem)


def _dispatch_kernel(fill_off, fill_n, tail, h_ref, dest_hbm, xs_out, idx_s, zero_s, sem_i, sem_d, sem_z, *, pad_bits):
    td = h_ref.shape[0]
    n_idx = td * TOP_K
    n_exp = fill_n.shape[0]
    zrows = zero_s.shape[0]
    i = pl.program_id(0)
    cp = pltpu.make_async_copy(dest_hbm.at[pl.ds(i * n_idx, n_idx)], idx_s, sem_i)
    cp.start()

    def fill(wait):
        def go(copy, cond):
            @pl.when(cond)
            def _():
                copy.wait() if wait else copy.start()

        def per_expert(e, _):
            off = fill_off[e]
            npad = fill_n[e]
            n_single = npad & (SUBLANES - 1)
            for r in range(SUBLANES - 1):
                go(pltpu.make_async_copy(zero_s.at[pl.ds(0, 1)], xs_out.at[pl.ds(off + r, 1)], sem_z),
                   r < n_single)
            off = pl.multiple_of(off + n_single, SUBLANES)
            for bit in reversed(range(SUBLANES.bit_length() - 1, pad_bits)):
                size = 1 << bit
                go(pltpu.make_async_copy(zero_s.at[pl.ds(0, size)], xs_out.at[pl.ds(off, size)], sem_z),
                   (npad & size) != 0)
                off = pl.multiple_of(off + (npad & size), SUBLANES)
            return 0

        lax.fori_loop(0, n_exp, per_expert, 0)

        def per_tail_chunk(j, _):
            off = pl.multiple_of(tail[0] + j * zrows, zrows)
            copy = pltpu.make_async_copy(zero_s, xs_out.at[pl.ds(off, zrows)], sem_z)
            copy.wait() if wait else copy.start()
            return 0

        lax.fori_loop(0, tail[1], per_tail_chunk, 0)

    @pl.when(i == 0)
    def _():
        zero_s[...] = jnp.zeros_like(zero_s)
        fill(False)

    cp.wait()

    def issue(t, _):
        for k in range(TOP_K):
            _row_copy(h_ref, t, xs_out, idx_s[t * TOP_K + k], sem_d).start(priority=k % 2)
        return 0

    lax.fori_loop(0, td, issue, 0, unroll=8)
    for k in range(TOP_K):
        pltpu.make_async_copy(h_ref, xs_out.at[pl.ds(0, td)], sem_d).wait()

    @pl.when(i == 0)
    def _():
        fill(True)


def _dispatch(h2, dest_flat, fill_off, fill_n, tail, n_slots, td, bm):
    n = h2.shape[0]
    tile = h2.shape[1:]
    pad_bits = (bm - 1).bit_length()
    grid_spec = pltpu.PrefetchScalarGridSpec(
        num_scalar_prefetch=3, grid=(n // td,),
        in_specs=[pl.BlockSpec((td, *tile), lambda i, fo, fn, tl: (i, 0, 0)),
                  pl.BlockSpec(memory_space=pl.ANY)],
        out_specs=pl.BlockSpec(memory_space=pl.ANY),
        scratch_shapes=[pltpu.SMEM((td * TOP_K,), jnp.int32), pltpu.VMEM((bm // 2, *tile), F32),
                        pltpu.SemaphoreType.DMA, pltpu.SemaphoreType.DMA, pltpu.SemaphoreType.DMA])
    return pl.pallas_call(
        functools.partial(_dispatch_kernel, pad_bits=pad_bits), grid_spec=grid_spec,
        out_shape=jax.ShapeDtypeStruct((n_slots, *tile), F32),
        compiler_params=_params("arbitrary"), name="moe_dispatch",
    )(fill_off, fill_n, tail, h2, dest_flat)


def _expert_kernel(be_ref, nu_ref, x_ref, wgu_ref, bgu_ref, wd_ref, bd_ref, y_ref):
    del be_ref
    d_ff = wd_ref.shape[1]

    @pl.when(pl.program_id(0) >= nu_ref[0])
    def _():
        y_ref[...] = jnp.zeros_like(y_ref)

    @pl.when(pl.program_id(0) < nu_ref[0])
    def _():
        x = jnp.concatenate([x_ref[:, sl, :] for sl in range(SUBLANES)], axis=1)
        gu = _dot(x.astype(BF16), wgu_ref[0]) + bgu_ref[0]
        gate = jnp.minimum(gu[:, :d_ff], SWIGLU_LIMIT)
        up = jnp.clip(gu[:, d_ff:], -SWIGLU_LIMIT, SWIGLU_LIMIT)
        act = gate * jax.nn.sigmoid(SWIGLU_ALPHA * gate) * (up + 1.0)
        y = _dot(act.astype(BF16), wd_ref[0]) + bd_ref[0]
        for sl in range(SUBLANES):
            y_ref[:, sl, :] = y[:, sl * LANES:(sl + 1) * LANES]


def _experts(xs, blk_expert, n_used, wgu, bgu, wd, bd, bm):
    n_slots = xs.shape[0]
    tile = xs.shape[1:]
    n_exp, d, f2 = wgu.shape
    d_ff = wd.shape[1]
    n_blocks = n_slots // bm
    row = lambda i, be, nu: (jnp.minimum(i, nu[0] - 1), 0, 0)
    grid_spec = pltpu.PrefetchScalarGridSpec(
        num_scalar_prefetch=2, grid=(n_blocks,),
        in_specs=[pl.BlockSpec((bm, *tile), row),
                  pl.BlockSpec((1, d, f2), lambda i, be, nu: (be[i], 0, 0)),
                  pl.BlockSpec((1, 1, f2), lambda i, be, nu: (be[i], 0, 0)),
                  pl.BlockSpec((1, d_ff, d), lambda i, be, nu: (be[i], 0, 0)),
                  pl.BlockSpec((1, 1, d), lambda i, be, nu: (be[i], 0, 0))],
        out_specs=pl.BlockSpec((bm, *tile), lambda i, be, nu: (i, 0, 0)))
    return pl.pallas_call(
        _expert_kernel, grid_spec=grid_spec,
        out_shape=jax.ShapeDtypeStruct((n_slots, *tile), F32),
        compiler_params=_params("arbitrary"), name="moe_experts",
    )(blk_expert, n_used, xs, wgu, bgu.reshape(n_exp, 1, f2), wd, bd.reshape(n_exp, 1, d))


def _combine_kernel(x1_ref, meta_ref, g2_ref, fw_ref, dest_hbm, y_hbm, o_ref, rows_s, idx_s, sem_i, sem_d):
    tc = x1_ref.shape[0]
    n_idx = tc * TOP_K
    i = pl.program_id(0)
    n = pl.num_programs(0)
    slot = i % 2

    def gather(j, sl):
        cp = pltpu.make_async_copy(dest_hbm.at[pl.ds(j * n_idx, n_idx)], idx_s, sem_i)
        cp.start()
        cp.wait()

        def issue(t, _):
            for k in range(TOP_K):
                _row_copy(y_hbm, idx_s[t * TOP_K + k], rows_s.at[sl, k], t, sem_d.at[sl]).start(priority=k % 2)
            return 0

        lax.fori_loop(0, tc, issue, 0, unroll=8)

    @pl.when(i == 0)
    def _():
        gather(0, 0)

    @pl.when(i + 1 < n)
    def _():
        gather(i + 1, 1 - slot)

    for k in range(TOP_K):
        pltpu.make_async_copy(y_hbm.at[pl.ds(0, tc)], rows_s.at[slot, k], sem_d.at[slot]).wait()

    def rows(k):
        return jnp.concatenate([rows_s[slot, k, :, sl, :] for sl in range(SUBLANES)], axis=1)

    meta = meta_ref[...]
    moe = meta[:, TOP_K:TOP_K + 1] * rows(0)
    for k in range(1, TOP_K):
        moe = moe + meta[:, TOP_K + k:TOP_K + k + 1] * rows(k)
    o_ref[...] = _rms(x1_ref[...] + g2_ref[0] * moe, fw_ref[...])


def _combine(x1, meta, g2, final_w, dest_flat, y, t_seq, tc):
    n, d = x1.shape
    return pl.pallas_call(
        _combine_kernel, grid=(n // tc,),
        in_specs=[pl.BlockSpec((tc, d), lambda i: (i, 0)),
                  pl.BlockSpec((tc, LANES), lambda i: (i, 0)),
                  pl.BlockSpec((1, 1, d), lambda i: (i * tc // t_seq, 0, 0)),
                  pl.BlockSpec((1, d), lambda i: (0, 0)),
                  pl.BlockSpec(memory_space=pl.ANY),
                  pl.BlockSpec(memory_space=pl.ANY)],
        out_specs=pl.BlockSpec((tc, d), lambda i: (i, 0)),
        out_shape=jax.ShapeDtypeStruct((n, d), F32),
        scratch_shapes=[pltpu.VMEM((2, TOP_K, tc, *y.shape[1:]), F32), pltpu.SMEM((tc * TOP_K,), jnp.int32),
                        pltpu.SemaphoreType.DMA, pltpu.SemaphoreType.DMA((2,))],
        compiler_params=_params("arbitrary"), name="moe_combine",
    )(x1, meta, g2, final_w, dest_flat, y)


def _gate_weights(wa, wx):
    _, heads, hd, _ = wa.shape
    hh = heads // 2
    eye = jnp.eye(hh, dtype=wa.dtype)

    def blockdiag(wsel):
        return jnp.einsum('hij,hg->higj', wsel, eye).reshape(hh * hd, hh * hd)

    halves = []
    for s in range(2):
        sl = slice(s * hh, (s + 1) * hh)
        halves.append(jnp.concatenate([blockdiag(wa[0, sl]), blockdiag(wx[0, sl]),
                                       blockdiag(wa[1, sl]), blockdiag(wx[1, sl])], axis=1))
    return jnp.stack(halves).astype(BF16)


def kernel(x, c, ctx, c_ctx, norm1_w, norm2_w, w_ada, b_ada, w_in, hg_lb_logits, hg_norm_w, rg_conv_w, rg_conv_b,
           rg_wa, rg_ba, rg_wx, rg_bx, rg_lambda, w_out, router_w, router_b, w_gate_up, b_gate_up, w_down,
           b_down, final_norm_w):
    b, t, d = x.shape
    tcx = ctx.shape[1]
    n_exp = router_w.shape[-1]
    n_tok = b * t
    depth = w_in.shape[0]
    lb_all = jnp.cumsum(jax.nn.softmax(hg_lb_logits.astype(F32), axis=0), axis=0)

    for l in range(depth):
        assert l == depth - 1, "context stream update of non-final layers is not implemented"
        pad = (-(b + 1)) % 8
        c_all = jnp.concatenate([c, c_ctx[None], jnp.zeros((pad, d), F32)], axis=0)
        mod = _mod(c_all, w_ada[l], b_ada[l])
        sh1, sc1, g1, sh2, sc2, g2 = [m[:b, None, :] for m in jnp.split(mod, 6, axis=-1)]
        csh1, csc1 = [m[b:b + 1, None, :] for m in jnp.split(mod, 6, axis=-1)[:2]]

        w_in_b = w_in[l].astype(BF16)
        nw1 = norm1_w[l].reshape(1, d)
        tm = min(512, t)
        p_lat = _inproj(x, sh1, sc1, nw1, w_in_b, tm)
        p_ctx = _inproj(ctx, csh1, csc1, nw1, w_in_b, min(256, tcx))

        hg = _hgrn2(p_lat, p_ctx, lb_all[l], hg_norm_w[l].reshape(1, -1))
        ch = rg_conv_w.shape[-1]
        wg = _gate_weights(rg_wa[l], rg_wx[l])
        bg = jnp.concatenate([rg_ba[l, 0], rg_bx[l, 0], rg_ba[l, 1], rg_bx[l, 1]]).reshape(1, 4 * ch)
        rg = _rglru(p_lat, p_ctx, rg_conv_w[l], rg_conv_b[l].reshape(1, ch), wg, bg, rg_lambda[l])

        x1, h2, meta, cnt = _mix(hg, rg, x, g1, sh2, sc2, norm2_w[l].reshape(1, d), w_out[l].astype(BF16),
                                 router_w[l].astype(BF16), router_b[l].reshape(1, n_exp), tm)

        bm = 512
        meta2 = meta.reshape(n_tok, LANES)
        idx = meta2[:, 0:TOP_K].astype(jnp.int32)
        rank = meta2[:, 2 * TOP_K:3 * TOP_K].astype(jnp.int32)
        counts = cnt[0, :n_exp].astype(jnp.int32)
        padded = (counts + bm - 1) // bm * bm
        pad_end = jnp.cumsum(padded)
        pad_start = pad_end - padded
        dest = (pad_start[idx] + rank).reshape(-1)
        n_blocks = -(-n_tok * TOP_K // bm) + n_exp
        blk_start = jnp.arange(n_blocks, dtype=jnp.int32) * bm
        blk_expert = jnp.minimum(jnp.sum(blk_start[:, None] >= pad_end[None, :], axis=1), n_exp - 1).astype(jnp.int32)
        n_used = (pad_end[-1:] // bm).astype(jnp.int32)

        tdc = min(256, t)
        n_slots = n_blocks * bm
        tail = jnp.stack([pad_end[-1], (n_slots - pad_end[-1]) // (bm // 2)]).astype(jnp.int32)
        xs = _dispatch(h2.reshape(n_tok, *h2.shape[2:]), dest, pad_start + counts, padded - counts, tail, n_slots,
                       min(512, t), bm)
        y = _experts(xs, blk_expert, n_used, w_gate_up[l].astype(BF16), b_gate_up[l], w_down[l].astype(BF16),
                     b_down[l], bm)
        out = _combine(x1.reshape(n_tok, d), meta2, g2, final_norm_w.reshape(1, d), dest, y, t, tdc)
        return out.reshape(b, t, d)
```

```python
import functools

import jax
import jax.numpy as jnp
from jax import lax
from jax.experimental import pallas as pl
from jax.experimental.pallas import tpu as pltpu

GRID_W = 64
HG_HEADS = 4
HG_CHUNK = 32
RG_HEADS = 8
RG_CONV = 4
RG_C = 8.0
TOP_K = 4
SWIGLU_LIMIT = 7.0
SWIGLU_ALPHA = 1.702
EPS = 1e-6

LANES = 128
SUBLANES = 8
VMEM_LIMIT = 56 * 1024 * 1024

F32 = jnp.float32
BF16 = jnp.bfloat16
HIGHEST = lax.Precision.HIGHEST


def _params(*sem):
    return pltpu.CompilerParams(dimension_semantics=sem, vmem_limit_bytes=VMEM_LIMIT)


def _sigmoid(x):
    return 0.5 * jnp.tanh(0.5 * x) + 0.5


def _silu(x):
    return x * _sigmoid(x)


def _rms(x, w):
    return x * lax.rsqrt(jnp.mean(x * x, axis=-1, keepdims=True) + EPS) * w


def _dot(a, b):
    return jnp.dot(a, b, preferred_element_type=F32)


def _dot_nt(a, b):
    return lax.dot_general(a, b, (((1,), (1,)), ((), ())), preferred_element_type=F32)


def _mod_kernel(c_ref, w_ref, b_ref, o_ref):
    o_ref[...] = jnp.dot(_silu(c_ref[...]), w_ref[...], preferred_element_type=F32,
                         precision=HIGHEST) + b_ref[...]


def _mod(c_all, w_ada, b_ada):
    r, d = c_all.shape
    n = w_ada.shape[1]
    tn = n // 4
    return pl.pallas_call(
        _mod_kernel, grid=(n // tn,),
        in_specs=[pl.BlockSpec((r, d), lambda j: (0, 0)),
                  pl.BlockSpec((d, tn), lambda j: (0, j)),
                  pl.BlockSpec((1, tn), lambda j: (0, j))],
        out_specs=pl.BlockSpec((r, tn), lambda j: (0, j)),
        out_shape=jax.ShapeDtypeStruct((r, n), F32),
        compiler_params=_params("arbitrary"), name="adaln_mod",
    )(c_all, w_ada, b_ada.reshape(1, n))


def _inproj_kernel(x_ref, sh_ref, sc_ref, nw_ref, w_ref, o_ref):
    h = _rms(x_ref[0], nw_ref[...]) * (1.0 + sc_ref[0]) + sh_ref[0]
    o_ref[0] = _dot(h.astype(BF16), w_ref[...])


def _inproj(x, shift, scale, norm_w, w_bf16, tm):
    b, t, d = x.shape
    n = w_bf16.shape[1]
    per_batch = shift.shape[0] == b
    mod_map = (lambda i, j: (i, 0, 0)) if per_batch else (lambda i, j: (0, 0, 0))
    return pl.pallas_call(
        _inproj_kernel, grid=(b, t // tm),
        in_specs=[pl.BlockSpec((1, tm, d), lambda i, j: (i, j, 0)),
                  pl.BlockSpec((1, 1, d), mod_map),
                  pl.BlockSpec((1, 1, d), mod_map),
                  pl.BlockSpec((1, d), lambda i, j: (0, 0)),
                  pl.BlockSpec((d, n), lambda i, j: (0, 0))],
        out_specs=pl.BlockSpec((1, tm, n), lambda i, j: (i, j, 0)),
        out_shape=jax.ShapeDtypeStruct((b, t, n), F32),
        compiler_params=_params("arbitrary", "arbitrary"), name="norm_inproj",
    )(x, shift, scale, norm_w, w_bf16)


def _split_bf16(x):
    hi = x.astype(BF16)
    return hi, (x - hi.astype(F32)).astype(BF16)


def _hg_kernel(q_ref, v_ref, zf_ref, zb_ref, g_ref, cv_ref, czf_ref, czb_ref, lb_ref, nw_ref, o_ref,
               o_s, qd_s, u_s, dec_s, st_s, cu_s, cdec_s, tot_s, *, t_lat, t_ctx):
    c = HG_CHUNK
    dk = q_ref.shape[-1]
    lb = lb_ref[...]
    lbf, lbb = lb[0:1], lb[1:2]

    def prep(rb, zf, zb, q, v, u_out, dec_out, blk, r0):
        nc = rb // c
        ri = lax.broadcasted_iota(jnp.int32, (rb, rb), 0)
        ci = lax.broadcasted_iota(jnp.int32, (rb, rb), 1)
        same = (ri // c) == (ci // c)
        low = same & (ci <= ri)
        upp = same & (ci >= ri)
        ff = lbf + (1.0 - lbf) * _sigmoid(zf)
        fb = lbb + (1.0 - lbb) * _sigmoid(zb)
        lgf, lgb = jnp.log(ff), jnp.log(fb)
        rhs = jnp.concatenate([*_split_bf16(lgf), *_split_bf16(lgb)], axis=1)
        pre = _dot(low.astype(BF16), rhs)
        suf = _dot(upp.astype(BF16), rhs)
        bcf = pre[:, :dk] + pre[:, dk:2 * dk]
        remf = suf[:, :dk] + suf[:, dk:2 * dk] - lgf
        bcb = suf[:, 2 * dk:3 * dk] + suf[:, 3 * dk:]
        remb = pre[:, 2 * dk:3 * dk] + pre[:, 3 * dk:] - lgb
        kkf, kkb = 1.0 - ff, 1.0 - fb
        kef = (kkf * jnp.exp(remf)).astype(BF16)
        keb = (kkb * jnp.exp(remb)).astype(BF16)
        chunk_of_row = lax.broadcasted_iota(jnp.int32, (rb, dk), 0) // c
        zero = jnp.zeros((rb, dk), BF16)
        keys = jnp.concatenate([jnp.where(chunk_of_row == j, ke, zero) for ke in (kef, keb) for j in range(nc)],
                               axis=1)
        u_all = _dot(v.T.astype(BF16), keys)
        c0 = blk * nc
        for d in range(2):
            for j in range(nc):
                u_out[d, c0 + j] = u_all[:, (d * nc + j) * dk:(d * nc + j + 1) * dk]
        half = tot_s.shape[0] // 2
        tot_s[0:rb, :] = bcf + remf
        tot_s[half:half + rb, :] = bcb + remb
        dec_out[0, pl.ds(pl.multiple_of(c0, nc), nc), :] = jnp.exp(tot_s[pl.ds(0, nc, stride=c), :])
        dec_out[1, pl.ds(pl.multiple_of(c0, nc), nc), :] = jnp.exp(tot_s[pl.ds(half, nc, stride=c), :])
        if q is None:
            return
        sq = _silu(q)
        qdf = (sq * jnp.exp(bcf)).astype(BF16)
        qdb = (sq * jnp.exp(bcb)).astype(BF16)
        kdf = (kkf * jnp.exp(-bcf)).astype(BF16)
        kdb = (kkb * jnp.exp(-bcb)).astype(BF16)
        p = jnp.where(low, _dot_nt(qdf, kdf), 0.0) + jnp.where(upp, _dot_nt(qdb, kdb), 0.0)
        o_s[pl.ds(r0, rb), :] = _dot(p.astype(BF16), v.astype(BF16))
        qd_s[pl.ds(r0, rb), 0:dk] = qdf
        qd_s[pl.ds(r0, rb), dk:2 * dk] = qdb

    def scan(n, ur, decr, keep, carry):
        def body(i, carry):
            sf, sb = carry
            j = n - 1 - i
            if keep:
                st_s[i, :, 0:dk] = sf.astype(BF16)
                st_s[j, :, dk:2 * dk] = sb.astype(BF16)
            return sf * decr[0, pl.ds(i, 1), :] + ur[0, i], sb * decr[1, pl.ds(j, 1), :] + ur[1, j]

        return lax.fori_loop(0, n, body, carry, unroll=2)

    rbc = min(256, t_ctx)
    for blk in range(t_ctx // rbc):
        r0 = blk * rbc
        prep(rbc, czf_ref[0, r0:r0 + rbc, :], czb_ref[0, r0:r0 + rbc, :], None, cv_ref[0, r0:r0 + rbc, :],
             cu_s, cdec_s, blk, r0)
    zero = jnp.zeros((dk, dk), F32)
    carry = scan(t_ctx // c, cu_s, cdec_s, False, (zero, zero))

    rbl = min(256, t_lat)

    def lat_prep(blk, _):
        r0 = pl.multiple_of(blk * rbl, rbl)
        rows = pl.ds(r0, rbl)
        prep(rbl, zf_ref[0, rows, :], zb_ref[0, rows, :], q_ref[0, rows, :], v_ref[0, rows, :], u_s, dec_s, blk, r0)
        return 0

    lax.fori_loop(0, t_lat // rbl, lat_prep, 0)
    scan(t_lat // c, u_s, dec_s, True, carry)

    def finish(blk, _):
        r0 = pl.multiple_of(blk * rbl, rbl)
        inter = [_dot_nt(qd_s[pl.ds(r0 + j * c, c), :], st_s[blk * (rbl // c) + j]) for j in range(rbl // c)]
        o = o_s[pl.ds(r0, rbl), :] + jnp.concatenate(inter, axis=0)
        o_ref[0, pl.ds(r0, rbl), :] = _rms(o, nw_ref[...]) * _silu(g_ref[0, pl.ds(r0, rbl), :])
        return 0

    lax.fori_loop(0, t_lat // rbl, finish, 0, unroll=2)


def _hgrn2(p_lat, p_ctx, lb, norm_w):
    b, t, _ = p_lat.shape
    tc = p_ctx.shape[1]
    hw = lb.shape[1]
    dk = hw // HG_HEADS
    nh = HG_HEADS
    c = HG_CHUNK

    def col(k, tt):
        return pl.BlockSpec((1, tt, dk), lambda i, h, k=k: (i, 0, k * nh + h))

    return pl.pallas_call(
        functools.partial(_hg_kernel, t_lat=t, t_ctx=tc), grid=(b, nh),
        in_specs=[col(0, t), col(1, t), col(2, t), col(3, t), col(4, t),
                  col(1, tc), col(2, tc), col(3, tc),
                  pl.BlockSpec((2, dk), lambda i, h: (0, h)),
                  pl.BlockSpec((1, dk), lambda i, h: (0, h))],
        out_specs=pl.BlockSpec((1, t, dk), lambda i, h: (i, 0, h)),
        out_shape=jax.ShapeDtypeStruct((b, t, hw), F32),
        scratch_shapes=[pltpu.VMEM((t, dk), F32),
                        pltpu.VMEM((t, 2 * dk), BF16),
                        pltpu.VMEM((2, t // c, dk, dk), F32),
                        pltpu.VMEM((2, t // c, dk), F32),
                        pltpu.VMEM((t // c, dk, 2 * dk), BF16),
                        pltpu.VMEM((2, tc // c, dk, dk), F32),
                        pltpu.VMEM((2, tc // c, dk), F32),
                        pltpu.VMEM((2 * min(256, max(t, tc)), dk), F32)],
        compiler_params=_params("arbitrary", "arbitrary"), name="hgrn2",
    )(p_lat, p_lat, p_lat, p_lat, p_lat, p_ctx, p_ctx, p_ctx, lb, norm_w)


def _shift_rows(x, k):
    n = x.shape[0]
    y = pltpu.roll(x, k % n, 0)
    r = lax.broadcasted_iota(jnp.int32, x.shape, 0)
    return jnp.where((r >= k) & (r < n + k), y, 0.0)


def _rg_kernel(rx_ref, rgate_ref, crx_ref, cw_ref, cb_ref, wg_ref, bg_ref, lam_ref, o_ref,
               xc_s, af_s, bf_s, ab_s, bb_s, hf_s, hb_s, caf_s, cbf_s, cab_s, cbb_s, *, t_lat, t_ctx):
    w = GRID_W
    rows = t_lat // w
    ch = rx_ref.shape[-1]
    half = ch // 2
    cw = cw_ref[...]
    cb = cb_ref[...]
    bg = bg_ref[...]
    nl = -lam_ref[...]
    cdec = -RG_C * (jnp.maximum(nl, 0.0) + jnp.log1p(jnp.exp(-jnp.abs(nl))))

    def conv(xm2, xm1, x0, xp1):
        return cb + cw[0:1] * xm2 + cw[1:2] * xm1 + cw[2:3] * x0 + cw[3:4] * xp1

    def gates(xc):
        xb = xc.astype(BF16)
        g0 = _dot(xb[:, :half], wg_ref[0])
        g1 = _dot(xb[:, half:], wg_ref[1])
        outs = []
        for d in range(2):
            pre = []
            for s in (2 * d, 2 * d + 1):
                pre.append(jnp.concatenate([g0[:, s * half:(s + 1) * half], g1[:, s * half:(s + 1) * half]],
                                           axis=1) + bg[:, s * ch:(s + 1) * ch])
            log_a = cdec[d:d + 1] * _sigmoid(pre[0])
            a = jnp.exp(log_a)
            mult = jnp.sqrt(-jnp.tanh(log_a) * (a * a + 1.0))
            outs += [a, mult * _sigmoid(pre[1]) * xc]
        return outs

    xctx = crx_ref[0]
    xcc = conv(_shift_rows(xctx, 2), _shift_rows(xctx, 1), xctx, _shift_rows(xctx, -1))
    caf_s[...], cbf_s[...], cab_s[...], cbb_s[...] = gates(xcc)

    def cstep(i, carry):
        hf, hb = carry
        hf = caf_s[pl.ds(i, 1), :] * hf + cbf_s[pl.ds(i, 1), :]
        j = t_ctx - 1 - i
        hb = cab_s[pl.ds(j, 1), :] * hb + cbb_s[pl.ds(j, 1), :]
        return hf, hb

    zrow = jnp.zeros((1, ch), F32)
    hf0, hb0 = lax.fori_loop(0, t_ctx, cstep, (zrow, zrow), unroll=8)

    def slab(rr):
        if 0 <= rr < rows:
            return rx_ref[0, rr * w:(rr + 1) * w, :]
        if rr < 0:
            return _shift_rows(rx_ref[0, (rr + rows) * w:(rr + rows + 1) * w, :], 1)
        return _shift_rows(rx_ref[0, (rr - rows) * w:(rr - rows + 1) * w, :], -1)

    for r in range(rows):
        xc_s[r * w:(r + 1) * w, :] = conv(slab(r - 2), slab(r - 1), slab(r), slab(r + 1))

    mb = min(256, t_lat)

    def gbody(i, _):
        r0 = pl.multiple_of(i * mb, mb)
        a_f, b_f, a_b, b_b = gates(xc_s[pl.ds(r0, mb), :])
        af_s[pl.ds(r0, mb), :] = a_f
        bf_s[pl.ds(r0, mb), :] = b_f
        ab_s[pl.ds(r0, mb), :] = a_b
        bb_s[pl.ds(r0, mb), :] = b_b
        return 0

    lax.fori_loop(0, t_lat // mb, gbody, 0)

    def l1(i, _):
        pf = pl.multiple_of(i * w, w)
        qf = pl.multiple_of((i - 1) * w, w)
        a = af_s[pl.ds(pf, w), :]
        af_s[pl.ds(pf, w), :] = a * af_s[pl.ds(qf, w), :]
        bf_s[pl.ds(pf, w), :] = a * bf_s[pl.ds(qf, w), :] + bf_s[pl.ds(pf, w), :]
        pb = pl.multiple_of((rows - 1 - i) * w, w)
        qb = pl.multiple_of((rows - i) * w, w)
        a = ab_s[pl.ds(pb, w), :]
        ab_s[pl.ds(pb, w), :] = a * ab_s[pl.ds(qb, w), :]
        bb_s[pl.ds(pb, w), :] = a * bb_s[pl.ds(qb, w), :] + bb_s[pl.ds(pb, w), :]
        return 0

    lax.fori_loop(1, rows, l1, 0)

    last = (rows - 1) * w

    def l2(i, carry):
        hf, hb = carry
        hf_s[pl.ds(i, 1), :] = hf
        hf = af_s[pl.ds(last + i, 1), :] * hf + bf_s[pl.ds(last + i, 1), :]
        j = w - 1 - i
        hb_s[pl.ds(j, 1), :] = hb
        hb = ab_s[pl.ds(j, 1), :] * hb + bb_s[pl.ds(j, 1), :]
        return hf, hb

    lax.fori_loop(0, w, l2, (hf0, hb0))

    def l3(i, _):
        p = pl.multiple_of(i * w, w)
        h = (af_s[pl.ds(p, w), :] * hf_s[...] + bf_s[pl.ds(p, w), :]
             + ab_s[pl.ds(p, w), :] * hb_s[...] + bb_s[pl.ds(p, w), :])
        o_ref[0, pl.ds(p, w), :] = jax.nn.gelu(rgate_ref[0, pl.ds(p, w), :]) * h
        return 0

    lax.fori_loop(0, rows, l3, 0)


def _rglru(p_lat, p_ctx, conv_w, conv_b, wg, bg, lam):
    b, t, _ = p_lat.shape
    tc = p_ctx.shape[1]
    ch = conv_w.shape[1]
    rx_blk = (p_lat.shape[2] - 2 * ch) // ch
    full = lambda shape: pl.BlockSpec(shape, lambda i: (0,) * len(shape))
    big = lambda: pltpu.VMEM((t, ch), F32)
    small = lambda: pltpu.VMEM((tc, ch), F32)
    return pl.pallas_call(
        functools.partial(_rg_kernel, t_lat=t, t_ctx=tc), grid=(b,),
        in_specs=[pl.BlockSpec((1, t, ch), lambda i: (i, 0, rx_blk)),
                  pl.BlockSpec((1, t, ch), lambda i: (i, 0, rx_blk + 1)),
                  pl.BlockSpec((1, tc, ch), lambda i: (i, 0, rx_blk)),
                  full(conv_w.shape), full(conv_b.shape), full(wg.shape), full(bg.shape), full(lam.shape)],
        out_specs=pl.BlockSpec((1, t, ch), lambda i: (i, 0, 0)),
        out_shape=jax.ShapeDtypeStruct((b, t, ch), F32),
        scratch_shapes=[big(), big(), big(), big(), big(),
                        pltpu.VMEM((GRID_W, ch), F32), pltpu.VMEM((GRID_W, ch), F32),
                        small(), small(), small(), small()],
        compiler_params=_params("arbitrary"), name="rglru",
    )(p_lat, p_lat, p_ctx, conv_w, conv_b, wg, bg, lam)


def _mix_kernel(hg_ref, rg_ref, x_ref, g1_ref, sh_ref, sc_ref, nw_ref, wo_ref, rw_ref, rb_ref,
                x1_ref, h2_ref, meta_ref, cnt_ref, base_s, *, n_exp):
    tm = x_ref.shape[1]

    @pl.when((pl.program_id(0) == 0) & (pl.program_id(1) == 0))
    def _():
        base_s[...] = jnp.zeros_like(base_s)

    hcat = jnp.concatenate([hg_ref[0], rg_ref[0]], axis=1).astype(BF16)
    x1 = x_ref[0] + g1_ref[0] * _dot(hcat, wo_ref[...])
    x1_ref[0] = x1
    h2 = _rms(x1, nw_ref[...]) * (1.0 + sc_ref[0]) + sh_ref[0]
    h2_ref[0] = h2
    logits = _dot(h2.astype(BF16), rw_ref[...]) + rb_ref[...]

    lane_e = lax.broadcasted_iota(jnp.int32, (tm, n_exp), 1)
    vals, idxs = [], []
    cur = logits
    for _ in range(TOP_K):
        m = jnp.max(cur, axis=1, keepdims=True)
        ix = jnp.min(jnp.where(cur == m, lane_e, n_exp), axis=1, keepdims=True)
        vals.append(m)
        idxs.append(ix)
        cur = jnp.where(lane_e == ix, -jnp.inf, cur)
    ex = [jnp.exp(v - vals[0]) for v in vals]
    den = ex[0] + ex[1] + ex[2] + ex[3]

    lane = lax.broadcasted_iota(jnp.int32, (tm, LANES), 1)
    onehot = jnp.zeros((tm, LANES), F32)
    for k in range(TOP_K):
        onehot = jnp.where(lane == idxs[k] + k * n_exp, 1.0, onehot)
    ri = lax.broadcasted_iota(jnp.int32, (tm, tm), 0)
    ci = lax.broadcasted_iota(jnp.int32, (tm, tm), 1)
    prefix = _dot((ci < ri).astype(BF16), onehot.astype(BF16))
    tot = jnp.broadcast_to(prefix[tm - 1:tm] + onehot[tm - 1:tm], (8, LANES))
    lane8 = lax.broadcasted_iota(jnp.int32, (8, LANES), 1)
    off = base_s[...]
    tot_all = tot
    for j in range(1, TOP_K):
        rolled = pltpu.roll(tot, j * n_exp, 1)
        off = off + jnp.where(lane8 >= j * n_exp, rolled, 0.0)
        tot_all = tot_all + rolled
    pos = onehot * (prefix + off[0:1])
    meta = jnp.zeros((tm, LANES), F32)
    for k in range(TOP_K):
        in_k = (lane >= k * n_exp) & (lane < (k + 1) * n_exp)
        rank = jnp.sum(jnp.where(in_k, pos, 0.0), axis=1, keepdims=True)
        meta = jnp.where(lane == k, idxs[k].astype(F32), meta)
        meta = jnp.where(lane == TOP_K + k, ex[k] / den, meta)
        meta = jnp.where(lane == 2 * TOP_K + k, rank, meta)
    meta_ref[0] = meta
    base_s[...] = base_s[...] + tot_all
    cnt_ref[...] = base_s[...]


def _mix(hg, rg, x, g1, sh2, sc2, norm_w, wo_bf16, router_w, router_b, tm):
    b, t, d = x.shape
    hw = hg.shape[2]
    n_exp = router_w.shape[1]
    assert TOP_K * n_exp == LANES
    tok = lambda last: pl.BlockSpec((1, tm, last), lambda i, j: (i, j, 0))
    per_b = pl.BlockSpec((1, 1, d), lambda i, j: (i, 0, 0))
    full = lambda shape: pl.BlockSpec(shape, lambda i, j: (0,) * len(shape))
    return pl.pallas_call(
        functools.partial(_mix_kernel, n_exp=n_exp), grid=(b, t // tm),
        in_specs=[tok(hw), tok(hw), tok(d), per_b, per_b, per_b, full((1, d)),
                  full(wo_bf16.shape), full(router_w.shape), full((1, n_exp))],
        out_specs=[tok(d), tok(d), tok(LANES), pl.BlockSpec((8, LANES), lambda i, j: (0, 0))],
        out_shape=[jax.ShapeDtypeStruct((b, t, d), F32), jax.ShapeDtypeStruct((b, t, d), F32),
                   jax.ShapeDtypeStruct((b, t, LANES), F32), jax.ShapeDtypeStruct((8, LANES), F32)],
        scratch_shapes=[pltpu.VMEM((8, LANES), F32)],
        compiler_params=_params("arbitrary", "arbitrary"), name="outproj_router",
    )(hg, rg, x, g1, sh2, sc2, norm_w, wo_bf16, router_w, router_b)


def _row_copy(src, s, dst, d, sem):
    return pltpu.make_async_copy(src.at[pl.ds(s, 1), :], dst.at[pl.ds(d, 1), :], sem)


def _dispatch_kernel(fill_off, fill_n, tail, h_ref, dest_hbm, xs_out, idx_s, zero_s, sem_i, sem_d, sem_z, *, pad_bits):
    td = h_ref.shape[0]
    n_idx = td * TOP_K
    n_exp = fill_n.shape[0]
    zrows = zero_s.shape[0]
    i = pl.program_id(0)
    cp = pltpu.make_async_copy(dest_hbm.at[pl.ds(i * n_idx, n_idx)], idx_s, sem_i)
    cp.start()

    def fill(wait):
        def go(copy, cond):
            @pl.when(cond)
            def _():
                copy.wait() if wait else copy.start()

        def per_expert(e, _):
            off = fill_off[e]
            npad = fill_n[e]
            n_single = npad & (SUBLANES - 1)
            for r in range(SUBLANES - 1):
                go(pltpu.make_async_copy(zero_s.at[pl.ds(0, 1), :], xs_out.at[pl.ds(off + r, 1), :], sem_z),
                   r < n_single)
            off = pl.multiple_of(off + n_single, SUBLANES)
            for bit in reversed(range(SUBLANES.bit_length() - 1, pad_bits)):
                size = 1 << bit
                go(pltpu.make_async_copy(zero_s.at[pl.ds(0, size), :], xs_out.at[pl.ds(off, size), :], sem_z),
                   (npad & size) != 0)
                off = pl.multiple_of(off + (npad & size), SUBLANES)
            return 0

        lax.fori_loop(0, n_exp, per_expert, 0)

        def per_tail_chunk(j, _):
            off = pl.multiple_of(tail[0] + j * zrows, zrows)
            copy = pltpu.make_async_copy(zero_s, xs_out.at[pl.ds(off, zrows), :], sem_z)
            copy.wait() if wait else copy.start()
            return 0

        lax.fori_loop(0, tail[1], per_tail_chunk, 0)

    @pl.when(i == 0)
    def _():
        zero_s[...] = jnp.zeros_like(zero_s)
        fill(False)

    cp.wait()

    def issue(t, _):
        for k in range(TOP_K):
            _row_copy(h_ref, t, xs_out, idx_s[t * TOP_K + k], sem_d).start(priority=k % 2)
        return 0

    lax.fori_loop(0, td, issue, 0, unroll=8)
    for k in range(TOP_K):
        pltpu.make_async_copy(h_ref, xs_out.at[pl.ds(0, td), :], sem_d).wait()

    @pl.when(i == 0)
    def _():
        fill(True)


def _dispatch(h2, dest_flat, fill_off, fill_n, tail, n_slots, td, bm):
    n, d = h2.shape
    pad_bits = (bm - 1).bit_length()
    grid_spec = pltpu.PrefetchScalarGridSpec(
        num_scalar_prefetch=3, grid=(n // td,),
        in_specs=[pl.BlockSpec((td, d), lambda i, fo, fn, tl: (i, 0)),
                  pl.BlockSpec(memory_space=pl.ANY)],
        out_specs=pl.BlockSpec(memory_space=pl.ANY),
        scratch_shapes=[pltpu.SMEM((td * TOP_K,), jnp.int32), pltpu.VMEM((bm // 2, d), F32),
                        pltpu.SemaphoreType.DMA, pltpu.SemaphoreType.DMA, pltpu.SemaphoreType.DMA])
    return pl.pallas_call(
        functools.partial(_dispatch_kernel, pad_bits=pad_bits), grid_spec=grid_spec,
        out_shape=jax.ShapeDtypeStruct((n_slots, d), F32),
        compiler_params=_params("arbitrary"), name="moe_dispatch",
    )(fill_off, fill_n, tail, h2, dest_flat)


def _expert_kernel(be_ref, nu_ref, x_ref, wgu_ref, bgu_ref, wd_ref, bd_ref, y_ref, wgu_s, wd_s):
    i = pl.program_id(0)
    d_ff = wd_ref.shape[1]

    @pl.when(i >= nu_ref[0])
    def _():
        y_ref[...] = jnp.zeros_like(y_ref)

    @pl.when(i < nu_ref[0])
    def _():
        @pl.when((i == 0) | (be_ref[i] != be_ref[jnp.maximum(i - 1, 0)]))
        def _():
            wgu_s[...] = wgu_ref[0].astype(BF16)
            wd_s[...] = wd_ref[0].astype(BF16)

        gu = _dot(x_ref[...].astype(BF16), wgu_s[...]) + bgu_ref[0]
        gate = jnp.minimum(gu[:, :d_ff], SWIGLU_LIMIT)
        up = jnp.clip(gu[:, d_ff:], -SWIGLU_LIMIT, SWIGLU_LIMIT)
        act = gate * _sigmoid(SWIGLU_ALPHA * gate) * (up + 1.0)
        y_ref[...] = _dot(act.astype(BF16), wd_s[...]) + bd_ref[0]


def _experts(xs, blk_expert, n_used, wgu, bgu, wd, bd, bm):
    n_slots, d = xs.shape
    n_exp, _, f2 = wgu.shape
    d_ff = wd.shape[1]
    n_blocks = n_slots // bm
    row = lambda i, be, nu: (jnp.minimum(i, nu[0] - 1), 0)
    grid_spec = pltpu.PrefetchScalarGridSpec(
        num_scalar_prefetch=2, grid=(n_blocks,),
        in_specs=[pl.BlockSpec((bm, d), row),
                  pl.BlockSpec((1, d, f2), lambda i, be, nu: (be[i], 0, 0)),
                  pl.BlockSpec((1, 1, f2), lambda i, be, nu: (be[i], 0, 0)),
                  pl.BlockSpec((1, d_ff, d), lambda i, be, nu: (be[i], 0, 0)),
                  pl.BlockSpec((1, 1, d), lambda i, be, nu: (be[i], 0, 0))],
        out_specs=pl.BlockSpec((bm, d), lambda i, be, nu: (i, 0)),
        scratch_shapes=[pltpu.VMEM((d, f2), BF16), pltpu.VMEM((d_ff, d), BF16)])
    return pl.pallas_call(
        _expert_kernel, grid_spec=grid_spec,
        out_shape=jax.ShapeDtypeStruct((n_slots, d), F32),
        compiler_params=_params("arbitrary"), name="moe_experts",
    )(blk_expert, n_used, xs, wgu, bgu.reshape(n_exp, 1, f2), wd, bd.reshape(n_exp, 1, d))


def _combine_kernel(x1_ref, meta_ref, g2_ref, fw_ref, dest_hbm, y_hbm, o_ref, rows_s, idx_s, sem_i, sem_d):
    tc = x1_ref.shape[0]
    n_idx = tc * TOP_K
    i = pl.program_id(0)
    n = pl.num_programs(0)
    slot = i % 2

    def gather(j, sl):
        cp = pltpu.make_async_copy(dest_hbm.at[pl.ds(j * n_idx, n_idx)], idx_s, sem_i)
        cp.start()
        cp.wait()

        def issue(t, _):
            for k in range(TOP_K):
                _row_copy(y_hbm, idx_s[t * TOP_K + k], rows_s.at[sl, k], t, sem_d.at[sl]).start(priority=k % 2)
            return 0

        lax.fori_loop(0, tc, issue, 0, unroll=8)

    @pl.when(i == 0)
    def _():
        gather(0, 0)

    @pl.when(i + 1 < n)
    def _():
        gather(i + 1, 1 - slot)

    for k in range(TOP_K):
        pltpu.make_async_copy(y_hbm.at[pl.ds(0, tc), :], rows_s.at[slot, k], sem_d.at[slot]).wait()

    meta = meta_ref[...]
    moe = meta[:, TOP_K:TOP_K + 1] * rows_s[slot, 0]
    for k in range(1, TOP_K):
        moe = moe + meta[:, TOP_K + k:TOP_K + k + 1] * rows_s[slot, k]
    o_ref[...] = _rms(x1_ref[...] + g2_ref[0] * moe, fw_ref[...])


def _combine(x1, meta, g2, final_w, dest_flat, y, t_seq, tc):
    n, d = x1.shape
    return pl.pallas_call(
        _combine_kernel, grid=(n // tc,),
        in_specs=[pl.BlockSpec((tc, d), lambda i: (i, 0)),
                  pl.BlockSpec((tc, LANES), lambda i: (i, 0)),
                  pl.BlockSpec((1, 1, d), lambda i: (i * tc // t_seq, 0, 0)),
                  pl.BlockSpec((1, d), lambda i: (0, 0)),
                  pl.BlockSpec(memory_space=pl.ANY),
                  pl.BlockSpec(memory_space=pl.ANY)],
        out_specs=pl.BlockSpec((tc, d), lambda i: (i, 0)),
        out_shape=jax.ShapeDtypeStruct((n, d), F32),
        scratch_shapes=[pltpu.VMEM((2, TOP_K, tc, d), F32), pltpu.SMEM((tc * TOP_K,), jnp.int32),
                        pltpu.SemaphoreType.DMA, pltpu.SemaphoreType.DMA((2,))],
        compiler_params=_params("arbitrary"), name="moe_combine",
    )(x1, meta, g2, final_w, dest_flat, y)


def _gate_weights(wa, wx):
    _, heads, hd, _ = wa.shape
    hh = heads // 2
    eye = jnp.eye(hh, dtype=wa.dtype)

    def blockdiag(wsel):
        return jnp.einsum('hij,hg->higj', wsel, eye).reshape(hh * hd, hh * hd)

    halves = []
    for s in range(2):
        sl = slice(s * hh, (s + 1) * hh)
        halves.append(jnp.concatenate([blockdiag(wa[0, sl]), blockdiag(wx[0, sl]),
                                       blockdiag(wa[1, sl]), blockdiag(wx[1, sl])], axis=1))
    return jnp.stack(halves).astype(BF16)


def kernel(x, c, ctx, c_ctx, norm1_w, norm2_w, w_ada, b_ada, w_in, hg_lb_logits, hg_norm_w, rg_conv_w, rg_conv_b,
           rg_wa, rg_ba, rg_wx, rg_bx, rg_lambda, w_out, router_w, router_b, w_gate_up, b_gate_up, w_down,
           b_down, final_norm_w):
    b, t, d = x.shape
    tcx = ctx.shape[1]
    n_exp = router_w.shape[-1]
    n_tok = b * t
    depth = w_in.shape[0]
    lb_all = jnp.cumsum(jax.nn.softmax(hg_lb_logits.astype(F32), axis=0), axis=0)

    for l in range(depth):
        assert l == depth - 1, "context stream update of non-final layers is not implemented"
        pad = (-(b + 1)) % 8
        c_all = jnp.concatenate([c, c_ctx[None], jnp.zeros((pad, d), F32)], axis=0)
        mod = _mod(c_all, w_ada[l], b_ada[l])
        sh1, sc1, g1, sh2, sc2, g2 = [m[:b, None, :] for m in jnp.split(mod, 6, axis=-1)]
        csh1, csc1 = [m[b:b + 1, None, :] for m in jnp.split(mod, 6, axis=-1)[:2]]

        w_in_b = w_in[l].astype(BF16)
        nw1 = norm1_w[l].reshape(1, d)
        tm = min(512, t)
        p_lat = _inproj(x, sh1, sc1, nw1, w_in_b, tm)
        p_ctx = _inproj(ctx, csh1, csc1, nw1, w_in_b, min(256, tcx))

        hg = _hgrn2(p_lat, p_ctx, lb_all[l], hg_norm_w[l].reshape(1, -1))
        ch = rg_conv_w.shape[-1]
        wg = _gate_weights(rg_wa[l], rg_wx[l])
        bg = jnp.concatenate([rg_ba[l, 0], rg_bx[l, 0], rg_ba[l, 1], rg_bx[l, 1]]).reshape(1, 4 * ch)
        rg = _rglru(p_lat, p_ctx, rg_conv_w[l], rg_conv_b[l].reshape(1, ch), wg, bg, rg_lambda[l])

        x1, h2, meta, cnt = _mix(hg, rg, x, g1, sh2, sc2, norm2_w[l].reshape(1, d), w_out[l].astype(BF16),
                                 router_w[l].astype(BF16), router_b[l].reshape(1, n_exp), tm)

        bm = 512
        meta2 = meta.reshape(n_tok, LANES)
        idx = meta2[:, 0:TOP_K].astype(jnp.int32)
        rank = meta2[:, 2 * TOP_K:3 * TOP_K].astype(jnp.int32)
        counts = cnt[0, :n_exp].astype(jnp.int32)
        padded = (counts + bm - 1) // bm * bm
        pad_end = jnp.cumsum(padded)
        pad_start = pad_end - padded
        dest = (pad_start[idx] + rank).reshape(-1)
        n_blocks = -(-n_tok * TOP_K // bm) + n_exp
        blk_start = jnp.arange(n_blocks, dtype=jnp.int32) * bm
        blk_expert = jnp.minimum(jnp.sum(blk_start[:, None] >= pad_end[None, :], axis=1), n_exp - 1).astype(jnp.int32)
        n_used = (pad_end[-1:] // bm).astype(jnp.int32)

        tdc = min(256, t)
        n_slots = n_blocks * bm
        tail = jnp.stack([pad_end[-1], (n_slots - pad_end[-1]) // (bm // 2)]).astype(jnp.int32)
        xs = _dispatch(h2.reshape(n_tok, d), dest, pad_start + counts, padded - counts, tail, n_slots,
                       min(512, t), bm)
        y = _experts(xs, blk_expert, n_used, w_gate_up[l], b_gate_up[l], w_down[l], b_down[l], bm)
        out = _combine(x1.reshape(n_tok, d), meta2, g2, final_norm_w.reshape(1, d), dest, y, t, tdc)
        return out.reshape(b, t, d)
```

```python
import functools

import jax
import jax.numpy as jnp
from jax import lax
from jax.experimental import pallas as pl
from jax.experimental.pallas import tpu as pltpu

GRID_W = 64
HG_HEADS = 4
HG_CHUNK = 32
RG_HEADS = 8
RG_CONV = 4
RG_C = 8.0
TOP_K = 4
SWIGLU_LIMIT = 7.0
SWIGLU_ALPHA = 1.702
EPS = 1e-6

LANES = 128
SUBLANES = 8
VMEM_LIMIT = 56 * 1024 * 1024

F32 = jnp.float32
BF16 = jnp.bfloat16
HIGHEST = lax.Precision.HIGHEST


def _params(*sem):
    return pltpu.CompilerParams(dimension_semantics=sem, vmem_limit_bytes=VMEM_LIMIT)


def _sigmoid(x):
    return 0.5 * jnp.tanh(0.5 * x) + 0.5


def _silu(x):
    return x * _sigmoid(x)


def _rms(x, w):
    return x * lax.rsqrt(jnp.mean(x * x, axis=-1, keepdims=True) + EPS) * w


def _dot(a, b):
    return jnp.dot(a, b, preferred_element_type=F32)


def _dot_nt(a, b):
    return lax.dot_general(a, b, (((1,), (1,)), ((), ())), preferred_element_type=F32)


def _mod_kernel(c_ref, w_ref, b_ref, o_ref):
    o_ref[...] = jnp.dot(_silu(c_ref[...]), w_ref[...], preferred_element_type=F32,
                         precision=HIGHEST) + b_ref[...]


def _mod(c_all, w_ada, b_ada):
    r, d = c_all.shape
    n = w_ada.shape[1]
    tn = n // 4
    return pl.pallas_call(
        _mod_kernel, grid=(n // tn,),
        in_specs=[pl.BlockSpec((r, d), lambda j: (0, 0)),
                  pl.BlockSpec((d, tn), lambda j: (0, j)),
                  pl.BlockSpec((1, tn), lambda j: (0, j))],
        out_specs=pl.BlockSpec((r, tn), lambda j: (0, j)),
        out_shape=jax.ShapeDtypeStruct((r, n), F32),
        compiler_params=_params("arbitrary"), name="adaln_mod",
    )(c_all, w_ada, b_ada.reshape(1, n))


def _inproj_kernel(x_ref, sh_ref, sc_ref, nw_ref, w_ref, o_ref):
    h = _rms(x_ref[0], nw_ref[...]) * (1.0 + sc_ref[0]) + sh_ref[0]
    o_ref[0] = _dot(h.astype(BF16), w_ref[...])


def _inproj(x, shift, scale, norm_w, w_bf16, tm):
    b, t, d = x.shape
    n = w_bf16.shape[1]
    per_batch = shift.shape[0] == b
    mod_map = (lambda i, j: (i, 0, 0)) if per_batch else (lambda i, j: (0, 0, 0))
    return pl.pallas_call(
        _inproj_kernel, grid=(b, t // tm),
        in_specs=[pl.BlockSpec((1, tm, d), lambda i, j: (i, j, 0)),
                  pl.BlockSpec((1, 1, d), mod_map),
                  pl.BlockSpec((1, 1, d), mod_map),
                  pl.BlockSpec((1, d), lambda i, j: (0, 0)),
                  pl.BlockSpec((d, n), lambda i, j: (0, 0))],
        out_specs=pl.BlockSpec((1, tm, n), lambda i, j: (i, j, 0)),
        out_shape=jax.ShapeDtypeStruct((b, t, n), F32),
        compiler_params=_params("arbitrary", "arbitrary"), name="norm_inproj",
    )(x, shift, scale, norm_w, w_bf16)


def _split_bf16(x):
    hi = x.astype(BF16)
    return hi, (x - hi.astype(F32)).astype(BF16)


def _hg_kernel(q_ref, v_ref, zf_ref, zb_ref, g_ref, cv_ref, czf_ref, czb_ref, lb_ref, nw_ref, o_ref,
               o_s, qd_s, u_s, dec_s, st_s, cu_s, cdec_s, tot_s, *, t_lat, t_ctx):
    c = HG_CHUNK
    dk = q_ref.shape[-1]
    lb = lb_ref[...]
    lbf, lbb = lb[0:1], lb[1:2]

    def prep(rb, zf, zb, q, v, u_out, dec_out, blk, r0):
        nc = rb // c
        ri = lax.broadcasted_iota(jnp.int32, (rb, rb), 0)
        ci = lax.broadcasted_iota(jnp.int32, (rb, rb), 1)
        same = (ri // c) == (ci // c)
        low = same & (ci <= ri)
        upp = same & (ci >= ri)
        ff = lbf + (1.0 - lbf) * _sigmoid(zf)
        fb = lbb + (1.0 - lbb) * _sigmoid(zb)
        lgf, lgb = jnp.log(ff), jnp.log(fb)
        rhs = jnp.concatenate([*_split_bf16(lgf), *_split_bf16(lgb)], axis=1)
        pre = _dot(low.astype(BF16), rhs)
        suf = _dot(upp.astype(BF16), rhs)
        bcf = pre[:, :dk] + pre[:, dk:2 * dk]
        remf = suf[:, :dk] + suf[:, dk:2 * dk] - lgf
        bcb = suf[:, 2 * dk:3 * dk] + suf[:, 3 * dk:]
        remb = pre[:, 2 * dk:3 * dk] + pre[:, 3 * dk:] - lgb
        kkf, kkb = 1.0 - ff, 1.0 - fb
        kef = (kkf * jnp.exp(remf)).astype(BF16)
        keb = (kkb * jnp.exp(remb)).astype(BF16)
        chunk_of_row = lax.broadcasted_iota(jnp.int32, (rb, dk), 0) // c
        zero = jnp.zeros((rb, dk), BF16)
        keys = jnp.concatenate([jnp.where(chunk_of_row == j, ke, zero) for ke in (kef, keb) for j in range(nc)],
                               axis=1)
        u_all = _dot(v.T.astype(BF16), keys)
        c0 = blk * nc
        for d in range(2):
            for j in range(nc):
                u_out[d, c0 + j] = u_all[:, (d * nc + j) * dk:(d * nc + j + 1) * dk]
        half = tot_s.shape[0] // 2
        tot_s[0:rb, :] = bcf + remf
        tot_s[half:half + rb, :] = bcb + remb
        dec_out[0, pl.ds(pl.multiple_of(c0, nc), nc), :] = jnp.exp(tot_s[pl.ds(0, nc, stride=c), :])
        dec_out[1, pl.ds(pl.multiple_of(c0, nc), nc), :] = jnp.exp(tot_s[pl.ds(half, nc, stride=c), :])
        if q is None:
            return
        sq = _silu(q)
        qdf = (sq * jnp.exp(bcf)).astype(BF16)
        qdb = (sq * jnp.exp(bcb)).astype(BF16)
        kdf = (kkf * jnp.exp(-bcf)).astype(BF16)
        kdb = (kkb * jnp.exp(-bcb)).astype(BF16)
        p = jnp.where(low, _dot_nt(qdf, kdf), 0.0) + jnp.where(upp, _dot_nt(qdb, kdb), 0.0)
        o_s[pl.ds(r0, rb), :] = _dot(p.astype(BF16), v.astype(BF16))
        qd_s[pl.ds(r0, rb), 0:dk] = qdf
        qd_s[pl.ds(r0, rb), dk:2 * dk] = qdb

    def scan(n, ur, decr, keep, carry):
        def body(i, carry):
            sf, sb = carry
            j = n - 1 - i
            if keep:
                st_s[i, :, 0:dk] = sf.astype(BF16)
                st_s[j, :, dk:2 * dk] = sb.astype(BF16)
            return sf * decr[0, pl.ds(i, 1), :] + ur[0, i], sb * decr[1, pl.ds(j, 1), :] + ur[1, j]

        return lax.fori_loop(0, n, body, carry, unroll=2)

    rbc = min(256, t_ctx)
    for blk in range(t_ctx // rbc):
        r0 = blk * rbc
        prep(rbc, czf_ref[0, r0:r0 + rbc, :], czb_ref[0, r0:r0 + rbc, :], None, cv_ref[0, r0:r0 + rbc, :],
             cu_s, cdec_s, blk, r0)
    zero = jnp.zeros((dk, dk), F32)
    carry = scan(t_ctx // c, cu_s, cdec_s, False, (zero, zero))

    rbl = min(256, t_lat)

    def lat_prep(blk, _):
        r0 = pl.multiple_of(blk * rbl, rbl)
        rows = pl.ds(r0, rbl)
        prep(rbl, zf_ref[0, rows, :], zb_ref[0, rows, :], q_ref[0, rows, :], v_ref[0, rows, :], u_s, dec_s, blk, r0)
        return 0

    lax.fori_loop(0, t_lat // rbl, lat_prep, 0, unroll=2)
    scan(t_lat // c, u_s, dec_s, True, carry)

    def finish(blk, _):
        r0 = pl.multiple_of(blk * rbl, rbl)
        inter = [_dot_nt(qd_s[pl.ds(r0 + j * c, c), :], st_s[blk * (rbl // c) + j]) for j in range(rbl // c)]
        o = o_s[pl.ds(r0, rbl), :] + jnp.concatenate(inter, axis=0)
        o_ref[0, pl.ds(r0, rbl), :] = _rms(o, nw_ref[...]) * _silu(g_ref[0, pl.ds(r0, rbl), :])
        return 0

    lax.fori_loop(0, t_lat // rbl, finish, 0, unroll=2)


def _hgrn2(p_lat, p_ctx, lb, norm_w):
    b, t, _ = p_lat.shape
    tc = p_ctx.shape[1]
    hw = lb.shape[1]
    dk = hw // HG_HEADS
    nh = HG_HEADS
    c = HG_CHUNK

    def col(k, tt):
        return pl.BlockSpec((1, tt, dk), lambda i, h, k=k: (i, 0, k * nh + h))

    return pl.pallas_call(
        functools.partial(_hg_kernel, t_lat=t, t_ctx=tc), grid=(b, nh),
        in_specs=[col(0, t), col(1, t), col(2, t), col(3, t), col(4, t),
                  col(1, tc), col(2, tc), col(3, tc),
                  pl.BlockSpec((2, dk), lambda i, h: (0, h)),
                  pl.BlockSpec((1, dk), lambda i, h: (0, h))],
        out_specs=pl.BlockSpec((1, t, dk), lambda i, h: (i, 0, h)),
        out_shape=jax.ShapeDtypeStruct((b, t, hw), F32),
        scratch_shapes=[pltpu.VMEM((t, dk), F32),
                        pltpu.VMEM((t, 2 * dk), BF16),
                        pltpu.VMEM((2, t // c, dk, dk), F32),
                        pltpu.VMEM((2, t // c, dk), F32),
                        pltpu.VMEM((t // c, dk, 2 * dk), BF16),
                        pltpu.VMEM((2, tc // c, dk, dk), F32),
                        pltpu.VMEM((2, tc // c, dk), F32),
                        pltpu.VMEM((2 * min(256, max(t, tc)), dk), F32)],
        compiler_params=_params("arbitrary", "arbitrary"), name="hgrn2",
    )(p_lat, p_lat, p_lat, p_lat, p_lat, p_ctx, p_ctx, p_ctx, lb, norm_w)


def _shift_rows(x, k):
    n = x.shape[0]
    y = pltpu.roll(x, k % n, 0)
    r = lax.broadcasted_iota(jnp.int32, x.shape, 0)
    return jnp.where((r >= k) & (r < n + k), y, 0.0)


def _rg_kernel(rx_ref, rgate_ref, crx_ref, cw_ref, cb_ref, wg_ref, bg_ref, lam_ref, o_ref,
               xc_s, af_s, bf_s, ab_s, bb_s, hf_s, hb_s, caf_s, cbf_s, cab_s, cbb_s, *, t_lat, t_ctx):
    w = GRID_W
    rows = t_lat // w
    ch = rx_ref.shape[-1]
    half = ch // 2
    cw = cw_ref[...]
    cb = cb_ref[...]
    bg = bg_ref[...]
    nl = -lam_ref[...]
    cdec = -RG_C * (jnp.maximum(nl, 0.0) + jnp.log1p(jnp.exp(-jnp.abs(nl))))

    def conv(xm2, xm1, x0, xp1):
        return cb + cw[0:1] * xm2 + cw[1:2] * xm1 + cw[2:3] * x0 + cw[3:4] * xp1

    def gates(xc):
        xb = xc.astype(BF16)
        g0 = _dot(xb[:, :half], wg_ref[0])
        g1 = _dot(xb[:, half:], wg_ref[1])
        outs = []
        for d in range(2):
            pre = []
            for s in (2 * d, 2 * d + 1):
                pre.append(jnp.concatenate([g0[:, s * half:(s + 1) * half], g1[:, s * half:(s + 1) * half]],
                                           axis=1) + bg[:, s * ch:(s + 1) * ch])
            log_a = cdec[d:d + 1] * _sigmoid(pre[0])
            a = jnp.exp(log_a)
            mult = jnp.sqrt(-jnp.tanh(log_a) * (a * a + 1.0))
            outs += [a, mult * _sigmoid(pre[1]) * xc]
        return outs

    xctx = crx_ref[0]
    xcc = conv(_shift_rows(xctx, 2), _shift_rows(xctx, 1), xctx, _shift_rows(xctx, -1))
    caf_s[...], cbf_s[...], cab_s[...], cbb_s[...] = gates(xcc)

    def cstep(i, carry):
        hf, hb = carry
        hf = caf_s[pl.ds(i, 1), :] * hf + cbf_s[pl.ds(i, 1), :]
        j = t_ctx - 1 - i
        hb = cab_s[pl.ds(j, 1), :] * hb + cbb_s[pl.ds(j, 1), :]
        return hf, hb

    zrow = jnp.zeros((1, ch), F32)
    hf0, hb0 = lax.fori_loop(0, t_ctx, cstep, (zrow, zrow), unroll=8)

    def slab(rr):
        if 0 <= rr < rows:
            return rx_ref[0, rr * w:(rr + 1) * w, :]
        if rr < 0:
            return _shift_rows(rx_ref[0, (rr + rows) * w:(rr + rows + 1) * w, :], 1)
        return _shift_rows(rx_ref[0, (rr - rows) * w:(rr - rows + 1) * w, :], -1)

    for r in range(rows):
        xc_s[r * w:(r + 1) * w, :] = conv(slab(r - 2), slab(r - 1), slab(r), slab(r + 1))

    mb = min(256, t_lat)

    def gbody(i, _):
        r0 = pl.multiple_of(i * mb, mb)
        a_f, b_f, a_b, b_b = gates(xc_s[pl.ds(r0, mb), :])
        af_s[pl.ds(r0, mb), :] = a_f
        bf_s[pl.ds(r0, mb), :] = b_f
        ab_s[pl.ds(r0, mb), :] = a_b
        bb_s[pl.ds(r0, mb), :] = b_b
        return 0

    lax.fori_loop(0, t_lat // mb, gbody, 0)

    def l1(i, _):
        pf = pl.multiple_of(i * w, w)
        qf = pl.multiple_of((i - 1) * w, w)
        a = af_s[pl.ds(pf, w), :]
        af_s[pl.ds(pf, w), :] = a * af_s[pl.ds(qf, w), :]
        bf_s[pl.ds(pf, w), :] = a * bf_s[pl.ds(qf, w), :] + bf_s[pl.ds(pf, w), :]
        pb = pl.multiple_of((rows - 1 - i) * w, w)
        qb = pl.multiple_of((rows - i) * w, w)
        a = ab_s[pl.ds(pb, w), :]
        ab_s[pl.ds(pb, w), :] = a * ab_s[pl.ds(qb, w), :]
        bb_s[pl.ds(pb, w), :] = a * bb_s[pl.ds(qb, w), :] + bb_s[pl.ds(pb, w), :]
        return 0

    lax.fori_loop(1, rows, l1, 0)

    last = (rows - 1) * w

    def l2(i, carry):
        hf, hb = carry
        hf_s[pl.ds(i, 1), :] = hf
        hf = af_s[pl.ds(last + i, 1), :] * hf + bf_s[pl.ds(last + i, 1), :]
        j = w - 1 - i
        hb_s[pl.ds(j, 1), :] = hb
        hb = ab_s[pl.ds(j, 1), :] * hb + bb_s[pl.ds(j, 1), :]
        return hf, hb

    lax.fori_loop(0, w, l2, (hf0, hb0))

    def l3(i, _):
        p = pl.multiple_of(i * w, w)
        h = (af_s[pl.ds(p, w), :] * hf_s[...] + bf_s[pl.ds(p, w), :]
             + ab_s[pl.ds(p, w), :] * hb_s[...] + bb_s[pl.ds(p, w), :])
        o_ref[0, pl.ds(p, w), :] = jax.nn.gelu(rgate_ref[0, pl.ds(p, w), :]) * h
        return 0

    lax.fori_loop(0, rows, l3, 0)


def _rglru(p_lat, p_ctx, conv_w, conv_b, wg, bg, lam):
    b, t, _ = p_lat.shape
    tc = p_ctx.shape[1]
    ch = conv_w.shape[1]
    rx_blk = (p_lat.shape[2] - 2 * ch) // ch
    full = lambda shape: pl.BlockSpec(shape, lambda i: (0,) * len(shape))
    big = lambda: pltpu.VMEM((t, ch), F32)
    small = lambda: pltpu.VMEM((tc, ch), F32)
    return pl.pallas_call(
        functools.partial(_rg_kernel, t_lat=t, t_ctx=tc), grid=(b,),
        in_specs=[pl.BlockSpec((1, t, ch), lambda i: (i, 0, rx_blk)),
                  pl.BlockSpec((1, t, ch), lambda i: (i, 0, rx_blk + 1)),
                  pl.BlockSpec((1, tc, ch), lambda i: (i, 0, rx_blk)),
                  full(conv_w.shape), full(conv_b.shape), full(wg.shape), full(bg.shape), full(lam.shape)],
        out_specs=pl.BlockSpec((1, t, ch), lambda i: (i, 0, 0)),
        out_shape=jax.ShapeDtypeStruct((b, t, ch), F32),
        scratch_shapes=[big(), big(), big(), big(), big(),
                        pltpu.VMEM((GRID_W, ch), F32), pltpu.VMEM((GRID_W, ch), F32),
                        small(), small(), small(), small()],
        compiler_params=_params("arbitrary"), name="rglru",
    )(p_lat, p_lat, p_ctx, conv_w, conv_b, wg, bg, lam)


def _mix_kernel(hg_ref, rg_ref, x_ref, g1_ref, sh_ref, sc_ref, nw_ref, wo_ref, rw_ref, rb_ref,
                x1_ref, h2_ref, meta_ref, cnt_ref, base_s, *, n_exp):
    tm = x_ref.shape[1]

    @pl.when((pl.program_id(0) == 0) & (pl.program_id(1) == 0))
    def _():
        base_s[...] = jnp.zeros_like(base_s)

    hcat = jnp.concatenate([hg_ref[0], rg_ref[0]], axis=1).astype(BF16)
    x1 = x_ref[0] + g1_ref[0] * _dot(hcat, wo_ref[...])
    x1_ref[0] = x1
    h2 = _rms(x1, nw_ref[...]) * (1.0 + sc_ref[0]) + sh_ref[0]
    h2_ref[0] = h2
    logits = _dot(h2.astype(BF16), rw_ref[...]) + rb_ref[...]

    lane_e = lax.broadcasted_iota(jnp.int32, (tm, n_exp), 1)
    vals, idxs = [], []
    cur = logits
    for _ in range(TOP_K):
        m = jnp.max(cur, axis=1, keepdims=True)
        ix = jnp.min(jnp.where(cur == m, lane_e, n_exp), axis=1, keepdims=True)
        vals.append(m)
        idxs.append(ix)
        cur = jnp.where(lane_e == ix, -jnp.inf, cur)
    ex = [jnp.exp(v - vals[0]) for v in vals]
    den = ex[0] + ex[1] + ex[2] + ex[3]

    lane = lax.broadcasted_iota(jnp.int32, (tm, LANES), 1)
    onehot = jnp.zeros((tm, LANES), F32)
    for k in range(TOP_K):
        onehot = jnp.where(lane == idxs[k] + k * n_exp, 1.0, onehot)
    ri = lax.broadcasted_iota(jnp.int32, (tm, tm), 0)
    ci = lax.broadcasted_iota(jnp.int32, (tm, tm), 1)
    prefix = _dot((ci < ri).astype(BF16), onehot.astype(BF16))
    tot = jnp.broadcast_to(prefix[tm - 1:tm] + onehot[tm - 1:tm], (8, LANES))
    lane8 = lax.broadcasted_iota(jnp.int32, (8, LANES), 1)
    off = base_s[...]
    tot_all = tot
    for j in range(1, TOP_K):
        rolled = pltpu.roll(tot, j * n_exp, 1)
        off = off + jnp.where(lane8 >= j * n_exp, rolled, 0.0)
        tot_all = tot_all + rolled
    pos = onehot * (prefix + off[0:1])
    meta = jnp.zeros((tm, LANES), F32)
    for k in range(TOP_K):
        in_k = (lane >= k * n_exp) & (lane < (k + 1) * n_exp)
        rank = jnp.sum(jnp.where(in_k, pos, 0.0), axis=1, keepdims=True)
        meta = jnp.where(lane == k, idxs[k].astype(F32), meta)
        meta = jnp.where(lane == TOP_K + k, ex[k] / den, meta)
        meta = jnp.where(lane == 2 * TOP_K + k, rank, meta)
    meta_ref[0] = meta
    base_s[...] = base_s[...] + tot_all
    cnt_ref[...] = base_s[...]


def _mix(hg, rg, x, g1, sh2, sc2, norm_w, wo_bf16, router_w, router_b, tm):
    b, t, d = x.shape
    hw = hg.shape[2]
    n_exp = router_w.shape[1]
    assert TOP_K * n_exp == LANES
    tok = lambda last: pl.BlockSpec((1, tm, last), lambda i, j: (i, j, 0))
    per_b = pl.BlockSpec((1, 1, d), lambda i, j: (i, 0, 0))
    full = lambda shape: pl.BlockSpec(shape, lambda i, j: (0,) * len(shape))
    return pl.pallas_call(
        functools.partial(_mix_kernel, n_exp=n_exp), grid=(b, t // tm),
        in_specs=[tok(hw), tok(hw), tok(d), per_b, per_b, per_b, full((1, d)),
                  full(wo_bf16.shape), full(router_w.shape), full((1, n_exp))],
        out_specs=[tok(d), tok(d), tok(LANES), pl.BlockSpec((8, LANES), lambda i, j: (0, 0))],
        out_shape=[jax.ShapeDtypeStruct((b, t, d), F32), jax.ShapeDtypeStruct((b, t, d), F32),
                   jax.ShapeDtypeStruct((b, t, LANES), F32), jax.ShapeDtypeStruct((8, LANES), F32)],
        scratch_shapes=[pltpu.VMEM((8, LANES), F32)],
        compiler_params=_params("arbitrary", "arbitrary"), name="outproj_router",
    )(hg, rg, x, g1, sh2, sc2, norm_w, wo_bf16, router_w, router_b)


def _row_copy(src, s, dst, d, sem):
    return pltpu.make_async_copy(src.at[pl.ds(s, 1), :], dst.at[pl.ds(d, 1), :], sem)


def _dispatch_kernel(fill_off, fill_n, tail, h_ref, dest_hbm, xs_out, idx0_s, idx1_s, zero_s, sem_i, sem_d, sem_z, *,
                     pad_bits):
    td = h_ref.shape[0]
    n_idx = td * TOP_K
    n_exp = fill_n.shape[0]
    zrows = zero_s.shape[0]
    i = pl.program_id(0)
    slot = i % 2
    idx_s = (idx0_s, idx1_s)

    def idx_copy(j, sl):
        return pltpu.make_async_copy(dest_hbm.at[pl.ds(j * n_idx, n_idx)], idx_s[sl], sem_i.at[sl])

    @pl.when(i == 0)
    def _():
        idx_copy(0, 0).start()

    for sl in range(2):
        @pl.when((i + 1 < pl.num_programs(0)) & (slot != sl))
        def _():
            idx_copy(i + 1, sl).start()

    def fill(wait):
        def go(copy, cond):
            @pl.when(cond)
            def _():
                copy.wait() if wait else copy.start()

        def per_expert(e, _):
            off = fill_off[e]
            npad = fill_n[e]
            n_single = npad & (SUBLANES - 1)
            for r in range(SUBLANES - 1):
                go(pltpu.make_async_copy(zero_s.at[pl.ds(0, 1), :], xs_out.at[pl.ds(off + r, 1), :], sem_z),
                   r < n_single)
            off = pl.multiple_of(off + n_single, SUBLANES)
            for bit in reversed(range(SUBLANES.bit_length() - 1, pad_bits)):
                size = 1 << bit
                go(pltpu.make_async_copy(zero_s.at[pl.ds(0, size), :], xs_out.at[pl.ds(off, size), :], sem_z),
                   (npad & size) != 0)
                off = pl.multiple_of(off + (npad & size), SUBLANES)
            return 0

        lax.fori_loop(0, n_exp, per_expert, 0)

        def per_tail_chunk(j, _):
            off = pl.multiple_of(tail[0] + j * zrows, zrows)
            copy = pltpu.make_async_copy(zero_s, xs_out.at[pl.ds(off, zrows), :], sem_z)
            copy.wait() if wait else copy.start()
            return 0

        lax.fori_loop(0, tail[1], per_tail_chunk, 0)

    @pl.when(i == 0)
    def _():
        zero_s[...] = jnp.zeros_like(zero_s)
        fill(False)

    for sl in range(2):
        @pl.when(slot == sl)
        def _():
            idx_copy(i, sl).wait()

            def issue(t, _):
                for k in range(TOP_K):
                    _row_copy(h_ref, t, xs_out, idx_s[sl][t * TOP_K + k], sem_d).start(priority=k % 2)
                return 0

            lax.fori_loop(0, td, issue, 0, unroll=8)

    for k in range(TOP_K):
        pltpu.make_async_copy(h_ref, xs_out.at[pl.ds(0, td), :], sem_d).wait()

    @pl.when(i == 0)
    def _():
        fill(True)


def _dispatch(h2, dest_flat, fill_off, fill_n, tail, n_slots, td, bm):
    n, d = h2.shape
    pad_bits = (bm - 1).bit_length()
    grid_spec = pltpu.PrefetchScalarGridSpec(
        num_scalar_prefetch=3, grid=(n // td,),
        in_specs=[pl.BlockSpec((td, d), lambda i, fo, fn, tl: (i, 0)),
                  pl.BlockSpec(memory_space=pl.ANY)],
        out_specs=pl.BlockSpec(memory_space=pl.ANY),
        scratch_shapes=[pltpu.SMEM((td * TOP_K,), jnp.int32), pltpu.SMEM((td * TOP_K,), jnp.int32),
                        pltpu.VMEM((bm // 2, d), F32),
                        pltpu.SemaphoreType.DMA((2,)), pltpu.SemaphoreType.DMA, pltpu.SemaphoreType.DMA])
    return pl.pallas_call(
        functools.partial(_dispatch_kernel, pad_bits=pad_bits), grid_spec=grid_spec,
        out_shape=jax.ShapeDtypeStruct((n_slots, d), F32),
        compiler_params=_params("arbitrary"), name="moe_dispatch",
    )(fill_off, fill_n, tail, h2, dest_flat)


def _expert_kernel(be_ref, nu_ref, x_ref, wgu_ref, bgu_ref, wd_ref, bd_ref, y_ref, wgu_s, wd_s):
    i = pl.program_id(0)
    d_ff = wd_ref.shape[1]

    @pl.when(i >= nu_ref[0])
    def _():
        y_ref[...] = jnp.zeros_like(y_ref)

    @pl.when(i < nu_ref[0])
    def _():
        @pl.when((i == 0) | (be_ref[i] != be_ref[jnp.maximum(i - 1, 0)]))
        def _():
            wgu_s[...] = wgu_ref[0].astype(BF16)
            wd_s[...] = wd_ref[0].astype(BF16)

        gu = _dot(x_ref[...].astype(BF16), wgu_s[...]) + bgu_ref[0]
        gate = jnp.minimum(gu[:, :d_ff], SWIGLU_LIMIT)
        up = jnp.clip(gu[:, d_ff:], -SWIGLU_LIMIT, SWIGLU_LIMIT)
        act = gate * _sigmoid(SWIGLU_ALPHA * gate) * (up + 1.0)
        y_ref[...] = _dot(act.astype(BF16), wd_s[...]) + bd_ref[0]


def _experts(xs, blk_expert, n_used, wgu, bgu, wd, bd, bm):
    n_slots, d = xs.shape
    n_exp, _, f2 = wgu.shape
    d_ff = wd.shape[1]
    n_blocks = n_slots // bm
    row = lambda i, be, nu: (jnp.minimum(i, nu[0] - 1), 0)
    grid_spec = pltpu.PrefetchScalarGridSpec(
        num_scalar_prefetch=2, grid=(n_blocks,),
        in_specs=[pl.BlockSpec((bm, d), row),
                  pl.BlockSpec((1, d, f2), lambda i, be, nu: (be[i], 0, 0)),
                  pl.BlockSpec((1, 1, f2), lambda i, be, nu: (be[i], 0, 0)),
                  pl.BlockSpec((1, d_ff, d), lambda i, be, nu: (be[i], 0, 0)),
                  pl.BlockSpec((1, 1, d), lambda i, be, nu: (be[i], 0, 0))],
        out_specs=pl.BlockSpec((bm, d), lambda i, be, nu: (i, 0)),
        scratch_shapes=[pltpu.VMEM((d, f2), BF16), pltpu.VMEM((d_ff, d), BF16)])
    return pl.pallas_call(
        _expert_kernel, grid_spec=grid_spec,
        out_shape=jax.ShapeDtypeStruct((n_slots, d), F32),
        compiler_params=_params("arbitrary"), name="moe_experts",
    )(blk_expert, n_used, xs, wgu, bgu.reshape(n_exp, 1, f2), wd, bd.reshape(n_exp, 1, d))


def _combine_kernel(x1_ref, meta_ref, g2_ref, fw_ref, dest_hbm, y_hbm, o_ref, rows_s, idx0_s, idx1_s, sem_i, sem_d):
    tc = x1_ref.shape[0]
    n_idx = tc * TOP_K
    i = pl.program_id(0)
    n = pl.num_programs(0)
    slot = i % 2
    idx_s = (idx0_s, idx1_s)

    def idx_copy(j, sl):
        return pltpu.make_async_copy(dest_hbm.at[pl.ds(j * n_idx, n_idx)], idx_s[sl], sem_i.at[sl])

    def gather(j, sl):
        idx_copy(j, sl).wait()

        def issue(t, _):
            for k in range(TOP_K):
                _row_copy(y_hbm, idx_s[sl][t * TOP_K + k], rows_s.at[sl, k], t, sem_d.at[sl]).start(priority=k % 2)
            return 0

        lax.fori_loop(0, tc, issue, 0, unroll=8)

    @pl.when(i == 0)
    def _():
        idx_copy(0, 0).start()

        @pl.when(n > 1)
        def _():
            idx_copy(1, 1).start()

        gather(0, 0)

    for sl in range(2):
        @pl.when((i + 1 < n) & (slot != sl))
        def _():
            gather(i + 1, sl)

    for sl in range(2):
        @pl.when((i + 2 < n) & (slot == sl))
        def _():
            idx_copy(i + 2, sl).start()

    for k in range(TOP_K):
        pltpu.make_async_copy(y_hbm.at[pl.ds(0, tc), :], rows_s.at[slot, k], sem_d.at[slot]).wait()

    meta = meta_ref[...]
    moe = meta[:, TOP_K:TOP_K + 1] * rows_s[slot, 0]
    for k in range(1, TOP_K):
        moe = moe + meta[:, TOP_K + k:TOP_K + k + 1] * rows_s[slot, k]
    o_ref[...] = _rms(x1_ref[...] + g2_ref[0] * moe, fw_ref[...])


def _combine(x1, meta, g2, final_w, dest_flat, y, t_seq, tc):
    n, d = x1.shape
    return pl.pallas_call(
        _combine_kernel, grid=(n // tc,),
        in_specs=[pl.BlockSpec((tc, d), lambda i: (i, 0)),
                  pl.BlockSpec((tc, LANES), lambda i: (i, 0)),
                  pl.BlockSpec((1, 1, d), lambda i: (i * tc // t_seq, 0, 0)),
                  pl.BlockSpec((1, d), lambda i: (0, 0)),
                  pl.BlockSpec(memory_space=pl.ANY),
                  pl.BlockSpec(memory_space=pl.ANY)],
        out_specs=pl.BlockSpec((tc, d), lambda i: (i, 0)),
        out_shape=jax.ShapeDtypeStruct((n, d), F32),
        scratch_shapes=[pltpu.VMEM((2, TOP_K, tc, d), F32),
                        pltpu.SMEM((tc * TOP_K,), jnp.int32), pltpu.SMEM((tc * TOP_K,), jnp.int32),
                        pltpu.SemaphoreType.DMA((2,)), pltpu.SemaphoreType.DMA((2,))],
        compiler_params=_params("arbitrary"), name="moe_combine",
    )(x1, meta, g2, final_w, dest_flat, y)


def _gate_weights(wa, wx):
    _, heads, hd, _ = wa.shape
    hh = heads // 2
    eye = jnp.eye(hh, dtype=wa.dtype)

    def blockdiag(wsel):
        return jnp.einsum('hij,hg->higj', wsel, eye).reshape(hh * hd, hh * hd)

    halves = []
    for s in range(2):
        sl = slice(s * hh, (s + 1) * hh)
        halves.append(jnp.concatenate([blockdiag(wa[0, sl]), blockdiag(wx[0, sl]),
                                       blockdiag(wa[1, sl]), blockdiag(wx[1, sl])], axis=1))
    return jnp.stack(halves).astype(BF16)


def kernel(x, c, ctx, c_ctx, norm1_w, norm2_w, w_ada, b_ada, w_in, hg_lb_logits, hg_norm_w, rg_conv_w, rg_conv_b,
           rg_wa, rg_ba, rg_wx, rg_bx, rg_lambda, w_out, router_w, router_b, w_gate_up, b_gate_up, w_down,
           b_down, final_norm_w):
    b, t, d = x.shape
    tcx = ctx.shape[1]
    n_exp = router_w.shape[-1]
    n_tok = b * t
    depth = w_in.shape[0]
    lb_all = jnp.cumsum(jax.nn.softmax(hg_lb_logits.astype(F32), axis=0), axis=0)

    for l in range(depth):
        assert l == depth - 1, "context stream update of non-final layers is not implemented"
        pad = (-(b + 1)) % 8
        c_all = jnp.concatenate([c, c_ctx[None], jnp.zeros((pad, d), F32)], axis=0)
        mod = _mod(c_all, w_ada[l], b_ada[l])
        sh1, sc1, g1, sh2, sc2, g2 = [m[:b, None, :] for m in jnp.split(mod, 6, axis=-1)]
        csh1, csc1 = [m[b:b + 1, None, :] for m in jnp.split(mod, 6, axis=-1)[:2]]

        w_in_b = w_in[l].astype(BF16)
        nw1 = norm1_w[l].reshape(1, d)
        tm = min(512, t)
        p_lat = _inproj(x, sh1, sc1, nw1, w_in_b, tm)
        p_ctx = _inproj(ctx, csh1, csc1, nw1, w_in_b, min(256, tcx))

        hg = _hgrn2(p_lat, p_ctx, lb_all[l], hg_norm_w[l].reshape(1, -1))
        ch = rg_conv_w.shape[-1]
        wg = _gate_weights(rg_wa[l], rg_wx[l])
        bg = jnp.concatenate([rg_ba[l, 0], rg_bx[l, 0], rg_ba[l, 1], rg_bx[l, 1]]).reshape(1, 4 * ch)
        rg = _rglru(p_lat, p_ctx, rg_conv_w[l], rg_conv_b[l].reshape(1, ch), wg, bg, rg_lambda[l])

        x1, h2, meta, cnt = _mix(hg, rg, x, g1, sh2, sc2, norm2_w[l].reshape(1, d), w_out[l].astype(BF16),
                                 router_w[l].astype(BF16), router_b[l].reshape(1, n_exp), tm)

        bm = 512
        meta2 = meta.reshape(n_tok, LANES)
        idx = meta2[:, 0:TOP_K].astype(jnp.int32)
        rank = meta2[:, 2 * TOP_K:3 * TOP_K].astype(jnp.int32)
        counts = cnt[0, :n_exp].astype(jnp.int32)
        padded = (counts + bm - 1) // bm * bm
        pad_end = jnp.cumsum(padded)
        pad_start = pad_end - padded
        dest = (pad_start[idx] + rank).reshape(-1)
        n_blocks = -(-n_tok * TOP_K // bm) + n_exp
        blk_start = jnp.arange(n_blocks, dtype=jnp.int32) * bm
        blk_expert = jnp.minimum(jnp.sum(blk_start[:, None] >= pad_end[None, :], axis=1), n_exp - 1).astype(jnp.int32)
        n_used = (pad_end[-1:] // bm).astype(jnp.int32)

        tdc = min(256, t)
        n_slots = n_blocks * bm
        tail = jnp.stack([pad_end[-1], (n_slots - pad_end[-1]) // (bm // 2)]).astype(jnp.int32)
        xs = _dispatch(h2.reshape(n_tok, d), dest, pad_start + counts, padded - counts, tail, n_slots,
                       min(512, t), bm)
        y = _experts(xs, blk_expert, n_used, w_gate_up[l], b_gate_up[l], w_down[l], b_down[l], bm)
        out = _combine(x1.reshape(n_tok, d), meta2, g2, final_norm_w.reshape(1, d), dest, y, t, tdc)
        return out.reshape(b, t, d)
```

```python
import functools

import jax
import jax.numpy as jnp
from jax import lax
from jax.experimental import pallas as pl
from jax.experimental.pallas import tpu as pltpu

GRID_W = 64
HG_HEADS = 4
HG_CHUNK = 32
RG_HEADS = 8
RG_CONV = 4
RG_C = 8.0
TOP_K = 4
SWIGLU_LIMIT = 7.0
SWIGLU_ALPHA = 1.702
EPS = 1e-6

LANES = 128
SUBLANES = 8
VMEM_LIMIT = 56 * 1024 * 1024

F32 = jnp.float32
BF16 = jnp.bfloat16
HIGHEST = lax.Precision.HIGHEST


def _params(*sem):
    return pltpu.CompilerParams(dimension_semantics=sem, vmem_limit_bytes=VMEM_LIMIT)


def _sigmoid(x):
    return 0.5 * jnp.tanh(0.5 * x) + 0.5


def _silu(x):
    return x * _sigmoid(x)


def _rms(x, w):
    return x * lax.rsqrt(jnp.mean(x * x, axis=-1, keepdims=True) + EPS) * w


def _dot(a, b):
    return jnp.dot(a, b, preferred_element_type=F32)


def _dot_nt(a, b):
    return lax.dot_general(a, b, (((1,), (1,)), ((), ())), preferred_element_type=F32)


def _mod_kernel(c_ref, w_ref, b_ref, o_ref):
    o_ref[...] = jnp.dot(_silu(c_ref[...]), w_ref[...], preferred_element_type=F32,
                         precision=HIGHEST) + b_ref[...]


def _mod(c_all, w_ada, b_ada):
    r, d = c_all.shape
    n = w_ada.shape[1]
    tn = n // 4
    return pl.pallas_call(
        _mod_kernel, grid=(n // tn,),
        in_specs=[pl.BlockSpec((r, d), lambda j: (0, 0)),
                  pl.BlockSpec((d, tn), lambda j: (0, j)),
                  pl.BlockSpec((1, tn), lambda j: (0, j))],
        out_specs=pl.BlockSpec((r, tn), lambda j: (0, j)),
        out_shape=jax.ShapeDtypeStruct((r, n), F32),
        compiler_params=_params("arbitrary"), name="adaln_mod",
    )(c_all, w_ada, b_ada.reshape(1, n))


def _inproj_kernel(x_ref, sh_ref, sc_ref, nw_ref, w_ref, o_ref):
    h = _rms(x_ref[0], nw_ref[...]) * (1.0 + sc_ref[0]) + sh_ref[0]
    o_ref[0] = _dot(h.astype(BF16), w_ref[...])


def _inproj(x, shift, scale, norm_w, w_bf16, tm):
    b, t, d = x.shape
    n = w_bf16.shape[1]
    per_batch = shift.shape[0] == b
    mod_map = (lambda i, j: (i, 0, 0)) if per_batch else (lambda i, j: (0, 0, 0))
    return pl.pallas_call(
        _inproj_kernel, grid=(b, t // tm),
        in_specs=[pl.BlockSpec((1, tm, d), lambda i, j: (i, j, 0)),
                  pl.BlockSpec((1, 1, d), mod_map),
                  pl.BlockSpec((1, 1, d), mod_map),
                  pl.BlockSpec((1, d), lambda i, j: (0, 0)),
                  pl.BlockSpec((d, n), lambda i, j: (0, 0))],
        out_specs=pl.BlockSpec((1, tm, n), lambda i, j: (i, j, 0)),
        out_shape=jax.ShapeDtypeStruct((b, t, n), F32),
        compiler_params=_params("arbitrary", "arbitrary"), name="norm_inproj",
    )(x, shift, scale, norm_w, w_bf16)


def _split_bf16(x):
    hi = x.astype(BF16)
    return hi, (x - hi.astype(F32)).astype(BF16)


def _hg_kernel(q_ref, v_ref, zf_ref, zb_ref, g_ref, cv_ref, czf_ref, czb_ref, lb_ref, nw_ref, o_ref,
               o_s, qd_s, u_s, dec_s, st_s, cu_s, cdec_s, tot_s, *, t_lat, t_ctx):
    c = HG_CHUNK
    dk = q_ref.shape[-1]
    lb = lb_ref[...]
    lbf, lbb = lb[0:1], lb[1:2]

    def prep(rb, zf, zb, q, v, u_out, dec_out, blk, r0):
        nc = rb // c
        ri = lax.broadcasted_iota(jnp.int32, (rb, rb), 0)
        ci = lax.broadcasted_iota(jnp.int32, (rb, rb), 1)
        same = (ri // c) == (ci // c)
        low = same & (ci <= ri)
        upp = same & (ci >= ri)
        ff = lbf + (1.0 - lbf) * _sigmoid(zf)
        fb = lbb + (1.0 - lbb) * _sigmoid(zb)
        lgf, lgb = jnp.log(ff), jnp.log(fb)
        rhs = jnp.concatenate([*_split_bf16(lgf), *_split_bf16(lgb)], axis=1)
        pre = _dot(low.astype(BF16), rhs)
        bcf = pre[:, :dk] + pre[:, dk:2 * dk]
        pfb = pre[:, 2 * dk:3 * dk] + pre[:, 3 * dk:]
        half = tot_s.shape[0] // 2
        tot_s[0:rb, :] = bcf
        tot_s[half:half + rb, :] = pfb
        totf = tot_s[pl.ds(c - 1, nc, stride=c), :]
        totb = tot_s[pl.ds(half + c - 1, nc, stride=c), :]

        def spread(tot):
            return jnp.broadcast_to(tot[:, None, :], (nc, c, dk)).reshape(rb, dk)

        remf = spread(totf) - bcf
        remb = pfb - lgb
        bcb = spread(totb) - remb
        kkf, kkb = 1.0 - ff, 1.0 - fb
        kef = (kkf * jnp.exp(remf)).astype(BF16)
        keb = (kkb * jnp.exp(remb)).astype(BF16)
        chunk_of_row = lax.broadcasted_iota(jnp.int32, (rb, dk), 0) // c
        zero = jnp.zeros((rb, dk), BF16)
        keys = jnp.concatenate([jnp.where(chunk_of_row == j, ke, zero) for ke in (kef, keb) for j in range(nc)],
                               axis=1)
        u_all = _dot(v.T.astype(BF16), keys)
        c0 = blk * nc
        for d in range(2):
            for j in range(nc):
                u_out[d, c0 + j] = u_all[:, (d * nc + j) * dk:(d * nc + j + 1) * dk]
        dec_out[0, pl.ds(pl.multiple_of(c0, nc), nc), :] = jnp.exp(totf)
        dec_out[1, pl.ds(pl.multiple_of(c0, nc), nc), :] = jnp.exp(totb)
        if q is None:
            return
        sq = _silu(q)
        qdf = (sq * jnp.exp(bcf)).astype(BF16)
        qdb = (sq * jnp.exp(bcb)).astype(BF16)
        kdf = (kkf * jnp.exp(-bcf)).astype(BF16)
        kdb = (kkb * jnp.exp(-bcb)).astype(BF16)
        p = jnp.where(low, _dot_nt(qdf, kdf), 0.0) + jnp.where(upp, _dot_nt(qdb, kdb), 0.0)
        o_s[pl.ds(r0, rb), :] = _dot(p.astype(BF16), v.astype(BF16))
        qd_s[pl.ds(r0, rb), 0:dk] = qdf
        qd_s[pl.ds(r0, rb), dk:2 * dk] = qdb

    def scan(n, ur, decr, keep, carry):
        def body(i, carry):
            sf, sb = carry
            j = n - 1 - i
            if keep:
                st_s[i, :, 0:dk] = sf.astype(BF16)
                st_s[j, :, dk:2 * dk] = sb.astype(BF16)
            return sf * decr[0, pl.ds(i, 1), :] + ur[0, i], sb * decr[1, pl.ds(j, 1), :] + ur[1, j]

        return lax.fori_loop(0, n, body, carry, unroll=2)

    rbc = min(256, t_ctx)
    for blk in range(t_ctx // rbc):
        r0 = blk * rbc
        prep(rbc, czf_ref[0, r0:r0 + rbc, :], czb_ref[0, r0:r0 + rbc, :], None, cv_ref[0, r0:r0 + rbc, :],
             cu_s, cdec_s, blk, r0)
    zero = jnp.zeros((dk, dk), F32)
    carry = scan(t_ctx // c, cu_s, cdec_s, False, (zero, zero))

    rbl = min(256, t_lat)

    def lat_prep(blk, _):
        r0 = pl.multiple_of(blk * rbl, rbl)
        rows = pl.ds(r0, rbl)
        prep(rbl, zf_ref[0, rows, :], zb_ref[0, rows, :], q_ref[0, rows, :], v_ref[0, rows, :], u_s, dec_s, blk, r0)
        return 0

    lax.fori_loop(0, t_lat // rbl, lat_prep, 0, unroll=2)
    scan(t_lat // c, u_s, dec_s, True, carry)

    def finish(blk, _):
        r0 = pl.multiple_of(blk * rbl, rbl)
        inter = [_dot_nt(qd_s[pl.ds(r0 + j * c, c), :], st_s[blk * (rbl // c) + j]) for j in range(rbl // c)]
        o = o_s[pl.ds(r0, rbl), :] + jnp.concatenate(inter, axis=0)
        o_ref[0, pl.ds(r0, rbl), :] = _rms(o, nw_ref[...]) * _silu(g_ref[0, pl.ds(r0, rbl), :])
        return 0

    lax.fori_loop(0, t_lat // rbl, finish, 0, unroll=4)


def _hgrn2(p_lat, p_ctx, lb, norm_w):
    b, t, _ = p_lat.shape
    tc = p_ctx.shape[1]
    hw = lb.shape[1]
    dk = hw // HG_HEADS
    nh = HG_HEADS
    c = HG_CHUNK

    def col(k, tt):
        return pl.BlockSpec((1, tt, dk), lambda i, h, k=k: (i, 0, k * nh + h))

    return pl.pallas_call(
        functools.partial(_hg_kernel, t_lat=t, t_ctx=tc), grid=(b, nh),
        in_specs=[col(0, t), col(1, t), col(2, t), col(3, t), col(4, t),
                  col(1, tc), col(2, tc), col(3, tc),
                  pl.BlockSpec((2, dk), lambda i, h: (0, h)),
                  pl.BlockSpec((1, dk), lambda i, h: (0, h))],
        out_specs=pl.BlockSpec((1, t, dk), lambda i, h: (i, 0, h)),
        out_shape=jax.ShapeDtypeStruct((b, t, hw), F32),
        scratch_shapes=[pltpu.VMEM((t, dk), F32),
                        pltpu.VMEM((t, 2 * dk), BF16),
                        pltpu.VMEM((2, t // c, dk, dk), F32),
                        pltpu.VMEM((2, t // c, dk), F32),
                        pltpu.VMEM((t // c, dk, 2 * dk), BF16),
                        pltpu.VMEM((2, tc // c, dk, dk), F32),
                        pltpu.VMEM((2, tc // c, dk), F32),
                        pltpu.VMEM((2 * min(256, max(t, tc)), dk), F32)],
        compiler_params=_params("arbitrary", "arbitrary"), name="hgrn2",
    )(p_lat, p_lat, p_lat, p_lat, p_lat, p_ctx, p_ctx, p_ctx, lb, norm_w)


def _shift_rows(x, k):
    n = x.shape[0]
    y = pltpu.roll(x, k % n, 0)
    r = lax.broadcasted_iota(jnp.int32, x.shape, 0)
    return jnp.where((r >= k) & (r < n + k), y, 0.0)


def _rg_kernel(rx_ref, rgate_ref, crx_ref, cw_ref, cb_ref, wg_ref, bg_ref, lam_ref, o_ref,
               xc_s, af_s, bf_s, ab_s, bb_s, hf_s, hb_s, caf_s, cbf_s, cab_s, cbb_s, *, t_lat, t_ctx):
    w = GRID_W
    rows = t_lat // w
    ch = rx_ref.shape[-1]
    half = ch // 2
    cw = cw_ref[...]
    cb = cb_ref[...]
    bg = bg_ref[...]
    nl = -lam_ref[...]
    cdec = -RG_C * (jnp.maximum(nl, 0.0) + jnp.log1p(jnp.exp(-jnp.abs(nl))))

    def conv(xm2, xm1, x0, xp1):
        return cb + cw[0:1] * xm2 + cw[1:2] * xm1 + cw[2:3] * x0 + cw[3:4] * xp1

    def gates(xc):
        xb = xc.astype(BF16)
        g0 = _dot(xb[:, :half], wg_ref[0])
        g1 = _dot(xb[:, half:], wg_ref[1])
        outs = []
        for d in range(2):
            pre = []
            for s in (2 * d, 2 * d + 1):
                pre.append(jnp.concatenate([g0[:, s * half:(s + 1) * half], g1[:, s * half:(s + 1) * half]],
                                           axis=1) + bg[:, s * ch:(s + 1) * ch])
            log_a = cdec[d:d + 1] * _sigmoid(pre[0])
            a = jnp.exp(log_a)
            mult = jnp.sqrt(-jnp.tanh(log_a) * (a * a + 1.0))
            outs += [a, mult * _sigmoid(pre[1]) * xc]
        return outs

    xctx = crx_ref[0]
    xcc = conv(_shift_rows(xctx, 2), _shift_rows(xctx, 1), xctx, _shift_rows(xctx, -1))
    caf_s[...], cbf_s[...], cab_s[...], cbb_s[...] = gates(xcc)

    def cstep(i, carry):
        hf, hb = carry
        hf = caf_s[pl.ds(i, 1), :] * hf + cbf_s[pl.ds(i, 1), :]
        j = t_ctx - 1 - i
        hb = cab_s[pl.ds(j, 1), :] * hb + cbb_s[pl.ds(j, 1), :]
        return hf, hb

    zrow = jnp.zeros((1, ch), F32)
    hf0, hb0 = lax.fori_loop(0, t_ctx, cstep, (zrow, zrow), unroll=8)

    def slab(rr):
        if 0 <= rr < rows:
            return rx_ref[0, rr * w:(rr + 1) * w, :]
        if rr < 0:
            return _shift_rows(rx_ref[0, (rr + rows) * w:(rr + rows + 1) * w, :], 1)
        return _shift_rows(rx_ref[0, (rr - rows) * w:(rr - rows + 1) * w, :], -1)

    for r in range(rows):
        xc_s[r * w:(r + 1) * w, :] = conv(slab(r - 2), slab(r - 1), slab(r), slab(r + 1))

    mb = min(256, t_lat)

    def gbody(i, _):
        r0 = pl.multiple_of(i * mb, mb)
        a_f, b_f, a_b, b_b = gates(xc_s[pl.ds(r0, mb), :])
        af_s[pl.ds(r0, mb), :] = a_f
        bf_s[pl.ds(r0, mb), :] = b_f
        ab_s[pl.ds(r0, mb), :] = a_b
        bb_s[pl.ds(r0, mb), :] = b_b
        return 0

    lax.fori_loop(0, t_lat // mb, gbody, 0)

    def l1(i, _):
        pf = pl.multiple_of(i * w, w)
        qf = pl.multiple_of((i - 1) * w, w)
        a = af_s[pl.ds(pf, w), :]
        af_s[pl.ds(pf, w), :] = a * af_s[pl.ds(qf, w), :]
        bf_s[pl.ds(pf, w), :] = a * bf_s[pl.ds(qf, w), :] + bf_s[pl.ds(pf, w), :]
        pb = pl.multiple_of((rows - 1 - i) * w, w)
        qb = pl.multiple_of((rows - i) * w, w)
        a = ab_s[pl.ds(pb, w), :]
        ab_s[pl.ds(pb, w), :] = a * ab_s[pl.ds(qb, w), :]
        bb_s[pl.ds(pb, w), :] = a * bb_s[pl.ds(qb, w), :] + bb_s[pl.ds(pb, w), :]
        return 0

    lax.fori_loop(1, rows, l1, 0)

    last = (rows - 1) * w

    def l2(i, carry):
        hf, hb = carry
        hf_s[pl.ds(i, 1), :] = hf
        hf = af_s[pl.ds(last + i, 1), :] * hf + bf_s[pl.ds(last + i, 1), :]
        j = w - 1 - i
        hb_s[pl.ds(j, 1), :] = hb
        hb = ab_s[pl.ds(j, 1), :] * hb + bb_s[pl.ds(j, 1), :]
        return hf, hb

    lax.fori_loop(0, w, l2, (hf0, hb0))

    def l3(i, _):
        p = pl.multiple_of(i * w, w)
        h = (af_s[pl.ds(p, w), :] * hf_s[...] + bf_s[pl.ds(p, w), :]
             + ab_s[pl.ds(p, w), :] * hb_s[...] + bb_s[pl.ds(p, w), :])
        o_ref[0, pl.ds(p, w), :] = jax.nn.gelu(rgate_ref[0, pl.ds(p, w), :]) * h
        return 0

    lax.fori_loop(0, rows, l3, 0)


def _rglru(p_lat, p_ctx, conv_w, conv_b, wg, bg, lam):
    b, t, _ = p_lat.shape
    tc = p_ctx.shape[1]
    ch = conv_w.shape[1]
    rx_blk = (p_lat.shape[2] - 2 * ch) // ch
    full = lambda shape: pl.BlockSpec(shape, lambda i: (0,) * len(shape))
    big = lambda: pltpu.VMEM((t, ch), F32)
    small = lambda: pltpu.VMEM((tc, ch), F32)
    return pl.pallas_call(
        functools.partial(_rg_kernel, t_lat=t, t_ctx=tc), grid=(b,),
        in_specs=[pl.BlockSpec((1, t, ch), lambda i: (i, 0, rx_blk)),
                  pl.BlockSpec((1, t, ch), lambda i: (i, 0, rx_blk + 1)),
                  pl.BlockSpec((1, tc, ch), lambda i: (i, 0, rx_blk)),
                  full(conv_w.shape), full(conv_b.shape), full(wg.shape), full(bg.shape), full(lam.shape)],
        out_specs=pl.BlockSpec((1, t, ch), lambda i: (i, 0, 0)),
        out_shape=jax.ShapeDtypeStruct((b, t, ch), F32),
        scratch_shapes=[big(), big(), big(), big(), big(),
                        pltpu.VMEM((GRID_W, ch), F32), pltpu.VMEM((GRID_W, ch), F32),
                        small(), small(), small(), small()],
        compiler_params=_params("arbitrary"), name="rglru",
    )(p_lat, p_lat, p_ctx, conv_w, conv_b, wg, bg, lam)


def _mix_kernel(hg_ref, rg_ref, x_ref, g1_ref, sh_ref, sc_ref, nw_ref, wo_ref, rw_ref, rb_ref,
                x1_ref, h2_ref, meta_ref, cnt_ref, base_s, *, n_exp):
    tm = x_ref.shape[1]

    @pl.when((pl.program_id(0) == 0) & (pl.program_id(1) == 0))
    def _():
        base_s[...] = jnp.zeros_like(base_s)

    hcat = jnp.concatenate([hg_ref[0], rg_ref[0]], axis=1).astype(BF16)
    x1 = x_ref[0] + g1_ref[0] * _dot(hcat, wo_ref[...])
    x1_ref[0] = x1
    h2 = _rms(x1, nw_ref[...]) * (1.0 + sc_ref[0]) + sh_ref[0]
    h2_ref[0] = h2
    logits = _dot(h2.astype(BF16), rw_ref[...]) + rb_ref[...]

    lane_e = lax.broadcasted_iota(jnp.int32, (tm, n_exp), 1)
    vals, idxs = [], []
    cur = logits
    for _ in range(TOP_K):
        m = jnp.max(cur, axis=1, keepdims=True)
        ix = jnp.min(jnp.where(cur == m, lane_e, n_exp), axis=1, keepdims=True)
        vals.append(m)
        idxs.append(ix)
        cur = jnp.where(lane_e == ix, -jnp.inf, cur)
    ex = [jnp.exp(v - vals[0]) for v in vals]
    den = ex[0] + ex[1] + ex[2] + ex[3]

    lane = lax.broadcasted_iota(jnp.int32, (tm, LANES), 1)
    onehot = jnp.zeros((tm, LANES), F32)
    for k in range(TOP_K):
        onehot = jnp.where(lane == idxs[k] + k * n_exp, 1.0, onehot)
    ri = lax.broadcasted_iota(jnp.int32, (tm, tm), 0)
    ci = lax.broadcasted_iota(jnp.int32, (tm, tm), 1)
    prefix = _dot((ci < ri).astype(BF16), onehot.astype(BF16))
    tot = jnp.broadcast_to(prefix[tm - 1:tm] + onehot[tm - 1:tm], (8, LANES))
    lane8 = lax.broadcasted_iota(jnp.int32, (8, LANES), 1)
    off = base_s[...]
    tot_all = tot
    for j in range(1, TOP_K):
        rolled = pltpu.roll(tot, j * n_exp, 1)
        off = off + jnp.where(lane8 >= j * n_exp, rolled, 0.0)
        tot_all = tot_all + rolled
    pos = onehot * (prefix + off[0:1])
    meta = jnp.zeros((tm, LANES), F32)
    for k in range(TOP_K):
        in_k = (lane >= k * n_exp) & (lane < (k + 1) * n_exp)
        rank = jnp.sum(jnp.where(in_k, pos, 0.0), axis=1, keepdims=True)
        meta = jnp.where(lane == k, idxs[k].astype(F32), meta)
        meta = jnp.where(lane == TOP_K + k, ex[k] / den, meta)
        meta = jnp.where(lane == 2 * TOP_K + k, rank, meta)
    meta_ref[0] = meta
    base_s[...] = base_s[...] + tot_all
    cnt_ref[...] = base_s[...]


def _mix(hg, rg, x, g1, sh2, sc2, norm_w, wo_bf16, router_w, router_b, tm):
    b, t, d = x.shape
    hw = hg.shape[2]
    n_exp = router_w.shape[1]
    assert TOP_K * n_exp == LANES
    tok = lambda last: pl.BlockSpec((1, tm, last), lambda i, j: (i, j, 0))
    per_b = pl.BlockSpec((1, 1, d), lambda i, j: (i, 0, 0))
    full = lambda shape: pl.BlockSpec(shape, lambda i, j: (0,) * len(shape))
    return pl.pallas_call(
        functools.partial(_mix_kernel, n_exp=n_exp), grid=(b, t // tm),
        in_specs=[tok(hw), tok(hw), tok(d), per_b, per_b, per_b, full((1, d)),
                  full(wo_bf16.shape), full(router_w.shape), full((1, n_exp))],
        out_specs=[tok(d), tok(d), tok(LANES), pl.BlockSpec((8, LANES), lambda i, j: (0, 0))],
        out_shape=[jax.ShapeDtypeStruct((b, t, d), F32), jax.ShapeDtypeStruct((b, t, d), F32),
                   jax.ShapeDtypeStruct((b, t, LANES), F32), jax.ShapeDtypeStruct((8, LANES), F32)],
        scratch_shapes=[pltpu.VMEM((8, LANES), F32)],
        compiler_params=_params("arbitrary", "arbitrary"), name="outproj_router",
    )(hg, rg, x, g1, sh2, sc2, norm_w, wo_bf16, router_w, router_b)


def _dispatch_kernel(fill_off, fill_n, tail, h_ref, dest_hbm, xs_out, idx0_s, idx1_s, zero_s, sem_i, sem_d, sem_z, *,
                     pad_bits):
    td = h_ref.shape[0] * SUBLANES
    n_idx = td * TOP_K
    n_exp = fill_n.shape[0]
    zrows = zero_s.shape[0]
    i = pl.program_id(0)
    slot = i % 2
    idx_s = (idx0_s, idx1_s)

    def idx_copy(j, sl):
        return pltpu.make_async_copy(dest_hbm.at[pl.ds(j * n_idx, n_idx)], idx_s[sl], sem_i.at[sl])

    @pl.when(i == 0)
    def _():
        idx_copy(0, 0).start()

    for sl in range(2):
        @pl.when((i + 1 < pl.num_programs(0)) & (slot != sl))
        def _():
            idx_copy(i + 1, sl).start()

    def fill(wait):
        def go(copy, cond):
            @pl.when(cond)
            def _():
                copy.wait() if wait else copy.start()

        def per_expert(e, _):
            off = fill_off[e]
            npad = fill_n[e]
            n_single = npad & (SUBLANES - 1)
            for r in range(SUBLANES - 1):
                go(pltpu.make_async_copy(zero_s.at[pl.ds(0, 1), :], xs_out.at[pl.ds(off + r, 1), :], sem_z),
                   r < n_single)
            off = pl.multiple_of(off + n_single, SUBLANES)
            for bit in reversed(range(SUBLANES.bit_length() - 1, pad_bits)):
                size = 1 << bit
                go(pltpu.make_async_copy(zero_s.at[pl.ds(0, size), :], xs_out.at[pl.ds(off, size), :], sem_z),
                   (npad & size) != 0)
                off = pl.multiple_of(off + (npad & size), SUBLANES)
            return 0

        lax.fori_loop(0, n_exp, per_expert, 0)

        def per_tail_chunk(j, _):
            off = pl.multiple_of(tail[0] + j * zrows, zrows)
            copy = pltpu.make_async_copy(zero_s, xs_out.at[pl.ds(off, zrows), :], sem_z)
            copy.wait() if wait else copy.start()
            return 0

        lax.fori_loop(0, tail[1], per_tail_chunk, 0)

    @pl.when(i == 0)
    def _():
        zero_s[...] = jnp.zeros_like(zero_s)
        fill(False)

    for sl in range(2):
        @pl.when(slot == sl)
        def _():
            idx_copy(i, sl).wait()

            def issue(g, _):
                for u in range(SUBLANES):
                    for k in range(TOP_K):
                        dst = idx_s[sl][g * (SUBLANES * TOP_K) + u * TOP_K + k]
                        pltpu.make_async_copy(h_ref.at[g, pl.ds(u, 1), :], xs_out.at[pl.ds(dst, 1), :],
                                              sem_d).start(priority=k % 2)
                return 0

            lax.fori_loop(0, td // SUBLANES, issue, 0)

    for _ in range(TOP_K * td // zrows):
        pltpu.make_async_copy(zero_s, xs_out.at[pl.ds(0, zrows), :], sem_d).wait()

    @pl.when(i == 0)
    def _():
        fill(True)


def _dispatch(h2, dest_flat, fill_off, fill_n, tail, n_slots, td, bm):
    n, d = h2.shape
    pad_bits = (bm - 1).bit_length()
    assert (TOP_K * td) % (bm // 2) == 0
    grid_spec = pltpu.PrefetchScalarGridSpec(
        num_scalar_prefetch=3, grid=(n // td,),
        in_specs=[pl.BlockSpec((td // SUBLANES, SUBLANES, d), lambda i, fo, fn, tl: (i, 0, 0)),
                  pl.BlockSpec(memory_space=pl.ANY)],
        out_specs=pl.BlockSpec(memory_space=pl.ANY),
        scratch_shapes=[pltpu.SMEM((td * TOP_K,), jnp.int32), pltpu.SMEM((td * TOP_K,), jnp.int32),
                        pltpu.VMEM((bm // 2, d), F32),
                        pltpu.SemaphoreType.DMA((2,)), pltpu.SemaphoreType.DMA, pltpu.SemaphoreType.DMA])
    return pl.pallas_call(
        functools.partial(_dispatch_kernel, pad_bits=pad_bits), grid_spec=grid_spec,
        out_shape=jax.ShapeDtypeStruct((n_slots, d), F32),
        compiler_params=_params("arbitrary"), name="moe_dispatch",
    )(fill_off, fill_n, tail, h2.reshape(n // SUBLANES, SUBLANES, d), dest_flat)


def _expert_kernel(be_ref, nu_ref, x_ref, wgu_ref, bgu_ref, wd_ref, bd_ref, y_ref, wgu_s, wd_s):
    i = pl.program_id(0)
    d_ff = wd_ref.shape[1]

    @pl.when(i >= nu_ref[0])
    def _():
        y_ref[...] = jnp.zeros_like(y_ref)

    @pl.when(i < nu_ref[0])
    def _():
        @pl.when((i == 0) | (be_ref[i] != be_ref[jnp.maximum(i - 1, 0)]))
        def _():
            wgu_s[...] = wgu_ref[0].astype(BF16)
            wd_s[...] = wd_ref[0].astype(BF16)

        gu = _dot(x_ref[...].astype(BF16), wgu_s[...]) + bgu_ref[0]
        gate = jnp.minimum(gu[:, :d_ff], SWIGLU_LIMIT)
        up = jnp.clip(gu[:, d_ff:], -SWIGLU_LIMIT, SWIGLU_LIMIT)
        act = gate * _sigmoid(SWIGLU_ALPHA * gate) * (up + 1.0)
        y_ref[...] = _dot(act.astype(BF16), wd_s[...]) + bd_ref[0]


def _experts(xs, blk_expert, n_used, wgu, bgu, wd, bd, bm):
    n_slots, d = xs.shape
    n_exp, _, f2 = wgu.shape
    d_ff = wd.shape[1]
    n_blocks = n_slots // bm
    row = lambda i, be, nu: (jnp.minimum(i, nu[0] - 1), 0)
    grid_spec = pltpu.PrefetchScalarGridSpec(
        num_scalar_prefetch=2, grid=(n_blocks,),
        in_specs=[pl.BlockSpec((bm, d), row),
                  pl.BlockSpec((1, d, f2), lambda i, be, nu: (be[i], 0, 0)),
                  pl.BlockSpec((1, 1, f2), lambda i, be, nu: (be[i], 0, 0)),
                  pl.BlockSpec((1, d_ff, d), lambda i, be, nu: (be[i], 0, 0)),
                  pl.BlockSpec((1, 1, d), lambda i, be, nu: (be[i], 0, 0))],
        out_specs=pl.BlockSpec((bm, d), lambda i, be, nu: (i, 0)),
        scratch_shapes=[pltpu.VMEM((d, f2), BF16), pltpu.VMEM((d_ff, d), BF16)])
    return pl.pallas_call(
        _expert_kernel, grid_spec=grid_spec,
        out_shape=jax.ShapeDtypeStruct((n_slots, d), F32),
        compiler_params=_params("arbitrary"), name="moe_experts",
    )(blk_expert, n_used, xs, wgu, bgu.reshape(n_exp, 1, f2), wd, bd.reshape(n_exp, 1, d))


def _combine_kernel(x1_ref, meta_ref, g2_ref, fw_ref, dest_hbm, y_hbm, o_ref, rows_s, idx0_s, idx1_s, sem_i, sem_d):
    tc = x1_ref.shape[0]
    n_idx = tc * TOP_K
    i = pl.program_id(0)
    n = pl.num_programs(0)
    slot = i % 2
    idx_s = (idx0_s, idx1_s)

    def idx_copy(j, sl):
        return pltpu.make_async_copy(dest_hbm.at[pl.ds(j * n_idx, n_idx)], idx_s[sl], sem_i.at[sl])

    def gather(j, sl):
        idx_copy(j, sl).wait()

        def issue(g, _):
            for u in range(SUBLANES):
                for k in range(TOP_K):
                    src = idx_s[sl][g * (SUBLANES * TOP_K) + u * TOP_K + k]
                    pltpu.make_async_copy(y_hbm.at[pl.ds(src, 1), :], rows_s.at[sl, k, g, pl.ds(u, 1), :],
                                          sem_d.at[sl]).start(priority=k % 2)
            return 0

        lax.fori_loop(0, tc // SUBLANES, issue, 0)

    @pl.when(i == 0)
    def _():
        idx_copy(0, 0).start()

        @pl.when(n > 1)
        def _():
            idx_copy(1, 1).start()

        gather(0, 0)

    for sl in range(2):
        @pl.when((i + 1 < n) & (slot != sl))
        def _():
            gather(i + 1, sl)

    for sl in range(2):
        @pl.when((i + 2 < n) & (slot == sl))
        def _():
            idx_copy(i + 2, sl).start()

    for k in range(TOP_K):
        pltpu.make_async_copy(rows_s.at[1 - slot, k], rows_s.at[slot, k], sem_d.at[slot]).wait()

    def rows(k):
        return rows_s[slot, k].reshape(tc, rows_s.shape[-1])

    meta = meta_ref[...]
    moe = meta[:, TOP_K:TOP_K + 1] * rows(0)
    for k in range(1, TOP_K):
        moe = moe + meta[:, TOP_K + k:TOP_K + k + 1] * rows(k)
    o_ref[...] = _rms(x1_ref[...] + g2_ref[0] * moe, fw_ref[...])


def _combine(x1, meta, g2, final_w, dest_flat, y, t_seq, tc):
    n, d = x1.shape
    return pl.pallas_call(
        _combine_kernel, grid=(n // tc,),
        in_specs=[pl.BlockSpec((tc, d), lambda i: (i, 0)),
                  pl.BlockSpec((tc, LANES), lambda i: (i, 0)),
                  pl.BlockSpec((1, 1, d), lambda i: (i * tc // t_seq, 0, 0)),
                  pl.BlockSpec((1, d), lambda i: (0, 0)),
                  pl.BlockSpec(memory_space=pl.ANY),
                  pl.BlockSpec(memory_space=pl.ANY)],
        out_specs=pl.BlockSpec((tc, d), lambda i: (i, 0)),
        out_shape=jax.ShapeDtypeStruct((n, d), F32),
        scratch_shapes=[pltpu.VMEM((2, TOP_K, tc // SUBLANES, SUBLANES, d), F32),
                        pltpu.SMEM((tc * TOP_K,), jnp.int32), pltpu.SMEM((tc * TOP_K,), jnp.int32),
                        pltpu.SemaphoreType.DMA((2,)), pltpu.SemaphoreType.DMA((2,))],
        compiler_params=_params("arbitrary"), name="moe_combine",
    )(x1, meta, g2, final_w, dest_flat, y)


def _gate_weights(wa, wx):
    _, heads, hd, _ = wa.shape
    hh = heads // 2
    eye = jnp.eye(hh, dtype=wa.dtype)

    def blockdiag(wsel):
        return jnp.einsum('hij,hg->higj', wsel, eye).reshape(hh * hd, hh * hd)

    halves = []
    for s in range(2):
        sl = slice(s * hh, (s + 1) * hh)
        halves.append(jnp.concatenate([blockdiag(wa[0, sl]), blockdiag(wx[0, sl]),
                                       blockdiag(wa[1, sl]), blockdiag(wx[1, sl])], axis=1))
    return jnp.stack(halves).astype(BF16)


def kernel(x, c, ctx, c_ctx, norm1_w, norm2_w, w_ada, b_ada, w_in, hg_lb_logits, hg_norm_w, rg_conv_w, rg_conv_b,
           rg_wa, rg_ba, rg_wx, rg_bx, rg_lambda, w_out, router_w, router_b, w_gate_up, b_gate_up, w_down,
           b_down, final_norm_w):
    b, t, d = x.shape
    tcx = ctx.shape[1]
    n_exp = router_w.shape[-1]
    n_tok = b * t
    depth = w_in.shape[0]
    lb_all = jnp.cumsum(jax.nn.softmax(hg_lb_logits.astype(F32), axis=0), axis=0)

    for l in range(depth):
        assert l == depth - 1, "context stream update of non-final layers is not implemented"
        pad = (-(b + 1)) % 8
        c_all = jnp.concatenate([c, c_ctx[None], jnp.zeros((pad, d), F32)], axis=0)
        mod = _mod(c_all, w_ada[l], b_ada[l])
        sh1, sc1, g1, sh2, sc2, g2 = [m[:b, None, :] for m in jnp.split(mod, 6, axis=-1)]
        csh1, csc1 = [m[b:b + 1, None, :] for m in jnp.split(mod, 6, axis=-1)[:2]]

        w_in_b = w_in[l].astype(BF16)
        nw1 = norm1_w[l].reshape(1, d)
        tm = min(512, t)
        p_lat = _inproj(x, sh1, sc1, nw1, w_in_b, tm)
        p_ctx = _inproj(ctx, csh1, csc1, nw1, w_in_b, min(256, tcx))

        hg = _hgrn2(p_lat, p_ctx, lb_all[l], hg_norm_w[l].reshape(1, -1))
        ch = rg_conv_w.shape[-1]
        wg = _gate_weights(rg_wa[l], rg_wx[l])
        bg = jnp.concatenate([rg_ba[l, 0], rg_bx[l, 0], rg_ba[l, 1], rg_bx[l, 1]]).reshape(1, 4 * ch)
        rg = _rglru(p_lat, p_ctx, rg_conv_w[l], rg_conv_b[l].reshape(1, ch), wg, bg, rg_lambda[l])

        x1, h2, meta, cnt = _mix(hg, rg, x, g1, sh2, sc2, norm2_w[l].reshape(1, d), w_out[l].astype(BF16),
                                 router_w[l].astype(BF16), router_b[l].reshape(1, n_exp), tm)

        bm = 512
        meta2 = meta.reshape(n_tok, LANES)
        idx = meta2[:, 0:TOP_K].astype(jnp.int32)
        rank = meta2[:, 2 * TOP_K:3 * TOP_K].astype(jnp.int32)
        counts = cnt[0, :n_exp].astype(jnp.int32)
        padded = (counts + bm - 1) // bm * bm
        pad_end = jnp.cumsum(padded)
        pad_start = pad_end - padded
        dest = (pad_start[idx] + rank).reshape(-1)
        n_blocks = -(-n_tok * TOP_K // bm) + n_exp
        blk_start = jnp.arange(n_blocks, dtype=jnp.int32) * bm
        blk_expert = jnp.minimum(jnp.sum(blk_start[:, None] >= pad_end[None, :], axis=1), n_exp - 1).astype(jnp.int32)
        n_used = (pad_end[-1:] // bm).astype(jnp.int32)

        tdc = min(256, t)
        n_slots = n_blocks * bm
        tail = jnp.stack([pad_end[-1], (n_slots - pad_end[-1]) // (bm // 2)]).astype(jnp.int32)
        xs = _dispatch(h2.reshape(n_tok, d), dest, pad_start + counts, padded - counts, tail, n_slots,
                       min(512, t), bm)
        y = _experts(xs, blk_expert, n_used, w_gate_up[l], b_gate_up[l], w_down[l], b_down[l], bm)
        out = _combine(x1.reshape(n_tok, d), meta2, g2, final_norm_w.reshape(1, d), dest, y, t, tdc)
        return out.reshape(b, t, d)
```

```python
import functools

import jax
import jax.numpy as jnp
from jax import lax
from jax.experimental import pallas as pl
from jax.experimental.pallas import tpu as pltpu

GRID_W = 64
HG_HEADS = 4
HG_CHUNK = 32
RG_HEADS = 8
RG_CONV = 4
RG_C = 8.0
TOP_K = 4
SWIGLU_LIMIT = 7.0
SWIGLU_ALPHA = 1.702
EPS = 1e-6

LANES = 128
SUBLANES = 8
VMEM_LIMIT = 56 * 1024 * 1024

F32 = jnp.float32
BF16 = jnp.bfloat16
HIGHEST = lax.Precision.HIGHEST


def _params(*sem):
    return pltpu.CompilerParams(dimension_semantics=sem, vmem_limit_bytes=VMEM_LIMIT)


def _sigmoid(x):
    return 0.5 * jnp.tanh(0.5 * x) + 0.5


def _silu(x):
    return x * _sigmoid(x)


def _rms(x, w):
    return x * lax.rsqrt(jnp.mean(x * x, axis=-1, keepdims=True) + EPS) * w


def _dot(a, b):
    return jnp.dot(a, b, preferred_element_type=F32)


def _dot_nt(a, b):
    return lax.dot_general(a, b, (((1,), (1,)), ((), ())), preferred_element_type=F32)


def _mod_kernel(c_ref, w_ref, b_ref, o_ref):
    o_ref[...] = jnp.dot(_silu(c_ref[...]), w_ref[...], preferred_element_type=F32,
                         precision=HIGHEST) + b_ref[...]


def _mod(c_all, w_ada, b_ada):
    r, d = c_all.shape
    n = w_ada.shape[1]
    tn = n // 4
    return pl.pallas_call(
        _mod_kernel, grid=(n // tn,),
        in_specs=[pl.BlockSpec((r, d), lambda j: (0, 0)),
                  pl.BlockSpec((d, tn), lambda j: (0, j)),
                  pl.BlockSpec((1, tn), lambda j: (0, j))],
        out_specs=pl.BlockSpec((r, tn), lambda j: (0, j)),
        out_shape=jax.ShapeDtypeStruct((r, n), F32),
        compiler_params=_params("arbitrary"), name="adaln_mod",
    )(c_all, w_ada, b_ada.reshape(1, n))


def _inproj_kernel(x_ref, sh_ref, sc_ref, nw_ref, w_ref, o_ref):
    h = _rms(x_ref[0], nw_ref[...]) * (1.0 + sc_ref[0]) + sh_ref[0]
    o_ref[0] = _dot(h.astype(BF16), w_ref[...])


def _inproj(x, shift, scale, norm_w, w_bf16, tm):
    b, t, d = x.shape
    n = w_bf16.shape[1]
    per_batch = shift.shape[0] == b
    mod_map = (lambda i, j: (i, 0, 0)) if per_batch else (lambda i, j: (0, 0, 0))
    return pl.pallas_call(
        _inproj_kernel, grid=(b, t // tm),
        in_specs=[pl.BlockSpec((1, tm, d), lambda i, j: (i, j, 0)),
                  pl.BlockSpec((1, 1, d), mod_map),
                  pl.BlockSpec((1, 1, d), mod_map),
                  pl.BlockSpec((1, d), lambda i, j: (0, 0)),
                  pl.BlockSpec((d, n), lambda i, j: (0, 0))],
        out_specs=pl.BlockSpec((1, tm, n), lambda i, j: (i, j, 0)),
        out_shape=jax.ShapeDtypeStruct((b, t, n), F32),
        compiler_params=_params("arbitrary", "arbitrary"), name="norm_inproj",
    )(x, shift, scale, norm_w, w_bf16)


def _split_bf16(x):
    hi = x.astype(BF16)
    return hi, (x - hi.astype(F32)).astype(BF16)


def _hg_kernel(q_ref, v_ref, zf_ref, zb_ref, g_ref, cv_ref, czf_ref, czb_ref, lb_ref, nw_ref, o_ref,
               o_s, qd_s, u_s, dec_s, st_s, cu_s, cdec_s, tot_s, *, t_lat, t_ctx):
    c = HG_CHUNK
    dk = q_ref.shape[-1]
    lb = lb_ref[...]
    lbf, lbb = lb[0:1], lb[1:2]

    def prep(rb, zf, zb, q, v, u_out, dec_out, blk, r0):
        nc = rb // c
        ri = lax.broadcasted_iota(jnp.int32, (rb, rb), 0)
        ci = lax.broadcasted_iota(jnp.int32, (rb, rb), 1)
        same = (ri // c) == (ci // c)
        low = same & (ci <= ri)
        upp = same & (ci >= ri)
        ff = lbf + (1.0 - lbf) * _sigmoid(zf)
        fb = lbb + (1.0 - lbb) * _sigmoid(zb)
        lgf, lgb = jnp.log(ff), jnp.log(fb)
        rhs = jnp.concatenate([*_split_bf16(lgf), *_split_bf16(lgb)], axis=1)
        pre = _dot(low.astype(BF16), rhs)
        bcf = pre[:, :dk] + pre[:, dk:2 * dk]
        pfb = pre[:, 2 * dk:3 * dk] + pre[:, 3 * dk:]
        half = tot_s.shape[0] // 2
        tot_s[0:rb, :] = bcf
        tot_s[half:half + rb, :] = pfb
        totf = tot_s[pl.ds(c - 1, nc, stride=c), :]
        totb = tot_s[pl.ds(half + c - 1, nc, stride=c), :]

        def spread(tot):
            return jnp.broadcast_to(tot[:, None, :], (nc, c, dk)).reshape(rb, dk)

        remf = spread(totf) - bcf
        remb = pfb - lgb
        bcb = spread(totb) - remb
        kkf, kkb = 1.0 - ff, 1.0 - fb
        kef = (kkf * jnp.exp(remf)).astype(BF16)
        keb = (kkb * jnp.exp(remb)).astype(BF16)
        chunk_of_row = lax.broadcasted_iota(jnp.int32, (rb, dk), 0) // c
        zero = jnp.zeros((rb, dk), BF16)
        keys = jnp.concatenate([jnp.where(chunk_of_row == j, ke, zero) for ke in (kef, keb) for j in range(nc)],
                               axis=1)
        u_all = _dot(v.T.astype(BF16), keys)
        c0 = blk * nc
        for d in range(2):
            for j in range(nc):
                u_out[d, c0 + j] = u_all[:, (d * nc + j) * dk:(d * nc + j + 1) * dk]
        dec_out[0, pl.ds(pl.multiple_of(c0, nc), nc), :] = jnp.exp(totf)
        dec_out[1, pl.ds(pl.multiple_of(c0, nc), nc), :] = jnp.exp(totb)
        if q is None:
            return
        sq = _silu(q)
        qdf = (sq * jnp.exp(bcf)).astype(BF16)
        qdb = (sq * jnp.exp(bcb)).astype(BF16)
        kdf = (kkf * jnp.exp(-bcf)).astype(BF16)
        kdb = (kkb * jnp.exp(-bcb)).astype(BF16)
        p = jnp.where(low, _dot_nt(qdf, kdf), 0.0) + jnp.where(upp, _dot_nt(qdb, kdb), 0.0)
        o_s[pl.ds(r0, rb), :] = _dot(p.astype(BF16), v.astype(BF16))
        qd_s[pl.ds(r0, rb), 0:dk] = qdf
        qd_s[pl.ds(r0, rb), dk:2 * dk] = qdb

    def scan(n, ur, decr, keep, carry):
        def body(i, carry):
            sf, sb = carry
            j = n - 1 - i
            if keep:
                st_s[i, :, 0:dk] = sf.astype(BF16)
                st_s[j, :, dk:2 * dk] = sb.astype(BF16)
            return sf * decr[0, pl.ds(i, 1), :] + ur[0, i], sb * decr[1, pl.ds(j, 1), :] + ur[1, j]

        return lax.fori_loop(0, n, body, carry, unroll=2)

    rbc = min(256, t_ctx)
    for blk in range(t_ctx // rbc):
        r0 = blk * rbc
        prep(rbc, czf_ref[0, r0:r0 + rbc, :], czb_ref[0, r0:r0 + rbc, :], None, cv_ref[0, r0:r0 + rbc, :],
             cu_s, cdec_s, blk, r0)
    zero = jnp.zeros((dk, dk), F32)
    carry = scan(t_ctx // c, cu_s, cdec_s, False, (zero, zero))

    rbl = min(256, t_lat)

    def lat_prep(blk, _):
        r0 = pl.multiple_of(blk * rbl, rbl)
        rows = pl.ds(r0, rbl)
        prep(rbl, zf_ref[0, rows, :], zb_ref[0, rows, :], q_ref[0, rows, :], v_ref[0, rows, :], u_s, dec_s, blk, r0)
        return 0

    lax.fori_loop(0, t_lat // rbl, lat_prep, 0, unroll=4)
    scan(t_lat // c, u_s, dec_s, True, carry)

    def finish(blk, _):
        r0 = pl.multiple_of(blk * rbl, rbl)
        inter = [_dot_nt(qd_s[pl.ds(r0 + j * c, c), :], st_s[blk * (rbl // c) + j]) for j in range(rbl // c)]
        o = o_s[pl.ds(r0, rbl), :] + jnp.concatenate(inter, axis=0)
        o_ref[0, pl.ds(r0, rbl), :] = _rms(o, nw_ref[...]) * _silu(g_ref[0, pl.ds(r0, rbl), :])
        return 0

    lax.fori_loop(0, t_lat // rbl, finish, 0, unroll=4)


def _hgrn2(p_lat, p_ctx, lb, norm_w):
    b, t, _ = p_lat.shape
    tc = p_ctx.shape[1]
    hw = lb.shape[1]
    dk = hw // HG_HEADS
    nh = HG_HEADS
    c = HG_CHUNK

    def col(k, tt):
        return pl.BlockSpec((1, tt, dk), lambda i, h, k=k: (i, 0, k * nh + h))

    return pl.pallas_call(
        functools.partial(_hg_kernel, t_lat=t, t_ctx=tc), grid=(b, nh),
        in_specs=[col(0, t), col(1, t), col(2, t), col(3, t), col(4, t),
                  col(1, tc), col(2, tc), col(3, tc),
                  pl.BlockSpec((2, dk), lambda i, h: (0, h)),
                  pl.BlockSpec((1, dk), lambda i, h: (0, h))],
        out_specs=pl.BlockSpec((1, t, dk), lambda i, h: (i, 0, h)),
        out_shape=jax.ShapeDtypeStruct((b, t, hw), F32),
        scratch_shapes=[pltpu.VMEM((t, dk), F32),
                        pltpu.VMEM((t, 2 * dk), BF16),
                        pltpu.VMEM((2, t // c, dk, dk), F32),
                        pltpu.VMEM((2, t // c, dk), F32),
                        pltpu.VMEM((t // c, dk, 2 * dk), BF16),
                        pltpu.VMEM((2, tc // c, dk, dk), F32),
                        pltpu.VMEM((2, tc // c, dk), F32),
                        pltpu.VMEM((2 * min(256, max(t, tc)), dk), F32)],
        compiler_params=_params("arbitrary", "arbitrary"), name="hgrn2",
    )(p_lat, p_lat, p_lat, p_lat, p_lat, p_ctx, p_ctx, p_ctx, lb, norm_w)


def _shift_rows(x, k):
    n = x.shape[0]
    y = pltpu.roll(x, k % n, 0)
    r = lax.broadcasted_iota(jnp.int32, x.shape, 0)
    return jnp.where((r >= k) & (r < n + k), y, 0.0)


def _rg_kernel(rx_ref, rgate_ref, crx_ref, cw_ref, cb_ref, wg_ref, bg_ref, lam_ref, o_ref,
               xc_s, af_s, bf_s, ab_s, bb_s, hf_s, hb_s, caf_s, cbf_s, cab_s, cbb_s, *, t_lat, t_ctx):
    w = GRID_W
    rows = t_lat // w
    ch = rx_ref.shape[-1]
    half = ch // 2
    cw = cw_ref[...]
    cb = cb_ref[...]
    bg = bg_ref[...]
    nl = -lam_ref[...]
    cdec = -RG_C * (jnp.maximum(nl, 0.0) + jnp.log1p(jnp.exp(-jnp.abs(nl))))

    def conv(xm2, xm1, x0, xp1):
        return cb + cw[0:1] * xm2 + cw[1:2] * xm1 + cw[2:3] * x0 + cw[3:4] * xp1

    def gates(xc):
        xb = xc.astype(BF16)
        g0 = _dot(xb[:, :half], wg_ref[0])
        g1 = _dot(xb[:, half:], wg_ref[1])
        outs = []
        for d in range(2):
            pre = []
            for s in (2 * d, 2 * d + 1):
                pre.append(jnp.concatenate([g0[:, s * half:(s + 1) * half], g1[:, s * half:(s + 1) * half]],
                                           axis=1) + bg[:, s * ch:(s + 1) * ch])
            log_a = cdec[d:d + 1] * _sigmoid(pre[0])
            a = jnp.exp(log_a)
            mult = jnp.sqrt(-jnp.tanh(log_a) * (a * a + 1.0))
            outs += [a, mult * _sigmoid(pre[1]) * xc]
        return outs

    xctx = crx_ref[0]
    xcc = conv(_shift_rows(xctx, 2), _shift_rows(xctx, 1), xctx, _shift_rows(xctx, -1))
    caf_s[...], cbf_s[...], cab_s[...], cbb_s[...] = gates(xcc)

    def cstep(i, carry):
        hf, hb = carry
        hf = caf_s[pl.ds(i, 1), :] * hf + cbf_s[pl.ds(i, 1), :]
        j = t_ctx - 1 - i
        hb = cab_s[pl.ds(j, 1), :] * hb + cbb_s[pl.ds(j, 1), :]
        return hf, hb

    zrow = jnp.zeros((1, ch), F32)
    hf0, hb0 = lax.fori_loop(0, t_ctx, cstep, (zrow, zrow), unroll=8)

    def slab(rr):
        if 0 <= rr < rows:
            return rx_ref[0, rr * w:(rr + 1) * w, :]
        if rr < 0:
            return _shift_rows(rx_ref[0, (rr + rows) * w:(rr + rows + 1) * w, :], 1)
        return _shift_rows(rx_ref[0, (rr - rows) * w:(rr - rows + 1) * w, :], -1)

    for r in range(rows):
        xc_s[r * w:(r + 1) * w, :] = conv(slab(r - 2), slab(r - 1), slab(r), slab(r + 1))

    mb = min(256, t_lat)

    def gbody(i, _):
        r0 = pl.multiple_of(i * mb, mb)
        a_f, b_f, a_b, b_b = gates(xc_s[pl.ds(r0, mb), :])
        af_s[pl.ds(r0, mb), :] = a_f
        bf_s[pl.ds(r0, mb), :] = b_f
        ab_s[pl.ds(r0, mb), :] = a_b
        bb_s[pl.ds(r0, mb), :] = b_b
        return 0

    lax.fori_loop(0, t_lat // mb, gbody, 0)

    def l1(i, _):
        pf = pl.multiple_of(i * w, w)
        qf = pl.multiple_of((i - 1) * w, w)
        a = af_s[pl.ds(pf, w), :]
        af_s[pl.ds(pf, w), :] = a * af_s[pl.ds(qf, w), :]
        bf_s[pl.ds(pf, w), :] = a * bf_s[pl.ds(qf, w), :] + bf_s[pl.ds(pf, w), :]
        pb = pl.multiple_of((rows - 1 - i) * w, w)
        qb = pl.multiple_of((rows - i) * w, w)
        a = ab_s[pl.ds(pb, w), :]
        ab_s[pl.ds(pb, w), :] = a * ab_s[pl.ds(qb, w), :]
        bb_s[pl.ds(pb, w), :] = a * bb_s[pl.ds(qb, w), :] + bb_s[pl.ds(pb, w), :]
        return 0

    lax.fori_loop(1, rows, l1, 0)

    last = (rows - 1) * w

    def l2(i, carry):
        hf, hb = carry
        hf_s[pl.ds(i, 1), :] = hf
        hf = af_s[pl.ds(last + i, 1), :] * hf + bf_s[pl.ds(last + i, 1), :]
        j = w - 1 - i
        hb_s[pl.ds(j, 1), :] = hb
        hb = ab_s[pl.ds(j, 1), :] * hb + bb_s[pl.ds(j, 1), :]
        return hf, hb

    lax.fori_loop(0, w, l2, (hf0, hb0))

    def l3(i, _):
        p = pl.multiple_of(i * w, w)
        h = (af_s[pl.ds(p, w), :] * hf_s[...] + bf_s[pl.ds(p, w), :]
             + ab_s[pl.ds(p, w), :] * hb_s[...] + bb_s[pl.ds(p, w), :])
        o_ref[0, pl.ds(p, w), :] = jax.nn.gelu(rgate_ref[0, pl.ds(p, w), :]) * h
        return 0

    lax.fori_loop(0, rows, l3, 0)


def _rglru(p_lat, p_ctx, conv_w, conv_b, wg, bg, lam):
    b, t, _ = p_lat.shape
    tc = p_ctx.shape[1]
    ch = conv_w.shape[1]
    rx_blk = (p_lat.shape[2] - 2 * ch) // ch
    full = lambda shape: pl.BlockSpec(shape, lambda i: (0,) * len(shape))
    big = lambda: pltpu.VMEM((t, ch), F32)
    small = lambda: pltpu.VMEM((tc, ch), F32)
    return pl.pallas_call(
        functools.partial(_rg_kernel, t_lat=t, t_ctx=tc), grid=(b,),
        in_specs=[pl.BlockSpec((1, t, ch), lambda i: (i, 0, rx_blk)),
                  pl.BlockSpec((1, t, ch), lambda i: (i, 0, rx_blk + 1)),
                  pl.BlockSpec((1, tc, ch), lambda i: (i, 0, rx_blk)),
                  full(conv_w.shape), full(conv_b.shape), full(wg.shape), full(bg.shape), full(lam.shape)],
        out_specs=pl.BlockSpec((1, t, ch), lambda i: (i, 0, 0)),
        out_shape=jax.ShapeDtypeStruct((b, t, ch), F32),
        scratch_shapes=[big(), big(), big(), big(), big(),
                        pltpu.VMEM((GRID_W, ch), F32), pltpu.VMEM((GRID_W, ch), F32),
                        small(), small(), small(), small()],
        compiler_params=_params("arbitrary"), name="rglru",
    )(p_lat, p_lat, p_ctx, conv_w, conv_b, wg, bg, lam)


def _mix_kernel(hg_ref, rg_ref, x_ref, g1_ref, sh_ref, sc_ref, nw_ref, wo_ref, rw_ref, rb_ref,
                x1_ref, h2_ref, meta_ref, cnt_ref, base_s, *, n_exp):
    tm = x_ref.shape[1]

    @pl.when((pl.program_id(0) == 0) & (pl.program_id(1) == 0))
    def _():
        base_s[...] = jnp.zeros_like(base_s)

    hcat = jnp.concatenate([hg_ref[0], rg_ref[0]], axis=1).astype(BF16)
    x1 = x_ref[0] + g1_ref[0] * _dot(hcat, wo_ref[...])
    x1_ref[0] = x1
    h2 = _rms(x1, nw_ref[...]) * (1.0 + sc_ref[0]) + sh_ref[0]
    h2_ref[0] = h2
    logits = _dot(h2.astype(BF16), rw_ref[...]) + rb_ref[...]

    lane_e = lax.broadcasted_iota(jnp.int32, (tm, n_exp), 1).astype(F32)
    vals, idxs = [], []
    cur = logits
    for _ in range(TOP_K):
        m = jnp.max(cur, axis=1, keepdims=True)
        ix = jnp.min(jnp.where(cur == m, lane_e, float(n_exp)), axis=1, keepdims=True)
        vals.append(m)
        idxs.append(ix)
        cur = jnp.where(lane_e == ix, -jnp.inf, cur)
    ex = [jnp.exp(v - vals[0]) for v in vals]
    den = ex[0] + ex[1] + ex[2] + ex[3]

    lane = lax.broadcasted_iota(jnp.int32, (tm, LANES), 1)
    lane_f = lane.astype(F32)
    onehot = jnp.zeros((tm, LANES), F32)
    for k in range(TOP_K):
        onehot = jnp.where(lane_f == idxs[k] + float(k * n_exp), 1.0, onehot)
    ri = lax.broadcasted_iota(jnp.int32, (tm, tm), 0)
    ci = lax.broadcasted_iota(jnp.int32, (tm, tm), 1)
    prefix = _dot((ci < ri).astype(BF16), onehot.astype(BF16))
    tot = jnp.broadcast_to(prefix[tm - 1:tm] + onehot[tm - 1:tm], (8, LANES))
    lane8 = lax.broadcasted_iota(jnp.int32, (8, LANES), 1)
    off = base_s[...]
    tot_all = tot
    for j in range(1, TOP_K):
        rolled = pltpu.roll(tot, j * n_exp, 1)
        off = off + jnp.where(lane8 >= j * n_exp, rolled, 0.0)
        tot_all = tot_all + rolled
    pos = onehot * (prefix + off[0:1])
    meta = jnp.zeros((tm, LANES), F32)
    for k in range(TOP_K):
        in_k = (lane >= k * n_exp) & (lane < (k + 1) * n_exp)
        rank = jnp.sum(jnp.where(in_k, pos, 0.0), axis=1, keepdims=True)
        meta = jnp.where(lane == k, idxs[k], meta)
        meta = jnp.where(lane == TOP_K + k, ex[k] / den, meta)
        meta = jnp.where(lane == 2 * TOP_K + k, rank, meta)
    meta_ref[0] = meta
    base_s[...] = base_s[...] + tot_all
    cnt_ref[...] = base_s[...]


def _mix(hg, rg, x, g1, sh2, sc2, norm_w, wo_bf16, router_w, router_b, tm):
    b, t, d = x.shape
    hw = hg.shape[2]
    n_exp = router_w.shape[1]
    assert TOP_K * n_exp == LANES
    tok = lambda last: pl.BlockSpec((1, tm, last), lambda i, j: (i, j, 0))
    per_b = pl.BlockSpec((1, 1, d), lambda i, j: (i, 0, 0))
    full = lambda shape: pl.BlockSpec(shape, lambda i, j: (0,) * len(shape))
    return pl.pallas_call(
        functools.partial(_mix_kernel, n_exp=n_exp), grid=(b, t // tm),
        in_specs=[tok(hw), tok(hw), tok(d), per_b, per_b, per_b, full((1, d)),
                  full(wo_bf16.shape), full(router_w.shape), full((1, n_exp))],
        out_specs=[tok(d), tok(d), tok(LANES), pl.BlockSpec((8, LANES), lambda i, j: (0, 0))],
        out_shape=[jax.ShapeDtypeStruct((b, t, d), F32), jax.ShapeDtypeStruct((b, t, d), F32),
                   jax.ShapeDtypeStruct((b, t, LANES), F32), jax.ShapeDtypeStruct((8, LANES), F32)],
        scratch_shapes=[pltpu.VMEM((8, LANES), F32)],
        compiler_params=_params("arbitrary", "arbitrary"), name="outproj_router",
    )(hg, rg, x, g1, sh2, sc2, norm_w, wo_bf16, router_w, router_b)


def _dispatch_kernel(fill_off, fill_n, tail, h_ref, dest_hbm, xs_out, idx0_s, idx1_s, zero_s, sem_i, sem_d, sem_z, *,
                     pad_bits):
    td = h_ref.shape[0] * SUBLANES
    n_idx = td * TOP_K
    n_exp = fill_n.shape[0]
    zrows = zero_s.shape[0]
    i = pl.program_id(0)
    slot = i % 2
    idx_s = (idx0_s, idx1_s)

    def idx_copy(j, sl):
        return pltpu.make_async_copy(dest_hbm.at[pl.ds(j * n_idx, n_idx)], idx_s[sl], sem_i.at[sl])

    @pl.when(i == 0)
    def _():
        idx_copy(0, 0).start()

    for sl in range(2):
        @pl.when((i + 1 < pl.num_programs(0)) & (slot != sl))
        def _():
            idx_copy(i + 1, sl).start()

    def fill(wait):
        def go(copy, cond):
            @pl.when(cond)
            def _():
                copy.wait() if wait else copy.start()

        def per_expert(e, _):
            off = fill_off[e]
            npad = fill_n[e]
            n_single = npad & (SUBLANES - 1)
            for r in range(SUBLANES - 1):
                go(pltpu.make_async_copy(zero_s.at[pl.ds(0, 1), :], xs_out.at[pl.ds(off + r, 1), :], sem_z),
                   r < n_single)
            off = pl.multiple_of(off + n_single, SUBLANES)
            for bit in reversed(range(SUBLANES.bit_length() - 1, pad_bits)):
                size = 1 << bit
                go(pltpu.make_async_copy(zero_s.at[pl.ds(0, size), :], xs_out.at[pl.ds(off, size), :], sem_z),
                   (npad & size) != 0)
                off = pl.multiple_of(off + (npad & size), SUBLANES)
            return 0

        lax.fori_loop(0, n_exp, per_expert, 0)

        def per_tail_chunk(j, _):
            off = pl.multiple_of(tail[0] + j * zrows, zrows)
            copy = pltpu.make_async_copy(zero_s, xs_out.at[pl.ds(off, zrows), :], sem_z)
            copy.wait() if wait else copy.start()
            return 0

        lax.fori_loop(0, tail[1], per_tail_chunk, 0)

    @pl.when(i == 0)
    def _():
        zero_s[...] = jnp.zeros_like(zero_s)
        fill(False)

    for sl in range(2):
        @pl.when(slot == sl)
        def _():
            idx_copy(i, sl).wait()

            def issue(g, _):
                for u in range(SUBLANES):
                    for k in range(TOP_K):
                        dst = idx_s[sl][g * (SUBLANES * TOP_K) + u * TOP_K + k]
                        pltpu.make_async_copy(h_ref.at[g, pl.ds(u, 1), :], xs_out.at[pl.ds(dst, 1), :],
                                              sem_d).start(priority=k % 2)
                return 0

            lax.fori_loop(0, td // SUBLANES, issue, 0)

    for _ in range(TOP_K * td // zrows):
        pltpu.make_async_copy(zero_s, xs_out.at[pl.ds(0, zrows), :], sem_d).wait()

    @pl.when(i == 0)
    def _():
        fill(True)


def _dispatch(h2, dest_flat, fill_off, fill_n, tail, n_slots, td, bm):
    n, d = h2.shape
    pad_bits = (bm - 1).bit_length()
    assert (TOP_K * td) % (bm // 2) == 0
    grid_spec = pltpu.PrefetchScalarGridSpec(
        num_scalar_prefetch=3, grid=(n // td,),
        in_specs=[pl.BlockSpec((td // SUBLANES, SUBLANES, d), lambda i, fo, fn, tl: (i, 0, 0)),
                  pl.BlockSpec(memory_space=pl.ANY)],
        out_specs=pl.BlockSpec(memory_space=pl.ANY),
        scratch_shapes=[pltpu.SMEM((td * TOP_K,), jnp.int32), pltpu.SMEM((td * TOP_K,), jnp.int32),
                        pltpu.VMEM((bm // 2, d), F32),
                        pltpu.SemaphoreType.DMA((2,)), pltpu.SemaphoreType.DMA, pltpu.SemaphoreType.DMA])
    return pl.pallas_call(
        functools.partial(_dispatch_kernel, pad_bits=pad_bits), grid_spec=grid_spec,
        out_shape=jax.ShapeDtypeStruct((n_slots, d), F32),
        compiler_params=_params("arbitrary"), name="moe_dispatch",
    )(fill_off, fill_n, tail, h2.reshape(n // SUBLANES, SUBLANES, d), dest_flat)


def _expert_kernel(be_ref, nu_ref, x_ref, wgu_ref, bgu_ref, wd_ref, bd_ref, y_ref, wgu_s, wd_s):
    i = pl.program_id(0)
    d_ff = wd_ref.shape[1]

    @pl.when(i >= nu_ref[0])
    def _():
        y_ref[...] = jnp.zeros_like(y_ref)

    @pl.when(i < nu_ref[0])
    def _():
        @pl.when((i == 0) | (be_ref[i] != be_ref[jnp.maximum(i - 1, 0)]))
        def _():
            wgu_s[...] = wgu_ref[0].astype(BF16)
            wd_s[...] = wd_ref[0].astype(BF16)

        gu = _dot(x_ref[...].astype(BF16), wgu_s[...]) + bgu_ref[0]
        gate = jnp.minimum(gu[:, :d_ff], SWIGLU_LIMIT)
        up = jnp.clip(gu[:, d_ff:], -SWIGLU_LIMIT, SWIGLU_LIMIT)
        act = gate * _sigmoid(SWIGLU_ALPHA * gate) * (up + 1.0)
        y_ref[...] = _dot(act.astype(BF16), wd_s[...]) + bd_ref[0]


def _experts(xs, blk_expert, n_used, wgu, bgu, wd, bd, bm):
    n_slots, d = xs.shape
    n_exp, _, f2 = wgu.shape
    d_ff = wd.shape[1]
    n_blocks = n_slots // bm
    row = lambda i, be, nu: (jnp.minimum(i, nu[0] - 1), 0)
    grid_spec = pltpu.PrefetchScalarGridSpec(
        num_scalar_prefetch=2, grid=(n_blocks,),
        in_specs=[pl.BlockSpec((bm, d), row),
                  pl.BlockSpec((1, d, f2), lambda i, be, nu: (be[i], 0, 0)),
                  pl.BlockSpec((1, 1, f2), lambda i, be, nu: (be[i], 0, 0)),
                  pl.BlockSpec((1, d_ff, d), lambda i, be, nu: (be[i], 0, 0)),
                  pl.BlockSpec((1, 1, d), lambda i, be, nu: (be[i], 0, 0))],
        out_specs=pl.BlockSpec((bm, d), lambda i, be, nu: (i, 0)),
        scratch_shapes=[pltpu.VMEM((d, f2), BF16), pltpu.VMEM((d_ff, d), BF16)])
    return pl.pallas_call(
        _expert_kernel, grid_spec=grid_spec,
        out_shape=jax.ShapeDtypeStruct((n_slots, d), F32),
        compiler_params=_params("arbitrary"), name="moe_experts",
    )(blk_expert, n_used, xs, wgu, bgu.reshape(n_exp, 1, f2), wd, bd.reshape(n_exp, 1, d))


def _combine_kernel(x1_ref, meta_ref, g2_ref, fw_ref, dest_hbm, y_hbm, o_ref, rows_s, idx0_s, idx1_s, sem_i, sem_d):
    tc = x1_ref.shape[0]
    n_idx = tc * TOP_K
    i = pl.program_id(0)
    n = pl.num_programs(0)
    slot = i % 2
    idx_s = (idx0_s, idx1_s)

    def idx_copy(j, sl):
        return pltpu.make_async_copy(dest_hbm.at[pl.ds(j * n_idx, n_idx)], idx_s[sl], sem_i.at[sl])

    def gather(j, sl):
        idx_copy(j, sl).wait()

        def issue(g, _):
            for u in range(SUBLANES):
                for k in range(TOP_K):
                    src = idx_s[sl][g * (SUBLANES * TOP_K) + u * TOP_K + k]
                    pltpu.make_async_copy(y_hbm.at[pl.ds(src, 1), :], rows_s.at[sl, k, g, pl.ds(u, 1), :],
                                          sem_d.at[sl]).start(priority=k % 2)
            return 0

        lax.fori_loop(0, tc // SUBLANES, issue, 0)

    @pl.when(i == 0)
    def _():
        idx_copy(0, 0).start()

        @pl.when(n > 1)
        def _():
            idx_copy(1, 1).start()

        gather(0, 0)

    for sl in range(2):
        @pl.when((i + 1 < n) & (slot != sl))
        def _():
            gather(i + 1, sl)

    for sl in range(2):
        @pl.when((i + 2 < n) & (slot == sl))
        def _():
            idx_copy(i + 2, sl).start()

    for k in range(TOP_K):
        pltpu.make_async_copy(rows_s.at[1 - slot, k], rows_s.at[slot, k], sem_d.at[slot]).wait()

    def rows(k):
        return rows_s[slot, k].reshape(tc, rows_s.shape[-1])

    meta = meta_ref[...]
    moe = meta[:, TOP_K:TOP_K + 1] * rows(0)
    for k in range(1, TOP_K):
        moe = moe + meta[:, TOP_K + k:TOP_K + k + 1] * rows(k)
    o_ref[...] = _rms(x1_ref[...] + g2_ref[0] * moe, fw_ref[...])


def _combine(x1, meta, g2, final_w, dest_flat, y, t_seq, tc):
    n, d = x1.shape
    return pl.pallas_call(
        _combine_kernel, grid=(n // tc,),
        in_specs=[pl.BlockSpec((tc, d), lambda i: (i, 0)),
                  pl.BlockSpec((tc, LANES), lambda i: (i, 0)),
                  pl.BlockSpec((1, 1, d), lambda i: (i * tc // t_seq, 0, 0)),
                  pl.BlockSpec((1, d), lambda i: (0, 0)),
                  pl.BlockSpec(memory_space=pl.ANY),
                  pl.BlockSpec(memory_space=pl.ANY)],
        out_specs=pl.BlockSpec((tc, d), lambda i: (i, 0)),
        out_shape=jax.ShapeDtypeStruct((n, d), F32),
        scratch_shapes=[pltpu.VMEM((2, TOP_K, tc // SUBLANES, SUBLANES, d), F32),
                        pltpu.SMEM((tc * TOP_K,), jnp.int32), pltpu.SMEM((tc * TOP_K,), jnp.int32),
                        pltpu.SemaphoreType.DMA((2,)), pltpu.SemaphoreType.DMA((2,))],
        compiler_params=_params("arbitrary"), name="moe_combine",
    )(x1, meta, g2, final_w, dest_flat, y)


def _gate_weights(wa, wx):
    _, heads, hd, _ = wa.shape
    hh = heads // 2
    eye = jnp.eye(hh, dtype=wa.dtype)

    def blockdiag(wsel):
        return jnp.einsum('hij,hg->higj', wsel, eye).reshape(hh * hd, hh * hd)

    halves = []
    for s in range(2):
        sl = slice(s * hh, (s + 1) * hh)
        halves.append(jnp.concatenate([blockdiag(wa[0, sl]), blockdiag(wx[0, sl]),
                                       blockdiag(wa[1, sl]), blockdiag(wx[1, sl])], axis=1))
    return jnp.stack(halves).astype(BF16)


def kernel(x, c, ctx, c_ctx, norm1_w, norm2_w, w_ada, b_ada, w_in, hg_lb_logits, hg_norm_w, rg_conv_w, rg_conv_b,
           rg_wa, rg_ba, rg_wx, rg_bx, rg_lambda, w_out, router_w, router_b, w_gate_up, b_gate_up, w_down,
           b_down, final_norm_w):
    b, t, d = x.shape
    tcx = ctx.shape[1]
    n_exp = router_w.shape[-1]
    n_tok = b * t
    depth = w_in.shape[0]
    lb_all = jnp.cumsum(jax.nn.softmax(hg_lb_logits.astype(F32), axis=0), axis=0)

    for l in range(depth):
        assert l == depth - 1, "context stream update of non-final layers is not implemented"
        pad = (-(b + 1)) % 8
        c_all = jnp.concatenate([c, c_ctx[None], jnp.zeros((pad, d), F32)], axis=0)
        mod = _mod(c_all, w_ada[l], b_ada[l])
        sh1, sc1, g1, sh2, sc2, g2 = [m[:b, None, :] for m in jnp.split(mod, 6, axis=-1)]
        csh1, csc1 = [m[b:b + 1, None, :] for m in jnp.split(mod, 6, axis=-1)[:2]]

        w_in_b = w_in[l].astype(BF16)
        nw1 = norm1_w[l].reshape(1, d)
        tm = min(512, t)
        p_lat = _inproj(x, sh1, sc1, nw1, w_in_b, tm)
        p_ctx = _inproj(ctx, csh1, csc1, nw1, w_in_b, min(256, tcx))

        hg = _hgrn2(p_lat, p_ctx, lb_all[l], hg_norm_w[l].reshape(1, -1))
        ch = rg_conv_w.shape[-1]
        wg = _gate_weights(rg_wa[l], rg_wx[l])
        bg = jnp.concatenate([rg_ba[l, 0], rg_bx[l, 0], rg_ba[l, 1], rg_bx[l, 1]]).reshape(1, 4 * ch)
        rg = _rglru(p_lat, p_ctx, rg_conv_w[l], rg_conv_b[l].reshape(1, ch), wg, bg, rg_lambda[l])

        x1, h2, meta, cnt = _mix(hg, rg, x, g1, sh2, sc2, norm2_w[l].reshape(1, d), w_out[l].astype(BF16),
                                 router_w[l].astype(BF16), router_b[l].reshape(1, n_exp), tm)

        bm = 512
        meta2 = meta.reshape(n_tok, LANES)
        idx = meta2[:, 0:TOP_K].astype(jnp.int32)
        rank = meta2[:, 2 * TOP_K:3 * TOP_K].astype(jnp.int32)
        counts = cnt[0, :n_exp].astype(jnp.int32)
        padded = (counts + bm - 1) // bm * bm
        pad_end = jnp.cumsum(padded)
        pad_start = pad_end - padded
        dest = (pad_start[idx] + rank).reshape(-1)
        n_blocks = -(-n_tok * TOP_K // bm) + n_exp
        blk_start = jnp.arange(n_blocks, dtype=jnp.int32) * bm
        blk_expert = jnp.minimum(jnp.sum(blk_start[:, None] >= pad_end[None, :], axis=1), n_exp - 1).astype(jnp.int32)
        n_used = (pad_end[-1:] // bm).astype(jnp.int32)

        tdc = min(256, t)
        n_slots = n_blocks * bm
        tail = jnp.stack([pad_end[-1], (n_slots - pad_end[-1]) // (bm // 2)]).astype(jnp.int32)
        xs = _dispatch(h2.reshape(n_tok, d), dest, pad_start + counts, padded - counts, tail, n_slots,
                       min(512, t), bm)
        y = _experts(xs, blk_expert, n_used, w_gate_up[l], b_gate_up[l], w_down[l], b_down[l], bm)
        out = _combine(x1.reshape(n_tok, d), meta2, g2, final_norm_w.reshape(1, d), dest, y, t, tdc)
        return out.reshape(b, t, d)
```

```python
import functools

import jax
import jax.numpy as jnp
from jax import lax
from jax.experimental import pallas as pl
from jax.experimental.pallas import tpu as pltpu

GRID_W = 64
HG_HEADS = 4
HG_CHUNK = 32
RG_HEADS = 8
RG_CONV = 4
RG_C = 8.0
TOP_K = 4
SWIGLU_LIMIT = 7.0
SWIGLU_ALPHA = 1.702
EPS = 1e-6

LANES = 128
SUBLANES = 8
VMEM_LIMIT = 56 * 1024 * 1024

TOKEN_TILE = 512
EXPERT_BLOCK = 512
DISPATCH_TILE = 1024
COMBINE_TILE = 512

F32 = jnp.float32
BF16 = jnp.bfloat16
HIGHEST = lax.Precision.HIGHEST


def _params(*sem):
    return pltpu.CompilerParams(dimension_semantics=sem, vmem_limit_bytes=VMEM_LIMIT)


def _sigmoid(x):
    return 0.5 * jnp.tanh(0.5 * x) + 0.5


def _silu(x):
    return x * _sigmoid(x)


def _rms(x, w):
    return x * lax.rsqrt(jnp.mean(x * x, axis=-1, keepdims=True) + EPS) * w


def _dot(a, b):
    return jnp.dot(a, b, preferred_element_type=F32)


def _dot_nt(a, b):
    return lax.dot_general(a, b, (((1,), (1,)), ((), ())), preferred_element_type=F32)


def _mod_kernel(c_ref, w_ref, b_ref, o_ref):
    o_ref[...] = jnp.dot(_silu(c_ref[...]), w_ref[...], preferred_element_type=F32,
                         precision=HIGHEST) + b_ref[...]


def _mod(c_all, w_ada, b_ada):
    r, d = c_all.shape
    n = w_ada.shape[1]
    tn = n // 4
    return pl.pallas_call(
        _mod_kernel, grid=(n // tn,),
        in_specs=[pl.BlockSpec((r, d), lambda j: (0, 0)),
                  pl.BlockSpec((d, tn), lambda j: (0, j)),
                  pl.BlockSpec((1, tn), lambda j: (0, j))],
        out_specs=pl.BlockSpec((r, tn), lambda j: (0, j)),
        out_shape=jax.ShapeDtypeStruct((r, n), F32),
        compiler_params=_params("arbitrary"), name="adaln_mod",
    )(c_all, w_ada, b_ada.reshape(1, n))


def _inproj_kernel(x_ref, sh_ref, sc_ref, nw_ref, w_ref, o_ref):
    h = _rms(x_ref[0], nw_ref[...]) * (1.0 + sc_ref[0]) + sh_ref[0]
    o_ref[0] = _dot(h.astype(BF16), w_ref[...])


def _inproj(x, shift, scale, norm_w, w_bf16, tm):
    b, t, d = x.shape
    n = w_bf16.shape[1]
    per_batch = shift.shape[0] == b
    mod_map = (lambda i, j: (i, 0, 0)) if per_batch else (lambda i, j: (0, 0, 0))
    return pl.pallas_call(
        _inproj_kernel, grid=(b, t // tm),
        in_specs=[pl.BlockSpec((1, tm, d), lambda i, j: (i, j, 0)),
                  pl.BlockSpec((1, 1, d), mod_map),
                  pl.BlockSpec((1, 1, d), mod_map),
                  pl.BlockSpec((1, d), lambda i, j: (0, 0)),
                  pl.BlockSpec((d, n), lambda i, j: (0, 0))],
        out_specs=pl.BlockSpec((1, tm, n), lambda i, j: (i, j, 0)),
        out_shape=jax.ShapeDtypeStruct((b, t, n), F32),
        compiler_params=_params("arbitrary", "arbitrary"), name="norm_inproj",
    )(x, shift, scale, norm_w, w_bf16)


def _split_bf16(x):
    hi = x.astype(BF16)
    return hi, (x - hi.astype(F32)).astype(BF16)


def _hg_kernel(q_ref, v_ref, zf_ref, zb_ref, g_ref, cv_ref, czf_ref, czb_ref, lb_ref, nw_ref, o_ref,
               o_s, qd_s, u_s, dec_s, st_s, cu_s, cdec_s, tot_s, *, t_lat, t_ctx):
    c = HG_CHUNK
    dk = q_ref.shape[-1]
    lb = lb_ref[...]
    lbf, lbb = lb[0:1], lb[1:2]

    def prep(rb, zf, zb, q, v, u_out, dec_out, blk, r0):
        nc = rb // c
        ri = lax.broadcasted_iota(jnp.int32, (rb, rb), 0)
        ci = lax.broadcasted_iota(jnp.int32, (rb, rb), 1)
        same = (ri // c) == (ci // c)
        low = same & (ci <= ri)
        upp = same & (ci >= ri)
        ff = lbf + (1.0 - lbf) * _sigmoid(zf)
        fb = lbb + (1.0 - lbb) * _sigmoid(zb)
        lgf, lgb = jnp.log(ff), jnp.log(fb)
        rhs = jnp.concatenate([*_split_bf16(lgf), *_split_bf16(lgb)], axis=1)
        pre = _dot(low.astype(BF16), rhs)
        bcf = pre[:, :dk] + pre[:, dk:2 * dk]
        pfb = pre[:, 2 * dk:3 * dk] + pre[:, 3 * dk:]
        half = tot_s.shape[0] // 2
        tot_s[0:rb, :] = bcf
        tot_s[half:half + rb, :] = pfb
        totf = tot_s[pl.ds(c - 1, nc, stride=c), :]
        totb = tot_s[pl.ds(half + c - 1, nc, stride=c), :]

        def spread(tot):
            return jnp.broadcast_to(tot[:, None, :], (nc, c, dk)).reshape(rb, dk)

        remf = spread(totf) - bcf
        remb = pfb - lgb
        bcb = spread(totb) - remb
        kkf, kkb = 1.0 - ff, 1.0 - fb
        kef = (kkf * jnp.exp(remf)).astype(BF16)
        keb = (kkb * jnp.exp(remb)).astype(BF16)
        chunk_of_row = lax.broadcasted_iota(jnp.int32, (rb, dk), 0) // c
        zero = jnp.zeros((rb, dk), BF16)
        keys = jnp.concatenate([jnp.where(chunk_of_row == j, ke, zero) for ke in (kef, keb) for j in range(nc)],
                               axis=1)
        u_all = _dot(v.T.astype(BF16), keys)
        c0 = blk * nc
        for d in range(2):
            for j in range(nc):
                u_out[d, c0 + j] = u_all[:, (d * nc + j) * dk:(d * nc + j + 1) * dk]
        dec_out[0, pl.ds(pl.multiple_of(c0, nc), nc), :] = jnp.exp(totf)
        dec_out[1, pl.ds(pl.multiple_of(c0, nc), nc), :] = jnp.exp(totb)
        if q is None:
            return
        sq = _silu(q)
        qdf = (sq * jnp.exp(bcf)).astype(BF16)
        qdb = (sq * jnp.exp(bcb)).astype(BF16)
        kdf = (kkf * jnp.exp(-bcf)).astype(BF16)
        kdb = (kkb * jnp.exp(-bcb)).astype(BF16)
        p = jnp.where(low, _dot_nt(qdf, kdf), 0.0) + jnp.where(upp, _dot_nt(qdb, kdb), 0.0)
        o_s[pl.ds(r0, rb), :] = _dot(p.astype(BF16), v.astype(BF16))
        qd_s[pl.ds(r0, rb), 0:dk] = qdf
        qd_s[pl.ds(r0, rb), dk:2 * dk] = qdb

    def scan(n, ur, decr, keep, carry):
        def body(i, carry):
            sf, sb = carry
            j = n - 1 - i
            if keep:
                st_s[i, :, 0:dk] = sf.astype(BF16)
                st_s[j, :, dk:2 * dk] = sb.astype(BF16)
            return sf * decr[0, pl.ds(i, 1), :] + ur[0, i], sb * decr[1, pl.ds(j, 1), :] + ur[1, j]

        return lax.fori_loop(0, n, body, carry, unroll=2)

    rbc = min(256, t_ctx)
    for blk in range(t_ctx // rbc):
        r0 = blk * rbc
        prep(rbc, czf_ref[0, r0:r0 + rbc, :], czb_ref[0, r0:r0 + rbc, :], None, cv_ref[0, r0:r0 + rbc, :],
             cu_s, cdec_s, blk, r0)
    zero = jnp.zeros((dk, dk), F32)
    carry = scan(t_ctx // c, cu_s, cdec_s, False, (zero, zero))

    rbl = min(256, t_lat)

    def lat_prep(blk, _):
        r0 = pl.multiple_of(blk * rbl, rbl)
        rows = pl.ds(r0, rbl)
        prep(rbl, zf_ref[0, rows, :], zb_ref[0, rows, :], q_ref[0, rows, :], v_ref[0, rows, :], u_s, dec_s, blk, r0)
        return 0

    lax.fori_loop(0, t_lat // rbl, lat_prep, 0, unroll=4)
    scan(t_lat // c, u_s, dec_s, True, carry)

    def finish(blk, _):
        r0 = pl.multiple_of(blk * rbl, rbl)
        inter = [_dot_nt(qd_s[pl.ds(r0 + j * c, c), :], st_s[blk * (rbl // c) + j]) for j in range(rbl // c)]
        o = o_s[pl.ds(r0, rbl), :] + jnp.concatenate(inter, axis=0)
        o_ref[0, pl.ds(r0, rbl), :] = _rms(o, nw_ref[...]) * _silu(g_ref[0, pl.ds(r0, rbl), :])
        return 0

    lax.fori_loop(0, t_lat // rbl, finish, 0, unroll=4)


def _hgrn2(p_lat, p_ctx, lb, norm_w):
    b, t, _ = p_lat.shape
    tc = p_ctx.shape[1]
    hw = lb.shape[1]
    dk = hw // HG_HEADS
    nh = HG_HEADS
    c = HG_CHUNK

    def col(k, tt):
        return pl.BlockSpec((1, tt, dk), lambda i, h, k=k: (i, 0, k * nh + h))

    return pl.pallas_call(
        functools.partial(_hg_kernel, t_lat=t, t_ctx=tc), grid=(b, nh),
        in_specs=[col(0, t), col(1, t), col(2, t), col(3, t), col(4, t),
                  col(1, tc), col(2, tc), col(3, tc),
                  pl.BlockSpec((2, dk), lambda i, h: (0, h)),
                  pl.BlockSpec((1, dk), lambda i, h: (0, h))],
        out_specs=pl.BlockSpec((1, t, dk), lambda i, h: (i, 0, h)),
        out_shape=jax.ShapeDtypeStruct((b, t, hw), F32),
        scratch_shapes=[pltpu.VMEM((t, dk), F32),
                        pltpu.VMEM((t, 2 * dk), BF16),
                        pltpu.VMEM((2, t // c, dk, dk), F32),
                        pltpu.VMEM((2, t // c, dk), F32),
                        pltpu.VMEM((t // c, dk, 2 * dk), BF16),
                        pltpu.VMEM((2, tc // c, dk, dk), F32),
                        pltpu.VMEM((2, tc // c, dk), F32),
                        pltpu.VMEM((2 * min(256, max(t, tc)), dk), F32)],
        compiler_params=_params("arbitrary", "arbitrary"), name="hgrn2",
    )(p_lat, p_lat, p_lat, p_lat, p_lat, p_ctx, p_ctx, p_ctx, lb, norm_w)


def _shift_rows(x, k):
    n = x.shape[0]
    y = pltpu.roll(x, k % n, 0)
    r = lax.broadcasted_iota(jnp.int32, x.shape, 0)
    return jnp.where((r >= k) & (r < n + k), y, 0.0)


def _rg_kernel(rx_ref, rgate_ref, crx_ref, cw_ref, cb_ref, wg_ref, bg_ref, lam_ref, o_ref,
               xc_s, af_s, bf_s, ab_s, bb_s, hf_s, hb_s, caf_s, cbf_s, cab_s, cbb_s, *, t_lat, t_ctx):
    w = GRID_W
    rows = t_lat // w
    ch = rx_ref.shape[-1]
    half = ch // 2
    cw = cw_ref[...]
    cb = cb_ref[...]
    bg = bg_ref[...]
    nl = -lam_ref[...]
    cdec = -RG_C * (jnp.maximum(nl, 0.0) + jnp.log1p(jnp.exp(-jnp.abs(nl))))

    def conv(xm2, xm1, x0, xp1):
        return cb + cw[0:1] * xm2 + cw[1:2] * xm1 + cw[2:3] * x0 + cw[3:4] * xp1

    def gates(xc):
        xb = xc.astype(BF16)
        g0 = _dot(xb[:, :half], wg_ref[0])
        g1 = _dot(xb[:, half:], wg_ref[1])
        outs = []
        for d in range(2):
            pre = []
            for s in (2 * d, 2 * d + 1):
                pre.append(jnp.concatenate([g0[:, s * half:(s + 1) * half], g1[:, s * half:(s + 1) * half]],
                                           axis=1) + bg[:, s * ch:(s + 1) * ch])
            log_a = cdec[d:d + 1] * _sigmoid(pre[0])
            a = jnp.exp(log_a)
            mult = jnp.sqrt(-jnp.tanh(log_a) * (a * a + 1.0))
            outs += [a, mult * _sigmoid(pre[1]) * xc]
        return outs

    xctx = crx_ref[0]
    xcc = conv(_shift_rows(xctx, 2), _shift_rows(xctx, 1), xctx, _shift_rows(xctx, -1))
    caf_s[...], cbf_s[...], cab_s[...], cbb_s[...] = gates(xcc)

    def cstep(i, carry):
        hf, hb = carry
        hf = caf_s[pl.ds(i, 1), :] * hf + cbf_s[pl.ds(i, 1), :]
        j = t_ctx - 1 - i
        hb = cab_s[pl.ds(j, 1), :] * hb + cbb_s[pl.ds(j, 1), :]
        return hf, hb

    zrow = jnp.zeros((1, ch), F32)
    hf0, hb0 = lax.fori_loop(0, t_ctx, cstep, (zrow, zrow), unroll=8)

    def slab(rr):
        if 0 <= rr < rows:
            return rx_ref[0, rr * w:(rr + 1) * w, :]
        if rr < 0:
            return _shift_rows(rx_ref[0, (rr + rows) * w:(rr + rows + 1) * w, :], 1)
        return _shift_rows(rx_ref[0, (rr - rows) * w:(rr - rows + 1) * w, :], -1)

    for r in range(rows):
        xc_s[r * w:(r + 1) * w, :] = conv(slab(r - 2), slab(r - 1), slab(r), slab(r + 1))

    mb = min(256, t_lat)

    def gbody(i, _):
        r0 = pl.multiple_of(i * mb, mb)
        a_f, b_f, a_b, b_b = gates(xc_s[pl.ds(r0, mb), :])
        af_s[pl.ds(r0, mb), :] = a_f
        bf_s[pl.ds(r0, mb), :] = b_f
        ab_s[pl.ds(r0, mb), :] = a_b
        bb_s[pl.ds(r0, mb), :] = b_b
        return 0

    lax.fori_loop(0, t_lat // mb, gbody, 0)

    def l1(i, _):
        pf = pl.multiple_of(i * w, w)
        qf = pl.multiple_of((i - 1) * w, w)
        a = af_s[pl.ds(pf, w), :]
        af_s[pl.ds(pf, w), :] = a * af_s[pl.ds(qf, w), :]
        bf_s[pl.ds(pf, w), :] = a * bf_s[pl.ds(qf, w), :] + bf_s[pl.ds(pf, w), :]
        pb = pl.multiple_of((rows - 1 - i) * w, w)
        qb = pl.multiple_of((rows - i) * w, w)
        a = ab_s[pl.ds(pb, w), :]
        ab_s[pl.ds(pb, w), :] = a * ab_s[pl.ds(qb, w), :]
        bb_s[pl.ds(pb, w), :] = a * bb_s[pl.ds(qb, w), :] + bb_s[pl.ds(pb, w), :]
        return 0

    lax.fori_loop(1, rows, l1, 0)

    last = (rows - 1) * w

    def l2(i, carry):
        hf, hb = carry
        hf_s[pl.ds(i, 1), :] = hf
        hf = af_s[pl.ds(last + i, 1), :] * hf + bf_s[pl.ds(last + i, 1), :]
        j = w - 1 - i
        hb_s[pl.ds(j, 1), :] = hb
        hb = ab_s[pl.ds(j, 1), :] * hb + bb_s[pl.ds(j, 1), :]
        return hf, hb

    lax.fori_loop(0, w, l2, (hf0, hb0))

    def l3(i, _):
        p = pl.multiple_of(i * w, w)
        h = (af_s[pl.ds(p, w), :] * hf_s[...] + bf_s[pl.ds(p, w), :]
             + ab_s[pl.ds(p, w), :] * hb_s[...] + bb_s[pl.ds(p, w), :])
        o_ref[0, pl.ds(p, w), :] = jax.nn.gelu(rgate_ref[0, pl.ds(p, w), :]) * h
        return 0

    lax.fori_loop(0, rows, l3, 0)


def _rglru(p_lat, p_ctx, conv_w, conv_b, wg, bg, lam):
    b, t, _ = p_lat.shape
    tc = p_ctx.shape[1]
    ch = conv_w.shape[1]
    rx_blk = (p_lat.shape[2] - 2 * ch) // ch
    full = lambda shape: pl.BlockSpec(shape, lambda i: (0,) * len(shape))
    big = lambda: pltpu.VMEM((t, ch), F32)
    small = lambda: pltpu.VMEM((tc, ch), F32)
    return pl.pallas_call(
        functools.partial(_rg_kernel, t_lat=t, t_ctx=tc), grid=(b,),
        in_specs=[pl.BlockSpec((1, t, ch), lambda i: (i, 0, rx_blk)),
                  pl.BlockSpec((1, t, ch), lambda i: (i, 0, rx_blk + 1)),
                  pl.BlockSpec((1, tc, ch), lambda i: (i, 0, rx_blk)),
                  full(conv_w.shape), full(conv_b.shape), full(wg.shape), full(bg.shape), full(lam.shape)],
        out_specs=pl.BlockSpec((1, t, ch), lambda i: (i, 0, 0)),
        out_shape=jax.ShapeDtypeStruct((b, t, ch), F32),
        scratch_shapes=[big(), big(), big(), big(), big(),
                        pltpu.VMEM((GRID_W, ch), F32), pltpu.VMEM((GRID_W, ch), F32),
                        small(), small(), small(), small()],
        compiler_params=_params("arbitrary"), name="rglru",
    )(p_lat, p_lat, p_ctx, conv_w, conv_b, wg, bg, lam)


def _mix_kernel(hg_ref, rg_ref, x_ref, g1_ref, sh_ref, sc_ref, nw_ref, wo_ref, rw_ref, rb_ref,
                x1_ref, h2_ref, meta_ref, cnt_ref, base_s, *, n_exp):
    tm = x_ref.shape[1]

    @pl.when((pl.program_id(0) == 0) & (pl.program_id(1) == 0))
    def _():
        base_s[...] = jnp.zeros_like(base_s)

    hcat = jnp.concatenate([hg_ref[0], rg_ref[0]], axis=1).astype(BF16)
    x1 = x_ref[0] + g1_ref[0] * _dot(hcat, wo_ref[...])
    x1_ref[0] = x1
    h2 = _rms(x1, nw_ref[...]) * (1.0 + sc_ref[0]) + sh_ref[0]
    h2_ref[0] = h2
    logits = _dot(h2.astype(BF16), rw_ref[...]) + rb_ref[...]

    lane_e = lax.broadcasted_iota(jnp.int32, (tm, n_exp), 1).astype(F32)
    vals, idxs = [], []
    cur = logits
    for _ in range(TOP_K):
        m = jnp.max(cur, axis=1, keepdims=True)
        ix = jnp.min(jnp.where(cur == m, lane_e, float(n_exp)), axis=1, keepdims=True)
        vals.append(m)
        idxs.append(ix)
        cur = jnp.where(lane_e == ix, -jnp.inf, cur)
    ex = [jnp.exp(v - vals[0]) for v in vals]
    den = ex[0] + ex[1] + ex[2] + ex[3]

    lane = lax.broadcasted_iota(jnp.int32, (tm, LANES), 1)
    lane_f = lane.astype(F32)
    onehot = jnp.zeros((tm, LANES), F32)
    for k in range(TOP_K):
        onehot = jnp.where(lane_f == idxs[k] + float(k * n_exp), 1.0, onehot)
    ri = lax.broadcasted_iota(jnp.int32, (tm, tm), 0)
    ci = lax.broadcasted_iota(jnp.int32, (tm, tm), 1)
    prefix = _dot((ci < ri).astype(BF16), onehot.astype(BF16))
    tot = jnp.broadcast_to(prefix[tm - 1:tm] + onehot[tm - 1:tm], (8, LANES))
    lane8 = lax.broadcasted_iota(jnp.int32, (8, LANES), 1)
    off = base_s[...]
    tot_all = tot
    for j in range(1, TOP_K):
        rolled = pltpu.roll(tot, j * n_exp, 1)
        off = off + jnp.where(lane8 >= j * n_exp, rolled, 0.0)
        tot_all = tot_all + rolled
    pos = onehot * (prefix + off[0:1])
    meta = jnp.zeros((tm, LANES), F32)
    for k in range(TOP_K):
        in_k = (lane >= k * n_exp) & (lane < (k + 1) * n_exp)
        rank = jnp.sum(jnp.where(in_k, pos, 0.0), axis=1, keepdims=True)
        meta = jnp.where(lane == k, idxs[k], meta)
        meta = jnp.where(lane == TOP_K + k, ex[k] / den, meta)
        meta = jnp.where(lane == 2 * TOP_K + k, rank, meta)
    meta_ref[0] = meta
    base_s[...] = base_s[...] + tot_all
    cnt_ref[...] = base_s[...]


def _mix(hg, rg, x, g1, sh2, sc2, norm_w, wo_bf16, router_w, router_b, tm):
    b, t, d = x.shape
    hw = hg.shape[2]
    n_exp = router_w.shape[1]
    assert TOP_K * n_exp == LANES
    tok = lambda last: pl.BlockSpec((1, tm, last), lambda i, j: (i, j, 0))
    per_b = pl.BlockSpec((1, 1, d), lambda i, j: (i, 0, 0))
    full = lambda shape: pl.BlockSpec(shape, lambda i, j: (0,) * len(shape))
    return pl.pallas_call(
        functools.partial(_mix_kernel, n_exp=n_exp), grid=(b, t // tm),
        in_specs=[tok(hw), tok(hw), tok(d), per_b, per_b, per_b, full((1, d)),
                  full(wo_bf16.shape), full(router_w.shape), full((1, n_exp))],
        out_specs=[tok(d), tok(d), tok(LANES), pl.BlockSpec((8, LANES), lambda i, j: (0, 0))],
        out_shape=[jax.ShapeDtypeStruct((b, t, d), F32), jax.ShapeDtypeStruct((b, t, d), F32),
                   jax.ShapeDtypeStruct((b, t, LANES), F32), jax.ShapeDtypeStruct((8, LANES), F32)],
        scratch_shapes=[pltpu.VMEM((8, LANES), F32)],
        compiler_params=_params("arbitrary", "arbitrary"), name="outproj_router",
    )(hg, rg, x, g1, sh2, sc2, norm_w, wo_bf16, router_w, router_b)


def _dispatch_kernel(fill_off, fill_n, tail, h_ref, dest_hbm, xs_out, idx0_s, idx1_s, zero_s, sem_i, sem_d, sem_z, *,
                     pad_bits):
    td = h_ref.shape[0] * SUBLANES
    n_idx = td * TOP_K
    n_exp = fill_n.shape[0]
    zrows = zero_s.shape[0]
    i = pl.program_id(0)
    slot = i % 2
    idx_s = (idx0_s, idx1_s)

    def idx_copy(j, sl):
        return pltpu.make_async_copy(dest_hbm.at[pl.ds(j * n_idx, n_idx)], idx_s[sl], sem_i.at[sl])

    @pl.when(i == 0)
    def _():
        idx_copy(0, 0).start()

    for sl in range(2):
        @pl.when((i + 1 < pl.num_programs(0)) & (slot != sl))
        def _():
            idx_copy(i + 1, sl).start()

    def fill(wait):
        def go(copy, cond):
            @pl.when(cond)
            def _():
                copy.wait() if wait else copy.start()

        def per_expert(e, _):
            off = fill_off[e]
            npad = fill_n[e]
            n_single = npad & (SUBLANES - 1)
            for r in range(SUBLANES - 1):
                go(pltpu.make_async_copy(zero_s.at[pl.ds(0, 1), :], xs_out.at[pl.ds(off + r, 1), :], sem_z),
                   r < n_single)
            off = pl.multiple_of(off + n_single, SUBLANES)
            for bit in reversed(range(SUBLANES.bit_length() - 1, pad_bits)):
                size = 1 << bit
                go(pltpu.make_async_copy(zero_s.at[pl.ds(0, size), :], xs_out.at[pl.ds(off, size), :], sem_z),
                   (npad & size) != 0)
                off = pl.multiple_of(off + (npad & size), SUBLANES)
            return 0

        lax.fori_loop(0, n_exp, per_expert, 0)

        def per_tail_chunk(j, _):
            off = pl.multiple_of(tail[0] + j * zrows, zrows)
            copy = pltpu.make_async_copy(zero_s, xs_out.at[pl.ds(off, zrows), :], sem_z)
            copy.wait() if wait else copy.start()
            return 0

        lax.fori_loop(0, tail[1], per_tail_chunk, 0)

    @pl.when(i == 0)
    def _():
        zero_s[...] = jnp.zeros_like(zero_s)
        fill(False)

    for sl in range(2):
        @pl.when(slot == sl)
        def _():
            idx_copy(i, sl).wait()

            def issue(g, _):
                for u in range(SUBLANES):
                    for k in range(TOP_K):
                        dst = idx_s[sl][g * (SUBLANES * TOP_K) + u * TOP_K + k]
                        pltpu.make_async_copy(h_ref.at[g, pl.ds(u, 1), :], xs_out.at[pl.ds(dst, 1), :],
                                              sem_d).start(priority=k % 2)
                return 0

            lax.fori_loop(0, td // SUBLANES, issue, 0)

    for _ in range(TOP_K * td // zrows):
        pltpu.make_async_copy(zero_s, xs_out.at[pl.ds(0, zrows), :], sem_d).wait()

    @pl.when(i == 0)
    def _():
        fill(True)


def _dispatch(h2, dest_flat, fill_off, fill_n, tail, n_slots, td, bm):
    n, d = h2.shape
    pad_bits = (bm - 1).bit_length()
    assert (TOP_K * td) % (bm // 2) == 0
    grid_spec = pltpu.PrefetchScalarGridSpec(
        num_scalar_prefetch=3, grid=(n // td,),
        in_specs=[pl.BlockSpec((td // SUBLANES, SUBLANES, d), lambda i, fo, fn, tl: (i, 0, 0)),
                  pl.BlockSpec(memory_space=pl.ANY)],
        out_specs=pl.BlockSpec(memory_space=pl.ANY),
        scratch_shapes=[pltpu.SMEM((td * TOP_K,), jnp.int32), pltpu.SMEM((td * TOP_K,), jnp.int32),
                        pltpu.VMEM((bm // 2, d), F32),
                        pltpu.SemaphoreType.DMA((2,)), pltpu.SemaphoreType.DMA, pltpu.SemaphoreType.DMA])
    return pl.pallas_call(
        functools.partial(_dispatch_kernel, pad_bits=pad_bits), grid_spec=grid_spec,
        out_shape=jax.ShapeDtypeStruct((n_slots, d), F32),
        compiler_params=_params("arbitrary"), name="moe_dispatch",
    )(fill_off, fill_n, tail, h2.reshape(n // SUBLANES, SUBLANES, d), dest_flat)


def _expert_kernel(be_ref, nu_ref, x_ref, wgu_ref, bgu_ref, wd_ref, bd_ref, y_ref, wgu_s, wd_s):
    i = pl.program_id(0)
    d_ff = wd_ref.shape[1]

    @pl.when(i >= nu_ref[0])
    def _():
        y_ref[...] = jnp.zeros_like(y_ref)

    @pl.when(i < nu_ref[0])
    def _():
        @pl.when((i == 0) | (be_ref[i] != be_ref[jnp.maximum(i - 1, 0)]))
        def _():
            wgu_s[...] = wgu_ref[0].astype(BF16)
            wd_s[...] = wd_ref[0].astype(BF16)

        gu = _dot(x_ref[...].astype(BF16), wgu_s[...]) + bgu_ref[0]
        gate = jnp.minimum(gu[:, :d_ff], SWIGLU_LIMIT)
        up = jnp.clip(gu[:, d_ff:], -SWIGLU_LIMIT, SWIGLU_LIMIT)
        act = gate * _sigmoid(SWIGLU_ALPHA * gate) * (up + 1.0)
        y_ref[...] = _dot(act.astype(BF16), wd_s[...]) + bd_ref[0]


def _experts(xs, blk_expert, n_used, wgu, bgu, wd, bd, bm):
    n_slots, d = xs.shape
    n_exp, _, f2 = wgu.shape
    d_ff = wd.shape[1]
    n_blocks = n_slots // bm
    row = lambda i, be, nu: (jnp.minimum(i, nu[0] - 1), 0)
    grid_spec = pltpu.PrefetchScalarGridSpec(
        num_scalar_prefetch=2, grid=(n_blocks,),
        in_specs=[pl.BlockSpec((bm, d), row),
                  pl.BlockSpec((1, d, f2), lambda i, be, nu: (be[i], 0, 0)),
                  pl.BlockSpec((1, 1, f2), lambda i, be, nu: (be[i], 0, 0)),
                  pl.BlockSpec((1, d_ff, d), lambda i, be, nu: (be[i], 0, 0)),
                  pl.BlockSpec((1, 1, d), lambda i, be, nu: (be[i], 0, 0))],
        out_specs=pl.BlockSpec((bm, d), lambda i, be, nu: (i, 0)),
        scratch_shapes=[pltpu.VMEM((d, f2), BF16), pltpu.VMEM((d_ff, d), BF16)])
    return pl.pallas_call(
        _expert_kernel, grid_spec=grid_spec,
        out_shape=jax.ShapeDtypeStruct((n_slots, d), F32),
        compiler_params=_params("arbitrary"), name="moe_experts",
    )(blk_expert, n_used, xs, wgu, bgu.reshape(n_exp, 1, f2), wd, bd.reshape(n_exp, 1, d))


def _combine_kernel(x1_ref, meta_ref, g2_ref, fw_ref, dest_hbm, y_hbm, o_ref, rows_s, idx0_s, idx1_s, sem_i, sem_d):
    tc = x1_ref.shape[0]
    n_idx = tc * TOP_K
    i = pl.program_id(0)
    n = pl.num_programs(0)
    slot = i % 2
    idx_s = (idx0_s, idx1_s)

    def idx_copy(j, sl):
        return pltpu.make_async_copy(dest_hbm.at[pl.ds(j * n_idx, n_idx)], idx_s[sl], sem_i.at[sl])

    def gather(j, sl):
        idx_copy(j, sl).wait()

        def issue(g, _):
            for u in range(SUBLANES):
                for k in range(TOP_K):
                    src = idx_s[sl][g * (SUBLANES * TOP_K) + u * TOP_K + k]
                    pltpu.make_async_copy(y_hbm.at[pl.ds(src, 1), :], rows_s.at[sl, k, g, pl.ds(u, 1), :],
                                          sem_d.at[sl]).start(priority=k % 2)
            return 0

        lax.fori_loop(0, tc // SUBLANES, issue, 0)

    @pl.when(i == 0)
    def _():
        idx_copy(0, 0).start()

        @pl.when(n > 1)
        def _():
            idx_copy(1, 1).start()

        gather(0, 0)

    for sl in range(2):
        @pl.when((i + 1 < n) & (slot != sl))
        def _():
            gather(i + 1, sl)

    for sl in range(2):
        @pl.when((i + 2 < n) & (slot == sl))
        def _():
            idx_copy(i + 2, sl).start()

    for k in range(TOP_K):
        pltpu.make_async_copy(rows_s.at[1 - slot, k], rows_s.at[slot, k], sem_d.at[slot]).wait()

    def rows(k):
        return rows_s[slot, k].reshape(tc, rows_s.shape[-1])

    meta = meta_ref[...]
    moe = meta[:, TOP_K:TOP_K + 1] * rows(0)
    for k in range(1, TOP_K):
        moe = moe + meta[:, TOP_K + k:TOP_K + k + 1] * rows(k)
    o_ref[...] = _rms(x1_ref[...] + g2_ref[0] * moe, fw_ref[...])


def _combine(x1, meta, g2, final_w, dest_flat, y, t_seq, tc):
    n, d = x1.shape
    return pl.pallas_call(
        _combine_kernel, grid=(n // tc,),
        in_specs=[pl.BlockSpec((tc, d), lambda i: (i, 0)),
                  pl.BlockSpec((tc, LANES), lambda i: (i, 0)),
                  pl.BlockSpec((1, 1, d), lambda i: (i * tc // t_seq, 0, 0)),
                  pl.BlockSpec((1, d), lambda i: (0, 0)),
                  pl.BlockSpec(memory_space=pl.ANY),
                  pl.BlockSpec(memory_space=pl.ANY)],
        out_specs=pl.BlockSpec((tc, d), lambda i: (i, 0)),
        out_shape=jax.ShapeDtypeStruct((n, d), F32),
        scratch_shapes=[pltpu.VMEM((2, TOP_K, tc // SUBLANES, SUBLANES, d), F32),
                        pltpu.SMEM((tc * TOP_K,), jnp.int32), pltpu.SMEM((tc * TOP_K,), jnp.int32),
                        pltpu.SemaphoreType.DMA((2,)), pltpu.SemaphoreType.DMA((2,))],
        compiler_params=_params("arbitrary"), name="moe_combine",
    )(x1, meta, g2, final_w, dest_flat, y)


def _gate_weights(wa, wx):
    _, heads, hd, _ = wa.shape
    hh = heads // 2
    eye = jnp.eye(hh, dtype=wa.dtype)

    def blockdiag(wsel):
        return jnp.einsum('hij,hg->higj', wsel, eye).reshape(hh * hd, hh * hd)

    halves = []
    for s in range(2):
        sl = slice(s * hh, (s + 1) * hh)
        halves.append(jnp.concatenate([blockdiag(wa[0, sl]), blockdiag(wx[0, sl]),
                                       blockdiag(wa[1, sl]), blockdiag(wx[1, sl])], axis=1))
    return jnp.stack(halves).astype(BF16)


def kernel(x, c, ctx, c_ctx, norm1_w, norm2_w, w_ada, b_ada, w_in, hg_lb_logits, hg_norm_w, rg_conv_w, rg_conv_b,
           rg_wa, rg_ba, rg_wx, rg_bx, rg_lambda, w_out, router_w, router_b, w_gate_up, b_gate_up, w_down,
           b_down, final_norm_w):
    b, t, d = x.shape
    tcx = ctx.shape[1]
    n_exp = router_w.shape[-1]
    n_tok = b * t
    depth = w_in.shape[0]
    lb_all = jnp.cumsum(jax.nn.softmax(hg_lb_logits.astype(F32), axis=0), axis=0)

    for l in range(depth):
        assert l == depth - 1, "context stream update of non-final layers is not implemented"
        pad = (-(b + 1)) % 8
        c_all = jnp.concatenate([c, c_ctx[None], jnp.zeros((pad, d), F32)], axis=0)
        mod = _mod(c_all, w_ada[l], b_ada[l])
        sh1, sc1, g1, sh2, sc2, g2 = [m[:b, None, :] for m in jnp.split(mod, 6, axis=-1)]
        csh1, csc1 = [m[b:b + 1, None, :] for m in jnp.split(mod, 6, axis=-1)[:2]]

        w_in_b = w_in[l].astype(BF16)
        nw1 = norm1_w[l].reshape(1, d)
        tm = min(TOKEN_TILE, t)
        p_lat = _inproj(x, sh1, sc1, nw1, w_in_b, tm)
        p_ctx = _inproj(ctx, csh1, csc1, nw1, w_in_b, min(256, tcx))

        hg = _hgrn2(p_lat, p_ctx, lb_all[l], hg_norm_w[l].reshape(1, -1))
        ch = rg_conv_w.shape[-1]
        wg = _gate_weights(rg_wa[l], rg_wx[l])
        bg = jnp.concatenate([rg_ba[l, 0], rg_bx[l, 0], rg_ba[l, 1], rg_bx[l, 1]]).reshape(1, 4 * ch)
        rg = _rglru(p_lat, p_ctx, rg_conv_w[l], rg_conv_b[l].reshape(1, ch), wg, bg, rg_lambda[l])

        x1, h2, meta, cnt = _mix(hg, rg, x, g1, sh2, sc2, norm2_w[l].reshape(1, d), w_out[l].astype(BF16),
                                 router_w[l].astype(BF16), router_b[l].reshape(1, n_exp), tm)

        bm = EXPERT_BLOCK
        meta2 = meta.reshape(n_tok, LANES)
        counts = cnt[0, :n_exp].astype(jnp.int32)
        padded = (counts + bm - 1) // bm * bm
        pad_end = jnp.cumsum(padded)
        pad_start = pad_end - padded
        dest = jnp.stack([pad_start[meta2[:, k].astype(jnp.int32)] + meta2[:, 2 * TOP_K + k].astype(jnp.int32)
                          for k in range(TOP_K)], axis=1).reshape(-1)
        n_blocks = -(-n_tok * TOP_K // bm) + n_exp
        blk_start = jnp.arange(n_blocks, dtype=jnp.int32) * bm
        blk_expert = jnp.minimum(jnp.sum(blk_start[:, None] >= pad_end[None, :], axis=1), n_exp - 1).astype(jnp.int32)
        n_used = (pad_end[-1:] // bm).astype(jnp.int32)

        n_slots = n_blocks * bm
        tail = jnp.stack([pad_end[-1], (n_slots - pad_end[-1]) // (bm // 2)]).astype(jnp.int32)
        xs = _dispatch(h2.reshape(n_tok, d), dest, pad_start + counts, padded - counts, tail, n_slots,
                       min(DISPATCH_TILE, t), bm)
        y = _experts(xs, blk_expert, n_used, w_gate_up[l], b_gate_up[l], w_down[l], b_down[l], bm)
        out = _combine(x1.reshape(n_tok, d), meta2, g2, final_norm_w.reshape(1, d), dest, y, t,
                       min(COMBINE_TILE, t))
        return out.reshape(b, t, d)
```

```python
import functools

import jax
import jax.numpy as jnp
from jax import lax
from jax.experimental import pallas as pl
from jax.experimental.pallas import tpu as pltpu

GRID_W = 64
HG_HEADS = 4
HG_CHUNK = 32
RG_HEADS = 8
RG_CONV = 4
RG_C = 8.0
TOP_K = 4
SWIGLU_LIMIT = 7.0
SWIGLU_ALPHA = 1.702
EPS = 1e-6

LANES = 128
SUBLANES = 8
VMEM_LIMIT = 56 * 1024 * 1024

TOKEN_TILE = 512
EXPERT_BLOCK = 512
DISPATCH_TILE = 1024
COMBINE_TILE = 512

F32 = jnp.float32
BF16 = jnp.bfloat16
HIGHEST = lax.Precision.HIGHEST


def _params(*sem):
    return pltpu.CompilerParams(dimension_semantics=sem, vmem_limit_bytes=VMEM_LIMIT)


def _sigmoid(x):
    return 0.5 * jnp.tanh(0.5 * x) + 0.5


def _silu(x):
    return x * _sigmoid(x)


def _rms(x, w):
    return x * lax.rsqrt(jnp.mean(x * x, axis=-1, keepdims=True) + EPS) * w


def _dot(a, b):
    return jnp.dot(a, b, preferred_element_type=F32)


def _dot_nt(a, b):
    return lax.dot_general(a, b, (((1,), (1,)), ((), ())), preferred_element_type=F32)


def _mod_kernel(c_ref, w_ref, b_ref, o_ref):
    o_ref[...] = jnp.dot(_silu(c_ref[...]), w_ref[...], preferred_element_type=F32,
                         precision=HIGHEST) + b_ref[...]


def _mod(c_all, w_ada, b_ada):
    r, d = c_all.shape
    n = w_ada.shape[1]
    tn = n // 4
    return pl.pallas_call(
        _mod_kernel, grid=(n // tn,),
        in_specs=[pl.BlockSpec((r, d), lambda j: (0, 0)),
                  pl.BlockSpec((d, tn), lambda j: (0, j)),
                  pl.BlockSpec((1, tn), lambda j: (0, j))],
        out_specs=pl.BlockSpec((r, tn), lambda j: (0, j)),
        out_shape=jax.ShapeDtypeStruct((r, n), F32),
        compiler_params=_params("arbitrary"), name="adaln_mod",
    )(c_all, w_ada, b_ada.reshape(1, n))


def _inproj_kernel(x_ref, sh_ref, sc_ref, nw_ref, w_ref, o_ref):
    h = _rms(x_ref[0], nw_ref[...]) * (1.0 + sc_ref[0]) + sh_ref[0]
    o_ref[0] = _dot(h.astype(BF16), w_ref[...])


def _inproj(x, shift, scale, norm_w, w_bf16, tm):
    b, t, d = x.shape
    n = w_bf16.shape[1]
    per_batch = shift.shape[0] == b
    mod_map = (lambda i, j: (i, 0, 0)) if per_batch else (lambda i, j: (0, 0, 0))
    return pl.pallas_call(
        _inproj_kernel, grid=(b, t // tm),
        in_specs=[pl.BlockSpec((1, tm, d), lambda i, j: (i, j, 0)),
                  pl.BlockSpec((1, 1, d), mod_map),
                  pl.BlockSpec((1, 1, d), mod_map),
                  pl.BlockSpec((1, d), lambda i, j: (0, 0)),
                  pl.BlockSpec((d, n), lambda i, j: (0, 0))],
        out_specs=pl.BlockSpec((1, tm, n), lambda i, j: (i, j, 0)),
        out_shape=jax.ShapeDtypeStruct((b, t, n), F32),
        compiler_params=_params("arbitrary", "arbitrary"), name="norm_inproj",
    )(x, shift, scale, norm_w, w_bf16)


def _split_bf16(x):
    hi = x.astype(BF16)
    return hi, (x - hi.astype(F32)).astype(BF16)


def _hg_kernel(q_ref, v_ref, zf_ref, zb_ref, g_ref, cv_ref, czf_ref, czb_ref, lb_ref, nw_ref, o_ref,
               o_s, qd_s, u_s, dec_s, st_s, cu_s, cdec_s, tot_s, *, t_lat, t_ctx):
    c = HG_CHUNK
    dk = q_ref.shape[-1]
    lb = lb_ref[...]
    lbf, lbb = lb[0:1], lb[1:2]

    def prep(rb, zf, zb, q, v, u_out, dec_out, blk, r0):
        nc = rb // c
        ri = lax.broadcasted_iota(jnp.int32, (rb, rb), 0)
        ci = lax.broadcasted_iota(jnp.int32, (rb, rb), 1)
        same = (ri // c) == (ci // c)
        low = same & (ci <= ri)
        upp = same & (ci >= ri)
        ff = lbf + (1.0 - lbf) * _sigmoid(zf)
        fb = lbb + (1.0 - lbb) * _sigmoid(zb)
        lgf, lgb = jnp.log(ff), jnp.log(fb)
        rhs = jnp.concatenate([*_split_bf16(lgf), *_split_bf16(lgb)], axis=1)
        pre = _dot(low.astype(BF16), rhs)
        bcf = pre[:, :dk] + pre[:, dk:2 * dk]
        pfb = pre[:, 2 * dk:3 * dk] + pre[:, 3 * dk:]
        half = tot_s.shape[0] // 2
        tot_s[0:rb, :] = bcf
        tot_s[half:half + rb, :] = pfb
        totf = tot_s[pl.ds(c - 1, nc, stride=c), :]
        totb = tot_s[pl.ds(half + c - 1, nc, stride=c), :]

        def spread(tot):
            return jnp.broadcast_to(tot[:, None, :], (nc, c, dk)).reshape(rb, dk)

        remf = spread(totf) - bcf
        remb = pfb - lgb
        bcb = spread(totb) - remb
        kkf, kkb = 1.0 - ff, 1.0 - fb
        kef = (kkf * jnp.exp(remf)).astype(BF16)
        keb = (kkb * jnp.exp(remb)).astype(BF16)
        chunk_of_row = lax.broadcasted_iota(jnp.int32, (rb, dk), 0) // c
        zero = jnp.zeros((rb, dk), BF16)
        keys = jnp.concatenate([jnp.where(chunk_of_row == j, ke, zero) for ke in (kef, keb) for j in range(nc)],
                               axis=1)
        u_all = _dot(v.T.astype(BF16), keys)
        c0 = blk * nc
        for d in range(2):
            for j in range(nc):
                u_out[d, c0 + j] = u_all[:, (d * nc + j) * dk:(d * nc + j + 1) * dk]
        dec_out[0, pl.ds(pl.multiple_of(c0, nc), nc), :] = jnp.exp(totf)
        dec_out[1, pl.ds(pl.multiple_of(c0, nc), nc), :] = jnp.exp(totb)
        if q is None:
            return
        sq = _silu(q)
        qdf = (sq * jnp.exp(bcf)).astype(BF16)
        qdb = (sq * jnp.exp(bcb)).astype(BF16)
        kdf = (kkf * jnp.exp(-bcf)).astype(BF16)
        kdb = (kkb * jnp.exp(-bcb)).astype(BF16)
        p = jnp.where(low, _dot_nt(qdf, kdf), 0.0) + jnp.where(upp, _dot_nt(qdb, kdb), 0.0)
        o_s[pl.ds(r0, rb), :] = _dot(p.astype(BF16), v.astype(BF16))
        qd_s[pl.ds(r0, rb), 0:dk] = qdf
        qd_s[pl.ds(r0, rb), dk:2 * dk] = qdb

    def scan(n, ur, decr, keep, carry):
        def body(i, carry):
            sf, sb = carry
            j = n - 1 - i
            if keep:
                st_s[i, :, 0:dk] = sf.astype(BF16)
                st_s[j, :, dk:2 * dk] = sb.astype(BF16)
            return sf * decr[0, pl.ds(i, 1), :] + ur[0, i], sb * decr[1, pl.ds(j, 1), :] + ur[1, j]

        return lax.fori_loop(0, n, body, carry, unroll=2)

    rbc = min(256, t_ctx)
    for blk in range(t_ctx // rbc):
        r0 = blk * rbc
        prep(rbc, czf_ref[0, r0:r0 + rbc, :], czb_ref[0, r0:r0 + rbc, :], None, cv_ref[0, r0:r0 + rbc, :],
             cu_s, cdec_s, blk, r0)
    zero = jnp.zeros((dk, dk), F32)
    carry = scan(t_ctx // c, cu_s, cdec_s, False, (zero, zero))

    rbl = min(256, t_lat)

    def lat_prep(blk, _):
        r0 = pl.multiple_of(blk * rbl, rbl)
        rows = pl.ds(r0, rbl)
        prep(rbl, zf_ref[0, rows, :], zb_ref[0, rows, :], q_ref[0, rows, :], v_ref[0, rows, :], u_s, dec_s, blk, r0)
        return 0

    lax.fori_loop(0, t_lat // rbl, lat_prep, 0, unroll=4)
    scan(t_lat // c, u_s, dec_s, True, carry)

    def finish(blk, _):
        r0 = pl.multiple_of(blk * rbl, rbl)
        inter = [_dot_nt(qd_s[pl.ds(r0 + j * c, c), :], st_s[blk * (rbl // c) + j]) for j in range(rbl // c)]
        o = o_s[pl.ds(r0, rbl), :] + jnp.concatenate(inter, axis=0)
        o_ref[0, pl.ds(r0, rbl), :] = _rms(o, nw_ref[...]) * _silu(g_ref[0, pl.ds(r0, rbl), :])
        return 0

    lax.fori_loop(0, t_lat // rbl, finish, 0, unroll=4)


def _hgrn2(p_lat, p_ctx, lb, norm_w):
    b, t, _ = p_lat.shape
    tc = p_ctx.shape[1]
    hw = lb.shape[1]
    dk = hw // HG_HEADS
    nh = HG_HEADS
    c = HG_CHUNK

    def col(k, tt):
        return pl.BlockSpec((1, tt, dk), lambda i, h, k=k: (i, 0, k * nh + h))

    return pl.pallas_call(
        functools.partial(_hg_kernel, t_lat=t, t_ctx=tc), grid=(b, nh),
        in_specs=[col(0, t), col(1, t), col(2, t), col(3, t), col(4, t),
                  col(1, tc), col(2, tc), col(3, tc),
                  pl.BlockSpec((2, dk), lambda i, h: (0, h)),
                  pl.BlockSpec((1, dk), lambda i, h: (0, h))],
        out_specs=pl.BlockSpec((1, t, dk), lambda i, h: (i, 0, h)),
        out_shape=jax.ShapeDtypeStruct((b, t, hw), F32),
        scratch_shapes=[pltpu.VMEM((t, dk), F32),
                        pltpu.VMEM((t, 2 * dk), BF16),
                        pltpu.VMEM((2, t // c, dk, dk), F32),
                        pltpu.VMEM((2, t // c, dk), F32),
                        pltpu.VMEM((t // c, dk, 2 * dk), BF16),
                        pltpu.VMEM((2, tc // c, dk, dk), F32),
                        pltpu.VMEM((2, tc // c, dk), F32),
                        pltpu.VMEM((2 * min(256, max(t, tc)), dk), F32)],
        compiler_params=_params("arbitrary", "arbitrary"), name="hgrn2",
    )(p_lat, p_lat, p_lat, p_lat, p_lat, p_ctx, p_ctx, p_ctx, lb, norm_w)


def _shift_rows(x, k):
    n = x.shape[0]
    y = pltpu.roll(x, k % n, 0)
    r = lax.broadcasted_iota(jnp.int32, x.shape, 0)
    return jnp.where((r >= k) & (r < n + k), y, 0.0)


def _rg_kernel(rx_ref, rgate_ref, crx_ref, cw_ref, cb_ref, wg_ref, bg_ref, lam_ref, o_ref,
               xc_s, af_s, bf_s, ab_s, bb_s, hf_s, hb_s, caf_s, cbf_s, cab_s, cbb_s, *, t_lat, t_ctx):
    w = GRID_W
    rows = t_lat // w
    ch = rx_ref.shape[-1]
    half = ch // 2
    cw = cw_ref[...]
    cb = cb_ref[...]
    bg = bg_ref[...]
    nl = -lam_ref[...]
    cdec = -RG_C * (jnp.maximum(nl, 0.0) + jnp.log1p(jnp.exp(-jnp.abs(nl))))

    def conv(xm2, xm1, x0, xp1):
        return cb + cw[0:1] * xm2 + cw[1:2] * xm1 + cw[2:3] * x0 + cw[3:4] * xp1

    def gates(xc):
        xb = xc.astype(BF16)
        g0 = _dot(xb[:, :half], wg_ref[0])
        g1 = _dot(xb[:, half:], wg_ref[1])
        outs = []
        for d in range(2):
            pre = []
            for s in (2 * d, 2 * d + 1):
                pre.append(jnp.concatenate([g0[:, s * half:(s + 1) * half], g1[:, s * half:(s + 1) * half]],
                                           axis=1) + bg[:, s * ch:(s + 1) * ch])
            log_a = cdec[d:d + 1] * _sigmoid(pre[0])
            a = jnp.exp(log_a)
            mult = jnp.sqrt(-jnp.tanh(log_a) * (a * a + 1.0))
            outs += [a, mult * _sigmoid(pre[1]) * xc]
        return outs

    xctx = crx_ref[0]
    xcc = conv(_shift_rows(xctx, 2), _shift_rows(xctx, 1), xctx, _shift_rows(xctx, -1))
    caf_s[...], cbf_s[...], cab_s[...], cbb_s[...] = gates(xcc)

    def cstep(i, carry):
        hf, hb = carry
        hf = caf_s[pl.ds(i, 1), :] * hf + cbf_s[pl.ds(i, 1), :]
        j = t_ctx - 1 - i
        hb = cab_s[pl.ds(j, 1), :] * hb + cbb_s[pl.ds(j, 1), :]
        return hf, hb

    zrow = jnp.zeros((1, ch), F32)
    hf0, hb0 = lax.fori_loop(0, t_ctx, cstep, (zrow, zrow), unroll=8)

    def slab(rr):
        if 0 <= rr < rows:
            return rx_ref[0, rr * w:(rr + 1) * w, :]
        if rr < 0:
            return _shift_rows(rx_ref[0, (rr + rows) * w:(rr + rows + 1) * w, :], 1)
        return _shift_rows(rx_ref[0, (rr - rows) * w:(rr - rows + 1) * w, :], -1)

    for r in range(rows):
        xc_s[r * w:(r + 1) * w, :] = conv(slab(r - 2), slab(r - 1), slab(r), slab(r + 1))

    mb = min(256, t_lat)

    def gbody(i, _):
        r0 = pl.multiple_of(i * mb, mb)
        a_f, b_f, a_b, b_b = gates(xc_s[pl.ds(r0, mb), :])
        af_s[pl.ds(r0, mb), :] = a_f
        bf_s[pl.ds(r0, mb), :] = b_f
        ab_s[pl.ds(r0, mb), :] = a_b
        bb_s[pl.ds(r0, mb), :] = b_b
        return 0

    lax.fori_loop(0, t_lat // mb, gbody, 0)

    def l1(i, _):
        pf = pl.multiple_of(i * w, w)
        qf = pl.multiple_of((i - 1) * w, w)
        a = af_s[pl.ds(pf, w), :]
        af_s[pl.ds(pf, w), :] = a * af_s[pl.ds(qf, w), :]
        bf_s[pl.ds(pf, w), :] = a * bf_s[pl.ds(qf, w), :] + bf_s[pl.ds(pf, w), :]
        pb = pl.multiple_of((rows - 1 - i) * w, w)
        qb = pl.multiple_of((rows - i) * w, w)
        a = ab_s[pl.ds(pb, w), :]
        ab_s[pl.ds(pb, w), :] = a * ab_s[pl.ds(qb, w), :]
        bb_s[pl.ds(pb, w), :] = a * bb_s[pl.ds(qb, w), :] + bb_s[pl.ds(pb, w), :]
        return 0

    lax.fori_loop(1, rows, l1, 0)

    last = (rows - 1) * w

    def l2(i, carry):
        hf, hb = carry
        hf_s[pl.ds(i, 1), :] = hf
        hf = af_s[pl.ds(last + i, 1), :] * hf + bf_s[pl.ds(last + i, 1), :]
        j = w - 1 - i
        hb_s[pl.ds(j, 1), :] = hb
        hb = ab_s[pl.ds(j, 1), :] * hb + bb_s[pl.ds(j, 1), :]
        return hf, hb

    lax.fori_loop(0, w, l2, (hf0, hb0))

    def l3(i, _):
        p = pl.multiple_of(i * w, w)
        h = (af_s[pl.ds(p, w), :] * hf_s[...] + bf_s[pl.ds(p, w), :]
             + ab_s[pl.ds(p, w), :] * hb_s[...] + bb_s[pl.ds(p, w), :])
        o_ref[0, pl.ds(p, w), :] = jax.nn.gelu(rgate_ref[0, pl.ds(p, w), :]) * h
        return 0

    lax.fori_loop(0, rows, l3, 0)


def _rglru(p_lat, p_ctx, conv_w, conv_b, wg, bg, lam):
    b, t, _ = p_lat.shape
    tc = p_ctx.shape[1]
    ch = conv_w.shape[1]
    rx_blk = (p_lat.shape[2] - 2 * ch) // ch
    full = lambda shape: pl.BlockSpec(shape, lambda i: (0,) * len(shape))
    big = lambda: pltpu.VMEM((t, ch), F32)
    small = lambda: pltpu.VMEM((tc, ch), F32)
    return pl.pallas_call(
        functools.partial(_rg_kernel, t_lat=t, t_ctx=tc), grid=(b,),
        in_specs=[pl.BlockSpec((1, t, ch), lambda i: (i, 0, rx_blk)),
                  pl.BlockSpec((1, t, ch), lambda i: (i, 0, rx_blk + 1)),
                  pl.BlockSpec((1, tc, ch), lambda i: (i, 0, rx_blk)),
                  full(conv_w.shape), full(conv_b.shape), full(wg.shape), full(bg.shape), full(lam.shape)],
        out_specs=pl.BlockSpec((1, t, ch), lambda i: (i, 0, 0)),
        out_shape=jax.ShapeDtypeStruct((b, t, ch), F32),
        scratch_shapes=[big(), big(), big(), big(), big(),
                        pltpu.VMEM((GRID_W, ch), F32), pltpu.VMEM((GRID_W, ch), F32),
                        small(), small(), small(), small()],
        compiler_params=_params("arbitrary"), name="rglru",
    )(p_lat, p_lat, p_ctx, conv_w, conv_b, wg, bg, lam)


def _mix_kernel(hg_ref, rg_ref, x_ref, g1_ref, sh_ref, sc_ref, nw_ref, wo_ref, rw_ref, rb_ref,
                x1_ref, h2_ref, meta_ref, cnt_ref, base_s, *, n_exp):
    tm = x_ref.shape[1]

    @pl.when((pl.program_id(0) == 0) & (pl.program_id(1) == 0))
    def _():
        base_s[...] = jnp.zeros_like(base_s)

    hcat = jnp.concatenate([hg_ref[0], rg_ref[0]], axis=1).astype(BF16)
    x1 = x_ref[0] + g1_ref[0] * _dot(hcat, wo_ref[...])
    x1_ref[0] = x1
    h2 = _rms(x1, nw_ref[...]) * (1.0 + sc_ref[0]) + sh_ref[0]
    h2_ref[0] = h2
    logits = _dot(h2.astype(BF16), rw_ref[...]) + rb_ref[...]

    lane_e = lax.broadcasted_iota(jnp.int32, (tm, n_exp), 1).astype(F32)
    vals, idxs = [], []
    cur = logits
    for _ in range(TOP_K):
        m = jnp.max(cur, axis=1, keepdims=True)
        ix = jnp.min(jnp.where(cur == m, lane_e, float(n_exp)), axis=1, keepdims=True)
        vals.append(m)
        idxs.append(ix)
        cur = jnp.where(lane_e == ix, -jnp.inf, cur)
    ex = [jnp.exp(v - vals[0]) for v in vals]
    den = ex[0] + ex[1] + ex[2] + ex[3]

    lane = lax.broadcasted_iota(jnp.int32, (tm, LANES), 1)
    lane_f = lane.astype(F32)
    onehot = jnp.zeros((tm, LANES), F32)
    for k in range(TOP_K):
        onehot = jnp.where(lane_f == idxs[k] + float(k * n_exp), 1.0, onehot)
    ri = lax.broadcasted_iota(jnp.int32, (tm, tm), 0)
    ci = lax.broadcasted_iota(jnp.int32, (tm, tm), 1)
    prefix = _dot((ci < ri).astype(BF16), onehot.astype(BF16))
    tot = jnp.broadcast_to(prefix[tm - 1:tm] + onehot[tm - 1:tm], (8, LANES))
    lane8 = lax.broadcasted_iota(jnp.int32, (8, LANES), 1)
    off = base_s[...]
    tot_all = tot
    for j in range(1, TOP_K):
        rolled = pltpu.roll(tot, j * n_exp, 1)
        off = off + jnp.where(lane8 >= j * n_exp, rolled, 0.0)
        tot_all = tot_all + rolled
    pos = onehot * (prefix + off[0:1])
    meta = jnp.zeros((tm, LANES), F32)
    for k in range(TOP_K):
        in_k = (lane >= k * n_exp) & (lane < (k + 1) * n_exp)
        rank = jnp.sum(jnp.where(in_k, pos, 0.0), axis=1, keepdims=True)
        meta = jnp.where(lane == k, idxs[k], meta)
        meta = jnp.where(lane == TOP_K + k, ex[k] / den, meta)
        meta = jnp.where(lane == 2 * TOP_K + k, rank, meta)
    meta_ref[0] = meta
    base_s[...] = base_s[...] + tot_all
    cnt_ref[...] = base_s[...]


def _mix(hg, rg, x, g1, sh2, sc2, norm_w, wo_bf16, router_w, router_b, tm):
    b, t, d = x.shape
    hw = hg.shape[2]
    n_exp = router_w.shape[1]
    assert TOP_K * n_exp == LANES
    tok = lambda last: pl.BlockSpec((1, tm, last), lambda i, j: (i, j, 0))
    per_b = pl.BlockSpec((1, 1, d), lambda i, j: (i, 0, 0))
    full = lambda shape: pl.BlockSpec(shape, lambda i, j: (0,) * len(shape))
    return pl.pallas_call(
        functools.partial(_mix_kernel, n_exp=n_exp), grid=(b, t // tm),
        in_specs=[tok(hw), tok(hw), tok(d), per_b, per_b, per_b, full((1, d)),
                  full(wo_bf16.shape), full(router_w.shape), full((1, n_exp))],
        out_specs=[tok(d), tok(d), tok(LANES), pl.BlockSpec((8, LANES), lambda i, j: (0, 0))],
        out_shape=[jax.ShapeDtypeStruct((b, t, d), F32), jax.ShapeDtypeStruct((b, t, d), F32),
                   jax.ShapeDtypeStruct((b, t, LANES), F32), jax.ShapeDtypeStruct((8, LANES), F32)],
        scratch_shapes=[pltpu.VMEM((8, LANES), F32)],
        compiler_params=_params("arbitrary", "arbitrary"), name="outproj_router",
    )(hg, rg, x, g1, sh2, sc2, norm_w, wo_bf16, router_w, router_b)


def _dispatch_kernel(fill_off, fill_n, tail, h_ref, dest_hbm, xs_out, idx0_s, idx1_s, zero_s, sem_i, sem_d, sem_z, *,
                     pad_bits):
    td = h_ref.shape[0] * SUBLANES
    n_idx = td * TOP_K
    n_exp = fill_n.shape[0]
    zrows = zero_s.shape[0]
    i = pl.program_id(0)
    slot = i % 2
    idx_s = (idx0_s, idx1_s)

    def idx_copy(j, sl):
        return pltpu.make_async_copy(dest_hbm.at[pl.ds(j * n_idx, n_idx)], idx_s[sl], sem_i.at[sl])

    @pl.when(i == 0)
    def _():
        idx_copy(0, 0).start()

    for sl in range(2):
        @pl.when((i + 1 < pl.num_programs(0)) & (slot != sl))
        def _():
            idx_copy(i + 1, sl).start()

    def fill(wait):
        def go(copy, cond):
            @pl.when(cond)
            def _():
                copy.wait() if wait else copy.start()

        def per_expert(e, _):
            off = fill_off[e]
            npad = fill_n[e]
            n_single = npad & (SUBLANES - 1)
            for r in range(SUBLANES - 1):
                go(pltpu.make_async_copy(zero_s.at[pl.ds(0, 1), :], xs_out.at[pl.ds(off + r, 1), :], sem_z),
                   r < n_single)
            off = pl.multiple_of(off + n_single, SUBLANES)
            for bit in reversed(range(SUBLANES.bit_length() - 1, pad_bits)):
                size = 1 << bit
                go(pltpu.make_async_copy(zero_s.at[pl.ds(0, size), :], xs_out.at[pl.ds(off, size), :], sem_z),
                   (npad & size) != 0)
                off = pl.multiple_of(off + (npad & size), SUBLANES)
            return 0

        lax.fori_loop(0, n_exp, per_expert, 0)

        def per_tail_chunk(j, _):
            off = pl.multiple_of(tail[0] + j * zrows, zrows)
            copy = pltpu.make_async_copy(zero_s, xs_out.at[pl.ds(off, zrows), :], sem_z)
            copy.wait() if wait else copy.start()
            return 0

        lax.fori_loop(0, tail[1], per_tail_chunk, 0)

    @pl.when(i == 0)
    def _():
        zero_s[...] = jnp.zeros_like(zero_s)
        fill(False)

    for sl in range(2):
        @pl.when(slot == sl)
        def _():
            idx_copy(i, sl).wait()

            def issue(g, _):
                for u in range(SUBLANES):
                    for k in range(TOP_K):
                        dst = idx_s[sl][g * (SUBLANES * TOP_K) + u * TOP_K + k]
                        pltpu.make_async_copy(h_ref.at[g, pl.ds(u, 1), :], xs_out.at[pl.ds(dst, 1), :],
                                              sem_d).start(priority=k % 2)
                return 0

            lax.fori_loop(0, td // SUBLANES, issue, 0)

    for _ in range(TOP_K * td // zrows):
        pltpu.make_async_copy(zero_s, xs_out.at[pl.ds(0, zrows), :], sem_d).wait()

    @pl.when(i == 0)
    def _():
        fill(True)


def _dispatch(h2, dest_flat, fill_off, fill_n, tail, n_slots, td, bm):
    n, d = h2.shape
    pad_bits = (bm - 1).bit_length()
    assert (TOP_K * td) % (bm // 2) == 0
    grid_spec = pltpu.PrefetchScalarGridSpec(
        num_scalar_prefetch=3, grid=(n // td,),
        in_specs=[pl.BlockSpec((td // SUBLANES, SUBLANES, d), lambda i, fo, fn, tl: (i, 0, 0)),
                  pl.BlockSpec(memory_space=pl.ANY)],
        out_specs=pl.BlockSpec(memory_space=pl.ANY),
        scratch_shapes=[pltpu.SMEM((td * TOP_K,), jnp.int32), pltpu.SMEM((td * TOP_K,), jnp.int32),
                        pltpu.VMEM((bm // 2, d), F32),
                        pltpu.SemaphoreType.DMA((2,)), pltpu.SemaphoreType.DMA, pltpu.SemaphoreType.DMA])
    return pl.pallas_call(
        functools.partial(_dispatch_kernel, pad_bits=pad_bits), grid_spec=grid_spec,
        out_shape=jax.ShapeDtypeStruct((n_slots, d), F32),
        compiler_params=_params("arbitrary"), name="moe_dispatch",
    )(fill_off, fill_n, tail, h2.reshape(n // SUBLANES, SUBLANES, d), dest_flat)


def _expert_kernel(be_ref, nu_ref, x_ref, wgu_ref, bgu_ref, wd_ref, bd_ref, y_ref, wgu_s, wd_s):
    i = pl.program_id(0)
    d_ff = wd_ref.shape[1]

    @pl.when(i >= nu_ref[0])
    def _():
        y_ref[...] = jnp.zeros_like(y_ref)

    @pl.when(i < nu_ref[0])
    def _():
        @pl.when((i == 0) | (be_ref[i] != be_ref[jnp.maximum(i - 1, 0)]))
        def _():
            wgu_s[...] = wgu_ref[0].astype(BF16)
            wd_s[...] = wd_ref[0].astype(BF16)

        gu = _dot(x_ref[...].astype(BF16), wgu_s[...]) + bgu_ref[0]
        gate = jnp.minimum(gu[:, :d_ff], SWIGLU_LIMIT)
        up = jnp.clip(gu[:, d_ff:], -SWIGLU_LIMIT, SWIGLU_LIMIT)
        act = gate * _sigmoid(SWIGLU_ALPHA * gate) * (up + 1.0)
        y_ref[...] = _dot(act.astype(BF16), wd_s[...]) + bd_ref[0]


def _experts(xs, blk_expert, n_used, wgu, bgu, wd, bd, bm):
    n_slots, d = xs.shape
    n_exp, _, f2 = wgu.shape
    d_ff = wd.shape[1]
    n_blocks = n_slots // bm
    row = lambda i, be, nu: (jnp.minimum(i, nu[0] - 1), 0)
    grid_spec = pltpu.PrefetchScalarGridSpec(
        num_scalar_prefetch=2, grid=(n_blocks,),
        in_specs=[pl.BlockSpec((bm, d), row),
                  pl.BlockSpec((1, d, f2), lambda i, be, nu: (be[i], 0, 0)),
                  pl.BlockSpec((1, 1, f2), lambda i, be, nu: (be[i], 0, 0)),
                  pl.BlockSpec((1, d_ff, d), lambda i, be, nu: (be[i], 0, 0)),
                  pl.BlockSpec((1, 1, d), lambda i, be, nu: (be[i], 0, 0))],
        out_specs=pl.BlockSpec((bm, d), lambda i, be, nu: (i, 0)),
        scratch_shapes=[pltpu.VMEM((d, f2), BF16), pltpu.VMEM((d_ff, d), BF16)])
    return pl.pallas_call(
        _expert_kernel, grid_spec=grid_spec,
        out_shape=jax.ShapeDtypeStruct((n_slots, d), F32),
        compiler_params=_params("arbitrary"), name="moe_experts",
    )(blk_expert, n_used, xs, wgu, bgu.reshape(n_exp, 1, f2), wd, bd.reshape(n_exp, 1, d))


def _combine_kernel(x1_ref, meta_ref, g2_ref, fw_ref, dest_hbm, y_hbm, o_ref, rows_s, idx0_s, idx1_s, sem_i, sem_d):
    tc = x1_ref.shape[0]
    n_idx = tc * TOP_K
    i = pl.program_id(0)
    n = pl.num_programs(0)
    slot = i % 2
    idx_s = (idx0_s, idx1_s)

    def idx_copy(j, sl):
        return pltpu.make_async_copy(dest_hbm.at[pl.ds(j * n_idx, n_idx)], idx_s[sl], sem_i.at[sl])

    def gather(j, sl):
        idx_copy(j, sl).wait()

        def issue(g, _):
            for u in range(SUBLANES):
                for k in range(TOP_K):
                    src = idx_s[sl][g * (SUBLANES * TOP_K) + u * TOP_K + k]
                    pltpu.make_async_copy(y_hbm.at[pl.ds(src, 1), :], rows_s.at[sl, k, g, pl.ds(u, 1), :],
                                          sem_d.at[sl]).start(priority=k % 2)
            return 0

        lax.fori_loop(0, tc // SUBLANES, issue, 0)

    @pl.when(i == 0)
    def _():
        idx_copy(0, 0).start()

        @pl.when(n > 1)
        def _():
            idx_copy(1, 1).start()

        gather(0, 0)

    for sl in range(2):
        @pl.when((i + 1 < n) & (slot != sl))
        def _():
            gather(i + 1, sl)

    for sl in range(2):
        @pl.when((i + 2 < n) & (slot == sl))
        def _():
            idx_copy(i + 2, sl).start()

    for k in range(TOP_K):
        pltpu.make_async_copy(rows_s.at[1 - slot, k], rows_s.at[slot, k], sem_d.at[slot]).wait()

    def rows(k):
        return rows_s[slot, k].reshape(tc, rows_s.shape[-1])

    meta = meta_ref[...]
    moe = meta[:, TOP_K:TOP_K + 1] * rows(0)
    for k in range(1, TOP_K):
        moe = moe + meta[:, TOP_K + k:TOP_K + k + 1] * rows(k)
    o_ref[...] = _rms(x1_ref[...] + g2_ref[0] * moe, fw_ref[...])


def _combine(x1, meta, g2, final_w, dest_flat, y, t_seq, tc):
    n, d = x1.shape
    return pl.pallas_call(
        _combine_kernel, grid=(n // tc,),
        in_specs=[pl.BlockSpec((tc, d), lambda i: (i, 0)),
                  pl.BlockSpec((tc, LANES), lambda i: (i, 0)),
                  pl.BlockSpec((1, 1, d), lambda i: (i * tc // t_seq, 0, 0)),
                  pl.BlockSpec((1, d), lambda i: (0, 0)),
                  pl.BlockSpec(memory_space=pl.ANY),
                  pl.BlockSpec(memory_space=pl.ANY)],
        out_specs=pl.BlockSpec((tc, d), lambda i: (i, 0)),
        out_shape=jax.ShapeDtypeStruct((n, d), F32),
        scratch_shapes=[pltpu.VMEM((2, TOP_K, tc // SUBLANES, SUBLANES, d), F32),
                        pltpu.SMEM((tc * TOP_K,), jnp.int32), pltpu.SMEM((tc * TOP_K,), jnp.int32),
                        pltpu.SemaphoreType.DMA((2,)), pltpu.SemaphoreType.DMA((2,))],
        compiler_params=_params("arbitrary"), name="moe_combine",
    )(x1, meta, g2, final_w, dest_flat, y)


def _gate_weights(wa, wx):
    _, heads, hd, _ = wa.shape
    hh = heads // 2
    eye = jnp.eye(hh, dtype=wa.dtype)

    def blockdiag(wsel):
        return jnp.einsum('hij,hg->higj', wsel, eye).reshape(hh * hd, hh * hd)

    halves = []
    for s in range(2):
        sl = slice(s * hh, (s + 1) * hh)
        halves.append(jnp.concatenate([blockdiag(wa[0, sl]), blockdiag(wx[0, sl]),
                                       blockdiag(wa[1, sl]), blockdiag(wx[1, sl])], axis=1))
    return jnp.stack(halves).astype(BF16)


def kernel(x, c, ctx, c_ctx, norm1_w, norm2_w, w_ada, b_ada, w_in, hg_lb_logits, hg_norm_w, rg_conv_w, rg_conv_b,
           rg_wa, rg_ba, rg_wx, rg_bx, rg_lambda, w_out, router_w, router_b, w_gate_up, b_gate_up, w_down,
           b_down, final_norm_w):
    b, t, d = x.shape
    tcx = ctx.shape[1]
    n_exp = router_w.shape[-1]
    n_tok = b * t
    depth = w_in.shape[0]
    lb_all = jnp.cumsum(jax.nn.softmax(hg_lb_logits.astype(F32), axis=0), axis=0)

    for l in range(depth):
        assert l == depth - 1, "context stream update of non-final layers is not implemented"
        pad = (-(b + 1)) % 8
        c_all = jnp.concatenate([c, c_ctx[None], jnp.zeros((pad, d), F32)], axis=0)
        mod = _mod(c_all, w_ada[l], b_ada[l])
        sh1, sc1, g1, sh2, sc2, g2 = [m[:b, None, :] for m in jnp.split(mod, 6, axis=-1)]
        csh1, csc1 = [m[b:b + 1, None, :] for m in jnp.split(mod, 6, axis=-1)[:2]]

        w_in_b = w_in[l].astype(BF16)
        nw1 = norm1_w[l].reshape(1, d)
        tm = min(TOKEN_TILE, t)
        p_lat = _inproj(x, sh1, sc1, nw1, w_in_b, tm)
        p_ctx = _inproj(ctx, csh1, csc1, nw1, w_in_b, min(256, tcx))

        hg = _hgrn2(p_lat, p_ctx, lb_all[l], hg_norm_w[l].reshape(1, -1))
        ch = rg_conv_w.shape[-1]
        wg = _gate_weights(rg_wa[l], rg_wx[l])
        bg = jnp.concatenate([rg_ba[l, 0], rg_bx[l, 0], rg_ba[l, 1], rg_bx[l, 1]]).reshape(1, 4 * ch)
        rg = _rglru(p_lat, p_ctx, rg_conv_w[l], rg_conv_b[l].reshape(1, ch), wg, bg, rg_lambda[l])

        x1, h2, meta, cnt = _mix(hg, rg, x, g1, sh2, sc2, norm2_w[l].reshape(1, d), w_out[l].astype(BF16),
                                 router_w[l].astype(BF16), router_b[l].reshape(1, n_exp), tm)

        bm = EXPERT_BLOCK
        meta2 = meta.reshape(n_tok, LANES)
        counts = cnt[0, :n_exp].astype(jnp.int32)
        padded = (counts + bm - 1) // bm * bm
        pad_end = jnp.cumsum(padded)
        pad_start = pad_end - padded
        idx = meta2[:, 0:TOP_K].astype(jnp.int32)
        rank = meta2[:, 2 * TOP_K:3 * TOP_K].astype(jnp.int32)
        dest = (pad_start[idx] + rank).reshape(-1)
        n_blocks = -(-n_tok * TOP_K // bm) + n_exp
        blk_start = jnp.arange(n_blocks, dtype=jnp.int32) * bm
        blk_expert = jnp.minimum(jnp.sum(blk_start[:, None] >= pad_end[None, :], axis=1), n_exp - 1).astype(jnp.int32)
        n_used = (pad_end[-1:] // bm).astype(jnp.int32)

        n_slots = n_blocks * bm
        tail = jnp.stack([pad_end[-1], (n_slots - pad_end[-1]) // (bm // 2)]).astype(jnp.int32)
        xs = _dispatch(h2.reshape(n_tok, d), dest, pad_start + counts, padded - counts, tail, n_slots,
                       min(DISPATCH_TILE, t), bm)
        y = _experts(xs, blk_expert, n_used, w_gate_up[l], b_gate_up[l], w_down[l], b_down[l], bm)
        out = _combine(x1.reshape(n_tok, d), meta2, g2, final_norm_w.reshape(1, d), dest, y, t,
                       min(COMBINE_TILE, t))
        return out.reshape(b, t, d)
```

```python
import functools

import jax
import jax.numpy as jnp
from jax import lax
from jax.experimental import pallas as pl
from jax.experimental.pallas import tpu as pltpu

GRID_W = 64
HG_HEADS = 4
HG_CHUNK = 32
RG_HEADS = 8
RG_CONV = 4
RG_C = 8.0
TOP_K = 4
SWIGLU_LIMIT = 7.0
SWIGLU_ALPHA = 1.702
EPS = 1e-6

LANES = 128
SUBLANES = 8
VMEM_LIMIT = 56 * 1024 * 1024

TOKEN_TILE = 512
EXPERT_BLOCK = 512
DISPATCH_TILE = 512
COMBINE_TILE = 256

F32 = jnp.float32
BF16 = jnp.bfloat16
HIGHEST = lax.Precision.HIGHEST


def _params(*sem):
    return pltpu.CompilerParams(dimension_semantics=sem, vmem_limit_bytes=VMEM_LIMIT)


def _sigmoid(x):
    return 0.5 * jnp.tanh(0.5 * x) + 0.5


def _silu(x):
    return x * _sigmoid(x)


def _rms(x, w):
    return x * lax.rsqrt(jnp.mean(x * x, axis=-1, keepdims=True) + EPS) * w


def _dot(a, b):
    return jnp.dot(a, b, preferred_element_type=F32)


def _dot_nt(a, b):
    return lax.dot_general(a, b, (((1,), (1,)), ((), ())), preferred_element_type=F32)


def _mod_kernel(c_ref, w_ref, b_ref, o_ref):
    o_ref[...] = jnp.dot(_silu(c_ref[...]), w_ref[...], preferred_element_type=F32,
                         precision=HIGHEST) + b_ref[...]


def _mod(c_all, w_ada, b_ada):
    r, d = c_all.shape
    n = w_ada.shape[1]
    tn = n // 4
    return pl.pallas_call(
        _mod_kernel, grid=(n // tn,),
        in_specs=[pl.BlockSpec((r, d), lambda j: (0, 0)),
                  pl.BlockSpec((d, tn), lambda j: (0, j)),
                  pl.BlockSpec((1, tn), lambda j: (0, j))],
        out_specs=pl.BlockSpec((r, tn), lambda j: (0, j)),
        out_shape=jax.ShapeDtypeStruct((r, n), F32),
        compiler_params=_params("arbitrary"), name="adaln_mod",
    )(c_all, w_ada, b_ada.reshape(1, n))


def _inproj_kernel(x_ref, sh_ref, sc_ref, nw_ref, w_ref, o_ref):
    h = _rms(x_ref[0], nw_ref[...]) * (1.0 + sc_ref[0]) + sh_ref[0]
    o_ref[0] = _dot(h.astype(BF16), w_ref[...])


def _inproj(x, shift, scale, norm_w, w_bf16, tm):
    b, t, d = x.shape
    n = w_bf16.shape[1]
    per_batch = shift.shape[0] == b
    mod_map = (lambda i, j: (i, 0, 0)) if per_batch else (lambda i, j: (0, 0, 0))
    return pl.pallas_call(
        _inproj_kernel, grid=(b, t // tm),
        in_specs=[pl.BlockSpec((1, tm, d), lambda i, j: (i, j, 0)),
                  pl.BlockSpec((1, 1, d), mod_map),
                  pl.BlockSpec((1, 1, d), mod_map),
                  pl.BlockSpec((1, d), lambda i, j: (0, 0)),
                  pl.BlockSpec((d, n), lambda i, j: (0, 0))],
        out_specs=pl.BlockSpec((1, tm, n), lambda i, j: (i, j, 0)),
        out_shape=jax.ShapeDtypeStruct((b, t, n), F32),
        compiler_params=_params("arbitrary", "arbitrary"), name="norm_inproj",
    )(x, shift, scale, norm_w, w_bf16)


def _split_bf16(x):
    hi = x.astype(BF16)
    return hi, (x - hi.astype(F32)).astype(BF16)


def _hg_kernel(q_ref, v_ref, zf_ref, zb_ref, g_ref, cv_ref, czf_ref, czb_ref, lb_ref, nw_ref, o_ref,
               o_s, qd_s, u_s, dec_s, st_s, cu_s, cdec_s, tot_s, *, t_lat, t_ctx):
    c = HG_CHUNK
    dk = q_ref.shape[-1]
    lb = lb_ref[...]
    lbf, lbb = lb[0:1], lb[1:2]

    def prep(rb, zf, zb, q, v, u_out, dec_out, blk, r0):
        nc = rb // c
        ri = lax.broadcasted_iota(jnp.int32, (rb, rb), 0)
        ci = lax.broadcasted_iota(jnp.int32, (rb, rb), 1)
        same = (ri // c) == (ci // c)
        low = same & (ci <= ri)
        upp = same & (ci >= ri)
        ff = lbf + (1.0 - lbf) * _sigmoid(zf)
        fb = lbb + (1.0 - lbb) * _sigmoid(zb)
        lgf, lgb = jnp.log(ff), jnp.log(fb)
        rhs = jnp.concatenate([*_split_bf16(lgf), *_split_bf16(lgb)], axis=1)
        pre = _dot(low.astype(BF16), rhs)
        bcf = pre[:, :dk] + pre[:, dk:2 * dk]
        pfb = pre[:, 2 * dk:3 * dk] + pre[:, 3 * dk:]
        half = tot_s.shape[0] // 2
        tot_s[0:rb, :] = bcf
        tot_s[half:half + rb, :] = pfb
        totf = tot_s[pl.ds(c - 1, nc, stride=c), :]
        totb = tot_s[pl.ds(half + c - 1, nc, stride=c), :]

        def spread(tot):
            return jnp.broadcast_to(tot[:, None, :], (nc, c, dk)).reshape(rb, dk)

        remf = spread(totf) - bcf
        remb = pfb - lgb
        bcb = spread(totb) - remb
        kkf, kkb = 1.0 - ff, 1.0 - fb
        kef = (kkf * jnp.exp(remf)).astype(BF16)
        keb = (kkb * jnp.exp(remb)).astype(BF16)
        chunk_of_row = lax.broadcasted_iota(jnp.int32, (rb, dk), 0) // c
        zero = jnp.zeros((rb, dk), BF16)
        keys = jnp.concatenate([jnp.where(chunk_of_row == j, ke, zero) for ke in (kef, keb) for j in range(nc)],
                               axis=1)
        u_all = _dot(v.T.astype(BF16), keys)
        c0 = blk * nc
        for d in range(2):
            for j in range(nc):
                u_out[d, c0 + j] = u_all[:, (d * nc + j) * dk:(d * nc + j + 1) * dk]
        dec_out[0, pl.ds(pl.multiple_of(c0, nc), nc), :] = jnp.exp(totf)
        dec_out[1, pl.ds(pl.multiple_of(c0, nc), nc), :] = jnp.exp(totb)
        if q is None:
            return
        sq = _silu(q)
        qdf = (sq * jnp.exp(bcf)).astype(BF16)
        qdb = (sq * jnp.exp(bcb)).astype(BF16)
        kdf = (kkf * jnp.exp(-bcf)).astype(BF16)
        kdb = (kkb * jnp.exp(-bcb)).astype(BF16)
        p = jnp.where(low, _dot_nt(qdf, kdf), 0.0) + jnp.where(upp, _dot_nt(qdb, kdb), 0.0)
        o_s[pl.ds(r0, rb), :] = _dot(p.astype(BF16), v.astype(BF16))
        qd_s[pl.ds(r0, rb), 0:dk] = qdf
        qd_s[pl.ds(r0, rb), dk:2 * dk] = qdb

    def scan(n, ur, decr, keep, carry):
        def body(i, carry):
            sf, sb = carry
            j = n - 1 - i
            if keep:
                st_s[i, :, 0:dk] = sf.astype(BF16)
                st_s[j, :, dk:2 * dk] = sb.astype(BF16)
            return sf * decr[0, pl.ds(i, 1), :] + ur[0, i], sb * decr[1, pl.ds(j, 1), :] + ur[1, j]

        return lax.fori_loop(0, n, body, carry, unroll=2)

    rbc = min(256, t_ctx)
    for blk in range(t_ctx // rbc):
        r0 = blk * rbc
        prep(rbc, czf_ref[0, r0:r0 + rbc, :], czb_ref[0, r0:r0 + rbc, :], None, cv_ref[0, r0:r0 + rbc, :],
             cu_s, cdec_s, blk, r0)
    zero = jnp.zeros((dk, dk), F32)
    carry = scan(t_ctx // c, cu_s, cdec_s, False, (zero, zero))

    rbl = min(256, t_lat)

    def lat_prep(blk, _):
        r0 = pl.multiple_of(blk * rbl, rbl)
        rows = pl.ds(r0, rbl)
        prep(rbl, zf_ref[0, rows, :], zb_ref[0, rows, :], q_ref[0, rows, :], v_ref[0, rows, :], u_s, dec_s, blk, r0)
        return 0

    lax.fori_loop(0, t_lat // rbl, lat_prep, 0, unroll=4)
    scan(t_lat // c, u_s, dec_s, True, carry)

    def finish(blk, _):
        r0 = pl.multiple_of(blk * rbl, rbl)
        inter = [_dot_nt(qd_s[pl.ds(r0 + j * c, c), :], st_s[blk * (rbl // c) + j]) for j in range(rbl // c)]
        o = o_s[pl.ds(r0, rbl), :] + jnp.concatenate(inter, axis=0)
        o_ref[0, pl.ds(r0, rbl), :] = _rms(o, nw_ref[...]) * _silu(g_ref[0, pl.ds(r0, rbl), :])
        return 0

    lax.fori_loop(0, t_lat // rbl, finish, 0, unroll=4)


def _hgrn2(p_lat, p_ctx, lb, norm_w):
    b, t, _ = p_lat.shape
    tc = p_ctx.shape[1]
    hw = lb.shape[1]
    dk = hw // HG_HEADS
    nh = HG_HEADS
    c = HG_CHUNK

    def col(k, tt):
        return pl.BlockSpec((1, tt, dk), lambda i, h, k=k: (i, 0, k * nh + h))

    return pl.pallas_call(
        functools.partial(_hg_kernel, t_lat=t, t_ctx=tc), grid=(b, nh),
        in_specs=[col(0, t), col(1, t), col(2, t), col(3, t), col(4, t),
                  col(1, tc), col(2, tc), col(3, tc),
                  pl.BlockSpec((2, dk), lambda i, h: (0, h)),
                  pl.BlockSpec((1, dk), lambda i, h: (0, h))],
        out_specs=pl.BlockSpec((1, t, dk), lambda i, h: (i, 0, h)),
        out_shape=jax.ShapeDtypeStruct((b, t, hw), F32),
        scratch_shapes=[pltpu.VMEM((t, dk), F32),
                        pltpu.VMEM((t, 2 * dk), BF16),
                        pltpu.VMEM((2, t // c, dk, dk), F32),
                        pltpu.VMEM((2, t // c, dk), F32),
                        pltpu.VMEM((t // c, dk, 2 * dk), BF16),
                        pltpu.VMEM((2, tc // c, dk, dk), F32),
                        pltpu.VMEM((2, tc // c, dk), F32),
                        pltpu.VMEM((2 * min(256, max(t, tc)), dk), F32)],
        compiler_params=_params("arbitrary", "arbitrary"), name="hgrn2",
    )(p_lat, p_lat, p_lat, p_lat, p_lat, p_ctx, p_ctx, p_ctx, lb, norm_w)


def _shift_rows(x, k):
    n = x.shape[0]
    y = pltpu.roll(x, k % n, 0)
    r = lax.broadcasted_iota(jnp.int32, x.shape, 0)
    return jnp.where((r >= k) & (r < n + k), y, 0.0)


def _rg_kernel(rx_ref, rgate_ref, crx_ref, cw_ref, cb_ref, wg_ref, bg_ref, lam_ref, o_ref,
               xc_s, af_s, bf_s, ab_s, bb_s, hf_s, hb_s, caf_s, cbf_s, cab_s, cbb_s, *, t_lat, t_ctx):
    w = GRID_W
    rows = t_lat // w
    ch = rx_ref.shape[-1]
    half = ch // 2
    cw = cw_ref[...]
    cb = cb_ref[...]
    bg = bg_ref[...]
    nl = -lam_ref[...]
    cdec = -RG_C * (jnp.maximum(nl, 0.0) + jnp.log1p(jnp.exp(-jnp.abs(nl))))

    def conv(xm2, xm1, x0, xp1):
        return cb + cw[0:1] * xm2 + cw[1:2] * xm1 + cw[2:3] * x0 + cw[3:4] * xp1

    def gates(xc):
        xb = xc.astype(BF16)
        g0 = _dot(xb[:, :half], wg_ref[0])
        g1 = _dot(xb[:, half:], wg_ref[1])
        outs = []
        for d in range(2):
            pre = []
            for s in (2 * d, 2 * d + 1):
                pre.append(jnp.concatenate([g0[:, s * half:(s + 1) * half], g1[:, s * half:(s + 1) * half]],
                                           axis=1) + bg[:, s * ch:(s + 1) * ch])
            log_a = cdec[d:d + 1] * _sigmoid(pre[0])
            a = jnp.exp(log_a)
            mult = jnp.sqrt(-jnp.tanh(log_a) * (a * a + 1.0))
            outs += [a, mult * _sigmoid(pre[1]) * xc]
        return outs

    xctx = crx_ref[0]
    xcc = conv(_shift_rows(xctx, 2), _shift_rows(xctx, 1), xctx, _shift_rows(xctx, -1))
    caf_s[...], cbf_s[...], cab_s[...], cbb_s[...] = gates(xcc)

    def cstep(i, carry):
        hf, hb = carry
        hf = caf_s[pl.ds(i, 1), :] * hf + cbf_s[pl.ds(i, 1), :]
        j = t_ctx - 1 - i
        hb = cab_s[pl.ds(j, 1), :] * hb + cbb_s[pl.ds(j, 1), :]
        return hf, hb

    zrow = jnp.zeros((1, ch), F32)
    hf0, hb0 = lax.fori_loop(0, t_ctx, cstep, (zrow, zrow), unroll=8)

    def slab(rr):
        if 0 <= rr < rows:
            return rx_ref[0, rr * w:(rr + 1) * w, :]
        if rr < 0:
            return _shift_rows(rx_ref[0, (rr + rows) * w:(rr + rows + 1) * w, :], 1)
        return _shift_rows(rx_ref[0, (rr - rows) * w:(rr - rows + 1) * w, :], -1)

    for r in range(rows):
        xc_s[r * w:(r + 1) * w, :] = conv(slab(r - 2), slab(r - 1), slab(r), slab(r + 1))

    mb = min(256, t_lat)

    def gbody(i, _):
        r0 = pl.multiple_of(i * mb, mb)
        a_f, b_f, a_b, b_b = gates(xc_s[pl.ds(r0, mb), :])
        af_s[pl.ds(r0, mb), :] = a_f
        bf_s[pl.ds(r0, mb), :] = b_f
        ab_s[pl.ds(r0, mb), :] = a_b
        bb_s[pl.ds(r0, mb), :] = b_b
        return 0

    lax.fori_loop(0, t_lat // mb, gbody, 0)

    def l1(i, _):
        pf = pl.multiple_of(i * w, w)
        qf = pl.multiple_of((i - 1) * w, w)
        a = af_s[pl.ds(pf, w), :]
        af_s[pl.ds(pf, w), :] = a * af_s[pl.ds(qf, w), :]
        bf_s[pl.ds(pf, w), :] = a * bf_s[pl.ds(qf, w), :] + bf_s[pl.ds(pf, w), :]
        pb = pl.multiple_of((rows - 1 - i) * w, w)
        qb = pl.multiple_of((rows - i) * w, w)
        a = ab_s[pl.ds(pb, w), :]
        ab_s[pl.ds(pb, w), :] = a * ab_s[pl.ds(qb, w), :]
        bb_s[pl.ds(pb, w), :] = a * bb_s[pl.ds(qb, w), :] + bb_s[pl.ds(pb, w), :]
        return 0

    lax.fori_loop(1, rows, l1, 0)

    last = (rows - 1) * w

    def l2(i, carry):
        hf, hb = carry
        hf_s[pl.ds(i, 1), :] = hf
        hf = af_s[pl.ds(last + i, 1), :] * hf + bf_s[pl.ds(last + i, 1), :]
        j = w - 1 - i
        hb_s[pl.ds(j, 1), :] = hb
        hb = ab_s[pl.ds(j, 1), :] * hb + bb_s[pl.ds(j, 1), :]
        return hf, hb

    lax.fori_loop(0, w, l2, (hf0, hb0))

    def l3(i, _):
        p = pl.multiple_of(i * w, w)
        h = (af_s[pl.ds(p, w), :] * hf_s[...] + bf_s[pl.ds(p, w), :]
             + ab_s[pl.ds(p, w), :] * hb_s[...] + bb_s[pl.ds(p, w), :])
        o_ref[0, pl.ds(p, w), :] = jax.nn.gelu(rgate_ref[0, pl.ds(p, w), :]) * h
        return 0

    lax.fori_loop(0, rows, l3, 0)


def _rglru(p_lat, p_ctx, conv_w, conv_b, wg, bg, lam):
    b, t, _ = p_lat.shape
    tc = p_ctx.shape[1]
    ch = conv_w.shape[1]
    rx_blk = (p_lat.shape[2] - 2 * ch) // ch
    full = lambda shape: pl.BlockSpec(shape, lambda i: (0,) * len(shape))
    big = lambda: pltpu.VMEM((t, ch), F32)
    small = lambda: pltpu.VMEM((tc, ch), F32)
    return pl.pallas_call(
        functools.partial(_rg_kernel, t_lat=t, t_ctx=tc), grid=(b,),
        in_specs=[pl.BlockSpec((1, t, ch), lambda i: (i, 0, rx_blk)),
                  pl.BlockSpec((1, t, ch), lambda i: (i, 0, rx_blk + 1)),
                  pl.BlockSpec((1, tc, ch), lambda i: (i, 0, rx_blk)),
                  full(conv_w.shape), full(conv_b.shape), full(wg.shape), full(bg.shape), full(lam.shape)],
        out_specs=pl.BlockSpec((1, t, ch), lambda i: (i, 0, 0)),
        out_shape=jax.ShapeDtypeStruct((b, t, ch), F32),
        scratch_shapes=[big(), big(), big(), big(), big(),
                        pltpu.VMEM((GRID_W, ch), F32), pltpu.VMEM((GRID_W, ch), F32),
                        small(), small(), small(), small()],
        compiler_params=_params("arbitrary"), name="rglru",
    )(p_lat, p_lat, p_ctx, conv_w, conv_b, wg, bg, lam)


def _mix_kernel(hg_ref, rg_ref, x_ref, g1_ref, sh_ref, sc_ref, nw_ref, wo_ref, rw_ref, rb_ref,
                x1_ref, h2_ref, meta_ref, cnt_ref, base_s, *, n_exp):
    tm = x_ref.shape[1]

    @pl.when((pl.program_id(0) == 0) & (pl.program_id(1) == 0))
    def _():
        base_s[...] = jnp.zeros_like(base_s)

    hcat = jnp.concatenate([hg_ref[0], rg_ref[0]], axis=1).astype(BF16)
    x1 = x_ref[0] + g1_ref[0] * _dot(hcat, wo_ref[...])
    x1_ref[0] = x1
    h2 = _rms(x1, nw_ref[...]) * (1.0 + sc_ref[0]) + sh_ref[0]
    h2_ref[0] = h2
    logits = _dot(h2.astype(BF16), rw_ref[...]) + rb_ref[...]

    lane_e = lax.broadcasted_iota(jnp.int32, (tm, n_exp), 1).astype(F32)
    vals, idxs = [], []
    cur = logits
    for _ in range(TOP_K):
        m = jnp.max(cur, axis=1, keepdims=True)
        ix = jnp.min(jnp.where(cur == m, lane_e, float(n_exp)), axis=1, keepdims=True)
        vals.append(m)
        idxs.append(ix)
        cur = jnp.where(lane_e == ix, -jnp.inf, cur)
    ex = [jnp.exp(v - vals[0]) for v in vals]
    den = ex[0] + ex[1] + ex[2] + ex[3]

    lane = lax.broadcasted_iota(jnp.int32, (tm, LANES), 1)
    lane_f = lane.astype(F32)
    onehot = jnp.zeros((tm, LANES), F32)
    for k in range(TOP_K):
        onehot = jnp.where(lane_f == idxs[k] + float(k * n_exp), 1.0, onehot)
    ri = lax.broadcasted_iota(jnp.int32, (tm, tm), 0)
    ci = lax.broadcasted_iota(jnp.int32, (tm, tm), 1)
    prefix = _dot((ci < ri).astype(BF16), onehot.astype(BF16))
    tot = jnp.broadcast_to(prefix[tm - 1:tm] + onehot[tm - 1:tm], (8, LANES))
    lane8 = lax.broadcasted_iota(jnp.int32, (8, LANES), 1)
    off = base_s[...]
    tot_all = tot
    for j in range(1, TOP_K):
        rolled = pltpu.roll(tot, j * n_exp, 1)
        off = off + jnp.where(lane8 >= j * n_exp, rolled, 0.0)
        tot_all = tot_all + rolled
    pos = onehot * (prefix + off[0:1])
    meta = jnp.zeros((tm, LANES), F32)
    for k in range(TOP_K):
        in_k = (lane >= k * n_exp) & (lane < (k + 1) * n_exp)
        rank = jnp.sum(jnp.where(in_k, pos, 0.0), axis=1, keepdims=True)
        meta = jnp.where(lane == k, idxs[k], meta)
        meta = jnp.where(lane == TOP_K + k, ex[k] / den, meta)
        meta = jnp.where(lane == 2 * TOP_K + k, rank, meta)
    meta_ref[0] = meta
    base_s[...] = base_s[...] + tot_all
    cnt_ref[...] = base_s[...]


def _mix(hg, rg, x, g1, sh2, sc2, norm_w, wo_bf16, router_w, router_b, tm):
    b, t, d = x.shape
    hw = hg.shape[2]
    n_exp = router_w.shape[1]
    assert TOP_K * n_exp == LANES
    tok = lambda last: pl.BlockSpec((1, tm, last), lambda i, j: (i, j, 0))
    per_b = pl.BlockSpec((1, 1, d), lambda i, j: (i, 0, 0))
    full = lambda shape: pl.BlockSpec(shape, lambda i, j: (0,) * len(shape))
    return pl.pallas_call(
        functools.partial(_mix_kernel, n_exp=n_exp), grid=(b, t // tm),
        in_specs=[tok(hw), tok(hw), tok(d), per_b, per_b, per_b, full((1, d)),
                  full(wo_bf16.shape), full(router_w.shape), full((1, n_exp))],
        out_specs=[tok(d), tok(d), tok(LANES), pl.BlockSpec((8, LANES), lambda i, j: (0, 0))],
        out_shape=[jax.ShapeDtypeStruct((b, t, d), F32), jax.ShapeDtypeStruct((b, t, d), F32),
                   jax.ShapeDtypeStruct((b, t, LANES), F32), jax.ShapeDtypeStruct((8, LANES), F32)],
        scratch_shapes=[pltpu.VMEM((8, LANES), F32)],
        compiler_params=_params("arbitrary", "arbitrary"), name="outproj_router",
    )(hg, rg, x, g1, sh2, sc2, norm_w, wo_bf16, router_w, router_b)


def _dispatch_kernel(fill_off, fill_n, tail, h_ref, dest_hbm, xs_out, idx0_s, idx1_s, zero_s, sem_i, sem_d, sem_z, *,
                     pad_bits):
    td = h_ref.shape[0] * SUBLANES
    n_idx = td * TOP_K
    n_exp = fill_n.shape[0]
    zrows = zero_s.shape[0]
    i = pl.program_id(0)
    slot = i % 2
    idx_s = (idx0_s, idx1_s)

    def idx_copy(j, sl):
        return pltpu.make_async_copy(dest_hbm.at[pl.ds(j * n_idx, n_idx)], idx_s[sl], sem_i.at[sl])

    @pl.when(i == 0)
    def _():
        idx_copy(0, 0).start()

    for sl in range(2):
        @pl.when((i + 1 < pl.num_programs(0)) & (slot != sl))
        def _():
            idx_copy(i + 1, sl).start()

    def fill(wait):
        def go(copy, cond):
            @pl.when(cond)
            def _():
                copy.wait() if wait else copy.start()

        def per_expert(e, _):
            off = fill_off[e]
            npad = fill_n[e]
            n_single = npad & (SUBLANES - 1)
            for r in range(SUBLANES - 1):
                go(pltpu.make_async_copy(zero_s.at[pl.ds(0, 1), :], xs_out.at[pl.ds(off + r, 1), :], sem_z),
                   r < n_single)
            off = pl.multiple_of(off + n_single, SUBLANES)
            for bit in reversed(range(SUBLANES.bit_length() - 1, pad_bits)):
                size = 1 << bit
                go(pltpu.make_async_copy(zero_s.at[pl.ds(0, size), :], xs_out.at[pl.ds(off, size), :], sem_z),
                   (npad & size) != 0)
                off = pl.multiple_of(off + (npad & size), SUBLANES)
            return 0

        lax.fori_loop(0, n_exp, per_expert, 0)

        def per_tail_chunk(j, _):
            off = pl.multiple_of(tail[0] + j * zrows, zrows)
            copy = pltpu.make_async_copy(zero_s, xs_out.at[pl.ds(off, zrows), :], sem_z)
            copy.wait() if wait else copy.start()
            return 0

        lax.fori_loop(0, tail[1], per_tail_chunk, 0)

    @pl.when(i == 0)
    def _():
        zero_s[...] = jnp.zeros_like(zero_s)
        fill(False)

    for sl in range(2):
        @pl.when(slot == sl)
        def _():
            idx_copy(i, sl).wait()

            for g in range(td // SUBLANES):
                for u in range(SUBLANES):
                    for k in range(TOP_K):
                        dst = idx_s[sl][(g * SUBLANES + u) * TOP_K + k]
                        pltpu.make_async_copy(h_ref.at[g, pl.ds(u, 1), :], xs_out.at[pl.ds(dst, 1), :],
                                              sem_d).start(priority=k % 2)

    for _ in range(TOP_K * td // zrows):
        pltpu.make_async_copy(zero_s, xs_out.at[pl.ds(0, zrows), :], sem_d).wait()

    @pl.when(i == 0)
    def _():
        fill(True)


def _dispatch(h2, dest_flat, fill_off, fill_n, tail, n_slots, td, bm):
    n, d = h2.shape
    pad_bits = (bm - 1).bit_length()
    assert (TOP_K * td) % (bm // 2) == 0
    grid_spec = pltpu.PrefetchScalarGridSpec(
        num_scalar_prefetch=3, grid=(n // td,),
        in_specs=[pl.BlockSpec((td // SUBLANES, SUBLANES, d), lambda i, fo, fn, tl: (i, 0, 0)),
                  pl.BlockSpec(memory_space=pl.ANY)],
        out_specs=pl.BlockSpec(memory_space=pl.ANY),
        scratch_shapes=[pltpu.SMEM((td * TOP_K,), jnp.int32), pltpu.SMEM((td * TOP_K,), jnp.int32),
                        pltpu.VMEM((bm // 2, d), F32),
                        pltpu.SemaphoreType.DMA((2,)), pltpu.SemaphoreType.DMA, pltpu.SemaphoreType.DMA])
    return pl.pallas_call(
        functools.partial(_dispatch_kernel, pad_bits=pad_bits), grid_spec=grid_spec,
        out_shape=jax.ShapeDtypeStruct((n_slots, d), F32),
        compiler_params=_params("arbitrary"), name="moe_dispatch",
    )(fill_off, fill_n, tail, h2.reshape(n // SUBLANES, SUBLANES, d), dest_flat)


def _expert_kernel(be_ref, nu_ref, x_ref, wgu_ref, bgu_ref, wd_ref, bd_ref, y_ref, wgu_s, wd_s):
    i = pl.program_id(0)
    d_ff = wd_ref.shape[1]

    @pl.when(i >= nu_ref[0])
    def _():
        y_ref[...] = jnp.zeros_like(y_ref)

    @pl.when(i < nu_ref[0])
    def _():
        @pl.when((i == 0) | (be_ref[i] != be_ref[jnp.maximum(i - 1, 0)]))
        def _():
            wgu_s[...] = wgu_ref[0].astype(BF16)
            wd_s[...] = wd_ref[0].astype(BF16)

        gu = _dot(x_ref[...].astype(BF16), wgu_s[...]) + bgu_ref[0]
        gate = jnp.minimum(gu[:, :d_ff], SWIGLU_LIMIT)
        up = jnp.clip(gu[:, d_ff:], -SWIGLU_LIMIT, SWIGLU_LIMIT)
        act = gate * _sigmoid(SWIGLU_ALPHA * gate) * (up + 1.0)
        y_ref[...] = _dot(act.astype(BF16), wd_s[...]) + bd_ref[0]


def _experts(xs, blk_expert, n_used, wgu, bgu, wd, bd, bm):
    n_slots, d = xs.shape
    n_exp, _, f2 = wgu.shape
    d_ff = wd.shape[1]
    n_blocks = n_slots // bm
    row = lambda i, be, nu: (jnp.minimum(i, nu[0] - 1), 0)
    grid_spec = pltpu.PrefetchScalarGridSpec(
        num_scalar_prefetch=2, grid=(n_blocks,),
        in_specs=[pl.BlockSpec((bm, d), row),
                  pl.BlockSpec((1, d, f2), lambda i, be, nu: (be[i], 0, 0)),
                  pl.BlockSpec((1, 1, f2), lambda i, be, nu: (be[i], 0, 0)),
                  pl.BlockSpec((1, d_ff, d), lambda i, be, nu: (be[i], 0, 0)),
                  pl.BlockSpec((1, 1, d), lambda i, be, nu: (be[i], 0, 0))],
        out_specs=pl.BlockSpec((bm, d), lambda i, be, nu: (i, 0)),
        scratch_shapes=[pltpu.VMEM((d, f2), BF16), pltpu.VMEM((d_ff, d), BF16)])
    return pl.pallas_call(
        _expert_kernel, grid_spec=grid_spec,
        out_shape=jax.ShapeDtypeStruct((n_slots, d), F32),
        compiler_params=_params("arbitrary"), name="moe_experts",
    )(blk_expert, n_used, xs, wgu, bgu.reshape(n_exp, 1, f2), wd, bd.reshape(n_exp, 1, d))


def _combine_kernel(x1_ref, meta_ref, g2_ref, fw_ref, dest_hbm, y_hbm, o_ref, rows_s, idx0_s, idx1_s, sem_i, sem_d):
    tc = x1_ref.shape[0]
    n_idx = tc * TOP_K
    i = pl.program_id(0)
    n = pl.num_programs(0)
    slot = i % 2
    idx_s = (idx0_s, idx1_s)

    def idx_copy(j, sl):
        return pltpu.make_async_copy(dest_hbm.at[pl.ds(j * n_idx, n_idx)], idx_s[sl], sem_i.at[sl])

    def gather(j, sl):
        idx_copy(j, sl).wait()

        for g in range(tc // SUBLANES):
            for u in range(SUBLANES):
                for k in range(TOP_K):
                    src = idx_s[sl][(g * SUBLANES + u) * TOP_K + k]
                    pltpu.make_async_copy(y_hbm.at[pl.ds(src, 1), :], rows_s.at[sl, k, g, pl.ds(u, 1), :],
                                          sem_d.at[sl]).start(priority=k % 2)

    @pl.when(i == 0)
    def _():
        idx_copy(0, 0).start()

        @pl.when(n > 1)
        def _():
            idx_copy(1, 1).start()

        gather(0, 0)

    for sl in range(2):
        @pl.when((i + 1 < n) & (slot != sl))
        def _():
            gather(i + 1, sl)

    for sl in range(2):
        @pl.when((i + 2 < n) & (slot == sl))
        def _():
            idx_copy(i + 2, sl).start()

    for k in range(TOP_K):
        pltpu.make_async_copy(rows_s.at[1 - slot, k], rows_s.at[slot, k], sem_d.at[slot]).wait()

    def rows(k):
        return rows_s[slot, k].reshape(tc, rows_s.shape[-1])

    meta = meta_ref[...]
    moe = meta[:, TOP_K:TOP_K + 1] * rows(0)
    for k in range(1, TOP_K):
        moe = moe + meta[:, TOP_K + k:TOP_K + k + 1] * rows(k)
    o_ref[...] = _rms(x1_ref[...] + g2_ref[0] * moe, fw_ref[...])


def _combine(x1, meta, g2, final_w, dest_flat, y, t_seq, tc):
    n, d = x1.shape
    return pl.pallas_call(
        _combine_kernel, grid=(n // tc,),
        in_specs=[pl.BlockSpec((tc, d), lambda i: (i, 0)),
                  pl.BlockSpec((tc, LANES), lambda i: (i, 0)),
                  pl.BlockSpec((1, 1, d), lambda i: (i * tc // t_seq, 0, 0)),
                  pl.BlockSpec((1, d), lambda i: (0, 0)),
                  pl.BlockSpec(memory_space=pl.ANY),
                  pl.BlockSpec(memory_space=pl.ANY)],
        out_specs=pl.BlockSpec((tc, d), lambda i: (i, 0)),
        out_shape=jax.ShapeDtypeStruct((n, d), F32),
        scratch_shapes=[pltpu.VMEM((2, TOP_K, tc // SUBLANES, SUBLANES, d), F32),
                        pltpu.SMEM((tc * TOP_K,), jnp.int32), pltpu.SMEM((tc * TOP_K,), jnp.int32),
                        pltpu.SemaphoreType.DMA((2,)), pltpu.SemaphoreType.DMA((2,))],
        compiler_params=_params("arbitrary"), name="moe_combine",
    )(x1, meta, g2, final_w, dest_flat, y)


def _gate_weights(wa, wx):
    _, heads, hd, _ = wa.shape
    hh = heads // 2
    eye = jnp.eye(hh, dtype=wa.dtype)

    def blockdiag(wsel):
        return jnp.einsum('hij,hg->higj', wsel, eye).reshape(hh * hd, hh * hd)

    halves = []
    for s in range(2):
        sl = slice(s * hh, (s + 1) * hh)
        halves.append(jnp.concatenate([blockdiag(wa[0, sl]), blockdiag(wx[0, sl]),
                                       blockdiag(wa[1, sl]), blockdiag(wx[1, sl])], axis=1))
    return jnp.stack(halves).astype(BF16)


def kernel(x, c, ctx, c_ctx, norm1_w, norm2_w, w_ada, b_ada, w_in, hg_lb_logits, hg_norm_w, rg_conv_w, rg_conv_b,
           rg_wa, rg_ba, rg_wx, rg_bx, rg_lambda, w_out, router_w, router_b, w_gate_up, b_gate_up, w_down,
           b_down, final_norm_w):
    b, t, d = x.shape
    tcx = ctx.shape[1]
    n_exp = router_w.shape[-1]
    n_tok = b * t
    depth = w_in.shape[0]
    lb_all = jnp.cumsum(jax.nn.softmax(hg_lb_logits.astype(F32), axis=0), axis=0)

    for l in range(depth):
        assert l == depth - 1, "context stream update of non-final layers is not implemented"
        pad = (-(b + 1)) % 8
        c_all = jnp.concatenate([c, c_ctx[None], jnp.zeros((pad, d), F32)], axis=0)
        mod = _mod(c_all, w_ada[l], b_ada[l])
        sh1, sc1, g1, sh2, sc2, g2 = [m[:b, None, :] for m in jnp.split(mod, 6, axis=-1)]
        csh1, csc1 = [m[b:b + 1, None, :] for m in jnp.split(mod, 6, axis=-1)[:2]]

        w_in_b = w_in[l].astype(BF16)
        nw1 = norm1_w[l].reshape(1, d)
        tm = min(TOKEN_TILE, t)
        p_lat = _inproj(x, sh1, sc1, nw1, w_in_b, tm)
        p_ctx = _inproj(ctx, csh1, csc1, nw1, w_in_b, min(256, tcx))

        hg = _hgrn2(p_lat, p_ctx, lb_all[l], hg_norm_w[l].reshape(1, -1))
        ch = rg_conv_w.shape[-1]
        wg = _gate_weights(rg_wa[l], rg_wx[l])
        bg = jnp.concatenate([rg_ba[l, 0], rg_bx[l, 0], rg_ba[l, 1], rg_bx[l, 1]]).reshape(1, 4 * ch)
        rg = _rglru(p_lat, p_ctx, rg_conv_w[l], rg_conv_b[l].reshape(1, ch), wg, bg, rg_lambda[l])

        x1, h2, meta, cnt = _mix(hg, rg, x, g1, sh2, sc2, norm2_w[l].reshape(1, d), w_out[l].astype(BF16),
                                 router_w[l].astype(BF16), router_b[l].reshape(1, n_exp), tm)

        bm = EXPERT_BLOCK
        meta2 = meta.reshape(n_tok, LANES)
        counts = cnt[0, :n_exp].astype(jnp.int32)
        padded = (counts + bm - 1) // bm * bm
        pad_end = jnp.cumsum(padded)
        pad_start = pad_end - padded
        idx = meta2[:, 0:TOP_K].astype(jnp.int32)
        rank = meta2[:, 2 * TOP_K:3 * TOP_K].astype(jnp.int32)
        dest = (pad_start[idx] + rank).reshape(-1)
        n_blocks = -(-n_tok * TOP_K // bm) + n_exp
        blk_start = jnp.arange(n_blocks, dtype=jnp.int32) * bm
        blk_expert = jnp.minimum(jnp.sum(blk_start[:, None] >= pad_end[None, :], axis=1), n_exp - 1).astype(jnp.int32)
        n_used = (pad_end[-1:] // bm).astype(jnp.int32)

        n_slots = n_blocks * bm
        tail = jnp.stack([pad_end[-1], (n_slots - pad_end[-1]) // (bm // 2)]).astype(jnp.int32)
        xs = _dispatch(h2.reshape(n_tok, d), dest, pad_start + counts, padded - counts, tail, n_slots,
                       min(DISPATCH_TILE, t), bm)
        y = _experts(xs, blk_expert, n_used, w_gate_up[l], b_gate_up[l], w_down[l], b_down[l], bm)
        out = _combine(x1.reshape(n_tok, d), meta2, g2, final_norm_w.reshape(1, d), dest, y, t,
                       min(COMBINE_TILE, t))
        return out.reshape(b, t, d)
```

```python
import functools

import jax
import jax.numpy as jnp
from jax import lax
from jax.experimental import pallas as pl
from jax.experimental.pallas import tpu as pltpu

GRID_W = 64
HG_HEADS = 4
HG_CHUNK = 32
RG_HEADS = 8
RG_CONV = 4
RG_C = 8.0
TOP_K = 4
SWIGLU_LIMIT = 7.0
SWIGLU_ALPHA = 1.702
EPS = 1e-6

LANES = 128
SUBLANES = 8
MXU_TILE = 256
VMEM_LIMIT = 56 * 1024 * 1024

ADALN_COLUMN_TILES = 4
TOKEN_TILE = 512
EXPERT_BLOCK = 512
DISPATCH_TILE = 512
COMBINE_TILE = 256

F32 = jnp.float32
BF16 = jnp.bfloat16
HIGHEST = lax.Precision.HIGHEST


def _params(*sem):
    return pltpu.CompilerParams(dimension_semantics=sem, vmem_limit_bytes=VMEM_LIMIT)


def _sigmoid(x):
    return 0.5 * jnp.tanh(0.5 * x) + 0.5


def _silu(x):
    return x * _sigmoid(x)


def _rms(x, w):
    return x * lax.rsqrt(jnp.mean(x * x, axis=-1, keepdims=True) + EPS) * w


def _dot(a, b):
    return jnp.dot(a, b, preferred_element_type=F32)


def _dot_nt(a, b):
    return lax.dot_general(a, b, (((1,), (1,)), ((), ())), preferred_element_type=F32)


def _mod_kernel(c_ref, w_ref, b_ref, o_ref):
    o_ref[...] = jnp.dot(_silu(c_ref[...]), w_ref[...], preferred_element_type=F32,
                         precision=HIGHEST) + b_ref[...]


def _mod(c_all, w_ada, b_ada):
    r, d = c_all.shape
    n = w_ada.shape[1]
    assert n % (ADALN_COLUMN_TILES * LANES) == 0
    tn = n // ADALN_COLUMN_TILES
    return pl.pallas_call(
        _mod_kernel, grid=(n // tn,),
        in_specs=[pl.BlockSpec((r, d), lambda j: (0, 0)),
                  pl.BlockSpec((d, tn), lambda j: (0, j)),
                  pl.BlockSpec((1, tn), lambda j: (0, j))],
        out_specs=pl.BlockSpec((r, tn), lambda j: (0, j)),
        out_shape=jax.ShapeDtypeStruct((r, n), F32),
        compiler_params=_params("arbitrary"), name="adaln_mod",
    )(c_all, w_ada, b_ada.reshape(1, n))


def _inproj_kernel(x_ref, sh_ref, sc_ref, nw_ref, w_ref, o_ref):
    h = _rms(x_ref[0], nw_ref[...]) * (1.0 + sc_ref[0]) + sh_ref[0]
    o_ref[0] = _dot(h.astype(BF16), w_ref[...])


def _inproj(x, shift, scale, norm_w, w_bf16, tm):
    b, t, d = x.shape
    n = w_bf16.shape[1]
    per_batch = shift.shape[0] == b
    mod_map = (lambda i, j: (i, 0, 0)) if per_batch else (lambda i, j: (0, 0, 0))
    return pl.pallas_call(
        _inproj_kernel, grid=(b, t // tm),
        in_specs=[pl.BlockSpec((1, tm, d), lambda i, j: (i, j, 0)),
                  pl.BlockSpec((1, 1, d), mod_map),
                  pl.BlockSpec((1, 1, d), mod_map),
                  pl.BlockSpec((1, d), lambda i, j: (0, 0)),
                  pl.BlockSpec((d, n), lambda i, j: (0, 0))],
        out_specs=pl.BlockSpec((1, tm, n), lambda i, j: (i, j, 0)),
        out_shape=jax.ShapeDtypeStruct((b, t, n), F32),
        compiler_params=_params("arbitrary", "arbitrary"), name="norm_inproj",
    )(x, shift, scale, norm_w, w_bf16)


def _split_bf16(x):
    hi = x.astype(BF16)
    return hi, (x - hi.astype(F32)).astype(BF16)


def _hg_kernel(q_ref, v_ref, zf_ref, zb_ref, g_ref, cv_ref, czf_ref, czb_ref, lb_ref, nw_ref, o_ref,
               o_s, qd_s, u_s, dec_s, st_s, cu_s, cdec_s, tot_s, *, t_lat, t_ctx):
    c = HG_CHUNK
    dk = q_ref.shape[-1]
    lb = lb_ref[...]
    lbf, lbb = lb[0:1], lb[1:2]

    def prep(rb, zf, zb, q, v, u_out, dec_out, blk, r0):
        nc = rb // c
        ri = lax.broadcasted_iota(jnp.int32, (rb, rb), 0)
        ci = lax.broadcasted_iota(jnp.int32, (rb, rb), 1)
        same = (ri // c) == (ci // c)
        low = same & (ci <= ri)
        upp = same & (ci >= ri)
        ff = lbf + (1.0 - lbf) * _sigmoid(zf)
        fb = lbb + (1.0 - lbb) * _sigmoid(zb)
        lgf, lgb = jnp.log(ff), jnp.log(fb)
        rhs = jnp.concatenate([*_split_bf16(lgf), *_split_bf16(lgb)], axis=1)
        pre = _dot(low.astype(BF16), rhs)
        bcf = pre[:, :dk] + pre[:, dk:2 * dk]
        pfb = pre[:, 2 * dk:3 * dk] + pre[:, 3 * dk:]
        half = tot_s.shape[0] // 2
        tot_s[0:rb, :] = bcf
        tot_s[half:half + rb, :] = pfb
        totf = tot_s[pl.ds(c - 1, nc, stride=c), :]
        totb = tot_s[pl.ds(half + c - 1, nc, stride=c), :]

        def spread(tot):
            return jnp.broadcast_to(tot[:, None, :], (nc, c, dk)).reshape(rb, dk)

        remf = spread(totf) - bcf
        remb = pfb - lgb
        bcb = spread(totb) - remb
        kkf, kkb = 1.0 - ff, 1.0 - fb
        kef = (kkf * jnp.exp(remf)).astype(BF16)
        keb = (kkb * jnp.exp(remb)).astype(BF16)
        chunk_of_row = lax.broadcasted_iota(jnp.int32, (rb, dk), 0) // c
        zero = jnp.zeros((rb, dk), BF16)
        keys = jnp.concatenate([jnp.where(chunk_of_row == j, ke, zero) for ke in (kef, keb) for j in range(nc)],
                               axis=1)
        u_all = _dot(v.T.astype(BF16), keys)
        c0 = blk * nc
        for d in range(2):
            for j in range(nc):
                u_out[d, c0 + j] = u_all[:, (d * nc + j) * dk:(d * nc + j + 1) * dk]
        dec_out[0, pl.ds(pl.multiple_of(c0, nc), nc), :] = jnp.exp(totf)
        dec_out[1, pl.ds(pl.multiple_of(c0, nc), nc), :] = jnp.exp(totb)
        if q is None:
            return
        sq = _silu(q)
        qdf = (sq * jnp.exp(bcf)).astype(BF16)
        qdb = (sq * jnp.exp(bcb)).astype(BF16)
        kdf = (kkf * jnp.exp(-bcf)).astype(BF16)
        kdb = (kkb * jnp.exp(-bcb)).astype(BF16)
        p = jnp.where(low, _dot_nt(qdf, kdf), 0.0) + jnp.where(upp, _dot_nt(qdb, kdb), 0.0)
        o_s[pl.ds(r0, rb), :] = _dot(p.astype(BF16), v.astype(BF16))
        qd_s[pl.ds(r0, rb), 0:dk] = qdf
        qd_s[pl.ds(r0, rb), dk:2 * dk] = qdb

    def scan(n, ur, decr, keep, carry):
        def body(i, carry):
            sf, sb = carry
            j = n - 1 - i
            if keep:
                st_s[i, :, 0:dk] = sf.astype(BF16)
                st_s[j, :, dk:2 * dk] = sb.astype(BF16)
            return sf * decr[0, pl.ds(i, 1), :] + ur[0, i], sb * decr[1, pl.ds(j, 1), :] + ur[1, j]

        return lax.fori_loop(0, n, body, carry, unroll=2)

    rbc = min(MXU_TILE, t_ctx)
    for blk in range(t_ctx // rbc):
        r0 = blk * rbc
        prep(rbc, czf_ref[0, r0:r0 + rbc, :], czb_ref[0, r0:r0 + rbc, :], None, cv_ref[0, r0:r0 + rbc, :],
             cu_s, cdec_s, blk, r0)
    zero = jnp.zeros((dk, dk), F32)
    carry = scan(t_ctx // c, cu_s, cdec_s, False, (zero, zero))

    rbl = min(MXU_TILE, t_lat)

    def lat_prep(blk, _):
        r0 = pl.multiple_of(blk * rbl, rbl)
        rows = pl.ds(r0, rbl)
        prep(rbl, zf_ref[0, rows, :], zb_ref[0, rows, :], q_ref[0, rows, :], v_ref[0, rows, :], u_s, dec_s, blk, r0)
        return 0

    lax.fori_loop(0, t_lat // rbl, lat_prep, 0, unroll=4)
    scan(t_lat // c, u_s, dec_s, True, carry)

    def finish(blk, _):
        r0 = pl.multiple_of(blk * rbl, rbl)
        inter = [_dot_nt(qd_s[pl.ds(r0 + j * c, c), :], st_s[blk * (rbl // c) + j]) for j in range(rbl // c)]
        o = o_s[pl.ds(r0, rbl), :] + jnp.concatenate(inter, axis=0)
        o_ref[0, pl.ds(r0, rbl), :] = _rms(o, nw_ref[...]) * _silu(g_ref[0, pl.ds(r0, rbl), :])
        return 0

    lax.fori_loop(0, t_lat // rbl, finish, 0, unroll=4)


def _hgrn2(p_lat, p_ctx, lb, norm_w):
    b, t, _ = p_lat.shape
    tc = p_ctx.shape[1]
    hw = lb.shape[1]
    dk = hw // HG_HEADS
    nh = HG_HEADS
    c = HG_CHUNK

    def col(k, tt):
        return pl.BlockSpec((1, tt, dk), lambda i, h, k=k: (i, 0, k * nh + h))

    return pl.pallas_call(
        functools.partial(_hg_kernel, t_lat=t, t_ctx=tc), grid=(b, nh),
        in_specs=[col(0, t), col(1, t), col(2, t), col(3, t), col(4, t),
                  col(1, tc), col(2, tc), col(3, tc),
                  pl.BlockSpec((2, dk), lambda i, h: (0, h)),
                  pl.BlockSpec((1, dk), lambda i, h: (0, h))],
        out_specs=pl.BlockSpec((1, t, dk), lambda i, h: (i, 0, h)),
        out_shape=jax.ShapeDtypeStruct((b, t, hw), F32),
        scratch_shapes=[pltpu.VMEM((t, dk), F32),
                        pltpu.VMEM((t, 2 * dk), BF16),
                        pltpu.VMEM((2, t // c, dk, dk), F32),
                        pltpu.VMEM((2, t // c, dk), F32),
                        pltpu.VMEM((t // c, dk, 2 * dk), BF16),
                        pltpu.VMEM((2, tc // c, dk, dk), F32),
                        pltpu.VMEM((2, tc // c, dk), F32),
                        pltpu.VMEM((2 * min(MXU_TILE, max(t, tc)), dk), F32)],
        compiler_params=_params("arbitrary", "arbitrary"), name="hgrn2",
    )(p_lat, p_lat, p_lat, p_lat, p_lat, p_ctx, p_ctx, p_ctx, lb, norm_w)


def _shift_rows(x, k):
    n = x.shape[0]
    y = pltpu.roll(x, k % n, 0)
    r = lax.broadcasted_iota(jnp.int32, x.shape, 0)
    return jnp.where((r >= k) & (r < n + k), y, 0.0)


def _rg_kernel(rx_ref, rgate_ref, crx_ref, cw_ref, cb_ref, wg_ref, bg_ref, lam_ref, o_ref,
               xc_s, af_s, bf_s, ab_s, bb_s, hf_s, hb_s, caf_s, cbf_s, cab_s, cbb_s, *, t_lat, t_ctx):
    w = GRID_W
    rows = t_lat // w
    ch = rx_ref.shape[-1]
    half = ch // 2
    cw = cw_ref[...]
    cb = cb_ref[...]
    bg = bg_ref[...]
    nl = -lam_ref[...]
    cdec = -RG_C * (jnp.maximum(nl, 0.0) + jnp.log1p(jnp.exp(-jnp.abs(nl))))

    def conv(xm2, xm1, x0, xp1):
        return cb + cw[0:1] * xm2 + cw[1:2] * xm1 + cw[2:3] * x0 + cw[3:4] * xp1

    def gates(xc):
        xb = xc.astype(BF16)
        g0 = _dot(xb[:, :half], wg_ref[0])
        g1 = _dot(xb[:, half:], wg_ref[1])
        outs = []
        for d in range(2):
            pre = []
            for s in (2 * d, 2 * d + 1):
                pre.append(jnp.concatenate([g0[:, s * half:(s + 1) * half], g1[:, s * half:(s + 1) * half]],
                                           axis=1) + bg[:, s * ch:(s + 1) * ch])
            log_a = cdec[d:d + 1] * _sigmoid(pre[0])
            a = jnp.exp(log_a)
            mult = jnp.sqrt(-jnp.tanh(log_a) * (a * a + 1.0))
            outs += [a, mult * _sigmoid(pre[1]) * xc]
        return outs

    xctx = crx_ref[0]
    xcc = conv(_shift_rows(xctx, 2), _shift_rows(xctx, 1), xctx, _shift_rows(xctx, -1))
    caf_s[...], cbf_s[...], cab_s[...], cbb_s[...] = gates(xcc)

    def cstep(i, carry):
        hf, hb = carry
        hf = caf_s[pl.ds(i, 1), :] * hf + cbf_s[pl.ds(i, 1), :]
        j = t_ctx - 1 - i
        hb = cab_s[pl.ds(j, 1), :] * hb + cbb_s[pl.ds(j, 1), :]
        return hf, hb

    zrow = jnp.zeros((1, ch), F32)
    hf0, hb0 = lax.fori_loop(0, t_ctx, cstep, (zrow, zrow), unroll=8)

    def slab(rr):
        if 0 <= rr < rows:
            return rx_ref[0, rr * w:(rr + 1) * w, :]
        if rr < 0:
            return _shift_rows(rx_ref[0, (rr + rows) * w:(rr + rows + 1) * w, :], 1)
        return _shift_rows(rx_ref[0, (rr - rows) * w:(rr - rows + 1) * w, :], -1)

    for r in range(rows):
        xc_s[r * w:(r + 1) * w, :] = conv(slab(r - 2), slab(r - 1), slab(r), slab(r + 1))

    mb = min(MXU_TILE, t_lat)

    def gbody(i, _):
        r0 = pl.multiple_of(i * mb, mb)
        a_f, b_f, a_b, b_b = gates(xc_s[pl.ds(r0, mb), :])
        af_s[pl.ds(r0, mb), :] = a_f
        bf_s[pl.ds(r0, mb), :] = b_f
        ab_s[pl.ds(r0, mb), :] = a_b
        bb_s[pl.ds(r0, mb), :] = b_b
        return 0

    lax.fori_loop(0, t_lat // mb, gbody, 0, unroll=2)

    def l1(i, _):
        pf = pl.multiple_of(i * w, w)
        qf = pl.multiple_of((i - 1) * w, w)
        a = af_s[pl.ds(pf, w), :]
        af_s[pl.ds(pf, w), :] = a * af_s[pl.ds(qf, w), :]
        bf_s[pl.ds(pf, w), :] = a * bf_s[pl.ds(qf, w), :] + bf_s[pl.ds(pf, w), :]
        pb = pl.multiple_of((rows - 1 - i) * w, w)
        qb = pl.multiple_of((rows - i) * w, w)
        a = ab_s[pl.ds(pb, w), :]
        ab_s[pl.ds(pb, w), :] = a * ab_s[pl.ds(qb, w), :]
        bb_s[pl.ds(pb, w), :] = a * bb_s[pl.ds(qb, w), :] + bb_s[pl.ds(pb, w), :]
        return 0

    lax.fori_loop(1, rows, l1, 0)

    last = (rows - 1) * w

    def l2(i, carry):
        hf, hb = carry
        hf_s[pl.ds(i, 1), :] = hf
        hf = af_s[pl.ds(last + i, 1), :] * hf + bf_s[pl.ds(last + i, 1), :]
        j = w - 1 - i
        hb_s[pl.ds(j, 1), :] = hb
        hb = ab_s[pl.ds(j, 1), :] * hb + bb_s[pl.ds(j, 1), :]
        return hf, hb

    lax.fori_loop(0, w, l2, (hf0, hb0), unroll=8)

    def l3(i, _):
        p = pl.multiple_of(i * w, w)
        h = (af_s[pl.ds(p, w), :] * hf_s[...] + bf_s[pl.ds(p, w), :]
             + ab_s[pl.ds(p, w), :] * hb_s[...] + bb_s[pl.ds(p, w), :])
        o_ref[0, pl.ds(p, w), :] = jax.nn.gelu(rgate_ref[0, pl.ds(p, w), :]) * h
        return 0

    lax.fori_loop(0, rows, l3, 0, unroll=2)


def _rglru(p_lat, p_ctx, conv_w, conv_b, wg, bg, lam):
    b, t, _ = p_lat.shape
    tc = p_ctx.shape[1]
    ch = conv_w.shape[1]
    rx_blk = (p_lat.shape[2] - 2 * ch) // ch
    full = lambda shape: pl.BlockSpec(shape, lambda i: (0,) * len(shape))
    big = lambda: pltpu.VMEM((t, ch), F32)
    small = lambda: pltpu.VMEM((tc, ch), F32)
    return pl.pallas_call(
        functools.partial(_rg_kernel, t_lat=t, t_ctx=tc), grid=(b,),
        in_specs=[pl.BlockSpec((1, t, ch), lambda i: (i, 0, rx_blk)),
                  pl.BlockSpec((1, t, ch), lambda i: (i, 0, rx_blk + 1)),
                  pl.BlockSpec((1, tc, ch), lambda i: (i, 0, rx_blk)),
                  full(conv_w.shape), full(conv_b.shape), full(wg.shape), full(bg.shape), full(lam.shape)],
        out_specs=pl.BlockSpec((1, t, ch), lambda i: (i, 0, 0)),
        out_shape=jax.ShapeDtypeStruct((b, t, ch), F32),
        scratch_shapes=[big(), big(), big(), big(), big(),
                        pltpu.VMEM((GRID_W, ch), F32), pltpu.VMEM((GRID_W, ch), F32),
                        small(), small(), small(), small()],
        compiler_params=_params("arbitrary"), name="rglru",
    )(p_lat, p_lat, p_ctx, conv_w, conv_b, wg, bg, lam)


def _mix_kernel(hg_ref, rg_ref, x_ref, g1_ref, sh_ref, sc_ref, nw_ref, wo_ref, rw_ref, rb_ref,
                x1_ref, h2_ref, meta_ref, cnt_ref, base_s, *, n_exp):
    tm = x_ref.shape[1]

    @pl.when((pl.program_id(0) == 0) & (pl.program_id(1) == 0))
    def _():
        base_s[...] = jnp.zeros_like(base_s)

    hcat = jnp.concatenate([hg_ref[0], rg_ref[0]], axis=1).astype(BF16)
    x1 = x_ref[0] + g1_ref[0] * _dot(hcat, wo_ref[...])
    x1_ref[0] = x1
    h2 = _rms(x1, nw_ref[...]) * (1.0 + sc_ref[0]) + sh_ref[0]
    h2_ref[0] = h2
    logits = _dot(h2.astype(BF16), rw_ref[...]) + rb_ref[...]

    lane_e = lax.broadcasted_iota(jnp.int32, (tm, n_exp), 1).astype(F32)
    vals, idxs = [], []
    cur = logits
    for _ in range(TOP_K):
        m = jnp.max(cur, axis=1, keepdims=True)
        ix = jnp.min(jnp.where(cur == m, lane_e, float(n_exp)), axis=1, keepdims=True)
        vals.append(m)
        idxs.append(ix)
        cur = jnp.where(lane_e == ix, -jnp.inf, cur)
    ex = [jnp.exp(v - vals[0]) for v in vals]
    den = ex[0] + ex[1] + ex[2] + ex[3]

    lane = lax.broadcasted_iota(jnp.int32, (tm, LANES), 1)
    lane_f = lane.astype(F32)
    onehot = jnp.zeros((tm, LANES), F32)
    for k in range(TOP_K):
        onehot = jnp.where(lane_f == idxs[k] + float(k * n_exp), 1.0, onehot)
    ri = lax.broadcasted_iota(jnp.int32, (tm, tm), 0)
    ci = lax.broadcasted_iota(jnp.int32, (tm, tm), 1)
    prefix = _dot((ci < ri).astype(BF16), onehot.astype(BF16))
    tot = jnp.broadcast_to(prefix[tm - 1:tm] + onehot[tm - 1:tm], (8, LANES))
    lane8 = lax.broadcasted_iota(jnp.int32, (8, LANES), 1)
    off = base_s[...]
    tot_all = tot
    for j in range(1, TOP_K):
        rolled = pltpu.roll(tot, j * n_exp, 1)
        off = off + jnp.where(lane8 >= j * n_exp, rolled, 0.0)
        tot_all = tot_all + rolled
    pos = onehot * (prefix + off[0:1])
    meta = jnp.zeros((tm, LANES), F32)
    for k in range(TOP_K):
        in_k = (lane >= k * n_exp) & (lane < (k + 1) * n_exp)
        rank = jnp.sum(jnp.where(in_k, pos, 0.0), axis=1, keepdims=True)
        meta = jnp.where(lane == k, idxs[k], meta)
        meta = jnp.where(lane == TOP_K + k, ex[k] / den, meta)
        meta = jnp.where(lane == 2 * TOP_K + k, rank, meta)
    meta_ref[0] = meta
    base_s[...] = base_s[...] + tot_all
    cnt_ref[...] = base_s[...]


def _mix(hg, rg, x, g1, sh2, sc2, norm_w, wo_bf16, router_w, router_b, tm):
    b, t, d = x.shape
    hw = hg.shape[2]
    n_exp = router_w.shape[1]
    assert TOP_K * n_exp == LANES
    tok = lambda last: pl.BlockSpec((1, tm, last), lambda i, j: (i, j, 0))
    per_b = pl.BlockSpec((1, 1, d), lambda i, j: (i, 0, 0))
    full = lambda shape: pl.BlockSpec(shape, lambda i, j: (0,) * len(shape))
    return pl.pallas_call(
        functools.partial(_mix_kernel, n_exp=n_exp), grid=(b, t // tm),
        in_specs=[tok(hw), tok(hw), tok(d), per_b, per_b, per_b, full((1, d)),
                  full(wo_bf16.shape), full(router_w.shape), full((1, n_exp))],
        out_specs=[tok(d), tok(d), tok(LANES), pl.BlockSpec((8, LANES), lambda i, j: (0, 0))],
        out_shape=[jax.ShapeDtypeStruct((b, t, d), F32), jax.ShapeDtypeStruct((b, t, d), F32),
                   jax.ShapeDtypeStruct((b, t, LANES), F32), jax.ShapeDtypeStruct((8, LANES), F32)],
        scratch_shapes=[pltpu.VMEM((8, LANES), F32)],
        compiler_params=_params("arbitrary", "arbitrary"), name="outproj_router",
    )(hg, rg, x, g1, sh2, sc2, norm_w, wo_bf16, router_w, router_b)


def _dispatch_kernel(fill_off, fill_n, tail, h_ref, dest_hbm, xs_out, idx0_s, idx1_s, zero_s, sem_i, sem_d, sem_z, *,
                     pad_bits):
    td = h_ref.shape[0] * SUBLANES
    n_idx = td * TOP_K
    n_exp = fill_n.shape[0]
    zrows = zero_s.shape[0]
    i = pl.program_id(0)
    slot = i % 2
    idx_s = (idx0_s, idx1_s)

    def idx_copy(j, sl):
        return pltpu.make_async_copy(dest_hbm.at[pl.ds(j * n_idx, n_idx)], idx_s[sl], sem_i.at[sl])

    @pl.when(i == 0)
    def _():
        idx_copy(0, 0).start()

    for sl in range(2):
        @pl.when((i + 1 < pl.num_programs(0)) & (slot != sl))
        def _():
            idx_copy(i + 1, sl).start()

    def fill(wait):
        def go(copy, cond):
            @pl.when(cond)
            def _():
                copy.wait() if wait else copy.start()

        def per_expert(e, _):
            off = fill_off[e]
            npad = fill_n[e]
            n_single = npad & (SUBLANES - 1)
            for r in range(SUBLANES - 1):
                go(pltpu.make_async_copy(zero_s.at[pl.ds(0, 1), :], xs_out.at[pl.ds(off + r, 1), :], sem_z),
                   r < n_single)
            off = pl.multiple_of(off + n_single, SUBLANES)
            for bit in reversed(range(SUBLANES.bit_length() - 1, pad_bits)):
                size = 1 << bit
                go(pltpu.make_async_copy(zero_s.at[pl.ds(0, size), :], xs_out.at[pl.ds(off, size), :], sem_z),
                   (npad & size) != 0)
                off = pl.multiple_of(off + (npad & size), SUBLANES)
            return 0

        lax.fori_loop(0, n_exp, per_expert, 0)

        def per_tail_chunk(j, _):
            off = pl.multiple_of(tail[0] + j * zrows, zrows)
            copy = pltpu.make_async_copy(zero_s, xs_out.at[pl.ds(off, zrows), :], sem_z)
            copy.wait() if wait else copy.start()
            return 0

        lax.fori_loop(0, tail[1], per_tail_chunk, 0)

    @pl.when(i == 0)
    def _():
        zero_s[...] = jnp.zeros_like(zero_s)
        fill(False)

    for sl in range(2):
        @pl.when(slot == sl)
        def _():
            idx_copy(i, sl).wait()

            for g in range(td // SUBLANES):
                for u in range(SUBLANES):
                    for k in range(TOP_K):
                        dst = idx_s[sl][(g * SUBLANES + u) * TOP_K + k]
                        pltpu.make_async_copy(h_ref.at[g, pl.ds(u, 1), :], xs_out.at[pl.ds(dst, 1), :],
                                              sem_d).start(priority=k % 2)

    for _ in range(TOP_K * td // zrows):
        pltpu.make_async_copy(zero_s, xs_out.at[pl.ds(0, zrows), :], sem_d).wait()

    @pl.when(i == 0)
    def _():
        fill(True)


def _dispatch(h2, dest_flat, fill_off, fill_n, tail, n_slots, td, bm):
    n, d = h2.shape
    pad_bits = (bm - 1).bit_length()
    assert (TOP_K * td) % (bm // 2) == 0
    grid_spec = pltpu.PrefetchScalarGridSpec(
        num_scalar_prefetch=3, grid=(n // td,),
        in_specs=[pl.BlockSpec((td // SUBLANES, SUBLANES, d), lambda i, fo, fn, tl: (i, 0, 0)),
                  pl.BlockSpec(memory_space=pl.ANY)],
        out_specs=pl.BlockSpec(memory_space=pl.ANY),
        scratch_shapes=[pltpu.SMEM((td * TOP_K,), jnp.int32), pltpu.SMEM((td * TOP_K,), jnp.int32),
                        pltpu.VMEM((bm // 2, d), F32),
                        pltpu.SemaphoreType.DMA((2,)), pltpu.SemaphoreType.DMA, pltpu.SemaphoreType.DMA])
    return pl.pallas_call(
        functools.partial(_dispatch_kernel, pad_bits=pad_bits), grid_spec=grid_spec,
        out_shape=jax.ShapeDtypeStruct((n_slots, d), F32),
        compiler_params=_params("arbitrary"), name="moe_dispatch",
    )(fill_off, fill_n, tail, h2.reshape(n // SUBLANES, SUBLANES, d), dest_flat)


def _expert_kernel(be_ref, nu_ref, x_ref, wgu_ref, bgu_ref, wd_ref, bd_ref, y_ref, wgu_s, wd_s):
    i = pl.program_id(0)
    d_ff = wd_ref.shape[1]

    @pl.when(i >= nu_ref[0])
    def _():
        y_ref[...] = jnp.zeros_like(y_ref)

    @pl.when(i < nu_ref[0])
    def _():
        @pl.when((i == 0) | (be_ref[i] != be_ref[jnp.maximum(i - 1, 0)]))
        def _():
            wgu_s[...] = wgu_ref[0].astype(BF16)
            wd_s[...] = wd_ref[0].astype(BF16)

        gu = _dot(x_ref[...].astype(BF16), wgu_s[...]) + bgu_ref[0]
        gate = jnp.minimum(gu[:, :d_ff], SWIGLU_LIMIT)
        up = jnp.clip(gu[:, d_ff:], -SWIGLU_LIMIT, SWIGLU_LIMIT)
        act = gate * _sigmoid(SWIGLU_ALPHA * gate) * (up + 1.0)
        y_ref[...] = _dot(act.astype(BF16), wd_s[...]) + bd_ref[0]


def _experts(xs, blk_expert, n_used, wgu, bgu, wd, bd, bm):
    n_slots, d = xs.shape
    n_exp, _, f2 = wgu.shape
    d_ff = wd.shape[1]
    n_blocks = n_slots // bm
    row = lambda i, be, nu: (jnp.minimum(i, nu[0] - 1), 0)
    grid_spec = pltpu.PrefetchScalarGridSpec(
        num_scalar_prefetch=2, grid=(n_blocks,),
        in_specs=[pl.BlockSpec((bm, d), row),
                  pl.BlockSpec((1, d, f2), lambda i, be, nu: (be[i], 0, 0)),
                  pl.BlockSpec((1, 1, f2), lambda i, be, nu: (be[i], 0, 0)),
                  pl.BlockSpec((1, d_ff, d), lambda i, be, nu: (be[i], 0, 0)),
                  pl.BlockSpec((1, 1, d), lambda i, be, nu: (be[i], 0, 0))],
        out_specs=pl.BlockSpec((bm, d), lambda i, be, nu: (i, 0)),
        scratch_shapes=[pltpu.VMEM((d, f2), BF16), pltpu.VMEM((d_ff, d), BF16)])
    return pl.pallas_call(
        _expert_kernel, grid_spec=grid_spec,
        out_shape=jax.ShapeDtypeStruct((n_slots, d), F32),
        compiler_params=_params("arbitrary"), name="moe_experts",
    )(blk_expert, n_used, xs, wgu, bgu.reshape(n_exp, 1, f2), wd, bd.reshape(n_exp, 1, d))


def _combine_kernel(x1_ref, meta_ref, g2_ref, fw_ref, dest_hbm, y_hbm, o_ref, rows_s, idx0_s, idx1_s, sem_i, sem_d):
    tc = x1_ref.shape[0]
    n_idx = tc * TOP_K
    i = pl.program_id(0)
    n = pl.num_programs(0)
    slot = i % 2
    idx_s = (idx0_s, idx1_s)

    def idx_copy(j, sl):
        return pltpu.make_async_copy(dest_hbm.at[pl.ds(j * n_idx, n_idx)], idx_s[sl], sem_i.at[sl])

    def gather(j, sl):
        idx_copy(j, sl).wait()

        for g in range(tc // SUBLANES):
            for u in range(SUBLANES):
                for k in range(TOP_K):
                    src = idx_s[sl][(g * SUBLANES + u) * TOP_K + k]
                    pltpu.make_async_copy(y_hbm.at[pl.ds(src, 1), :], rows_s.at[sl, k, g, pl.ds(u, 1), :],
                                          sem_d.at[sl]).start(priority=k % 2)

    @pl.when(i == 0)
    def _():
        idx_copy(0, 0).start()

        @pl.when(n > 1)
        def _():
            idx_copy(1, 1).start()

        gather(0, 0)

    for sl in range(2):
        @pl.when((i + 1 < n) & (slot != sl))
        def _():
            gather(i + 1, sl)

    for sl in range(2):
        @pl.when((i + 2 < n) & (slot == sl))
        def _():
            idx_copy(i + 2, sl).start()

    for k in range(TOP_K):
        pltpu.make_async_copy(rows_s.at[1 - slot, k], rows_s.at[slot, k], sem_d.at[slot]).wait()

    def rows(k):
        return rows_s[slot, k].reshape(tc, rows_s.shape[-1])

    meta = meta_ref[...]
    moe = meta[:, TOP_K:TOP_K + 1] * rows(0)
    for k in range(1, TOP_K):
        moe = moe + meta[:, TOP_K + k:TOP_K + k + 1] * rows(k)
    o_ref[...] = _rms(x1_ref[...] + g2_ref[0] * moe, fw_ref[...])


def _combine(x1, meta, g2, final_w, dest_flat, y, t_seq, tc):
    n, d = x1.shape
    return pl.pallas_call(
        _combine_kernel, grid=(n // tc,),
        in_specs=[pl.BlockSpec((tc, d), lambda i: (i, 0)),
                  pl.BlockSpec((tc, LANES), lambda i: (i, 0)),
                  pl.BlockSpec((1, 1, d), lambda i: (i * tc // t_seq, 0, 0)),
                  pl.BlockSpec((1, d), lambda i: (0, 0)),
                  pl.BlockSpec(memory_space=pl.ANY),
                  pl.BlockSpec(memory_space=pl.ANY)],
        out_specs=pl.BlockSpec((tc, d), lambda i: (i, 0)),
        out_shape=jax.ShapeDtypeStruct((n, d), F32),
        scratch_shapes=[pltpu.VMEM((2, TOP_K, tc // SUBLANES, SUBLANES, d), F32),
                        pltpu.SMEM((tc * TOP_K,), jnp.int32), pltpu.SMEM((tc * TOP_K,), jnp.int32),
                        pltpu.SemaphoreType.DMA((2,)), pltpu.SemaphoreType.DMA((2,))],
        compiler_params=_params("arbitrary"), name="moe_combine",
    )(x1, meta, g2, final_w, dest_flat, y)


def _gate_weights(wa, wx):
    _, heads, hd, _ = wa.shape
    hh = heads // 2
    eye = jnp.eye(hh, dtype=wa.dtype)

    def blockdiag(wsel):
        return jnp.einsum('hij,hg->higj', wsel, eye).reshape(hh * hd, hh * hd)

    halves = []
    for s in range(2):
        sl = slice(s * hh, (s + 1) * hh)
        halves.append(jnp.concatenate([blockdiag(wa[0, sl]), blockdiag(wx[0, sl]),
                                       blockdiag(wa[1, sl]), blockdiag(wx[1, sl])], axis=1))
    return jnp.stack(halves).astype(BF16)


def kernel(x, c, ctx, c_ctx, norm1_w, norm2_w, w_ada, b_ada, w_in, hg_lb_logits, hg_norm_w, rg_conv_w, rg_conv_b,
           rg_wa, rg_ba, rg_wx, rg_bx, rg_lambda, w_out, router_w, router_b, w_gate_up, b_gate_up, w_down,
           b_down, final_norm_w):
    b, t, d = x.shape
    tcx = ctx.shape[1]
    n_exp = router_w.shape[-1]
    n_tok = b * t
    depth = w_in.shape[0]
    lb_all = jnp.cumsum(jax.nn.softmax(hg_lb_logits.astype(F32), axis=0), axis=0)

    for l in range(depth):
        assert l == depth - 1, "context stream update of non-final layers is not implemented"
        pad = (-(b + 1)) % 8
        c_all = jnp.concatenate([c, c_ctx[None], jnp.zeros((pad, d), F32)], axis=0)
        mod = _mod(c_all, w_ada[l], b_ada[l])
        sh1, sc1, g1, sh2, sc2, g2 = [m[:b, None, :] for m in jnp.split(mod, 6, axis=-1)]
        csh1, csc1 = [m[b:b + 1, None, :] for m in jnp.split(mod, 6, axis=-1)[:2]]

        w_in_b = w_in[l].astype(BF16)
        nw1 = norm1_w[l].reshape(1, d)
        tm = min(TOKEN_TILE, t)
        p_lat = _inproj(x, sh1, sc1, nw1, w_in_b, tm)
        p_ctx = _inproj(ctx, csh1, csc1, nw1, w_in_b, min(MXU_TILE, tcx))

        hg = _hgrn2(p_lat, p_ctx, lb_all[l], hg_norm_w[l].reshape(1, -1))
        ch = rg_conv_w.shape[-1]
        wg = _gate_weights(rg_wa[l], rg_wx[l])
        bg = jnp.concatenate([rg_ba[l, 0], rg_bx[l, 0], rg_ba[l, 1], rg_bx[l, 1]]).reshape(1, 4 * ch)
        rg = _rglru(p_lat, p_ctx, rg_conv_w[l], rg_conv_b[l].reshape(1, ch), wg, bg, rg_lambda[l])

        x1, h2, meta, cnt = _mix(hg, rg, x, g1, sh2, sc2, norm2_w[l].reshape(1, d), w_out[l].astype(BF16),
                                 router_w[l].astype(BF16), router_b[l].reshape(1, n_exp), tm)

        bm = EXPERT_BLOCK
        meta2 = meta.reshape(n_tok, LANES)
        counts = cnt[0, :n_exp].astype(jnp.int32)
        padded = (counts + bm - 1) // bm * bm
        pad_end = jnp.cumsum(padded)
        pad_start = pad_end - padded
        idx = meta2[:, 0:TOP_K].astype(jnp.int32)
        rank = meta2[:, 2 * TOP_K:3 * TOP_K].astype(jnp.int32)
        dest = (pad_start[idx] + rank).reshape(-1)
        n_blocks = -(-n_tok * TOP_K // bm) + n_exp
        blk_start = jnp.arange(n_blocks, dtype=jnp.int32) * bm
        blk_expert = jnp.minimum(jnp.sum(blk_start[:, None] >= pad_end[None, :], axis=1), n_exp - 1).astype(jnp.int32)
        n_used = (pad_end[-1:] // bm).astype(jnp.int32)

        n_slots = n_blocks * bm
        tail = jnp.stack([pad_end[-1], (n_slots - pad_end[-1]) // (bm // 2)]).astype(jnp.int32)
        xs = _dispatch(h2.reshape(n_tok, d), dest, pad_start + counts, padded - counts, tail, n_slots,
                       min(DISPATCH_TILE, t), bm)
        y = _experts(xs, blk_expert, n_used, w_gate_up[l], b_gate_up[l], w_down[l], b_down[l], bm)
        out = _combine(x1.reshape(n_tok, d), meta2, g2, final_norm_w.reshape(1, d), dest, y, t,
                       min(COMBINE_TILE, t))
        return out.reshape(b, t, d)
```

```python
import functools

import jax
import jax.numpy as jnp
from jax import lax
from jax.experimental import pallas as pl
from jax.experimental.pallas import tpu as pltpu

GRID_W = 64
HG_HEADS = 4
HG_CHUNK = 32
RG_HEADS = 8
RG_CONV = 4
RG_C = 8.0
TOP_K = 4
SWIGLU_LIMIT = 7.0
SWIGLU_ALPHA = 1.702
EPS = 1e-6

LANES = 128
SUBLANES = 8
MXU_TILE = 256
VMEM_LIMIT = 56 * 1024 * 1024

ADALN_COLUMN_TILES = 4
TOKEN_TILE = 512
EXPERT_BLOCK = 512
DISPATCH_TILE = 512
COMBINE_TILE = 512

F32 = jnp.float32
BF16 = jnp.bfloat16
HIGHEST = lax.Precision.HIGHEST


def _params(*sem):
    return pltpu.CompilerParams(dimension_semantics=sem, vmem_limit_bytes=VMEM_LIMIT)


def _sigmoid(x):
    return 0.5 * jnp.tanh(0.5 * x) + 0.5


def _silu(x):
    return x * _sigmoid(x)


def _rms(x, w):
    return x * lax.rsqrt(jnp.mean(x * x, axis=-1, keepdims=True) + EPS) * w


def _dot(a, b):
    return jnp.dot(a, b, preferred_element_type=F32)


def _dot_nt(a, b):
    return lax.dot_general(a, b, (((1,), (1,)), ((), ())), preferred_element_type=F32)


def _mod_kernel(c_ref, w_ref, b_ref, o_ref):
    o_ref[...] = jnp.dot(_silu(c_ref[...]), w_ref[...], preferred_element_type=F32,
                         precision=HIGHEST) + b_ref[...]


def _mod(c_all, w_ada, b_ada):
    r, d = c_all.shape
    n = w_ada.shape[1]
    assert n % (ADALN_COLUMN_TILES * LANES) == 0
    tn = n // ADALN_COLUMN_TILES
    return pl.pallas_call(
        _mod_kernel, grid=(n // tn,),
        in_specs=[pl.BlockSpec((r, d), lambda j: (0, 0)),
                  pl.BlockSpec((d, tn), lambda j: (0, j)),
                  pl.BlockSpec((1, tn), lambda j: (0, j))],
        out_specs=pl.BlockSpec((r, tn), lambda j: (0, j)),
        out_shape=jax.ShapeDtypeStruct((r, n), F32),
        compiler_params=_params("arbitrary"), name="adaln_mod",
    )(c_all, w_ada, b_ada.reshape(1, n))


def _inproj_kernel(x_ref, sh_ref, sc_ref, nw_ref, w_ref, o_ref):
    h = _rms(x_ref[0], nw_ref[...]) * (1.0 + sc_ref[0]) + sh_ref[0]
    o_ref[0] = _dot(h.astype(BF16), w_ref[...])


def _inproj(x, shift, scale, norm_w, w_bf16, tm):
    b, t, d = x.shape
    n = w_bf16.shape[1]
    per_batch = shift.shape[0] == b
    mod_map = (lambda i, j: (i, 0, 0)) if per_batch else (lambda i, j: (0, 0, 0))
    return pl.pallas_call(
        _inproj_kernel, grid=(b, t // tm),
        in_specs=[pl.BlockSpec((1, tm, d), lambda i, j: (i, j, 0)),
                  pl.BlockSpec((1, 1, d), mod_map),
                  pl.BlockSpec((1, 1, d), mod_map),
                  pl.BlockSpec((1, d), lambda i, j: (0, 0)),
                  pl.BlockSpec((d, n), lambda i, j: (0, 0))],
        out_specs=pl.BlockSpec((1, tm, n), lambda i, j: (i, j, 0)),
        out_shape=jax.ShapeDtypeStruct((b, t, n), F32),
        compiler_params=_params("arbitrary", "arbitrary"), name="norm_inproj",
    )(x, shift, scale, norm_w, w_bf16)


def _split_bf16(x):
    hi = x.astype(BF16)
    return hi, (x - hi.astype(F32)).astype(BF16)


def _hg_kernel(q_ref, v_ref, zf_ref, zb_ref, g_ref, cv_ref, czf_ref, czb_ref, lb_ref, nw_ref, o_ref,
               o_s, qd_s, u_s, dec_s, st_s, cu_s, cdec_s, tot_s, *, t_lat, t_ctx):
    c = HG_CHUNK
    dk = q_ref.shape[-1]
    lb = lb_ref[...]
    lbf, lbb = lb[0:1], lb[1:2]

    def prep(rb, zf, zb, q, v, u_out, dec_out, blk, r0):
        nc = rb // c
        ri = lax.broadcasted_iota(jnp.int32, (rb, rb), 0)
        ci = lax.broadcasted_iota(jnp.int32, (rb, rb), 1)
        same = (ri // c) == (ci // c)
        low = same & (ci <= ri)
        upp = same & (ci >= ri)
        ff = lbf + (1.0 - lbf) * _sigmoid(zf)
        fb = lbb + (1.0 - lbb) * _sigmoid(zb)
        lgf, lgb = jnp.log(ff), jnp.log(fb)
        rhs = jnp.concatenate([*_split_bf16(lgf), *_split_bf16(lgb)], axis=1)
        pre = _dot(low.astype(BF16), rhs)
        bcf = pre[:, :dk] + pre[:, dk:2 * dk]
        pfb = pre[:, 2 * dk:3 * dk] + pre[:, 3 * dk:]
        half = tot_s.shape[0] // 2
        tot_s[0:rb, :] = bcf
        tot_s[half:half + rb, :] = pfb
        totf = tot_s[pl.ds(c - 1, nc, stride=c), :]
        totb = tot_s[pl.ds(half + c - 1, nc, stride=c), :]

        def spread(tot):
            return jnp.broadcast_to(tot[:, None, :], (nc, c, dk)).reshape(rb, dk)

        remf = spread(totf) - bcf
        remb = pfb - lgb
        bcb = spread(totb) - remb
        kkf, kkb = 1.0 - ff, 1.0 - fb
        kef = (kkf * jnp.exp(remf)).astype(BF16)
        keb = (kkb * jnp.exp(remb)).astype(BF16)
        chunk_of_row = lax.broadcasted_iota(jnp.int32, (rb, dk), 0) // c
        zero = jnp.zeros((rb, dk), BF16)
        keys = jnp.concatenate([jnp.where(chunk_of_row == j, ke, zero) for ke in (kef, keb) for j in range(nc)],
                               axis=1)
        u_all = _dot(v.T.astype(BF16), keys)
        c0 = blk * nc
        for d in range(2):
            for j in range(nc):
                u_out[d, c0 + j] = u_all[:, (d * nc + j) * dk:(d * nc + j + 1) * dk]
        dec_out[0, pl.ds(pl.multiple_of(c0, nc), nc), :] = jnp.exp(totf)
        dec_out[1, pl.ds(pl.multiple_of(c0, nc), nc), :] = jnp.exp(totb)
        if q is None:
            return
        sq = _silu(q)
        qdf = (sq * jnp.exp(bcf)).astype(BF16)
        qdb = (sq * jnp.exp(bcb)).astype(BF16)
        kdf = (kkf * jnp.exp(-bcf)).astype(BF16)
        kdb = (kkb * jnp.exp(-bcb)).astype(BF16)
        p = jnp.where(low, _dot_nt(qdf, kdf), 0.0) + jnp.where(upp, _dot_nt(qdb, kdb), 0.0)
        o_s[pl.ds(r0, rb), :] = _dot(p.astype(BF16), v.astype(BF16))
        qd_s[pl.ds(r0, rb), 0:dk] = qdf
        qd_s[pl.ds(r0, rb), dk:2 * dk] = qdb

    def scan(n, ur, decr, keep, carry):
        def body(i, carry):
            sf, sb = carry
            j = n - 1 - i
            if keep:
                st_s[i, :, 0:dk] = sf.astype(BF16)
                st_s[j, :, dk:2 * dk] = sb.astype(BF16)
            return sf * decr[0, pl.ds(i, 1), :] + ur[0, i], sb * decr[1, pl.ds(j, 1), :] + ur[1, j]

        return lax.fori_loop(0, n, body, carry, unroll=2)

    rbc = min(MXU_TILE, t_ctx)
    for blk in range(t_ctx // rbc):
        r0 = blk * rbc
        prep(rbc, czf_ref[0, r0:r0 + rbc, :], czb_ref[0, r0:r0 + rbc, :], None, cv_ref[0, r0:r0 + rbc, :],
             cu_s, cdec_s, blk, r0)
    zero = jnp.zeros((dk, dk), F32)
    carry = scan(t_ctx // c, cu_s, cdec_s, False, (zero, zero))

    rbl = min(MXU_TILE, t_lat)

    def lat_prep(blk, _):
        r0 = pl.multiple_of(blk * rbl, rbl)
        rows = pl.ds(r0, rbl)
        prep(rbl, zf_ref[0, rows, :], zb_ref[0, rows, :], q_ref[0, rows, :], v_ref[0, rows, :], u_s, dec_s, blk, r0)
        return 0

    lax.fori_loop(0, t_lat // rbl, lat_prep, 0, unroll=4)
    scan(t_lat // c, u_s, dec_s, True, carry)

    def finish(blk, _):
        r0 = pl.multiple_of(blk * rbl, rbl)
        inter = [_dot_nt(qd_s[pl.ds(r0 + j * c, c), :], st_s[blk * (rbl // c) + j]) for j in range(rbl // c)]
        o = o_s[pl.ds(r0, rbl), :] + jnp.concatenate(inter, axis=0)
        o_ref[0, pl.ds(r0, rbl), :] = _rms(o, nw_ref[...]) * _silu(g_ref[0, pl.ds(r0, rbl), :])
        return 0

    lax.fori_loop(0, t_lat // rbl, finish, 0, unroll=4)


def _hgrn2(p_lat, p_ctx, lb, norm_w):
    b, t, _ = p_lat.shape
    tc = p_ctx.shape[1]
    hw = lb.shape[1]
    dk = hw // HG_HEADS
    nh = HG_HEADS
    c = HG_CHUNK

    def col(k, tt):
        return pl.BlockSpec((1, tt, dk), lambda i, h, k=k: (i, 0, k * nh + h))

    return pl.pallas_call(
        functools.partial(_hg_kernel, t_lat=t, t_ctx=tc), grid=(b, nh),
        in_specs=[col(0, t), col(1, t), col(2, t), col(3, t), col(4, t),
                  col(1, tc), col(2, tc), col(3, tc),
                  pl.BlockSpec((2, dk), lambda i, h: (0, h)),
                  pl.BlockSpec((1, dk), lambda i, h: (0, h))],
        out_specs=pl.BlockSpec((1, t, dk), lambda i, h: (i, 0, h)),
        out_shape=jax.ShapeDtypeStruct((b, t, hw), F32),
        scratch_shapes=[pltpu.VMEM((t, dk), F32),
                        pltpu.VMEM((t, 2 * dk), BF16),
                        pltpu.VMEM((2, t // c, dk, dk), F32),
                        pltpu.VMEM((2, t // c, dk), F32),
                        pltpu.VMEM((t // c, dk, 2 * dk), BF16),
                        pltpu.VMEM((2, tc // c, dk, dk), F32),
                        pltpu.VMEM((2, tc // c, dk), F32),
                        pltpu.VMEM((2 * min(MXU_TILE, max(t, tc)), dk), F32)],
        compiler_params=_params("arbitrary", "arbitrary"), name="hgrn2",
    )(p_lat, p_lat, p_lat, p_lat, p_lat, p_ctx, p_ctx, p_ctx, lb, norm_w)


def _shift_rows(x, k):
    n = x.shape[0]
    y = pltpu.roll(x, k % n, 0)
    r = lax.broadcasted_iota(jnp.int32, x.shape, 0)
    return jnp.where((r >= k) & (r < n + k), y, 0.0)


def _rg_kernel(rx_ref, rgate_ref, crx_ref, cw_ref, cb_ref, wg_ref, bg_ref, lam_ref, o_ref,
               xc_s, af_s, bf_s, ab_s, bb_s, hf_s, hb_s, caf_s, cbf_s, cab_s, cbb_s, *, t_lat, t_ctx):
    w = GRID_W
    rows = t_lat // w
    ch = rx_ref.shape[-1]
    half = ch // 2
    cw = cw_ref[...]
    cb = cb_ref[...]
    bg = bg_ref[...]
    nl = -lam_ref[...]
    cdec = -RG_C * (jnp.maximum(nl, 0.0) + jnp.log1p(jnp.exp(-jnp.abs(nl))))

    def conv(xm2, xm1, x0, xp1):
        return cb + cw[0:1] * xm2 + cw[1:2] * xm1 + cw[2:3] * x0 + cw[3:4] * xp1

    def gates(xc):
        xb = xc.astype(BF16)
        g0 = _dot(xb[:, :half], wg_ref[0])
        g1 = _dot(xb[:, half:], wg_ref[1])
        outs = []
        for d in range(2):
            pre = []
            for s in (2 * d, 2 * d + 1):
                pre.append(jnp.concatenate([g0[:, s * half:(s + 1) * half], g1[:, s * half:(s + 1) * half]],
                                           axis=1) + bg[:, s * ch:(s + 1) * ch])
            log_a = cdec[d:d + 1] * _sigmoid(pre[0])
            a = jnp.exp(log_a)
            mult = jnp.sqrt(-jnp.tanh(log_a) * (a * a + 1.0))
            outs += [a, mult * _sigmoid(pre[1]) * xc]
        return outs

    xctx = crx_ref[0]
    xcc = conv(_shift_rows(xctx, 2), _shift_rows(xctx, 1), xctx, _shift_rows(xctx, -1))
    caf_s[...], cbf_s[...], cab_s[...], cbb_s[...] = gates(xcc)

    def cstep(i, carry):
        hf, hb = carry
        hf = caf_s[pl.ds(i, 1), :] * hf + cbf_s[pl.ds(i, 1), :]
        j = t_ctx - 1 - i
        hb = cab_s[pl.ds(j, 1), :] * hb + cbb_s[pl.ds(j, 1), :]
        return hf, hb

    zrow = jnp.zeros((1, ch), F32)
    hf0, hb0 = lax.fori_loop(0, t_ctx, cstep, (zrow, zrow), unroll=8)

    def slab(rr):
        if 0 <= rr < rows:
            return rx_ref[0, rr * w:(rr + 1) * w, :]
        if rr < 0:
            return _shift_rows(rx_ref[0, (rr + rows) * w:(rr + rows + 1) * w, :], 1)
        return _shift_rows(rx_ref[0, (rr - rows) * w:(rr - rows + 1) * w, :], -1)

    for r in range(rows):
        xc_s[r * w:(r + 1) * w, :] = conv(slab(r - 2), slab(r - 1), slab(r), slab(r + 1))

    mb = min(MXU_TILE, t_lat)

    def gbody(i, _):
        r0 = pl.multiple_of(i * mb, mb)
        a_f, b_f, a_b, b_b = gates(xc_s[pl.ds(r0, mb), :])
        af_s[pl.ds(r0, mb), :] = a_f
        bf_s[pl.ds(r0, mb), :] = b_f
        ab_s[pl.ds(r0, mb), :] = a_b
        bb_s[pl.ds(r0, mb), :] = b_b
        return 0

    lax.fori_loop(0, t_lat // mb, gbody, 0, unroll=2)

    def l1(i, _):
        pf = pl.multiple_of(i * w, w)
        qf = pl.multiple_of((i - 1) * w, w)
        a = af_s[pl.ds(pf, w), :]
        af_s[pl.ds(pf, w), :] = a * af_s[pl.ds(qf, w), :]
        bf_s[pl.ds(pf, w), :] = a * bf_s[pl.ds(qf, w), :] + bf_s[pl.ds(pf, w), :]
        pb = pl.multiple_of((rows - 1 - i) * w, w)
        qb = pl.multiple_of((rows - i) * w, w)
        a = ab_s[pl.ds(pb, w), :]
        ab_s[pl.ds(pb, w), :] = a * ab_s[pl.ds(qb, w), :]
        bb_s[pl.ds(pb, w), :] = a * bb_s[pl.ds(qb, w), :] + bb_s[pl.ds(pb, w), :]
        return 0

    lax.fori_loop(1, rows, l1, 0)

    last = (rows - 1) * w

    def l2(i, carry):
        hf, hb = carry
        hf_s[pl.ds(i, 1), :] = hf
        hf = af_s[pl.ds(last + i, 1), :] * hf + bf_s[pl.ds(last + i, 1), :]
        j = w - 1 - i
        hb_s[pl.ds(j, 1), :] = hb
        hb = ab_s[pl.ds(j, 1), :] * hb + bb_s[pl.ds(j, 1), :]
        return hf, hb

    lax.fori_loop(0, w, l2, (hf0, hb0), unroll=8)

    def l3(i, _):
        p = pl.multiple_of(i * w, w)
        h = (af_s[pl.ds(p, w), :] * hf_s[...] + bf_s[pl.ds(p, w), :]
             + ab_s[pl.ds(p, w), :] * hb_s[...] + bb_s[pl.ds(p, w), :])
        o_ref[0, pl.ds(p, w), :] = jax.nn.gelu(rgate_ref[0, pl.ds(p, w), :]) * h
        return 0

    lax.fori_loop(0, rows, l3, 0, unroll=2)


def _rglru(p_lat, p_ctx, conv_w, conv_b, wg, bg, lam):
    b, t, _ = p_lat.shape
    tc = p_ctx.shape[1]
    ch = conv_w.shape[1]
    rx_blk = (p_lat.shape[2] - 2 * ch) // ch
    full = lambda shape: pl.BlockSpec(shape, lambda i: (0,) * len(shape))
    big = lambda: pltpu.VMEM((t, ch), F32)
    small = lambda: pltpu.VMEM((tc, ch), F32)
    return pl.pallas_call(
        functools.partial(_rg_kernel, t_lat=t, t_ctx=tc), grid=(b,),
        in_specs=[pl.BlockSpec((1, t, ch), lambda i: (i, 0, rx_blk)),
                  pl.BlockSpec((1, t, ch), lambda i: (i, 0, rx_blk + 1)),
                  pl.BlockSpec((1, tc, ch), lambda i: (i, 0, rx_blk)),
                  full(conv_w.shape), full(conv_b.shape), full(wg.shape), full(bg.shape), full(lam.shape)],
        out_specs=pl.BlockSpec((1, t, ch), lambda i: (i, 0, 0)),
        out_shape=jax.ShapeDtypeStruct((b, t, ch), F32),
        scratch_shapes=[big(), big(), big(), big(), big(),
                        pltpu.VMEM((GRID_W, ch), F32), pltpu.VMEM((GRID_W, ch), F32),
                        small(), small(), small(), small()],
        compiler_params=_params("arbitrary"), name="rglru",
    )(p_lat, p_lat, p_ctx, conv_w, conv_b, wg, bg, lam)


def _mix_kernel(hg_ref, rg_ref, x_ref, g1_ref, sh_ref, sc_ref, nw_ref, wo_ref, rw_ref, rb_ref,
                x1_ref, h2_ref, meta_ref, cnt_ref, base_s, *, n_exp):
    tm = x_ref.shape[1]

    @pl.when((pl.program_id(0) == 0) & (pl.program_id(1) == 0))
    def _():
        base_s[...] = jnp.zeros_like(base_s)

    hcat = jnp.concatenate([hg_ref[0], rg_ref[0]], axis=1).astype(BF16)
    x1 = x_ref[0] + g1_ref[0] * _dot(hcat, wo_ref[...])
    x1_ref[0] = x1
    h2 = _rms(x1, nw_ref[...]) * (1.0 + sc_ref[0]) + sh_ref[0]
    h2_ref[0] = h2
    logits = _dot(h2.astype(BF16), rw_ref[...]) + rb_ref[...]

    lane_e = lax.broadcasted_iota(jnp.int32, (tm, n_exp), 1).astype(F32)
    vals, idxs = [], []
    cur = logits
    for _ in range(TOP_K):
        m = jnp.max(cur, axis=1, keepdims=True)
        ix = jnp.min(jnp.where(cur == m, lane_e, float(n_exp)), axis=1, keepdims=True)
        vals.append(m)
        idxs.append(ix)
        cur = jnp.where(lane_e == ix, -jnp.inf, cur)
    ex = [jnp.exp(v - vals[0]) for v in vals]
    den = ex[0] + ex[1] + ex[2] + ex[3]

    lane = lax.broadcasted_iota(jnp.int32, (tm, LANES), 1)
    lane_f = lane.astype(F32)
    onehot = jnp.zeros((tm, LANES), F32)
    for k in range(TOP_K):
        onehot = jnp.where(lane_f == idxs[k] + float(k * n_exp), 1.0, onehot)
    ri = lax.broadcasted_iota(jnp.int32, (tm, tm), 0)
    ci = lax.broadcasted_iota(jnp.int32, (tm, tm), 1)
    prefix = _dot((ci < ri).astype(BF16), onehot.astype(BF16))
    tot = jnp.broadcast_to(prefix[tm - 1:tm] + onehot[tm - 1:tm], (8, LANES))
    lane8 = lax.broadcasted_iota(jnp.int32, (8, LANES), 1)
    off = base_s[...]
    tot_all = tot
    for j in range(1, TOP_K):
        rolled = pltpu.roll(tot, j * n_exp, 1)
        off = off + jnp.where(lane8 >= j * n_exp, rolled, 0.0)
        tot_all = tot_all + rolled
    pos = onehot * (prefix + off[0:1])
    meta = jnp.zeros((tm, LANES), F32)
    for k in range(TOP_K):
        in_k = (lane >= k * n_exp) & (lane < (k + 1) * n_exp)
        rank = jnp.sum(jnp.where(in_k, pos, 0.0), axis=1, keepdims=True)
        meta = jnp.where(lane == k, idxs[k], meta)
        meta = jnp.where(lane == TOP_K + k, ex[k] / den, meta)
        meta = jnp.where(lane == 2 * TOP_K + k, rank, meta)
    meta_ref[0] = meta
    base_s[...] = base_s[...] + tot_all
    cnt_ref[...] = base_s[...]


def _mix(hg, rg, x, g1, sh2, sc2, norm_w, wo_bf16, router_w, router_b, tm):
    b, t, d = x.shape
    hw = hg.shape[2]
    n_exp = router_w.shape[1]
    assert TOP_K * n_exp == LANES
    tok = lambda last: pl.BlockSpec((1, tm, last), lambda i, j: (i, j, 0))
    per_b = pl.BlockSpec((1, 1, d), lambda i, j: (i, 0, 0))
    full = lambda shape: pl.BlockSpec(shape, lambda i, j: (0,) * len(shape))
    return pl.pallas_call(
        functools.partial(_mix_kernel, n_exp=n_exp), grid=(b, t // tm),
        in_specs=[tok(hw), tok(hw), tok(d), per_b, per_b, per_b, full((1, d)),
                  full(wo_bf16.shape), full(router_w.shape), full((1, n_exp))],
        out_specs=[tok(d), tok(d), tok(LANES), pl.BlockSpec((8, LANES), lambda i, j: (0, 0))],
        out_shape=[jax.ShapeDtypeStruct((b, t, d), F32), jax.ShapeDtypeStruct((b, t, d), F32),
                   jax.ShapeDtypeStruct((b, t, LANES), F32), jax.ShapeDtypeStruct((8, LANES), F32)],
        scratch_shapes=[pltpu.VMEM((8, LANES), F32)],
        compiler_params=_params("arbitrary", "arbitrary"), name="outproj_router",
    )(hg, rg, x, g1, sh2, sc2, norm_w, wo_bf16, router_w, router_b)


def _dispatch_kernel(fill_off, fill_n, tail, h_ref, dest_hbm, xs_out, idx0_s, idx1_s, zero_s, sem_i, sem_d, sem_z, *,
                     pad_bits):
    td = h_ref.shape[0] * SUBLANES
    n_idx = td * TOP_K
    n_exp = fill_n.shape[0]
    zrows = zero_s.shape[0]
    i = pl.program_id(0)
    slot = i % 2
    idx_s = (idx0_s, idx1_s)

    def idx_copy(j, sl):
        return pltpu.make_async_copy(dest_hbm.at[pl.ds(j * n_idx, n_idx)], idx_s[sl], sem_i.at[sl])

    @pl.when(i == 0)
    def _():
        idx_copy(0, 0).start()

    for sl in range(2):
        @pl.when((i + 1 < pl.num_programs(0)) & (slot != sl))
        def _():
            idx_copy(i + 1, sl).start()

    def fill(wait):
        def go(copy, cond):
            @pl.when(cond)
            def _():
                copy.wait() if wait else copy.start()

        def per_expert(e, _):
            off = fill_off[e]
            npad = fill_n[e]
            n_single = npad & (SUBLANES - 1)
            for r in range(SUBLANES - 1):
                go(pltpu.make_async_copy(zero_s.at[pl.ds(0, 1), :], xs_out.at[pl.ds(off + r, 1), :], sem_z),
                   r < n_single)
            off = pl.multiple_of(off + n_single, SUBLANES)
            for bit in reversed(range(SUBLANES.bit_length() - 1, pad_bits)):
                size = 1 << bit
                go(pltpu.make_async_copy(zero_s.at[pl.ds(0, size), :], xs_out.at[pl.ds(off, size), :], sem_z),
                   (npad & size) != 0)
                off = pl.multiple_of(off + (npad & size), SUBLANES)
            return 0

        lax.fori_loop(0, n_exp, per_expert, 0)

        def per_tail_chunk(j, _):
            off = pl.multiple_of(tail[0] + j * zrows, zrows)
            copy = pltpu.make_async_copy(zero_s, xs_out.at[pl.ds(off, zrows), :], sem_z)
            copy.wait() if wait else copy.start()
            return 0

        lax.fori_loop(0, tail[1], per_tail_chunk, 0)

    @pl.when(i == 0)
    def _():
        zero_s[...] = jnp.zeros_like(zero_s)
        fill(False)

    for sl in range(2):
        @pl.when(slot == sl)
        def _():
            idx_copy(i, sl).wait()

            for g in range(td // SUBLANES):
                for u in range(SUBLANES):
                    for k in range(TOP_K):
                        dst = idx_s[sl][(g * SUBLANES + u) * TOP_K + k]
                        pltpu.make_async_copy(h_ref.at[g, pl.ds(u, 1), :], xs_out.at[pl.ds(dst, 1), :],
                                              sem_d).start(priority=k % 2)

    for _ in range(TOP_K * td // zrows):
        pltpu.make_async_copy(zero_s, xs_out.at[pl.ds(0, zrows), :], sem_d).wait()

    @pl.when(i == 0)
    def _():
        fill(True)


def _dispatch(h2, dest_flat, fill_off, fill_n, tail, n_slots, td, bm):
    n, d = h2.shape
    pad_bits = (bm - 1).bit_length()
    assert (TOP_K * td) % (bm // 2) == 0
    grid_spec = pltpu.PrefetchScalarGridSpec(
        num_scalar_prefetch=3, grid=(n // td,),
        in_specs=[pl.BlockSpec((td // SUBLANES, SUBLANES, d), lambda i, fo, fn, tl: (i, 0, 0)),
                  pl.BlockSpec(memory_space=pl.ANY)],
        out_specs=pl.BlockSpec(memory_space=pl.ANY),
        scratch_shapes=[pltpu.SMEM((td * TOP_K,), jnp.int32), pltpu.SMEM((td * TOP_K,), jnp.int32),
                        pltpu.VMEM((bm // 2, d), F32),
                        pltpu.SemaphoreType.DMA((2,)), pltpu.SemaphoreType.DMA, pltpu.SemaphoreType.DMA])
    return pl.pallas_call(
        functools.partial(_dispatch_kernel, pad_bits=pad_bits), grid_spec=grid_spec,
        out_shape=jax.ShapeDtypeStruct((n_slots, d), F32),
        compiler_params=_params("arbitrary"), name="moe_dispatch",
    )(fill_off, fill_n, tail, h2.reshape(n // SUBLANES, SUBLANES, d), dest_flat)


def _expert_kernel(be_ref, nu_ref, x_ref, wgu_ref, bgu_ref, wd_ref, bd_ref, y_ref, wgu_s, wd_s):
    i = pl.program_id(0)
    d_ff = wd_ref.shape[1]

    @pl.when(i >= nu_ref[0])
    def _():
        y_ref[...] = jnp.zeros_like(y_ref)

    @pl.when(i < nu_ref[0])
    def _():
        @pl.when((i == 0) | (be_ref[i] != be_ref[jnp.maximum(i - 1, 0)]))
        def _():
            wgu_s[...] = wgu_ref[0].astype(BF16)
            wd_s[...] = wd_ref[0].astype(BF16)

        gu = _dot(x_ref[...].astype(BF16), wgu_s[...]) + bgu_ref[0]
        gate = jnp.minimum(gu[:, :d_ff], SWIGLU_LIMIT)
        up = jnp.clip(gu[:, d_ff:], -SWIGLU_LIMIT, SWIGLU_LIMIT)
        act = gate * _sigmoid(SWIGLU_ALPHA * gate) * (up + 1.0)
        y_ref[...] = _dot(act.astype(BF16), wd_s[...]) + bd_ref[0]


def _experts(xs, blk_expert, n_used, wgu, bgu, wd, bd, bm):
    n_slots, d = xs.shape
    n_exp, _, f2 = wgu.shape
    d_ff = wd.shape[1]
    n_blocks = n_slots // bm
    row = lambda i, be, nu: (jnp.minimum(i, nu[0] - 1), 0)
    grid_spec = pltpu.PrefetchScalarGridSpec(
        num_scalar_prefetch=2, grid=(n_blocks,),
        in_specs=[pl.BlockSpec((bm, d), row),
                  pl.BlockSpec((1, d, f2), lambda i, be, nu: (be[i], 0, 0)),
                  pl.BlockSpec((1, 1, f2), lambda i, be, nu: (be[i], 0, 0)),
                  pl.BlockSpec((1, d_ff, d), lambda i, be, nu: (be[i], 0, 0)),
                  pl.BlockSpec((1, 1, d), lambda i, be, nu: (be[i], 0, 0))],
        out_specs=pl.BlockSpec((bm, d), lambda i, be, nu: (i, 0)),
        scratch_shapes=[pltpu.VMEM((d, f2), BF16), pltpu.VMEM((d_ff, d), BF16)])
    return pl.pallas_call(
        _expert_kernel, grid_spec=grid_spec,
        out_shape=jax.ShapeDtypeStruct((n_slots, d), F32),
        compiler_params=_params("arbitrary"), name="moe_experts",
    )(blk_expert, n_used, xs, wgu, bgu.reshape(n_exp, 1, f2), wd, bd.reshape(n_exp, 1, d))


def _combine_kernel(x1_ref, meta_ref, g2_ref, fw_ref, dest_hbm, y_hbm, o_ref, rows_s, idx0_s, idx1_s, sem_i, sem_d):
    tc = x1_ref.shape[0]
    n_idx = tc * TOP_K
    i = pl.program_id(0)
    n = pl.num_programs(0)
    slot = i % 2
    idx_s = (idx0_s, idx1_s)

    def idx_copy(j, sl):
        return pltpu.make_async_copy(dest_hbm.at[pl.ds(j * n_idx, n_idx)], idx_s[sl], sem_i.at[sl])

    def gather(j, sl):
        idx_copy(j, sl).wait()

        for g in range(tc // SUBLANES):
            for u in range(SUBLANES):
                for k in range(TOP_K):
                    src = idx_s[sl][(g * SUBLANES + u) * TOP_K + k]
                    pltpu.make_async_copy(y_hbm.at[pl.ds(src, 1), :], rows_s.at[sl, k, g, pl.ds(u, 1), :],
                                          sem_d.at[sl]).start(priority=k % 2)

    @pl.when(i == 0)
    def _():
        idx_copy(0, 0).start()

        @pl.when(n > 1)
        def _():
            idx_copy(1, 1).start()

    @pl.when((i == 0) | ((i + 1 < n) & (slot == 1)))
    def _():
        gather(jnp.where(i == 0, 0, i + 1), 0)

    @pl.when((i + 1 < n) & (slot == 0))
    def _():
        gather(i + 1, 1)

    for sl in range(2):
        @pl.when((i + 2 < n) & (slot == sl))
        def _():
            idx_copy(i + 2, sl).start()

    for k in range(TOP_K):
        pltpu.make_async_copy(rows_s.at[1 - slot, k], rows_s.at[slot, k], sem_d.at[slot]).wait()

    def rows(k):
        return rows_s[slot, k].reshape(tc, rows_s.shape[-1])

    meta = meta_ref[...]
    moe = meta[:, TOP_K:TOP_K + 1] * rows(0)
    for k in range(1, TOP_K):
        moe = moe + meta[:, TOP_K + k:TOP_K + k + 1] * rows(k)
    o_ref[...] = _rms(x1_ref[...] + g2_ref[0] * moe, fw_ref[...])


def _combine(x1, meta, g2, final_w, dest_flat, y, t_seq, tc):
    n, d = x1.shape
    return pl.pallas_call(
        _combine_kernel, grid=(n // tc,),
        in_specs=[pl.BlockSpec((tc, d), lambda i: (i, 0)),
                  pl.BlockSpec((tc, LANES), lambda i: (i, 0)),
                  pl.BlockSpec((1, 1, d), lambda i: (i * tc // t_seq, 0, 0)),
                  pl.BlockSpec((1, d), lambda i: (0, 0)),
                  pl.BlockSpec(memory_space=pl.ANY),
                  pl.BlockSpec(memory_space=pl.ANY)],
        out_specs=pl.BlockSpec((tc, d), lambda i: (i, 0)),
        out_shape=jax.ShapeDtypeStruct((n, d), F32),
        scratch_shapes=[pltpu.VMEM((2, TOP_K, tc // SUBLANES, SUBLANES, d), F32),
                        pltpu.SMEM((tc * TOP_K,), jnp.int32), pltpu.SMEM((tc * TOP_K,), jnp.int32),
                        pltpu.SemaphoreType.DMA((2,)), pltpu.SemaphoreType.DMA((2,))],
        compiler_params=_params("arbitrary"), name="moe_combine",
    )(x1, meta, g2, final_w, dest_flat, y)


def _gate_weights(wa, wx):
    _, heads, hd, _ = wa.shape
    hh = heads // 2
    eye = jnp.eye(hh, dtype=wa.dtype)

    def blockdiag(wsel):
        return jnp.einsum('hij,hg->higj', wsel, eye).reshape(hh * hd, hh * hd)

    halves = []
    for s in range(2):
        sl = slice(s * hh, (s + 1) * hh)
        halves.append(jnp.concatenate([blockdiag(wa[0, sl]), blockdiag(wx[0, sl]),
                                       blockdiag(wa[1, sl]), blockdiag(wx[1, sl])], axis=1))
    return jnp.stack(halves).astype(BF16)


def kernel(x, c, ctx, c_ctx, norm1_w, norm2_w, w_ada, b_ada, w_in, hg_lb_logits, hg_norm_w, rg_conv_w, rg_conv_b,
           rg_wa, rg_ba, rg_wx, rg_bx, rg_lambda, w_out, router_w, router_b, w_gate_up, b_gate_up, w_down,
           b_down, final_norm_w):
    b, t, d = x.shape
    tcx = ctx.shape[1]
    n_exp = router_w.shape[-1]
    n_tok = b * t
    depth = w_in.shape[0]
    lb_all = jnp.cumsum(jax.nn.softmax(hg_lb_logits.astype(F32), axis=0), axis=0)

    for l in range(depth):
        assert l == depth - 1, "context stream update of non-final layers is not implemented"
        pad = (-(b + 1)) % 8
        c_all = jnp.concatenate([c, c_ctx[None], jnp.zeros((pad, d), F32)], axis=0)
        mod = _mod(c_all, w_ada[l], b_ada[l])
        sh1, sc1, g1, sh2, sc2, g2 = [m[:b, None, :] for m in jnp.split(mod, 6, axis=-1)]
        csh1, csc1 = [m[b:b + 1, None, :] for m in jnp.split(mod, 6, axis=-1)[:2]]

        w_in_b = w_in[l].astype(BF16)
        nw1 = norm1_w[l].reshape(1, d)
        tm = min(TOKEN_TILE, t)
        p_lat = _inproj(x, sh1, sc1, nw1, w_in_b, tm)
        p_ctx = _inproj(ctx, csh1, csc1, nw1, w_in_b, min(MXU_TILE, tcx))

        hg = _hgrn2(p_lat, p_ctx, lb_all[l], hg_norm_w[l].reshape(1, -1))
        ch = rg_conv_w.shape[-1]
        wg = _gate_weights(rg_wa[l], rg_wx[l])
        bg = jnp.concatenate([rg_ba[l, 0], rg_bx[l, 0], rg_ba[l, 1], rg_bx[l, 1]]).reshape(1, 4 * ch)
        rg = _rglru(p_lat, p_ctx, rg_conv_w[l], rg_conv_b[l].reshape(1, ch), wg, bg, rg_lambda[l])

        x1, h2, meta, cnt = _mix(hg, rg, x, g1, sh2, sc2, norm2_w[l].reshape(1, d), w_out[l].astype(BF16),
                                 router_w[l].astype(BF16), router_b[l].reshape(1, n_exp), tm)

        bm = EXPERT_BLOCK
        meta2 = meta.reshape(n_tok, LANES)
        counts = cnt[0, :n_exp].astype(jnp.int32)
        padded = (counts + bm - 1) // bm * bm
        pad_end = jnp.cumsum(padded)
        pad_start = pad_end - padded
        idx = meta2[:, 0:TOP_K].astype(jnp.int32)
        rank = meta2[:, 2 * TOP_K:3 * TOP_K].astype(jnp.int32)
        dest = (pad_start[idx] + rank).reshape(-1)
        n_blocks = -(-n_tok * TOP_K // bm) + n_exp
        blk_start = jnp.arange(n_blocks, dtype=jnp.int32) * bm
        blk_expert = jnp.minimum(jnp.sum(blk_start[:, None] >= pad_end[None, :], axis=1), n_exp - 1).astype(jnp.int32)
        n_used = (pad_end[-1:] // bm).astype(jnp.int32)

        n_slots = n_blocks * bm
        tail = jnp.stack([pad_end[-1], (n_slots - pad_end[-1]) // (bm // 2)]).astype(jnp.int32)
        xs = _dispatch(h2.reshape(n_tok, d), dest, pad_start + counts, padded - counts, tail, n_slots,
                       min(DISPATCH_TILE, t), bm)
        y = _experts(xs, blk_expert, n_used, w_gate_up[l], b_gate_up[l], w_down[l], b_down[l], bm)
        out = _combine(x1.reshape(n_tok, d), meta2, g2, final_norm_w.reshape(1, d), dest, y, t,
                       min(COMBINE_TILE, t))
        return out.reshape(b, t, d)
```

```python
import functools

import jax
import jax.numpy as jnp
from jax import lax
from jax.experimental import pallas as pl
from jax.experimental.pallas import tpu as pltpu

GRID_W = 64
HG_HEADS = 4
HG_CHUNK = 32
RG_HEADS = 8
RG_CONV = 4
RG_C = 8.0
TOP_K = 4
SWIGLU_LIMIT = 7.0
SWIGLU_ALPHA = 1.702
EPS = 1e-6

LANES = 128
SUBLANES = 8
MXU_TILE = 256
VMEM_LIMIT = 56 * 1024 * 1024

ADALN_COLUMN_TILES = 4
TOKEN_TILE = 512
EXPERT_BLOCK = 512
DISPATCH_TILE = 512
COMBINE_TILE = 512

F32 = jnp.float32
BF16 = jnp.bfloat16
HIGHEST = lax.Precision.HIGHEST


def _params(*sem):
    return pltpu.CompilerParams(dimension_semantics=sem, vmem_limit_bytes=VMEM_LIMIT)


def _sigmoid(x):
    return 0.5 * jnp.tanh(0.5 * x) + 0.5


def _silu(x):
    return x * _sigmoid(x)


def _rms(x, w):
    return x * lax.rsqrt(jnp.mean(x * x, axis=-1, keepdims=True) + EPS) * w


def _dot(a, b):
    return jnp.dot(a, b, preferred_element_type=F32)


def _dot_nt(a, b):
    return lax.dot_general(a, b, (((1,), (1,)), ((), ())), preferred_element_type=F32)


def _mod_kernel(c_ref, w_ref, b_ref, o_ref):
    o_ref[...] = jnp.dot(_silu(c_ref[...]), w_ref[...], preferred_element_type=F32,
                         precision=HIGHEST) + b_ref[...]


def _mod(c_all, w_ada, b_ada):
    r, d = c_all.shape
    n = w_ada.shape[1]
    assert n % (ADALN_COLUMN_TILES * LANES) == 0
    tn = n // ADALN_COLUMN_TILES
    return pl.pallas_call(
        _mod_kernel, grid=(n // tn,),
        in_specs=[pl.BlockSpec((r, d), lambda j: (0, 0)),
                  pl.BlockSpec((d, tn), lambda j: (0, j)),
                  pl.BlockSpec((1, tn), lambda j: (0, j))],
        out_specs=pl.BlockSpec((r, tn), lambda j: (0, j)),
        out_shape=jax.ShapeDtypeStruct((r, n), F32),
        compiler_params=_params("arbitrary"), name="adaln_mod",
    )(c_all, w_ada, b_ada.reshape(1, n))


def _inproj_kernel(x_ref, sh_ref, sc_ref, nw_ref, w_ref, o_ref):
    h = _rms(x_ref[0], nw_ref[...]) * (1.0 + sc_ref[0]) + sh_ref[0]
    o_ref[0] = _dot(h.astype(BF16), w_ref[...])


def _inproj(x, shift, scale, norm_w, w_bf16, tm):
    b, t, d = x.shape
    n = w_bf16.shape[1]
    per_batch = shift.shape[0] == b
    mod_map = (lambda i, j: (i, 0, 0)) if per_batch else (lambda i, j: (0, 0, 0))
    return pl.pallas_call(
        _inproj_kernel, grid=(b, t // tm),
        in_specs=[pl.BlockSpec((1, tm, d), lambda i, j: (i, j, 0)),
                  pl.BlockSpec((1, 1, d), mod_map),
                  pl.BlockSpec((1, 1, d), mod_map),
                  pl.BlockSpec((1, d), lambda i, j: (0, 0)),
                  pl.BlockSpec((d, n), lambda i, j: (0, 0))],
        out_specs=pl.BlockSpec((1, tm, n), lambda i, j: (i, j, 0)),
        out_shape=jax.ShapeDtypeStruct((b, t, n), F32),
        compiler_params=_params("arbitrary", "arbitrary"), name="norm_inproj",
    )(x, shift, scale, norm_w, w_bf16)


def _split_bf16(x):
    hi = x.astype(BF16)
    return hi, (x - hi.astype(F32)).astype(BF16)


def _hg_kernel(q_ref, v_ref, zf_ref, zb_ref, g_ref, cv_ref, czf_ref, czb_ref, lb_ref, nw_ref, o_ref,
               o_s, qd_s, u_s, dec_s, st_s, cu_s, cdec_s, tot_s, *, t_lat, t_ctx):
    c = HG_CHUNK
    dk = q_ref.shape[-1]
    lb = lb_ref[...]
    lbf, lbb = lb[0:1], lb[1:2]

    def prep(rb, zf, zb, q, v, u_out, dec_out, blk, r0):
        nc = rb // c
        ri = lax.broadcasted_iota(jnp.int32, (rb, rb), 0)
        ci = lax.broadcasted_iota(jnp.int32, (rb, rb), 1)
        same = (ri // c) == (ci // c)
        low = same & (ci <= ri)
        upp = same & (ci >= ri)
        ff = lbf + (1.0 - lbf) * _sigmoid(zf)
        fb = lbb + (1.0 - lbb) * _sigmoid(zb)
        lgf, lgb = jnp.log(ff), jnp.log(fb)
        rhs = jnp.concatenate([*_split_bf16(lgf), *_split_bf16(lgb)], axis=1)
        pre = _dot(low.astype(BF16), rhs)
        bcf = pre[:, :dk] + pre[:, dk:2 * dk]
        pfb = pre[:, 2 * dk:3 * dk] + pre[:, 3 * dk:]
        half = tot_s.shape[0] // 2
        tot_s[0:rb, :] = bcf
        tot_s[half:half + rb, :] = pfb
        totf = tot_s[pl.ds(c - 1, nc, stride=c), :]
        totb = tot_s[pl.ds(half + c - 1, nc, stride=c), :]

        def spread(tot):
            return jnp.broadcast_to(tot[:, None, :], (nc, c, dk)).reshape(rb, dk)

        remf = spread(totf) - bcf
        remb = pfb - lgb
        bcb = spread(totb) - remb
        kkf, kkb = 1.0 - ff, 1.0 - fb
        kef = (kkf * jnp.exp(remf)).astype(BF16)
        keb = (kkb * jnp.exp(remb)).astype(BF16)
        chunk_of_row = lax.broadcasted_iota(jnp.int32, (rb, dk), 0) // c
        zero = jnp.zeros((rb, dk), BF16)
        keys = jnp.concatenate([jnp.where(chunk_of_row == j, ke, zero) for ke in (kef, keb) for j in range(nc)],
                               axis=1)
        u_all = _dot(v.T.astype(BF16), keys)
        c0 = blk * nc
        for d in range(2):
            for j in range(nc):
                u_out[d, c0 + j] = u_all[:, (d * nc + j) * dk:(d * nc + j + 1) * dk]
        dec_out[0, pl.ds(pl.multiple_of(c0, nc), nc), :] = jnp.exp(totf)
        dec_out[1, pl.ds(pl.multiple_of(c0, nc), nc), :] = jnp.exp(totb)
        if q is None:
            return
        sq = _silu(q)
        qdf = (sq * jnp.exp(bcf)).astype(BF16)
        qdb = (sq * jnp.exp(bcb)).astype(BF16)
        kdf = (kkf * jnp.exp(-bcf)).astype(BF16)
        kdb = (kkb * jnp.exp(-bcb)).astype(BF16)
        p = jnp.where(low, _dot_nt(qdf, kdf), 0.0) + jnp.where(upp, _dot_nt(qdb, kdb), 0.0)
        o_s[pl.ds(r0, rb), :] = _dot(p.astype(BF16), v.astype(BF16))
        qd_s[pl.ds(r0, rb), 0:dk] = qdf
        qd_s[pl.ds(r0, rb), dk:2 * dk] = qdb

    def scan(n, ur, decr, keep, carry):
        def body(i, carry):
            sf, sb = carry
            j = n - 1 - i
            if keep:
                st_s[i, :, 0:dk] = sf.astype(BF16)
                st_s[j, :, dk:2 * dk] = sb.astype(BF16)
            return sf * decr[0, pl.ds(i, 1), :] + ur[0, i], sb * decr[1, pl.ds(j, 1), :] + ur[1, j]

        return lax.fori_loop(0, n, body, carry, unroll=2)

    rbc = min(MXU_TILE, t_ctx)
    for blk in range(t_ctx // rbc):
        r0 = blk * rbc
        prep(rbc, czf_ref[0, r0:r0 + rbc, :], czb_ref[0, r0:r0 + rbc, :], None, cv_ref[0, r0:r0 + rbc, :],
             cu_s, cdec_s, blk, r0)
    zero = jnp.zeros((dk, dk), F32)
    carry = scan(t_ctx // c, cu_s, cdec_s, False, (zero, zero))

    rbl = min(MXU_TILE, t_lat)

    def lat_prep(blk, _):
        r0 = pl.multiple_of(blk * rbl, rbl)
        rows = pl.ds(r0, rbl)
        prep(rbl, zf_ref[0, rows, :], zb_ref[0, rows, :], q_ref[0, rows, :], v_ref[0, rows, :], u_s, dec_s, blk, r0)
        return 0

    lax.fori_loop(0, t_lat // rbl, lat_prep, 0, unroll=4)
    scan(t_lat // c, u_s, dec_s, True, carry)

    def finish(blk, _):
        r0 = pl.multiple_of(blk * rbl, rbl)
        inter = [_dot_nt(qd_s[pl.ds(r0 + j * c, c), :], st_s[blk * (rbl // c) + j]) for j in range(rbl // c)]
        o = o_s[pl.ds(r0, rbl), :] + jnp.concatenate(inter, axis=0)
        o_ref[0, pl.ds(r0, rbl), :] = _rms(o, nw_ref[...]) * _silu(g_ref[0, pl.ds(r0, rbl), :])
        return 0

    lax.fori_loop(0, t_lat // rbl, finish, 0, unroll=4)


def _hgrn2(p_lat, p_ctx, lb, norm_w):
    b, t, _ = p_lat.shape
    tc = p_ctx.shape[1]
    hw = lb.shape[1]
    dk = hw // HG_HEADS
    nh = HG_HEADS
    c = HG_CHUNK

    def col(k, tt):
        return pl.BlockSpec((1, tt, dk), lambda i, h, k=k: (i, 0, k * nh + h))

    return pl.pallas_call(
        functools.partial(_hg_kernel, t_lat=t, t_ctx=tc), grid=(b, nh),
        in_specs=[col(0, t), col(1, t), col(2, t), col(3, t), col(4, t),
                  col(1, tc), col(2, tc), col(3, tc),
                  pl.BlockSpec((2, dk), lambda i, h: (0, h)),
                  pl.BlockSpec((1, dk), lambda i, h: (0, h))],
        out_specs=pl.BlockSpec((1, t, dk), lambda i, h: (i, 0, h)),
        out_shape=jax.ShapeDtypeStruct((b, t, hw), F32),
        scratch_shapes=[pltpu.VMEM((t, dk), F32),
                        pltpu.VMEM((t, 2 * dk), BF16),
                        pltpu.VMEM((2, t // c, dk, dk), F32),
                        pltpu.VMEM((2, t // c, dk), F32),
                        pltpu.VMEM((t // c, dk, 2 * dk), BF16),
                        pltpu.VMEM((2, tc // c, dk, dk), F32),
                        pltpu.VMEM((2, tc // c, dk), F32),
                        pltpu.VMEM((2 * min(MXU_TILE, max(t, tc)), dk), F32)],
        compiler_params=_params("arbitrary", "arbitrary"), name="hgrn2",
    )(p_lat, p_lat, p_lat, p_lat, p_lat, p_ctx, p_ctx, p_ctx, lb, norm_w)


def _shift_rows(x, k):
    n = x.shape[0]
    y = pltpu.roll(x, k % n, 0)
    r = lax.broadcasted_iota(jnp.int32, x.shape, 0)
    return jnp.where((r >= k) & (r < n + k), y, 0.0)


def _rg_kernel(rx_ref, rgate_ref, crx_ref, cw_ref, cb_ref, wg_ref, bg_ref, lam_ref, o_ref,
               xc_s, af_s, bf_s, ab_s, bb_s, hf_s, hb_s, caf_s, cbf_s, cab_s, cbb_s, *, t_lat, t_ctx):
    w = GRID_W
    rows = t_lat // w
    ch = rx_ref.shape[-1]
    half = ch // 2
    cw = cw_ref[...]
    cb = cb_ref[...]
    bg = bg_ref[...]
    nl = -lam_ref[...]
    hdec = (-0.5 * RG_C) * (jnp.maximum(nl, 0.0) + jnp.log1p(jnp.exp(-jnp.abs(nl))))

    def conv(xm2, xm1, x0, xp1):
        return cb + cw[0:1] * xm2 + cw[1:2] * xm1 + cw[2:3] * x0 + cw[3:4] * xp1

    def gates(xc):
        xb = xc.astype(BF16)
        g0 = _dot(xb[:, :half], wg_ref[0])
        g1 = _dot(xb[:, half:], wg_ref[1])
        outs = []
        for d in range(2):
            pre = []
            for s in (2 * d, 2 * d + 1):
                pre.append(jnp.concatenate([g0[:, s * half:(s + 1) * half], g1[:, s * half:(s + 1) * half]],
                                           axis=1) + bg[:, s * ch:(s + 1) * ch])
            log_a = hdec[d:d + 1] * jnp.tanh(0.5 * pre[0]) + hdec[d:d + 1]
            a = jnp.exp(log_a)
            one_m_a2 = -jnp.tanh(log_a) * (a * a + 1.0)
            half_mult = 0.5 * jnp.where(one_m_a2 > 0.0, one_m_a2 * lax.rsqrt(one_m_a2), 0.0)
            m = half_mult * xc
            outs += [a, m * jnp.tanh(0.5 * pre[1]) + m]
        return outs

    xctx = crx_ref[0]
    xcc = conv(_shift_rows(xctx, 2), _shift_rows(xctx, 1), xctx, _shift_rows(xctx, -1))
    caf_s[...], cbf_s[...], cab_s[...], cbb_s[...] = gates(xcc)

    def cstep(i, carry):
        hf, hb = carry
        hf = caf_s[pl.ds(i, 1), :] * hf + cbf_s[pl.ds(i, 1), :]
        j = t_ctx - 1 - i
        hb = cab_s[pl.ds(j, 1), :] * hb + cbb_s[pl.ds(j, 1), :]
        return hf, hb

    zrow = jnp.zeros((1, ch), F32)
    hf0, hb0 = lax.fori_loop(0, t_ctx, cstep, (zrow, zrow), unroll=8)

    def slab(rr):
        if 0 <= rr < rows:
            return rx_ref[0, rr * w:(rr + 1) * w, :]
        if rr < 0:
            return _shift_rows(rx_ref[0, (rr + rows) * w:(rr + rows + 1) * w, :], 1)
        return _shift_rows(rx_ref[0, (rr - rows) * w:(rr - rows + 1) * w, :], -1)

    for r in range(rows):
        xc_s[r * w:(r + 1) * w, :] = conv(slab(r - 2), slab(r - 1), slab(r), slab(r + 1))

    mb = min(MXU_TILE, t_lat)

    def gbody(i, _):
        r0 = pl.multiple_of(i * mb, mb)
        a_f, b_f, a_b, b_b = gates(xc_s[pl.ds(r0, mb), :])
        af_s[pl.ds(r0, mb), :] = a_f
        bf_s[pl.ds(r0, mb), :] = b_f
        ab_s[pl.ds(r0, mb), :] = a_b
        bb_s[pl.ds(r0, mb), :] = b_b
        return 0

    lax.fori_loop(0, t_lat // mb, gbody, 0, unroll=2)

    def l1(i, _):
        pf = pl.multiple_of(i * w, w)
        qf = pl.multiple_of((i - 1) * w, w)
        a = af_s[pl.ds(pf, w), :]
        af_s[pl.ds(pf, w), :] = a * af_s[pl.ds(qf, w), :]
        bf_s[pl.ds(pf, w), :] = a * bf_s[pl.ds(qf, w), :] + bf_s[pl.ds(pf, w), :]
        pb = pl.multiple_of((rows - 1 - i) * w, w)
        qb = pl.multiple_of((rows - i) * w, w)
        a = ab_s[pl.ds(pb, w), :]
        ab_s[pl.ds(pb, w), :] = a * ab_s[pl.ds(qb, w), :]
        bb_s[pl.ds(pb, w), :] = a * bb_s[pl.ds(qb, w), :] + bb_s[pl.ds(pb, w), :]
        return 0

    lax.fori_loop(1, rows, l1, 0)

    last = (rows - 1) * w

    def l2(i, carry):
        hf, hb = carry
        hf_s[pl.ds(i, 1), :] = hf
        hf = af_s[pl.ds(last + i, 1), :] * hf + bf_s[pl.ds(last + i, 1), :]
        j = w - 1 - i
        hb_s[pl.ds(j, 1), :] = hb
        hb = ab_s[pl.ds(j, 1), :] * hb + bb_s[pl.ds(j, 1), :]
        return hf, hb

    lax.fori_loop(0, w, l2, (hf0, hb0), unroll=8)

    def l3(i, _):
        p = pl.multiple_of(i * w, w)
        h = (af_s[pl.ds(p, w), :] * hf_s[...] + bf_s[pl.ds(p, w), :]
             + ab_s[pl.ds(p, w), :] * hb_s[...] + bb_s[pl.ds(p, w), :])
        o_ref[0, pl.ds(p, w), :] = jax.nn.gelu(rgate_ref[0, pl.ds(p, w), :]) * h
        return 0

    lax.fori_loop(0, rows, l3, 0, unroll=2)


def _rglru(p_lat, p_ctx, conv_w, conv_b, wg, bg, lam):
    b, t, _ = p_lat.shape
    tc = p_ctx.shape[1]
    ch = conv_w.shape[1]
    rx_blk = (p_lat.shape[2] - 2 * ch) // ch
    full = lambda shape: pl.BlockSpec(shape, lambda i: (0,) * len(shape))
    big = lambda: pltpu.VMEM((t, ch), F32)
    small = lambda: pltpu.VMEM((tc, ch), F32)
    return pl.pallas_call(
        functools.partial(_rg_kernel, t_lat=t, t_ctx=tc), grid=(b,),
        in_specs=[pl.BlockSpec((1, t, ch), lambda i: (i, 0, rx_blk)),
                  pl.BlockSpec((1, t, ch), lambda i: (i, 0, rx_blk + 1)),
                  pl.BlockSpec((1, tc, ch), lambda i: (i, 0, rx_blk)),
                  full(conv_w.shape), full(conv_b.shape), full(wg.shape), full(bg.shape), full(lam.shape)],
        out_specs=pl.BlockSpec((1, t, ch), lambda i: (i, 0, 0)),
        out_shape=jax.ShapeDtypeStruct((b, t, ch), F32),
        scratch_shapes=[big(), big(), big(), big(), big(),
                        pltpu.VMEM((GRID_W, ch), F32), pltpu.VMEM((GRID_W, ch), F32),
                        small(), small(), small(), small()],
        compiler_params=_params("arbitrary"), name="rglru",
    )(p_lat, p_lat, p_ctx, conv_w, conv_b, wg, bg, lam)


def _mix_kernel(hg_ref, rg_ref, x_ref, g1_ref, sh_ref, sc_ref, nw_ref, wo_ref, rw_ref, rb_ref,
                x1_ref, h2_ref, meta_ref, cnt_ref, base_s, *, n_exp):
    tm = x_ref.shape[1]

    @pl.when((pl.program_id(0) == 0) & (pl.program_id(1) == 0))
    def _():
        base_s[...] = jnp.zeros_like(base_s)

    hcat = jnp.concatenate([hg_ref[0], rg_ref[0]], axis=1).astype(BF16)
    x1 = x_ref[0] + g1_ref[0] * _dot(hcat, wo_ref[...])
    x1_ref[0] = x1
    h2 = _rms(x1, nw_ref[...]) * (1.0 + sc_ref[0]) + sh_ref[0]
    h2_ref[0] = h2
    logits = _dot(h2.astype(BF16), rw_ref[...]) + rb_ref[...]

    lane_e = lax.broadcasted_iota(jnp.int32, (tm, n_exp), 1).astype(F32)
    vals, idxs = [], []
    cur = logits
    for _ in range(TOP_K):
        m = jnp.max(cur, axis=1, keepdims=True)
        ix = jnp.min(jnp.where(cur == m, lane_e, float(n_exp)), axis=1, keepdims=True)
        vals.append(m)
        idxs.append(ix)
        cur = jnp.where(lane_e == ix, -jnp.inf, cur)
    ex = [jnp.exp(v - vals[0]) for v in vals]
    den = ex[0] + ex[1] + ex[2] + ex[3]

    lane = lax.broadcasted_iota(jnp.int32, (tm, LANES), 1)
    lane_f = lane.astype(F32)
    onehot = jnp.zeros((tm, LANES), F32)
    for k in range(TOP_K):
        onehot = jnp.where(lane_f == idxs[k] + float(k * n_exp), 1.0, onehot)
    ri = lax.broadcasted_iota(jnp.int32, (tm, tm), 0)
    ci = lax.broadcasted_iota(jnp.int32, (tm, tm), 1)
    prefix = _dot((ci < ri).astype(BF16), onehot.astype(BF16))
    tot = jnp.broadcast_to(prefix[tm - 1:tm] + onehot[tm - 1:tm], (8, LANES))
    lane8 = lax.broadcasted_iota(jnp.int32, (8, LANES), 1)
    off = base_s[...]
    tot_all = tot
    for j in range(1, TOP_K):
        rolled = pltpu.roll(tot, j * n_exp, 1)
        off = off + jnp.where(lane8 >= j * n_exp, rolled, 0.0)
        tot_all = tot_all + rolled
    pos = onehot * (prefix + off[0:1])
    meta = jnp.zeros((tm, LANES), F32)
    for k in range(TOP_K):
        in_k = (lane >= k * n_exp) & (lane < (k + 1) * n_exp)
        rank = jnp.sum(jnp.where(in_k, pos, 0.0), axis=1, keepdims=True)
        meta = jnp.where(lane == k, idxs[k], meta)
        meta = jnp.where(lane == TOP_K + k, ex[k] / den, meta)
        meta = jnp.where(lane == 2 * TOP_K + k, rank, meta)
    meta_ref[0] = meta
    base_s[...] = base_s[...] + tot_all
    cnt_ref[...] = base_s[...]


def _mix(hg, rg, x, g1, sh2, sc2, norm_w, wo_bf16, router_w, router_b, tm):
    b, t, d = x.shape
    hw = hg.shape[2]
    n_exp = router_w.shape[1]
    assert TOP_K * n_exp == LANES
    tok = lambda last: pl.BlockSpec((1, tm, last), lambda i, j: (i, j, 0))
    per_b = pl.BlockSpec((1, 1, d), lambda i, j: (i, 0, 0))
    full = lambda shape: pl.BlockSpec(shape, lambda i, j: (0,) * len(shape))
    return pl.pallas_call(
        functools.partial(_mix_kernel, n_exp=n_exp), grid=(b, t // tm),
        in_specs=[tok(hw), tok(hw), tok(d), per_b, per_b, per_b, full((1, d)),
                  full(wo_bf16.shape), full(router_w.shape), full((1, n_exp))],
        out_specs=[tok(d), tok(d), tok(LANES), pl.BlockSpec((8, LANES), lambda i, j: (0, 0))],
        out_shape=[jax.ShapeDtypeStruct((b, t, d), F32), jax.ShapeDtypeStruct((b, t, d), F32),
                   jax.ShapeDtypeStruct((b, t, LANES), F32), jax.ShapeDtypeStruct((8, LANES), F32)],
        scratch_shapes=[pltpu.VMEM((8, LANES), F32)],
        compiler_params=_params("arbitrary", "arbitrary"), name="outproj_router",
    )(hg, rg, x, g1, sh2, sc2, norm_w, wo_bf16, router_w, router_b)


def _dispatch_kernel(fill_off, fill_n, tail, h_ref, dest_hbm, xs_out, idx0_s, idx1_s, zero_s, sem_i, sem_d, sem_z, *,
                     pad_bits):
    td = h_ref.shape[0] * SUBLANES
    n_idx = td * TOP_K
    n_exp = fill_n.shape[0]
    zrows = zero_s.shape[0]
    i = pl.program_id(0)
    slot = i % 2
    idx_s = (idx0_s, idx1_s)

    def idx_copy(j, sl):
        return pltpu.make_async_copy(dest_hbm.at[pl.ds(j * n_idx, n_idx)], idx_s[sl], sem_i.at[sl])

    @pl.when(i == 0)
    def _():
        idx_copy(0, 0).start()

    for sl in range(2):
        @pl.when((i + 1 < pl.num_programs(0)) & (slot != sl))
        def _():
            idx_copy(i + 1, sl).start()

    def fill(wait):
        def go(copy, cond):
            @pl.when(cond)
            def _():
                copy.wait() if wait else copy.start()

        def per_expert(e, _):
            off = fill_off[e]
            npad = fill_n[e]
            n_single = npad & (SUBLANES - 1)
            for r in range(SUBLANES - 1):
                go(pltpu.make_async_copy(zero_s.at[pl.ds(0, 1), :], xs_out.at[pl.ds(off + r, 1), :], sem_z),
                   r < n_single)
            off = pl.multiple_of(off + n_single, SUBLANES)
            for bit in reversed(range(SUBLANES.bit_length() - 1, pad_bits)):
                size = 1 << bit
                go(pltpu.make_async_copy(zero_s.at[pl.ds(0, size), :], xs_out.at[pl.ds(off, size), :], sem_z),
                   (npad & size) != 0)
                off = pl.multiple_of(off + (npad & size), SUBLANES)
            return 0

        lax.fori_loop(0, n_exp, per_expert, 0)

        def per_tail_chunk(j, _):
            off = pl.multiple_of(tail[0] + j * zrows, zrows)
            copy = pltpu.make_async_copy(zero_s, xs_out.at[pl.ds(off, zrows), :], sem_z)
            copy.wait() if wait else copy.start()
            return 0

        lax.fori_loop(0, tail[1], per_tail_chunk, 0)

    @pl.when(i == 0)
    def _():
        zero_s[...] = jnp.zeros_like(zero_s)
        fill(False)

    for sl in range(2):
        @pl.when(slot == sl)
        def _():
            idx_copy(i, sl).wait()

            for g in range(td // SUBLANES):
                for u in range(SUBLANES):
                    for k in range(TOP_K):
                        dst = idx_s[sl][(g * SUBLANES + u) * TOP_K + k]
                        pltpu.make_async_copy(h_ref.at[g, pl.ds(u, 1), :], xs_out.at[pl.ds(dst, 1), :],
                                              sem_d).start(priority=k % 2)

    for _ in range(TOP_K * td // zrows):
        pltpu.make_async_copy(zero_s, xs_out.at[pl.ds(0, zrows), :], sem_d).wait()

    @pl.when(i == 0)
    def _():
        fill(True)


def _dispatch(h2, dest_flat, fill_off, fill_n, tail, n_slots, td, bm):
    n, d = h2.shape
    pad_bits = (bm - 1).bit_length()
    assert (TOP_K * td) % (bm // 2) == 0
    grid_spec = pltpu.PrefetchScalarGridSpec(
        num_scalar_prefetch=3, grid=(n // td,),
        in_specs=[pl.BlockSpec((td // SUBLANES, SUBLANES, d), lambda i, fo, fn, tl: (i, 0, 0)),
                  pl.BlockSpec(memory_space=pl.ANY)],
        out_specs=pl.BlockSpec(memory_space=pl.ANY),
        scratch_shapes=[pltpu.SMEM((td * TOP_K,), jnp.int32), pltpu.SMEM((td * TOP_K,), jnp.int32),
                        pltpu.VMEM((bm // 2, d), F32),
                        pltpu.SemaphoreType.DMA((2,)), pltpu.SemaphoreType.DMA, pltpu.SemaphoreType.DMA])
    return pl.pallas_call(
        functools.partial(_dispatch_kernel, pad_bits=pad_bits), grid_spec=grid_spec,
        out_shape=jax.ShapeDtypeStruct((n_slots, d), F32),
        compiler_params=_params("arbitrary"), name="moe_dispatch",
    )(fill_off, fill_n, tail, h2.reshape(n // SUBLANES, SUBLANES, d), dest_flat)


def _expert_kernel(be_ref, nu_ref, x_ref, wgu_ref, bgu_ref, wd_ref, bd_ref, y_ref, wgu_s, wd_s):
    i = pl.program_id(0)
    d_ff = wd_ref.shape[1]

    @pl.when(i >= nu_ref[0])
    def _():
        y_ref[...] = jnp.zeros_like(y_ref)

    @pl.when(i < nu_ref[0])
    def _():
        @pl.when((i == 0) | (be_ref[i] != be_ref[jnp.maximum(i - 1, 0)]))
        def _():
            wgu_s[...] = wgu_ref[0].astype(BF16)
            wd_s[...] = wd_ref[0].astype(BF16)

        gu = _dot(x_ref[...].astype(BF16), wgu_s[...]) + bgu_ref[0]
        gate = jnp.minimum(gu[:, :d_ff], SWIGLU_LIMIT)
        up = jnp.clip(gu[:, d_ff:], -SWIGLU_LIMIT, SWIGLU_LIMIT)
        act = gate * _sigmoid(SWIGLU_ALPHA * gate) * (up + 1.0)
        y_ref[...] = _dot(act.astype(BF16), wd_s[...]) + bd_ref[0]


def _experts(xs, blk_expert, n_used, wgu, bgu, wd, bd, bm):
    n_slots, d = xs.shape
    n_exp, _, f2 = wgu.shape
    d_ff = wd.shape[1]
    n_blocks = n_slots // bm
    row = lambda i, be, nu: (jnp.minimum(i, nu[0] - 1), 0)
    grid_spec = pltpu.PrefetchScalarGridSpec(
        num_scalar_prefetch=2, grid=(n_blocks,),
        in_specs=[pl.BlockSpec((bm, d), row),
                  pl.BlockSpec((1, d, f2), lambda i, be, nu: (be[i], 0, 0)),
                  pl.BlockSpec((1, 1, f2), lambda i, be, nu: (be[i], 0, 0)),
                  pl.BlockSpec((1, d_ff, d), lambda i, be, nu: (be[i], 0, 0)),
                  pl.BlockSpec((1, 1, d), lambda i, be, nu: (be[i], 0, 0))],
        out_specs=pl.BlockSpec((bm, d), lambda i, be, nu: (i, 0)),
        scratch_shapes=[pltpu.VMEM((d, f2), BF16), pltpu.VMEM((d_ff, d), BF16)])
    return pl.pallas_call(
        _expert_kernel, grid_spec=grid_spec,
        out_shape=jax.ShapeDtypeStruct((n_slots, d), F32),
        compiler_params=_params("arbitrary"), name="moe_experts",
    )(blk_expert, n_used, xs, wgu, bgu.reshape(n_exp, 1, f2), wd, bd.reshape(n_exp, 1, d))


def _combine_kernel(x1_ref, meta_ref, g2_ref, fw_ref, dest_hbm, y_hbm, o_ref, rows_s, idx0_s, idx1_s, sem_i, sem_d):
    tc = x1_ref.shape[0]
    n_idx = tc * TOP_K
    i = pl.program_id(0)
    n = pl.num_programs(0)
    slot = i % 2
    idx_s = (idx0_s, idx1_s)

    def idx_copy(j, sl):
        return pltpu.make_async_copy(dest_hbm.at[pl.ds(j * n_idx, n_idx)], idx_s[sl], sem_i.at[sl])

    def gather(j, sl):
        idx_copy(j, sl).wait()

        for g in range(tc // SUBLANES):
            for u in range(SUBLANES):
                for k in range(TOP_K):
                    src = idx_s[sl][(g * SUBLANES + u) * TOP_K + k]
                    pltpu.make_async_copy(y_hbm.at[pl.ds(src, 1), :], rows_s.at[sl, k, g, pl.ds(u, 1), :],
                                          sem_d.at[sl]).start(priority=k % 2)

    @pl.when(i == 0)
    def _():
        idx_copy(0, 0).start()

        @pl.when(n > 1)
        def _():
            idx_copy(1, 1).start()

    @pl.when((i == 0) | ((i + 1 < n) & (slot == 1)))
    def _():
        gather(jnp.where(i == 0, 0, i + 1), 0)

    @pl.when((i + 1 < n) & (slot == 0))
    def _():
        gather(i + 1, 1)

    for sl in range(2):
        @pl.when((i + 2 < n) & (slot == sl))
        def _():
            idx_copy(i + 2, sl).start()

    for k in range(TOP_K):
        pltpu.make_async_copy(rows_s.at[1 - slot, k], rows_s.at[slot, k], sem_d.at[slot]).wait()

    def rows(k):
        return rows_s[slot, k].reshape(tc, rows_s.shape[-1])

    meta = meta_ref[...]
    moe = meta[:, TOP_K:TOP_K + 1] * rows(0)
    for k in range(1, TOP_K):
        moe = moe + meta[:, TOP_K + k:TOP_K + k + 1] * rows(k)
    o_ref[...] = _rms(x1_ref[...] + g2_ref[0] * moe, fw_ref[...])


def _combine(x1, meta, g2, final_w, dest_flat, y, t_seq, tc):
    n, d = x1.shape
    return pl.pallas_call(
        _combine_kernel, grid=(n // tc,),
        in_specs=[pl.BlockSpec((tc, d), lambda i: (i, 0)),
                  pl.BlockSpec((tc, LANES), lambda i: (i, 0)),
                  pl.BlockSpec((1, 1, d), lambda i: (i * tc // t_seq, 0, 0)),
                  pl.BlockSpec((1, d), lambda i: (0, 0)),
                  pl.BlockSpec(memory_space=pl.ANY),
                  pl.BlockSpec(memory_space=pl.ANY)],
        out_specs=pl.BlockSpec((tc, d), lambda i: (i, 0)),
        out_shape=jax.ShapeDtypeStruct((n, d), F32),
        scratch_shapes=[pltpu.VMEM((2, TOP_K, tc // SUBLANES, SUBLANES, d), F32),
                        pltpu.SMEM((tc * TOP_K,), jnp.int32), pltpu.SMEM((tc * TOP_K,), jnp.int32),
                        pltpu.SemaphoreType.DMA((2,)), pltpu.SemaphoreType.DMA((2,))],
        compiler_params=_params("arbitrary"), name="moe_combine",
    )(x1, meta, g2, final_w, dest_flat, y)


def _gate_weights(wa, wx):
    _, heads, hd, _ = wa.shape
    hh = heads // 2
    eye = jnp.eye(hh, dtype=wa.dtype)

    def blockdiag(wsel):
        return jnp.einsum('hij,hg->higj', wsel, eye).reshape(hh * hd, hh * hd)

    halves = []
    for s in range(2):
        sl = slice(s * hh, (s + 1) * hh)
        halves.append(jnp.concatenate([blockdiag(wa[0, sl]), blockdiag(wx[0, sl]),
                                       blockdiag(wa[1, sl]), blockdiag(wx[1, sl])], axis=1))
    return jnp.stack(halves).astype(BF16)


def kernel(x, c, ctx, c_ctx, norm1_w, norm2_w, w_ada, b_ada, w_in, hg_lb_logits, hg_norm_w, rg_conv_w, rg_conv_b,
           rg_wa, rg_ba, rg_wx, rg_bx, rg_lambda, w_out, router_w, router_b, w_gate_up, b_gate_up, w_down,
           b_down, final_norm_w):
    b, t, d = x.shape
    tcx = ctx.shape[1]
    n_exp = router_w.shape[-1]
    n_tok = b * t
    depth = w_in.shape[0]
    lb_all = jnp.cumsum(jax.nn.softmax(hg_lb_logits.astype(F32), axis=0), axis=0)

    for l in range(depth):
        assert l == depth - 1, "context stream update of non-final layers is not implemented"
        pad = (-(b + 1)) % 8
        c_all = jnp.concatenate([c, c_ctx[None], jnp.zeros((pad, d), F32)], axis=0)
        mod = _mod(c_all, w_ada[l], b_ada[l])
        sh1, sc1, g1, sh2, sc2, g2 = [m[:b, None, :] for m in jnp.split(mod, 6, axis=-1)]
        csh1, csc1 = [m[b:b + 1, None, :] for m in jnp.split(mod, 6, axis=-1)[:2]]

        w_in_b = w_in[l].astype(BF16)
        nw1 = norm1_w[l].reshape(1, d)
        tm = min(TOKEN_TILE, t)
        p_lat = _inproj(x, sh1, sc1, nw1, w_in_b, tm)
        p_ctx = _inproj(ctx, csh1, csc1, nw1, w_in_b, min(MXU_TILE, tcx))

        hg = _hgrn2(p_lat, p_ctx, lb_all[l], hg_norm_w[l].reshape(1, -1))
        ch = rg_conv_w.shape[-1]
        wg = _gate_weights(rg_wa[l], rg_wx[l])
        bg = jnp.concatenate([rg_ba[l, 0], rg_bx[l, 0], rg_ba[l, 1], rg_bx[l, 1]]).reshape(1, 4 * ch)
        rg = _rglru(p_lat, p_ctx, rg_conv_w[l], rg_conv_b[l].reshape(1, ch), wg, bg, rg_lambda[l])

        x1, h2, meta, cnt = _mix(hg, rg, x, g1, sh2, sc2, norm2_w[l].reshape(1, d), w_out[l].astype(BF16),
                                 router_w[l].astype(BF16), router_b[l].reshape(1, n_exp), tm)

        bm = EXPERT_BLOCK
        meta2 = meta.reshape(n_tok, LANES)
        counts = cnt[0, :n_exp].astype(jnp.int32)
        padded = (counts + bm - 1) // bm * bm
        pad_end = jnp.cumsum(padded)
        pad_start = pad_end - padded
        idx = meta2[:, 0:TOP_K].astype(jnp.int32)
        rank = meta2[:, 2 * TOP_K:3 * TOP_K].astype(jnp.int32)
        dest = (pad_start[idx] + rank).reshape(-1)
        n_blocks = -(-n_tok * TOP_K // bm) + n_exp
        blk_start = jnp.arange(n_blocks, dtype=jnp.int32) * bm
        blk_expert = jnp.minimum(jnp.sum(blk_start[:, None] >= pad_end[None, :], axis=1), n_exp - 1).astype(jnp.int32)
        n_used = (pad_end[-1:] // bm).astype(jnp.int32)

        n_slots = n_blocks * bm
        tail = jnp.stack([pad_end[-1], (n_slots - pad_end[-1]) // (bm // 2)]).astype(jnp.int32)
        xs = _dispatch(h2.reshape(n_tok, d), dest, pad_start + counts, padded - counts, tail, n_slots,
                       min(DISPATCH_TILE, t), bm)
        y = _experts(xs, blk_expert, n_used, w_gate_up[l], b_gate_up[l], w_down[l], b_down[l], bm)
        out = _combine(x1.reshape(n_tok, d), meta2, g2, final_norm_w.reshape(1, d), dest, y, t,
                       min(COMBINE_TILE, t))
        return out.reshape(b, t, d)
```

```python
import functools

import jax
import jax.numpy as jnp
from jax import lax
from jax.experimental import pallas as pl
from jax.experimental.pallas import tpu as pltpu

GRID_W = 64
HG_HEADS = 4
HG_CHUNK = 32
RG_HEADS = 8
RG_CONV = 4
RG_C = 8.0
TOP_K = 4
SWIGLU_LIMIT = 7.0
SWIGLU_ALPHA = 1.702
EPS = 1e-6

LANES = 128
SUBLANES = 8
MXU_TILE = 256
VMEM_LIMIT = 56 * 1024 * 1024

ADALN_COLUMN_TILES = 4
TOKEN_TILE = 512
ROUTER_TILE = 512
EXPERT_BLOCK = 512
DISPATCH_TILE = 512
COMBINE_TILE = 512

F32 = jnp.float32
BF16 = jnp.bfloat16
HIGHEST = lax.Precision.HIGHEST


def _params(*sem):
    return pltpu.CompilerParams(dimension_semantics=sem, vmem_limit_bytes=VMEM_LIMIT)


def _sigmoid(x):
    return 0.5 * jnp.tanh(0.5 * x) + 0.5


def _silu(x):
    return x * _sigmoid(x)


def _rms(x, w):
    return x * lax.rsqrt(jnp.mean(x * x, axis=-1, keepdims=True) + EPS) * w


def _dot(a, b):
    return jnp.dot(a, b, preferred_element_type=F32)


def _dot_nt(a, b):
    return lax.dot_general(a, b, (((1,), (1,)), ((), ())), preferred_element_type=F32)


def _mod_kernel(c_ref, w_ref, b_ref, o_ref):
    o_ref[...] = jnp.dot(_silu(c_ref[...]), w_ref[...], preferred_element_type=F32,
                         precision=HIGHEST) + b_ref[...]


def _mod(c_all, w_ada, b_ada):
    r, d = c_all.shape
    n = w_ada.shape[1]
    assert n % (ADALN_COLUMN_TILES * LANES) == 0
    tn = n // ADALN_COLUMN_TILES
    return pl.pallas_call(
        _mod_kernel, grid=(n // tn,),
        in_specs=[pl.BlockSpec((r, d), lambda j: (0, 0)),
                  pl.BlockSpec((d, tn), lambda j: (0, j)),
                  pl.BlockSpec((1, tn), lambda j: (0, j))],
        out_specs=pl.BlockSpec((r, tn), lambda j: (0, j)),
        out_shape=jax.ShapeDtypeStruct((r, n), F32),
        compiler_params=_params("arbitrary"), name="adaln_mod",
    )(c_all, w_ada, b_ada.reshape(1, n))


def _inproj_kernel(x_ref, sh_ref, sc_ref, nw_ref, w_ref, o_ref):
    h = _rms(x_ref[0], nw_ref[...]) * (1.0 + sc_ref[0]) + sh_ref[0]
    p = _dot(h.astype(BF16), w_ref[...])
    for j in range(o_ref.shape[1]):
        o_ref[0, j] = p[:, j * LANES:(j + 1) * LANES]


def _inproj(x, shift, scale, norm_w, w_bf16, tm):
    b, t, d = x.shape
    n = w_bf16.shape[1]
    assert n % LANES == 0
    per_batch = shift.shape[0] == b
    mod_map = (lambda i, j: (i, 0, 0)) if per_batch else (lambda i, j: (0, 0, 0))
    return pl.pallas_call(
        _inproj_kernel, grid=(b, t // tm),
        in_specs=[pl.BlockSpec((1, tm, d), lambda i, j: (i, j, 0)),
                  pl.BlockSpec((1, 1, d), mod_map),
                  pl.BlockSpec((1, 1, d), mod_map),
                  pl.BlockSpec((1, d), lambda i, j: (0, 0)),
                  pl.BlockSpec((d, n), lambda i, j: (0, 0))],
        out_specs=pl.BlockSpec((1, n // LANES, tm, LANES), lambda i, j: (i, 0, j, 0)),
        out_shape=jax.ShapeDtypeStruct((b, n // LANES, t, LANES), F32),
        compiler_params=_params("arbitrary", "arbitrary"), name="norm_inproj",
    )(x, shift, scale, norm_w, w_bf16)


def _split_bf16(x):
    hi = x.astype(BF16)
    return hi, (x - hi.astype(F32)).astype(BF16)


def _hg_kernel(q_ref, v_ref, zf_ref, zb_ref, g_ref, cv_ref, czf_ref, czb_ref, lb_ref, nw_ref, o_ref,
               o_s, qd_s, u_s, dec_s, st_s, cu_s, cdec_s, tot_s, *, t_lat, t_ctx):
    c = HG_CHUNK
    dk = q_ref.shape[-1]
    lb = lb_ref[...]
    lbf, lbb = lb[0:1], lb[1:2]

    def prep(rb, zf, zb, q, v, u_out, dec_out, blk, r0):
        nc = rb // c
        ri = lax.broadcasted_iota(jnp.int32, (rb, rb), 0)
        ci = lax.broadcasted_iota(jnp.int32, (rb, rb), 1)
        same = (ri // c) == (ci // c)
        low = same & (ci <= ri)
        upp = same & (ci >= ri)
        ff = lbf + (1.0 - lbf) * _sigmoid(zf)
        fb = lbb + (1.0 - lbb) * _sigmoid(zb)
        lgf, lgb = jnp.log(ff), jnp.log(fb)
        rhs = jnp.concatenate([*_split_bf16(lgf), *_split_bf16(lgb)], axis=1)
        pre = _dot(low.astype(BF16), rhs)
        bcf = pre[:, :dk] + pre[:, dk:2 * dk]
        pfb = pre[:, 2 * dk:3 * dk] + pre[:, 3 * dk:]
        half = tot_s.shape[0] // 2
        tot_s[0:rb, :] = bcf
        tot_s[half:half + rb, :] = pfb
        totf = tot_s[pl.ds(c - 1, nc, stride=c), :]
        totb = tot_s[pl.ds(half + c - 1, nc, stride=c), :]

        def spread(tot):
            return jnp.broadcast_to(tot[:, None, :], (nc, c, dk)).reshape(rb, dk)

        remf = spread(totf) - bcf
        remb = pfb - lgb
        bcb = spread(totb) - remb
        kkf, kkb = 1.0 - ff, 1.0 - fb
        kef = (kkf * jnp.exp(remf)).astype(BF16)
        keb = (kkb * jnp.exp(remb)).astype(BF16)
        chunk_of_row = lax.broadcasted_iota(jnp.int32, (rb, dk), 0) // c
        zero = jnp.zeros((rb, dk), BF16)
        keys = jnp.concatenate([jnp.where(chunk_of_row == j, ke, zero) for ke in (kef, keb) for j in range(nc)],
                               axis=1)
        u_all = _dot(v.T.astype(BF16), keys)
        c0 = blk * nc
        for d in range(2):
            for j in range(nc):
                u_out[d, c0 + j] = u_all[:, (d * nc + j) * dk:(d * nc + j + 1) * dk]
        dec_out[0, pl.ds(pl.multiple_of(c0, nc), nc), :] = jnp.exp(totf)
        dec_out[1, pl.ds(pl.multiple_of(c0, nc), nc), :] = jnp.exp(totb)
        if q is None:
            return
        sq = _silu(q)
        qdf = (sq * jnp.exp(bcf)).astype(BF16)
        qdb = (sq * jnp.exp(bcb)).astype(BF16)
        kdf = (kkf * jnp.exp(-bcf)).astype(BF16)
        kdb = (kkb * jnp.exp(-bcb)).astype(BF16)
        p = jnp.where(low, _dot_nt(qdf, kdf), 0.0) + jnp.where(upp, _dot_nt(qdb, kdb), 0.0)
        o_s[pl.ds(r0, rb), :] = _dot(p.astype(BF16), v.astype(BF16))
        qd_s[pl.ds(r0, rb), 0:dk] = qdf
        qd_s[pl.ds(r0, rb), dk:2 * dk] = qdb

    def scan(n, ur, decr, keep, carry):
        def body(i, carry):
            sf, sb = carry
            j = n - 1 - i
            if keep:
                st_s[i, :, 0:dk] = sf.astype(BF16)
                st_s[j, :, dk:2 * dk] = sb.astype(BF16)
            return sf * decr[0, pl.ds(i, 1), :] + ur[0, i], sb * decr[1, pl.ds(j, 1), :] + ur[1, j]

        return lax.fori_loop(0, n, body, carry, unroll=2)

    rbc = min(MXU_TILE, t_ctx)
    for blk in range(t_ctx // rbc):
        r0 = blk * rbc
        prep(rbc, czf_ref[0, r0:r0 + rbc, :], czb_ref[0, r0:r0 + rbc, :], None, cv_ref[0, r0:r0 + rbc, :],
             cu_s, cdec_s, blk, r0)
    zero = jnp.zeros((dk, dk), F32)
    carry = scan(t_ctx // c, cu_s, cdec_s, False, (zero, zero))

    rbl = min(MXU_TILE, t_lat)

    def lat_prep(blk, _):
        r0 = pl.multiple_of(blk * rbl, rbl)
        rows = pl.ds(r0, rbl)
        prep(rbl, zf_ref[0, rows, :], zb_ref[0, rows, :], q_ref[0, rows, :], v_ref[0, rows, :], u_s, dec_s, blk, r0)
        return 0

    lax.fori_loop(0, t_lat // rbl, lat_prep, 0, unroll=4)
    scan(t_lat // c, u_s, dec_s, True, carry)

    def finish(blk, _):
        r0 = pl.multiple_of(blk * rbl, rbl)
        inter = [_dot_nt(qd_s[pl.ds(r0 + j * c, c), :], st_s[blk * (rbl // c) + j]) for j in range(rbl // c)]
        o = o_s[pl.ds(r0, rbl), :] + jnp.concatenate(inter, axis=0)
        o_ref[0, pl.ds(r0, rbl), :] = _rms(o, nw_ref[...]) * _silu(g_ref[0, pl.ds(r0, rbl), :])
        return 0

    lax.fori_loop(0, t_lat // rbl, finish, 0, unroll=4)


def _hgrn2(p_lat, p_ctx, lb, norm_w):
    b, _, t, _ = p_lat.shape
    tc = p_ctx.shape[2]
    hw = lb.shape[1]
    dk = hw // HG_HEADS
    nh = HG_HEADS
    c = HG_CHUNK
    assert dk == LANES

    def col(k, tt):
        return pl.BlockSpec((None, 1, tt, dk), lambda i, h, k=k: (i, k * nh + h, 0, 0))

    return pl.pallas_call(
        functools.partial(_hg_kernel, t_lat=t, t_ctx=tc), grid=(b, nh),
        in_specs=[col(0, t), col(1, t), col(2, t), col(3, t), col(4, t),
                  col(1, tc), col(2, tc), col(3, tc),
                  pl.BlockSpec((2, dk), lambda i, h: (0, h)),
                  pl.BlockSpec((1, dk), lambda i, h: (0, h))],
        out_specs=pl.BlockSpec((1, t, dk), lambda i, h: (i, 0, h)),
        out_shape=jax.ShapeDtypeStruct((b, t, hw), F32),
        scratch_shapes=[pltpu.VMEM((t, dk), F32),
                        pltpu.VMEM((t, 2 * dk), BF16),
                        pltpu.VMEM((2, t // c, dk, dk), F32),
                        pltpu.VMEM((2, t // c, dk), F32),
                        pltpu.VMEM((t // c, dk, 2 * dk), BF16),
                        pltpu.VMEM((2, tc // c, dk, dk), F32),
                        pltpu.VMEM((2, tc // c, dk), F32),
                        pltpu.VMEM((2 * min(MXU_TILE, max(t, tc)), dk), F32)],
        compiler_params=_params("arbitrary", "arbitrary"), name="hgrn2",
    )(p_lat, p_lat, p_lat, p_lat, p_lat, p_ctx, p_ctx, p_ctx, lb, norm_w)


def _shift_rows(x, k):
    n = x.shape[0]
    y = pltpu.roll(x, k % n, 0)
    r = lax.broadcasted_iota(jnp.int32, x.shape, 0)
    return jnp.where((r >= k) & (r < n + k), y, 0.0)


def _rg_kernel(rx_ref, rgate_ref, crx_ref, cw_ref, cb_ref, wg_ref, bg_ref, lam_ref, o_ref,
               xc_s, af_s, bf_s, ab_s, bb_s, hf_s, hb_s, caf_s, cbf_s, cab_s, cbb_s, *, t_lat, t_ctx):
    w = GRID_W
    rows = t_lat // w
    ch = rx_ref.shape[0] * rx_ref.shape[-1]
    half = ch // 2

    def wide(ref, rws):
        return jnp.concatenate([ref[j, rws, :] for j in range(ref.shape[0])], axis=1)

    cw = cw_ref[...]
    cb = cb_ref[...]
    bg = bg_ref[...]
    nl = -lam_ref[...]
    hdec = (-0.5 * RG_C) * (jnp.maximum(nl, 0.0) + jnp.log1p(jnp.exp(-jnp.abs(nl))))

    def conv(xm2, xm1, x0, xp1):
        return cb + cw[0:1] * xm2 + cw[1:2] * xm1 + cw[2:3] * x0 + cw[3:4] * xp1

    def gates(xc):
        xb = xc.astype(BF16)
        g0 = _dot(xb[:, :half], wg_ref[0])
        g1 = _dot(xb[:, half:], wg_ref[1])
        outs = []
        for d in range(2):
            pre = []
            for s in (2 * d, 2 * d + 1):
                pre.append(jnp.concatenate([g0[:, s * half:(s + 1) * half], g1[:, s * half:(s + 1) * half]],
                                           axis=1) + bg[:, s * ch:(s + 1) * ch])
            log_a = hdec[d:d + 1] * jnp.tanh(0.5 * pre[0]) + hdec[d:d + 1]
            a = jnp.exp(log_a)
            one_m_a2 = -jnp.tanh(log_a) * (a * a + 1.0)
            half_mult = 0.5 * jnp.where(one_m_a2 > 0.0, one_m_a2 * lax.rsqrt(one_m_a2), 0.0)
            m = half_mult * xc
            outs += [a, m * jnp.tanh(0.5 * pre[1]) + m]
        return outs

    xctx = wide(crx_ref, slice(None))
    xcc = conv(_shift_rows(xctx, 2), _shift_rows(xctx, 1), xctx, _shift_rows(xctx, -1))
    caf_s[...], cbf_s[...], cab_s[...], cbb_s[...] = gates(xcc)

    def cstep(i, carry):
        hf, hb = carry
        hf = caf_s[pl.ds(i, 1), :] * hf + cbf_s[pl.ds(i, 1), :]
        j = t_ctx - 1 - i
        hb = cab_s[pl.ds(j, 1), :] * hb + cbb_s[pl.ds(j, 1), :]
        return hf, hb

    zrow = jnp.zeros((1, ch), F32)
    hf0, hb0 = lax.fori_loop(0, t_ctx, cstep, (zrow, zrow), unroll=8)

    def slab(rr):
        if 0 <= rr < rows:
            return wide(rx_ref, slice(rr * w, (rr + 1) * w))
        if rr < 0:
            return _shift_rows(wide(rx_ref, slice((rr + rows) * w, (rr + rows + 1) * w)), 1)
        return _shift_rows(wide(rx_ref, slice((rr - rows) * w, (rr - rows + 1) * w)), -1)

    for r in range(rows):
        xc_s[r * w:(r + 1) * w, :] = conv(slab(r - 2), slab(r - 1), slab(r), slab(r + 1))

    mb = min(MXU_TILE, t_lat)

    def gbody(i, _):
        r0 = pl.multiple_of(i * mb, mb)
        a_f, b_f, a_b, b_b = gates(xc_s[pl.ds(r0, mb), :])
        af_s[pl.ds(r0, mb), :] = a_f
        bf_s[pl.ds(r0, mb), :] = b_f
        ab_s[pl.ds(r0, mb), :] = a_b
        bb_s[pl.ds(r0, mb), :] = b_b
        return 0

    lax.fori_loop(0, t_lat // mb, gbody, 0, unroll=2)

    def l1(i, _):
        pf = pl.multiple_of(i * w, w)
        qf = pl.multiple_of((i - 1) * w, w)
        a = af_s[pl.ds(pf, w), :]
        af_s[pl.ds(pf, w), :] = a * af_s[pl.ds(qf, w), :]
        bf_s[pl.ds(pf, w), :] = a * bf_s[pl.ds(qf, w), :] + bf_s[pl.ds(pf, w), :]
        pb = pl.multiple_of((rows - 1 - i) * w, w)
        qb = pl.multiple_of((rows - i) * w, w)
        a = ab_s[pl.ds(pb, w), :]
        ab_s[pl.ds(pb, w), :] = a * ab_s[pl.ds(qb, w), :]
        bb_s[pl.ds(pb, w), :] = a * bb_s[pl.ds(qb, w), :] + bb_s[pl.ds(pb, w), :]
        return 0

    lax.fori_loop(1, rows, l1, 0)

    last = (rows - 1) * w

    def l2(i, carry):
        hf, hb = carry
        hf_s[pl.ds(i, 1), :] = hf
        hf = af_s[pl.ds(last + i, 1), :] * hf + bf_s[pl.ds(last + i, 1), :]
        j = w - 1 - i
        hb_s[pl.ds(j, 1), :] = hb
        hb = ab_s[pl.ds(j, 1), :] * hb + bb_s[pl.ds(j, 1), :]
        return hf, hb

    lax.fori_loop(0, w, l2, (hf0, hb0), unroll=8)

    def l3(i, _):
        p = pl.multiple_of(i * w, w)
        h = (af_s[pl.ds(p, w), :] * hf_s[...] + bf_s[pl.ds(p, w), :]
             + ab_s[pl.ds(p, w), :] * hb_s[...] + bb_s[pl.ds(p, w), :])
        o_ref[0, pl.ds(p, w), :] = jax.nn.gelu(wide(rgate_ref, pl.ds(p, w))) * h
        return 0

    lax.fori_loop(0, rows, l3, 0, unroll=2)


def _rglru(p_lat, p_ctx, conv_w, conv_b, wg, bg, lam):
    b, n_slabs, t, _ = p_lat.shape
    tc = p_ctx.shape[2]
    ch = conv_w.shape[1]
    cs = ch // LANES
    rx_blk = n_slabs // cs - 2
    full = lambda shape: pl.BlockSpec(shape, lambda i: (0,) * len(shape))
    big = lambda: pltpu.VMEM((t, ch), F32)
    small = lambda: pltpu.VMEM((tc, ch), F32)
    return pl.pallas_call(
        functools.partial(_rg_kernel, t_lat=t, t_ctx=tc), grid=(b,),
        in_specs=[pl.BlockSpec((None, cs, t, LANES), lambda i: (i, rx_blk, 0, 0)),
                  pl.BlockSpec((None, cs, t, LANES), lambda i: (i, rx_blk + 1, 0, 0)),
                  pl.BlockSpec((None, cs, tc, LANES), lambda i: (i, rx_blk, 0, 0)),
                  full(conv_w.shape), full(conv_b.shape), full(wg.shape), full(bg.shape), full(lam.shape)],
        out_specs=pl.BlockSpec((1, t, ch), lambda i: (i, 0, 0)),
        out_shape=jax.ShapeDtypeStruct((b, t, ch), F32),
        scratch_shapes=[big(), big(), big(), big(), big(),
                        pltpu.VMEM((GRID_W, ch), F32), pltpu.VMEM((GRID_W, ch), F32),
                        small(), small(), small(), small()],
        compiler_params=_params("arbitrary"), name="rglru",
    )(p_lat, p_lat, p_ctx, conv_w, conv_b, wg, bg, lam)


def _mix_kernel(hg_ref, rg_ref, x_ref, g1_ref, sh_ref, sc_ref, nw_ref, wo_ref, rw_ref, rb_ref,
                x1_ref, h2_ref, meta_ref, cnt_ref, base_s, *, n_exp):
    tm = x_ref.shape[1]

    @pl.when((pl.program_id(0) == 0) & (pl.program_id(1) == 0))
    def _():
        base_s[...] = jnp.zeros_like(base_s)

    hcat = jnp.concatenate([hg_ref[0], rg_ref[0]], axis=1).astype(BF16)
    x1 = x_ref[0] + g1_ref[0] * _dot(hcat, wo_ref[...])
    x1_ref[0] = x1
    h2 = _rms(x1, nw_ref[...]) * (1.0 + sc_ref[0]) + sh_ref[0]
    h2_ref[0] = h2
    logits = _dot(h2.astype(BF16), rw_ref[...]) + rb_ref[...]

    lane_e = lax.broadcasted_iota(jnp.int32, (tm, n_exp), 1).astype(F32)
    vals, idxs = [], []
    cur = logits
    for _ in range(TOP_K):
        m = jnp.max(cur, axis=1, keepdims=True)
        ix = jnp.min(jnp.where(cur == m, lane_e, float(n_exp)), axis=1, keepdims=True)
        vals.append(m)
        idxs.append(ix)
        cur = jnp.where(lane_e == ix, -jnp.inf, cur)
    ex = [jnp.exp(v - vals[0]) for v in vals]
    den = ex[0] + ex[1] + ex[2] + ex[3]

    lane = lax.broadcasted_iota(jnp.int32, (tm, LANES), 1)
    lane_f = lane.astype(F32)
    onehot = jnp.zeros((tm, LANES), F32)
    for k in range(TOP_K):
        onehot = jnp.where(lane_f == idxs[k] + float(k * n_exp), 1.0, onehot)
    ri = lax.broadcasted_iota(jnp.int32, (tm, tm), 0)
    ci = lax.broadcasted_iota(jnp.int32, (tm, tm), 1)
    prefix = _dot((ci < ri).astype(BF16), onehot.astype(BF16))
    tot = jnp.broadcast_to(prefix[tm - 1:tm] + onehot[tm - 1:tm], (8, LANES))
    lane8 = lax.broadcasted_iota(jnp.int32, (8, LANES), 1)
    off = base_s[...]
    tot_all = tot
    for j in range(1, TOP_K):
        rolled = pltpu.roll(tot, j * n_exp, 1)
        off = off + jnp.where(lane8 >= j * n_exp, rolled, 0.0)
        tot_all = tot_all + rolled
    pos = onehot * (prefix + off[0:1])
    meta = jnp.zeros((tm, LANES), F32)
    for k in range(TOP_K):
        in_k = (lane >= k * n_exp) & (lane < (k + 1) * n_exp)
        rank = jnp.sum(jnp.where(in_k, pos, 0.0), axis=1, keepdims=True)
        meta = jnp.where(lane == k, idxs[k], meta)
        meta = jnp.where(lane == TOP_K + k, ex[k] / den, meta)
        meta = jnp.where(lane == 2 * TOP_K + k, rank, meta)
    meta_ref[0] = meta
    base_s[...] = base_s[...] + tot_all
    cnt_ref[...] = base_s[...]


def _mix(hg, rg, x, g1, sh2, sc2, norm_w, wo_bf16, router_w, router_b, tm):
    b, t, d = x.shape
    hw = hg.shape[2]
    n_exp = router_w.shape[1]
    assert TOP_K * n_exp == LANES
    tok = lambda last: pl.BlockSpec((1, tm, last), lambda i, j: (i, j, 0))
    per_b = pl.BlockSpec((1, 1, d), lambda i, j: (i, 0, 0))
    full = lambda shape: pl.BlockSpec(shape, lambda i, j: (0,) * len(shape))
    return pl.pallas_call(
        functools.partial(_mix_kernel, n_exp=n_exp), grid=(b, t // tm),
        in_specs=[tok(hw), tok(hw), tok(d), per_b, per_b, per_b, full((1, d)),
                  full(wo_bf16.shape), full(router_w.shape), full((1, n_exp))],
        out_specs=[tok(d), tok(d), tok(LANES), pl.BlockSpec((8, LANES), lambda i, j: (0, 0))],
        out_shape=[jax.ShapeDtypeStruct((b, t, d), F32), jax.ShapeDtypeStruct((b, t, d), F32),
                   jax.ShapeDtypeStruct((b, t, LANES), F32), jax.ShapeDtypeStruct((8, LANES), F32)],
        scratch_shapes=[pltpu.VMEM((8, LANES), F32)],
        compiler_params=_params("arbitrary", "arbitrary"), name="outproj_router",
    )(hg, rg, x, g1, sh2, sc2, norm_w, wo_bf16, router_w, router_b)


def _dispatch_kernel(fill_off, fill_n, tail, h_ref, dest_hbm, xs_out, idx0_s, idx1_s, zero_s, sem_i, sem_d, sem_z, *,
                     pad_bits):
    td = h_ref.shape[0] * SUBLANES
    n_idx = td * TOP_K
    n_exp = fill_n.shape[0]
    zrows = zero_s.shape[0]
    i = pl.program_id(0)
    slot = i % 2
    idx_s = (idx0_s, idx1_s)

    def idx_copy(j, sl):
        return pltpu.make_async_copy(dest_hbm.at[pl.ds(j * n_idx, n_idx)], idx_s[sl], sem_i.at[sl])

    @pl.when(i == 0)
    def _():
        idx_copy(0, 0).start()

    for sl in range(2):
        @pl.when((i + 1 < pl.num_programs(0)) & (slot != sl))
        def _():
            idx_copy(i + 1, sl).start()

    def fill(wait):
        def go(copy, cond):
            @pl.when(cond)
            def _():
                copy.wait() if wait else copy.start()

        def per_expert(e, _):
            off = fill_off[e]
            npad = fill_n[e]
            n_single = npad & (SUBLANES - 1)
            for r in range(SUBLANES - 1):
                go(pltpu.make_async_copy(zero_s.at[pl.ds(0, 1), :], xs_out.at[pl.ds(off + r, 1), :], sem_z),
                   r < n_single)
            off = pl.multiple_of(off + n_single, SUBLANES)
            for bit in reversed(range(SUBLANES.bit_length() - 1, pad_bits)):
                size = 1 << bit
                go(pltpu.make_async_copy(zero_s.at[pl.ds(0, size), :], xs_out.at[pl.ds(off, size), :], sem_z),
                   (npad & size) != 0)
                off = pl.multiple_of(off + (npad & size), SUBLANES)
            return 0

        lax.fori_loop(0, n_exp, per_expert, 0)

        def per_tail_chunk(j, _):
            off = pl.multiple_of(tail[0] + j * zrows, zrows)
            copy = pltpu.make_async_copy(zero_s, xs_out.at[pl.ds(off, zrows), :], sem_z)
            copy.wait() if wait else copy.start()
            return 0

        lax.fori_loop(0, tail[1], per_tail_chunk, 0)

    @pl.when(i == 0)
    def _():
        zero_s[...] = jnp.zeros_like(zero_s)
        fill(False)

    for sl in range(2):
        @pl.when(slot == sl)
        def _():
            idx_copy(i, sl).wait()

            for g in range(td // SUBLANES):
                for u in range(SUBLANES):
                    for k in range(TOP_K):
                        dst = idx_s[sl][(g * SUBLANES + u) * TOP_K + k]
                        pltpu.make_async_copy(h_ref.at[g, pl.ds(u, 1), :], xs_out.at[pl.ds(dst, 1), :],
                                              sem_d).start(priority=k % 2)

    for _ in range(TOP_K * td // zrows):
        pltpu.make_async_copy(zero_s, xs_out.at[pl.ds(0, zrows), :], sem_d).wait()

    @pl.when(i == 0)
    def _():
        fill(True)


def _dispatch(h2, dest_flat, fill_off, fill_n, tail, n_slots, td, bm):
    n, d = h2.shape
    pad_bits = (bm - 1).bit_length()
    assert (TOP_K * td) % (bm // 2) == 0
    grid_spec = pltpu.PrefetchScalarGridSpec(
        num_scalar_prefetch=3, grid=(n // td,),
        in_specs=[pl.BlockSpec((td // SUBLANES, SUBLANES, d), lambda i, fo, fn, tl: (i, 0, 0)),
                  pl.BlockSpec(memory_space=pl.ANY)],
        out_specs=pl.BlockSpec(memory_space=pl.ANY),
        scratch_shapes=[pltpu.SMEM((td * TOP_K,), jnp.int32), pltpu.SMEM((td * TOP_K,), jnp.int32),
                        pltpu.VMEM((bm // 2, d), F32),
                        pltpu.SemaphoreType.DMA((2,)), pltpu.SemaphoreType.DMA, pltpu.SemaphoreType.DMA])
    return pl.pallas_call(
        functools.partial(_dispatch_kernel, pad_bits=pad_bits), grid_spec=grid_spec,
        out_shape=jax.ShapeDtypeStruct((n_slots, d), F32),
        compiler_params=_params("arbitrary"), name="moe_dispatch",
    )(fill_off, fill_n, tail, h2.reshape(n // SUBLANES, SUBLANES, d), dest_flat)


def _expert_kernel(be_ref, nu_ref, x_ref, wgu_ref, bgu_ref, wd_ref, bd_ref, y_ref, wgu_s, wd_s):
    i = pl.program_id(0)
    d_ff = wd_ref.shape[1]

    @pl.when(i >= nu_ref[0])
    def _():
        y_ref[...] = jnp.zeros_like(y_ref)

    @pl.when(i < nu_ref[0])
    def _():
        @pl.when((i == 0) | (be_ref[i] != be_ref[jnp.maximum(i - 1, 0)]))
        def _():
            wgu_s[...] = wgu_ref[0].astype(BF16)
            wd_s[...] = wd_ref[0].astype(BF16)

        gu = _dot(x_ref[...].astype(BF16), wgu_s[...]) + bgu_ref[0]
        gate = jnp.minimum(gu[:, :d_ff], SWIGLU_LIMIT)
        up = jnp.clip(gu[:, d_ff:], -SWIGLU_LIMIT, SWIGLU_LIMIT)
        act = gate * _sigmoid(SWIGLU_ALPHA * gate) * (up + 1.0)
        y_ref[...] = _dot(act.astype(BF16), wd_s[...]) + bd_ref[0]


def _experts(xs, blk_expert, n_used, wgu, bgu, wd, bd, bm):
    n_slots, d = xs.shape
    n_exp, _, f2 = wgu.shape
    d_ff = wd.shape[1]
    n_blocks = n_slots // bm
    row = lambda i, be, nu: (jnp.minimum(i, nu[0] - 1), 0)
    grid_spec = pltpu.PrefetchScalarGridSpec(
        num_scalar_prefetch=2, grid=(n_blocks,),
        in_specs=[pl.BlockSpec((bm, d), row),
                  pl.BlockSpec((1, d, f2), lambda i, be, nu: (be[i], 0, 0)),
                  pl.BlockSpec((1, 1, f2), lambda i, be, nu: (be[i], 0, 0)),
                  pl.BlockSpec((1, d_ff, d), lambda i, be, nu: (be[i], 0, 0)),
                  pl.BlockSpec((1, 1, d), lambda i, be, nu: (be[i], 0, 0))],
        out_specs=pl.BlockSpec((bm, d), lambda i, be, nu: (i, 0)),
        scratch_shapes=[pltpu.VMEM((d, f2), BF16), pltpu.VMEM((d_ff, d), BF16)])
    return pl.pallas_call(
        _expert_kernel, grid_spec=grid_spec,
        out_shape=jax.ShapeDtypeStruct((n_slots, d), F32),
        compiler_params=_params("arbitrary"), name="moe_experts",
    )(blk_expert, n_used, xs, wgu, bgu.reshape(n_exp, 1, f2), wd, bd.reshape(n_exp, 1, d))


def _combine_kernel(x1_ref, meta_ref, g2_ref, fw_ref, dest_hbm, y_hbm, o_ref, rows_s, idx0_s, idx1_s, sem_i, sem_d):
    tc = x1_ref.shape[0]
    n_idx = tc * TOP_K
    i = pl.program_id(0)
    n = pl.num_programs(0)
    slot = i % 2
    idx_s = (idx0_s, idx1_s)

    def idx_copy(j, sl):
        return pltpu.make_async_copy(dest_hbm.at[pl.ds(j * n_idx, n_idx)], idx_s[sl], sem_i.at[sl])

    def gather(j, sl):
        idx_copy(j, sl).wait()

        for g in range(tc // SUBLANES):
            for u in range(SUBLANES):
                for k in range(TOP_K):
                    src = idx_s[sl][(g * SUBLANES + u) * TOP_K + k]
                    pltpu.make_async_copy(y_hbm.at[pl.ds(src, 1), :], rows_s.at[sl, k, g, pl.ds(u, 1), :],
                                          sem_d.at[sl]).start(priority=k % 2)

    @pl.when(i == 0)
    def _():
        idx_copy(0, 0).start()

        @pl.when(n > 1)
        def _():
            idx_copy(1, 1).start()

    @pl.when((i == 0) | ((i + 1 < n) & (slot == 1)))
    def _():
        gather(jnp.where(i == 0, 0, i + 1), 0)

    @pl.when((i + 1 < n) & (slot == 0))
    def _():
        gather(i + 1, 1)

    for sl in range(2):
        @pl.when((i + 2 < n) & (slot == sl))
        def _():
            idx_copy(i + 2, sl).start()

    for k in range(TOP_K):
        pltpu.make_async_copy(rows_s.at[1 - slot, k], rows_s.at[slot, k], sem_d.at[slot]).wait()

    def rows(k):
        return rows_s[slot, k].reshape(tc, rows_s.shape[-1])

    meta = meta_ref[...]
    moe = meta[:, TOP_K:TOP_K + 1] * rows(0)
    for k in range(1, TOP_K):
        moe = moe + meta[:, TOP_K + k:TOP_K + k + 1] * rows(k)
    o_ref[...] = _rms(x1_ref[...] + g2_ref[0] * moe, fw_ref[...])


def _combine(x1, meta, g2, final_w, dest_flat, y, t_seq, tc):
    n, d = x1.shape
    return pl.pallas_call(
        _combine_kernel, grid=(n // tc,),
        in_specs=[pl.BlockSpec((tc, d), lambda i: (i, 0)),
                  pl.BlockSpec((tc, LANES), lambda i: (i, 0)),
                  pl.BlockSpec((1, 1, d), lambda i: (i * tc // t_seq, 0, 0)),
                  pl.BlockSpec((1, d), lambda i: (0, 0)),
                  pl.BlockSpec(memory_space=pl.ANY),
                  pl.BlockSpec(memory_space=pl.ANY)],
        out_specs=pl.BlockSpec((tc, d), lambda i: (i, 0)),
        out_shape=jax.ShapeDtypeStruct((n, d), F32),
        scratch_shapes=[pltpu.VMEM((2, TOP_K, tc // SUBLANES, SUBLANES, d), F32),
                        pltpu.SMEM((tc * TOP_K,), jnp.int32), pltpu.SMEM((tc * TOP_K,), jnp.int32),
                        pltpu.SemaphoreType.DMA((2,)), pltpu.SemaphoreType.DMA((2,))],
        compiler_params=_params("arbitrary"), name="moe_combine",
    )(x1, meta, g2, final_w, dest_flat, y)


def _gate_weights(wa, wx):
    _, heads, hd, _ = wa.shape
    hh = heads // 2
    eye = jnp.eye(hh, dtype=wa.dtype)

    def blockdiag(wsel):
        return jnp.einsum('hij,hg->higj', wsel, eye).reshape(hh * hd, hh * hd)

    halves = []
    for s in range(2):
        sl = slice(s * hh, (s + 1) * hh)
        halves.append(jnp.concatenate([blockdiag(wa[0, sl]), blockdiag(wx[0, sl]),
                                       blockdiag(wa[1, sl]), blockdiag(wx[1, sl])], axis=1))
    return jnp.stack(halves).astype(BF16)


def kernel(x, c, ctx, c_ctx, norm1_w, norm2_w, w_ada, b_ada, w_in, hg_lb_logits, hg_norm_w, rg_conv_w, rg_conv_b,
           rg_wa, rg_ba, rg_wx, rg_bx, rg_lambda, w_out, router_w, router_b, w_gate_up, b_gate_up, w_down,
           b_down, final_norm_w):
    b, t, d = x.shape
    tcx = ctx.shape[1]
    n_exp = router_w.shape[-1]
    n_tok = b * t
    depth = w_in.shape[0]
    lb_all = jnp.cumsum(jax.nn.softmax(hg_lb_logits.astype(F32), axis=0), axis=0)

    for l in range(depth):
        assert l == depth - 1, "context stream update of non-final layers is not implemented"
        pad = (-(b + 1)) % 8
        c_all = jnp.concatenate([c, c_ctx[None], jnp.zeros((pad, d), F32)], axis=0)
        mod = _mod(c_all, w_ada[l], b_ada[l])
        sh1, sc1, g1, sh2, sc2, g2 = [m[:b, None, :] for m in jnp.split(mod, 6, axis=-1)]
        csh1, csc1 = [m[b:b + 1, None, :] for m in jnp.split(mod, 6, axis=-1)[:2]]

        w_in_b = w_in[l].astype(BF16)
        nw1 = norm1_w[l].reshape(1, d)
        tm = min(TOKEN_TILE, t)
        p_lat = _inproj(x, sh1, sc1, nw1, w_in_b, tm)
        p_ctx = _inproj(ctx, csh1, csc1, nw1, w_in_b, min(MXU_TILE, tcx))

        hg = _hgrn2(p_lat, p_ctx, lb_all[l], hg_norm_w[l].reshape(1, -1))
        ch = rg_conv_w.shape[-1]
        wg = _gate_weights(rg_wa[l], rg_wx[l])
        bg = jnp.concatenate([rg_ba[l, 0], rg_bx[l, 0], rg_ba[l, 1], rg_bx[l, 1]]).reshape(1, 4 * ch)
        rg = _rglru(p_lat, p_ctx, rg_conv_w[l], rg_conv_b[l].reshape(1, ch), wg, bg, rg_lambda[l])

        x1, h2, meta, cnt = _mix(hg, rg, x, g1, sh2, sc2, norm2_w[l].reshape(1, d), w_out[l].astype(BF16),
                                 router_w[l].astype(BF16), router_b[l].reshape(1, n_exp), min(ROUTER_TILE, t))

        bm = EXPERT_BLOCK
        meta2 = meta.reshape(n_tok, LANES)
        counts = cnt[0, :n_exp].astype(jnp.int32)
        padded = (counts + bm - 1) // bm * bm
        pad_end = jnp.cumsum(padded)
        pad_start = pad_end - padded
        idx = meta2[:, 0:TOP_K].astype(jnp.int32)
        rank = meta2[:, 2 * TOP_K:3 * TOP_K].astype(jnp.int32)
        dest = (pad_start[idx] + rank).reshape(-1)
        n_blocks = -(-n_tok * TOP_K // bm) + n_exp
        blk_start = jnp.arange(n_blocks, dtype=jnp.int32) * bm
        blk_expert = jnp.minimum(jnp.sum(blk_start[:, None] >= pad_end[None, :], axis=1), n_exp - 1).astype(jnp.int32)
        n_used = (pad_end[-1:] // bm).astype(jnp.int32)

        n_slots = n_blocks * bm
        tail = jnp.stack([pad_end[-1], (n_slots - pad_end[-1]) // (bm // 2)]).astype(jnp.int32)
        xs = _dispatch(h2.reshape(n_tok, d), dest, pad_start + counts, padded - counts, tail, n_slots,
                       min(DISPATCH_TILE, t), bm)
        y = _experts(xs, blk_expert, n_used, w_gate_up[l], b_gate_up[l], w_down[l], b_down[l], bm)
        out = _combine(x1.reshape(n_tok, d), meta2, g2, final_norm_w.reshape(1, d), dest, y, t,
                       min(COMBINE_TILE, t))
        return out.reshape(b, t, d)
```

```python
import functools

import jax
import jax.numpy as jnp
from jax import lax
from jax.experimental import pallas as pl
from jax.experimental.pallas import tpu as pltpu

GRID_W = 64
HG_HEADS = 4
HG_CHUNK = 32
RG_HEADS = 8
RG_CONV = 4
RG_C = 8.0
TOP_K = 4
SWIGLU_LIMIT = 7.0
SWIGLU_ALPHA = 1.702
EPS = 1e-6

LANES = 128
SUBLANES = 8
MXU_TILE = 256
VMEM_LIMIT = 56 * 1024 * 1024

ADALN_COLUMN_TILES = 4
TOKEN_TILE = 512
ROUTER_TILE = 512
EXPERT_BLOCK = 512
DISPATCH_TILE = 512
COMBINE_TILE = 512

F32 = jnp.float32
BF16 = jnp.bfloat16
HIGHEST = lax.Precision.HIGHEST


def _params(*sem):
    return pltpu.CompilerParams(dimension_semantics=sem, vmem_limit_bytes=VMEM_LIMIT)


def _sigmoid(x):
    return 0.5 * jnp.tanh(0.5 * x) + 0.5


def _silu(x):
    return x * _sigmoid(x)


def _rms(x, w):
    return x * lax.rsqrt(jnp.mean(x * x, axis=-1, keepdims=True) + EPS) * w


def _dot(a, b):
    return jnp.dot(a, b, preferred_element_type=F32)


def _dot_nt(a, b):
    return lax.dot_general(a, b, (((1,), (1,)), ((), ())), preferred_element_type=F32)


def _mod_kernel(c_ref, w_ref, b_ref, o_ref):
    o_ref[...] = jnp.dot(_silu(c_ref[...]), w_ref[...], preferred_element_type=F32,
                         precision=HIGHEST) + b_ref[...]


def _mod(c_all, w_ada, b_ada):
    r, d = c_all.shape
    n = w_ada.shape[1]
    assert n % (ADALN_COLUMN_TILES * LANES) == 0
    tn = n // ADALN_COLUMN_TILES
    return pl.pallas_call(
        _mod_kernel, grid=(n // tn,),
        in_specs=[pl.BlockSpec((r, d), lambda j: (0, 0)),
                  pl.BlockSpec((d, tn), lambda j: (0, j)),
                  pl.BlockSpec((1, tn), lambda j: (0, j))],
        out_specs=pl.BlockSpec((r, tn), lambda j: (0, j)),
        out_shape=jax.ShapeDtypeStruct((r, n), F32),
        compiler_params=_params("arbitrary"), name="adaln_mod",
    )(c_all, w_ada, b_ada.reshape(1, n))


def _inproj_kernel(x_ref, sh_ref, sc_ref, nw_ref, w_ref, o_ref):
    h = _rms(x_ref[0], nw_ref[...]) * (1.0 + sc_ref[0]) + sh_ref[0]
    p = _dot(h.astype(BF16), w_ref[...])
    for j in range(o_ref.shape[1]):
        o_ref[0, j] = p[:, j * LANES:(j + 1) * LANES]


def _inproj(x, shift, scale, norm_w, w_bf16, tm):
    b, t, d = x.shape
    n = w_bf16.shape[1]
    assert n % LANES == 0
    per_batch = shift.shape[0] == b
    mod_map = (lambda i, j: (i, 0, 0)) if per_batch else (lambda i, j: (0, 0, 0))
    return pl.pallas_call(
        _inproj_kernel, grid=(b, t // tm),
        in_specs=[pl.BlockSpec((1, tm, d), lambda i, j: (i, j, 0)),
                  pl.BlockSpec((1, 1, d), mod_map),
                  pl.BlockSpec((1, 1, d), mod_map),
                  pl.BlockSpec((1, d), lambda i, j: (0, 0)),
                  pl.BlockSpec((d, n), lambda i, j: (0, 0))],
        out_specs=pl.BlockSpec((1, n // LANES, tm, LANES), lambda i, j: (i, 0, j, 0)),
        out_shape=jax.ShapeDtypeStruct((b, n // LANES, t, LANES), F32),
        compiler_params=_params("arbitrary", "arbitrary"), name="norm_inproj",
    )(x, shift, scale, norm_w, w_bf16)


def _split_bf16(x):
    hi = x.astype(BF16)
    return hi, (x - hi.astype(F32)).astype(BF16)


def _hg_kernel(q_ref, v_ref, zf_ref, zb_ref, g_ref, cv_ref, czf_ref, czb_ref, lb_ref, nw_ref, o_ref,
               o_s, qd_s, u_s, dec_s, st_s, cu_s, cdec_s, tot_s, *, t_lat, t_ctx):
    c = HG_CHUNK
    dk = q_ref.shape[-1]
    lb = lb_ref[...]
    lbf, lbb = lb[0:1], lb[1:2]

    def prep(rb, zf, zb, q, v, u_out, dec_out, blk, r0):
        nc = rb // c
        ri = lax.broadcasted_iota(jnp.int32, (rb, rb), 0)
        ci = lax.broadcasted_iota(jnp.int32, (rb, rb), 1)
        same = (ri // c) == (ci // c)
        low = same & (ci <= ri)
        upp = same & (ci >= ri)
        ff = lbf + (1.0 - lbf) * _sigmoid(zf)
        fb = lbb + (1.0 - lbb) * _sigmoid(zb)
        lgf, lgb = jnp.log(ff), jnp.log(fb)
        rhs = jnp.concatenate([*_split_bf16(lgf), *_split_bf16(lgb)], axis=1)
        pre = _dot(low.astype(BF16), rhs)
        bcf = pre[:, :dk] + pre[:, dk:2 * dk]
        pfb = pre[:, 2 * dk:3 * dk] + pre[:, 3 * dk:]
        half = tot_s.shape[0] // 2
        tot_s[0:rb, :] = bcf
        tot_s[half:half + rb, :] = pfb
        totf = tot_s[pl.ds(c - 1, nc, stride=c), :]
        totb = tot_s[pl.ds(half + c - 1, nc, stride=c), :]

        def spread(tot):
            return jnp.broadcast_to(tot[:, None, :], (nc, c, dk)).reshape(rb, dk)

        remf = spread(totf) - bcf
        remb = pfb - lgb
        bcb = spread(totb) - remb
        kkf, kkb = 1.0 - ff, 1.0 - fb
        kef = (kkf * jnp.exp(remf)).astype(BF16)
        keb = (kkb * jnp.exp(remb)).astype(BF16)
        chunk_of_row = lax.broadcasted_iota(jnp.int32, (rb, dk), 0) // c
        zero = jnp.zeros((rb, dk), BF16)
        keys = jnp.concatenate([jnp.where(chunk_of_row == j, ke, zero) for ke in (kef, keb) for j in range(nc)],
                               axis=1)
        u_all = _dot(v.T.astype(BF16), keys)
        c0 = blk * nc
        for d in range(2):
            for j in range(nc):
                u_out[d, c0 + j] = u_all[:, (d * nc + j) * dk:(d * nc + j + 1) * dk]
        dec_out[0, pl.ds(pl.multiple_of(c0, nc), nc), :] = jnp.exp(totf)
        dec_out[1, pl.ds(pl.multiple_of(c0, nc), nc), :] = jnp.exp(totb)
        if q is None:
            return
        sq = _silu(q)
        qdf = (sq * jnp.exp(bcf)).astype(BF16)
        qdb = (sq * jnp.exp(bcb)).astype(BF16)
        kdf = (kkf * jnp.exp(-bcf)).astype(BF16)
        kdb = (kkb * jnp.exp(-bcb)).astype(BF16)
        p = jnp.where(low, _dot_nt(qdf, kdf), 0.0) + jnp.where(upp, _dot_nt(qdb, kdb), 0.0)
        o_s[pl.ds(r0, rb), :] = _dot(p.astype(BF16), v.astype(BF16))
        qd_s[pl.ds(r0, rb), 0:dk] = qdf
        qd_s[pl.ds(r0, rb), dk:2 * dk] = qdb

    def scan(n, ur, decr, keep, carry):
        def body(i, carry):
            sf, sb = carry
            j = n - 1 - i
            if keep:
                st_s[i, :, 0:dk] = sf.astype(BF16)
                st_s[j, :, dk:2 * dk] = sb.astype(BF16)
            return sf * decr[0, pl.ds(i, 1), :] + ur[0, i], sb * decr[1, pl.ds(j, 1), :] + ur[1, j]

        return lax.fori_loop(0, n, body, carry, unroll=2)

    rbc = min(MXU_TILE, t_ctx)
    for blk in range(t_ctx // rbc):
        r0 = blk * rbc
        prep(rbc, czf_ref[0, r0:r0 + rbc, :], czb_ref[0, r0:r0 + rbc, :], None, cv_ref[0, r0:r0 + rbc, :],
             cu_s, cdec_s, blk, r0)
    zero = jnp.zeros((dk, dk), F32)
    carry = scan(t_ctx // c, cu_s, cdec_s, False, (zero, zero))

    rbl = min(MXU_TILE, t_lat)

    def lat_prep(blk, _):
        r0 = pl.multiple_of(blk * rbl, rbl)
        rows = pl.ds(r0, rbl)
        prep(rbl, zf_ref[0, rows, :], zb_ref[0, rows, :], q_ref[0, rows, :], v_ref[0, rows, :], u_s, dec_s, blk, r0)
        return 0

    lax.fori_loop(0, t_lat // rbl, lat_prep, 0, unroll=4)
    scan(t_lat // c, u_s, dec_s, True, carry)

    def finish(blk, _):
        r0 = pl.multiple_of(blk * rbl, rbl)
        inter = [_dot_nt(qd_s[pl.ds(r0 + j * c, c), :], st_s[blk * (rbl // c) + j]) for j in range(rbl // c)]
        o = o_s[pl.ds(r0, rbl), :] + jnp.concatenate(inter, axis=0)
        o_ref[0, pl.ds(r0, rbl), :] = _rms(o, nw_ref[...]) * _silu(g_ref[0, pl.ds(r0, rbl), :])
        return 0

    lax.fori_loop(0, t_lat // rbl, finish, 0, unroll=4)


def _hgrn2(p_lat, p_ctx, lb, norm_w):
    b, _, t, _ = p_lat.shape
    tc = p_ctx.shape[2]
    hw = lb.shape[1]
    dk = hw // HG_HEADS
    nh = HG_HEADS
    c = HG_CHUNK
    assert dk == LANES

    def col(k, tt):
        return pl.BlockSpec((None, 1, tt, dk), lambda i, h, k=k: (i, k * nh + h, 0, 0))

    return pl.pallas_call(
        functools.partial(_hg_kernel, t_lat=t, t_ctx=tc), grid=(b, nh),
        in_specs=[col(0, t), col(1, t), col(2, t), col(3, t), col(4, t),
                  col(1, tc), col(2, tc), col(3, tc),
                  pl.BlockSpec((2, dk), lambda i, h: (0, h)),
                  pl.BlockSpec((1, dk), lambda i, h: (0, h))],
        out_specs=pl.BlockSpec((1, t, dk), lambda i, h: (i, 0, h)),
        out_shape=jax.ShapeDtypeStruct((b, t, hw), F32),
        scratch_shapes=[pltpu.VMEM((t, dk), F32),
                        pltpu.VMEM((t, 2 * dk), BF16),
                        pltpu.VMEM((2, t // c, dk, dk), F32),
                        pltpu.VMEM((2, t // c, dk), F32),
                        pltpu.VMEM((t // c, dk, 2 * dk), BF16),
                        pltpu.VMEM((2, tc // c, dk, dk), F32),
                        pltpu.VMEM((2, tc // c, dk), F32),
                        pltpu.VMEM((2 * min(MXU_TILE, max(t, tc)), dk), F32)],
        compiler_params=_params("arbitrary", "arbitrary"), name="hgrn2",
    )(p_lat, p_lat, p_lat, p_lat, p_lat, p_ctx, p_ctx, p_ctx, lb, norm_w)


def _shift_rows(x, k):
    n = x.shape[0]
    y = pltpu.roll(x, k % n, 0)
    r = lax.broadcasted_iota(jnp.int32, x.shape, 0)
    return jnp.where((r >= k) & (r < n + k), y, 0.0)


def _rg_kernel(rx_ref, rgate_ref, crx_ref, cw_ref, cb_ref, wg_ref, bg_ref, lam_ref, o_ref,
               xc_s, af_s, bf_s, ab_s, bb_s, hf_s, hb_s, caf_s, cbf_s, cab_s, cbb_s, *, t_lat, t_ctx):
    w = GRID_W
    rows = t_lat // w
    ch = rx_ref.shape[0] * rx_ref.shape[-1]
    half = ch // 2

    def wide(ref, rws):
        return jnp.concatenate([ref[j, rws, :] for j in range(ref.shape[0])], axis=1)

    cw = cw_ref[...]
    cb = cb_ref[...]
    bg = bg_ref[...]
    nl = -lam_ref[...]
    hdec = (-0.5 * RG_C) * (jnp.maximum(nl, 0.0) + jnp.log1p(jnp.exp(-jnp.abs(nl))))

    def conv(xm2, xm1, x0, xp1):
        return cb + cw[0:1] * xm2 + cw[1:2] * xm1 + cw[2:3] * x0 + cw[3:4] * xp1

    def gates(xc):
        xb = xc.astype(BF16)
        g0 = _dot(xb[:, :half], wg_ref[0])
        g1 = _dot(xb[:, half:], wg_ref[1])
        outs = []
        for d in range(2):
            pre = []
            for s in (2 * d, 2 * d + 1):
                pre.append(jnp.concatenate([g0[:, s * half:(s + 1) * half], g1[:, s * half:(s + 1) * half]],
                                           axis=1) + bg[:, s * ch:(s + 1) * ch])
            log_a = hdec[d:d + 1] * jnp.tanh(0.5 * pre[0]) + hdec[d:d + 1]
            a = jnp.exp(log_a)
            one_m_a2 = -jnp.tanh(log_a) * (a * a + 1.0)
            half_mult = 0.5 * jnp.where(one_m_a2 > 0.0, one_m_a2 * lax.rsqrt(one_m_a2), 0.0)
            m = half_mult * xc
            outs += [a, m * jnp.tanh(0.5 * pre[1]) + m]
        return outs

    xctx = wide(crx_ref, slice(None))
    xcc = conv(_shift_rows(xctx, 2), _shift_rows(xctx, 1), xctx, _shift_rows(xctx, -1))
    caf_s[...], cbf_s[...], cab_s[...], cbb_s[...] = gates(xcc)

    def cstep(i, carry):
        hf, hb = carry
        hf = caf_s[pl.ds(i, 1), :] * hf + cbf_s[pl.ds(i, 1), :]
        j = t_ctx - 1 - i
        hb = cab_s[pl.ds(j, 1), :] * hb + cbb_s[pl.ds(j, 1), :]
        return hf, hb

    zrow = jnp.zeros((1, ch), F32)
    hf0, hb0 = lax.fori_loop(0, t_ctx, cstep, (zrow, zrow), unroll=8)

    def slab(rr):
        if 0 <= rr < rows:
            return wide(rx_ref, slice(rr * w, (rr + 1) * w))
        if rr < 0:
            return _shift_rows(wide(rx_ref, slice((rr + rows) * w, (rr + rows + 1) * w)), 1)
        return _shift_rows(wide(rx_ref, slice((rr - rows) * w, (rr - rows + 1) * w)), -1)

    for r in range(rows):
        xc_s[r * w:(r + 1) * w, :] = conv(slab(r - 2), slab(r - 1), slab(r), slab(r + 1))

    mb = min(MXU_TILE, t_lat)

    def gbody(i, _):
        r0 = pl.multiple_of(i * mb, mb)
        a_f, b_f, a_b, b_b = gates(xc_s[pl.ds(r0, mb), :])
        af_s[pl.ds(r0, mb), :] = a_f
        bf_s[pl.ds(r0, mb), :] = b_f
        ab_s[pl.ds(r0, mb), :] = a_b
        bb_s[pl.ds(r0, mb), :] = b_b
        return 0

    lax.fori_loop(0, t_lat // mb, gbody, 0, unroll=2)

    def l1(i, _):
        pf = pl.multiple_of(i * w, w)
        qf = pl.multiple_of((i - 1) * w, w)
        a = af_s[pl.ds(pf, w), :]
        af_s[pl.ds(pf, w), :] = a * af_s[pl.ds(qf, w), :]
        bf_s[pl.ds(pf, w), :] = a * bf_s[pl.ds(qf, w), :] + bf_s[pl.ds(pf, w), :]
        pb = pl.multiple_of((rows - 1 - i) * w, w)
        qb = pl.multiple_of((rows - i) * w, w)
        a = ab_s[pl.ds(pb, w), :]
        ab_s[pl.ds(pb, w), :] = a * ab_s[pl.ds(qb, w), :]
        bb_s[pl.ds(pb, w), :] = a * bb_s[pl.ds(qb, w), :] + bb_s[pl.ds(pb, w), :]
        return 0

    lax.fori_loop(1, rows, l1, 0)

    last = (rows - 1) * w

    def l2(i, carry):
        hf, hb = carry
        hf_s[pl.ds(i, 1), :] = hf
        hf = af_s[pl.ds(last + i, 1), :] * hf + bf_s[pl.ds(last + i, 1), :]
        j = w - 1 - i
        hb_s[pl.ds(j, 1), :] = hb
        hb = ab_s[pl.ds(j, 1), :] * hb + bb_s[pl.ds(j, 1), :]
        return hf, hb

    lax.fori_loop(0, w, l2, (hf0, hb0), unroll=8)

    def l3(i, _):
        p = pl.multiple_of(i * w, w)
        h = (af_s[pl.ds(p, w), :] * hf_s[...] + bf_s[pl.ds(p, w), :]
             + ab_s[pl.ds(p, w), :] * hb_s[...] + bb_s[pl.ds(p, w), :])
        o_ref[0, pl.ds(p, w), :] = jax.nn.gelu(wide(rgate_ref, pl.ds(p, w))) * h
        return 0

    lax.fori_loop(0, rows, l3, 0, unroll=2)


def _rglru(p_lat, p_ctx, conv_w, conv_b, wg, bg, lam):
    b, n_slabs, t, _ = p_lat.shape
    tc = p_ctx.shape[2]
    ch = conv_w.shape[1]
    cs = ch // LANES
    rx_blk = n_slabs // cs - 2
    full = lambda shape: pl.BlockSpec(shape, lambda i: (0,) * len(shape))
    big = lambda: pltpu.VMEM((t, ch), F32)
    small = lambda: pltpu.VMEM((tc, ch), F32)
    return pl.pallas_call(
        functools.partial(_rg_kernel, t_lat=t, t_ctx=tc), grid=(b,),
        in_specs=[pl.BlockSpec((None, cs, t, LANES), lambda i: (i, rx_blk, 0, 0)),
                  pl.BlockSpec((None, cs, t, LANES), lambda i: (i, rx_blk + 1, 0, 0)),
                  pl.BlockSpec((None, cs, tc, LANES), lambda i: (i, rx_blk, 0, 0)),
                  full(conv_w.shape), full(conv_b.shape), full(wg.shape), full(bg.shape), full(lam.shape)],
        out_specs=pl.BlockSpec((1, t, ch), lambda i: (i, 0, 0)),
        out_shape=jax.ShapeDtypeStruct((b, t, ch), F32),
        scratch_shapes=[big(), big(), big(), big(), big(),
                        pltpu.VMEM((GRID_W, ch), F32), pltpu.VMEM((GRID_W, ch), F32),
                        small(), small(), small(), small()],
        compiler_params=_params("arbitrary"), name="rglru",
    )(p_lat, p_lat, p_ctx, conv_w, conv_b, wg, bg, lam)


def _mix_kernel(hg_ref, rg_ref, x_ref, g1_ref, sh_ref, sc_ref, nw_ref, wo_ref, rw_ref, rb_ref,
                x1_ref, h2_ref, meta_ref, cnt_ref, base_s, *, n_exp):
    tm = x_ref.shape[1]

    @pl.when((pl.program_id(0) == 0) & (pl.program_id(1) == 0))
    def _():
        base_s[...] = jnp.zeros_like(base_s)

    hcat = jnp.concatenate([hg_ref[0], rg_ref[0]], axis=1).astype(BF16)
    x1 = x_ref[0] + g1_ref[0] * _dot(hcat, wo_ref[...])
    x1_ref[0] = x1
    h2 = _rms(x1, nw_ref[...]) * (1.0 + sc_ref[0]) + sh_ref[0]
    h2_ref[0] = h2
    logits = _dot(h2.astype(BF16), rw_ref[...]) + rb_ref[...]

    lane_e = lax.broadcasted_iota(jnp.int32, (tm, n_exp), 1).astype(F32)
    vals, idxs = [], []
    cur = logits
    for _ in range(TOP_K):
        m = jnp.max(cur, axis=1, keepdims=True)
        ix = jnp.min(jnp.where(cur == m, lane_e, float(n_exp)), axis=1, keepdims=True)
        vals.append(m)
        idxs.append(ix)
        cur = jnp.where(lane_e == ix, -jnp.inf, cur)
    ex = [jnp.exp(v - vals[0]) for v in vals]
    den = ex[0] + ex[1] + ex[2] + ex[3]

    lane = lax.broadcasted_iota(jnp.int32, (tm, LANES), 1)
    lane_f = lane.astype(F32)
    onehot = jnp.zeros((tm, LANES), F32)
    for k in range(TOP_K):
        onehot = jnp.where(lane_f == idxs[k] + float(k * n_exp), 1.0, onehot)
    ri = lax.broadcasted_iota(jnp.int32, (tm, tm), 0)
    ci = lax.broadcasted_iota(jnp.int32, (tm, tm), 1)
    prefix = _dot((ci < ri).astype(BF16), onehot.astype(BF16))
    tot = jnp.broadcast_to(prefix[tm - 1:tm] + onehot[tm - 1:tm], (8, LANES))
    lane8 = lax.broadcasted_iota(jnp.int32, (8, LANES), 1)
    off = base_s[...]
    tot_all = tot
    for j in range(1, TOP_K):
        rolled = pltpu.roll(tot, j * n_exp, 1)
        off = off + jnp.where(lane8 >= j * n_exp, rolled, 0.0)
        tot_all = tot_all + rolled
    pos = onehot * (prefix + off[0:1])
    meta = jnp.zeros((tm, LANES), F32)
    for k in range(TOP_K):
        in_k = (lane >= k * n_exp) & (lane < (k + 1) * n_exp)
        rank = jnp.sum(jnp.where(in_k, pos, 0.0), axis=1, keepdims=True)
        meta = jnp.where(lane == k, idxs[k], meta)
        meta = jnp.where(lane == TOP_K + k, ex[k] / den, meta)
        meta = jnp.where(lane == 2 * TOP_K + k, rank, meta)
    meta_ref[0] = meta
    base_s[...] = base_s[...] + tot_all
    cnt_ref[...] = base_s[...]


def _mix(hg, rg, x, g1, sh2, sc2, norm_w, wo_bf16, router_w, router_b, tm):
    b, t, d = x.shape
    hw = hg.shape[2]
    n_exp = router_w.shape[1]
    assert TOP_K * n_exp == LANES
    tok = lambda last: pl.BlockSpec((1, tm, last), lambda i, j: (i, j, 0))
    per_b = pl.BlockSpec((1, 1, d), lambda i, j: (i, 0, 0))
    full = lambda shape: pl.BlockSpec(shape, lambda i, j: (0,) * len(shape))
    return pl.pallas_call(
        functools.partial(_mix_kernel, n_exp=n_exp), grid=(b, t // tm),
        in_specs=[tok(hw), tok(hw), tok(d), per_b, per_b, per_b, full((1, d)),
                  full(wo_bf16.shape), full(router_w.shape), full((1, n_exp))],
        out_specs=[tok(d), tok(d), tok(LANES), pl.BlockSpec((8, LANES), lambda i, j: (0, 0))],
        out_shape=[jax.ShapeDtypeStruct((b, t, d), F32), jax.ShapeDtypeStruct((b, t, d), F32),
                   jax.ShapeDtypeStruct((b, t, LANES), F32), jax.ShapeDtypeStruct((8, LANES), F32)],
        scratch_shapes=[pltpu.VMEM((8, LANES), F32)],
        compiler_params=_params("arbitrary", "arbitrary"), name="outproj_router",
    )(hg, rg, x, g1, sh2, sc2, norm_w, wo_bf16, router_w, router_b)


def _dispatch_kernel(fill_off, fill_n, tail, h_ref, dest_hbm, xs_out, idx0_s, idx1_s, zero_s, sem_i, sem_d, sem_z, *,
                     pad_bits):
    td = h_ref.shape[0] * SUBLANES
    n_idx = td * TOP_K
    n_exp = fill_n.shape[0]
    zrows = zero_s.shape[0]
    i = pl.program_id(0)
    slot = i % 2
    idx_s = (idx0_s, idx1_s)

    def idx_copy(j, sl):
        return pltpu.make_async_copy(dest_hbm.at[pl.ds(j * n_idx, n_idx)], idx_s[sl], sem_i.at[sl])

    @pl.when(i == 0)
    def _():
        idx_copy(0, 0).start()

    for sl in range(2):
        @pl.when((i + 1 < pl.num_programs(0)) & (slot != sl))
        def _():
            idx_copy(i + 1, sl).start()

    def fill(wait):
        def go(copy, cond):
            @pl.when(cond)
            def _():
                copy.wait() if wait else copy.start()

        def per_expert(e, _):
            off = fill_off[e]
            npad = fill_n[e]
            n_single = npad & (SUBLANES - 1)
            for r in range(SUBLANES - 1):
                go(pltpu.make_async_copy(zero_s.at[pl.ds(0, 1), :], xs_out.at[pl.ds(off + r, 1), :], sem_z),
                   r < n_single)
            off = pl.multiple_of(off + n_single, SUBLANES)
            for bit in reversed(range(SUBLANES.bit_length() - 1, pad_bits)):
                size = 1 << bit
                go(pltpu.make_async_copy(zero_s.at[pl.ds(0, size), :], xs_out.at[pl.ds(off, size), :], sem_z),
                   (npad & size) != 0)
                off = pl.multiple_of(off + (npad & size), SUBLANES)
            return 0

        lax.fori_loop(0, n_exp, per_expert, 0)

        def per_tail_chunk(j, _):
            off = pl.multiple_of(tail[0] + j * zrows, zrows)
            copy = pltpu.make_async_copy(zero_s, xs_out.at[pl.ds(off, zrows), :], sem_z)
            copy.wait() if wait else copy.start()
            return 0

        lax.fori_loop(0, tail[1], per_tail_chunk, 0)

    @pl.when(i == 0)
    def _():
        zero_s[...] = jnp.zeros_like(zero_s)
        fill(False)

    for sl in range(2):
        @pl.when(slot == sl)
        def _():
            idx_copy(i, sl).wait()

            for g in range(td // SUBLANES):
                for u in range(SUBLANES):
                    for k in range(TOP_K):
                        dst = idx_s[sl][(g * SUBLANES + u) * TOP_K + k]
                        pltpu.make_async_copy(h_ref.at[g, pl.ds(u, 1), :], xs_out.at[pl.ds(dst, 1), :],
                                              sem_d).start(priority=k % 2)

    for _ in range(TOP_K * td // zrows):
        pltpu.make_async_copy(zero_s, xs_out.at[pl.ds(0, zrows), :], sem_d).wait()

    @pl.when(i == 0)
    def _():
        fill(True)


def _dispatch(h2, dest_flat, fill_off, fill_n, tail, n_slots, td, bm):
    n, d = h2.shape
    pad_bits = (bm - 1).bit_length()
    assert (TOP_K * td) % (bm // 2) == 0
    grid_spec = pltpu.PrefetchScalarGridSpec(
        num_scalar_prefetch=3, grid=(n // td,),
        in_specs=[pl.BlockSpec((td // SUBLANES, SUBLANES, d), lambda i, fo, fn, tl: (i, 0, 0)),
                  pl.BlockSpec(memory_space=pl.ANY)],
        out_specs=pl.BlockSpec(memory_space=pl.ANY),
        scratch_shapes=[pltpu.SMEM((td * TOP_K,), jnp.int32), pltpu.SMEM((td * TOP_K,), jnp.int32),
                        pltpu.VMEM((bm // 2, d), F32),
                        pltpu.SemaphoreType.DMA((2,)), pltpu.SemaphoreType.DMA, pltpu.SemaphoreType.DMA])
    return pl.pallas_call(
        functools.partial(_dispatch_kernel, pad_bits=pad_bits), grid_spec=grid_spec,
        out_shape=jax.ShapeDtypeStruct((n_slots, d), F32),
        compiler_params=_params("arbitrary"), name="moe_dispatch",
    )(fill_off, fill_n, tail, h2.reshape(n // SUBLANES, SUBLANES, d), dest_flat)


def _expert_kernel(be_ref, nu_ref, x_ref, wgu_ref, bgu_ref, wd_ref, bd_ref, y_ref, wgu_s, wd_s):
    i = pl.program_id(0)
    d_ff = wd_ref.shape[1]

    @pl.when(i >= nu_ref[0])
    def _():
        y_ref[...] = jnp.zeros_like(y_ref)

    @pl.when(i < nu_ref[0])
    def _():
        @pl.when((i == 0) | (be_ref[i] != be_ref[jnp.maximum(i - 1, 0)]))
        def _():
            wgu_s[...] = wgu_ref[0].astype(BF16)
            wd_s[...] = wd_ref[0].astype(BF16)

        gu = _dot(x_ref[...].astype(BF16), wgu_s[...]) + bgu_ref[0]
        gate = jnp.minimum(gu[:, :d_ff], SWIGLU_LIMIT)
        up = jnp.clip(gu[:, d_ff:], -SWIGLU_LIMIT, SWIGLU_LIMIT)
        act = gate * _sigmoid(SWIGLU_ALPHA * gate) * (up + 1.0)
        y_ref[...] = _dot(act.astype(BF16), wd_s[...]) + bd_ref[0]


def _experts(xs, blk_expert, n_used, wgu, bgu, wd, bd, bm):
    n_slots, d = xs.shape
    n_exp, _, f2 = wgu.shape
    d_ff = wd.shape[1]
    n_blocks = n_slots // bm
    row = lambda i, be, nu: (jnp.minimum(i, nu[0] - 1), 0)
    grid_spec = pltpu.PrefetchScalarGridSpec(
        num_scalar_prefetch=2, grid=(n_blocks,),
        in_specs=[pl.BlockSpec((bm, d), row),
                  pl.BlockSpec((1, d, f2), lambda i, be, nu: (be[i], 0, 0)),
                  pl.BlockSpec((1, 1, f2), lambda i, be, nu: (be[i], 0, 0)),
                  pl.BlockSpec((1, d_ff, d), lambda i, be, nu: (be[i], 0, 0)),
                  pl.BlockSpec((1, 1, d), lambda i, be, nu: (be[i], 0, 0))],
        out_specs=pl.BlockSpec((bm, d), lambda i, be, nu: (i, 0)),
        scratch_shapes=[pltpu.VMEM((d, f2), BF16), pltpu.VMEM((d_ff, d), BF16)])
    return pl.pallas_call(
        _expert_kernel, grid_spec=grid_spec,
        out_shape=jax.ShapeDtypeStruct((n_slots, d), F32),
        compiler_params=_params("arbitrary"), name="moe_experts",
    )(blk_expert, n_used, xs, wgu, bgu.reshape(n_exp, 1, f2), wd, bd.reshape(n_exp, 1, d))


def _combine_kernel(x1_ref, meta_ref, g2_ref, fw_ref, dest_hbm, y_hbm, o_ref, rows0_s, rows1_s, idx0_s, idx1_s,
                    sem_i, sem_d):
    tc, d = x1_ref.shape
    n_idx = tc * TOP_K
    i = pl.program_id(0)
    n = pl.num_programs(0)
    slot = i % 2
    idx_s = (idx0_s, idx1_s)
    rows_s = (rows0_s, rows1_s)

    def idx_copy(j, sl):
        return pltpu.make_async_copy(dest_hbm.at[pl.ds(j * n_idx, n_idx)], idx_s[sl], sem_i.at[sl])

    def row_copy(src, sl, k, g, u):
        return pltpu.make_async_copy(y_hbm.at[pl.ds(src, 1), :], rows_s[sl].at[k, g, pl.ds(u, 1), :], sem_d.at[sl])

    def wait_rows(sl):
        for k in range(TOP_K):
            pltpu.make_async_copy(rows_s[1 - sl].at[k], rows_s[sl].at[k], sem_d.at[sl]).wait()

    def finish(sl):
        meta = meta_ref[...]
        moe = meta[:, TOP_K:TOP_K + 1] * rows_s[sl][0].reshape(tc, d)
        for k in range(1, TOP_K):
            moe = moe + meta[:, TOP_K + k:TOP_K + k + 1] * rows_s[sl][k].reshape(tc, d)
        o_ref[...] = _rms(x1_ref[...] + g2_ref[0] * moe, fw_ref[...])

    @pl.when(i == 0)
    def _():
        idx_copy(0, 0).start()

        @pl.when(n > 1)
        def _():
            idx_copy(1, 1).start()

        idx_copy(0, 0).wait()

        def first(g, _):
            for u in range(SUBLANES):
                for k in range(TOP_K):
                    row_copy(idx0_s[g * (SUBLANES * TOP_K) + u * TOP_K + k], 0, k, g, u).start(priority=k % 2)
            return 0

        lax.fori_loop(0, tc // SUBLANES, first, 0)

    for sl in range(2):
        @pl.when((slot == sl) & (i + 1 < n))
        def _():
            wait_rows(sl)
            idx_copy(i + 1, 1 - sl).wait()
            for g in range(tc // SUBLANES):
                for u in range(SUBLANES):
                    for k in range(TOP_K):
                        row_copy(idx_s[1 - sl][(g * SUBLANES + u) * TOP_K + k], 1 - sl, k, g, u).start(priority=k % 2)
            finish(sl)

        @pl.when((slot == sl) & (i + 1 == n))
        def _():
            wait_rows(sl)
            finish(sl)

        @pl.when((slot == sl) & (i + 2 < n))
        def _():
            idx_copy(i + 2, sl).start()


def _combine(x1, meta, g2, final_w, dest_flat, y, t_seq, tc):
    n, d = x1.shape
    return pl.pallas_call(
        _combine_kernel, grid=(n // tc,),
        in_specs=[pl.BlockSpec((tc, d), lambda i: (i, 0)),
                  pl.BlockSpec((tc, LANES), lambda i: (i, 0)),
                  pl.BlockSpec((1, 1, d), lambda i: (i * tc // t_seq, 0, 0)),
                  pl.BlockSpec((1, d), lambda i: (0, 0)),
                  pl.BlockSpec(memory_space=pl.ANY),
                  pl.BlockSpec(memory_space=pl.ANY)],
        out_specs=pl.BlockSpec((tc, d), lambda i: (i, 0)),
        out_shape=jax.ShapeDtypeStruct((n, d), F32),
        scratch_shapes=[pltpu.VMEM((TOP_K, tc // SUBLANES, SUBLANES, d), F32),
                        pltpu.VMEM((TOP_K, tc // SUBLANES, SUBLANES, d), F32),
                        pltpu.SMEM((tc * TOP_K,), jnp.int32), pltpu.SMEM((tc * TOP_K,), jnp.int32),
                        pltpu.SemaphoreType.DMA((2,)), pltpu.SemaphoreType.DMA((2,))],
        compiler_params=_params("arbitrary"), name="moe_combine",
    )(x1, meta, g2, final_w, dest_flat, y)


def _gate_weights(wa, wx):
    _, heads, hd, _ = wa.shape
    hh = heads // 2
    eye = jnp.eye(hh, dtype=wa.dtype)

    def blockdiag(wsel):
        return jnp.einsum('hij,hg->higj', wsel, eye).reshape(hh * hd, hh * hd)

    halves = []
    for s in range(2):
        sl = slice(s * hh, (s + 1) * hh)
        halves.append(jnp.concatenate([blockdiag(wa[0, sl]), blockdiag(wx[0, sl]),
                                       blockdiag(wa[1, sl]), blockdiag(wx[1, sl])], axis=1))
    return jnp.stack(halves).astype(BF16)


def kernel(x, c, ctx, c_ctx, norm1_w, norm2_w, w_ada, b_ada, w_in, hg_lb_logits, hg_norm_w, rg_conv_w, rg_conv_b,
           rg_wa, rg_ba, rg_wx, rg_bx, rg_lambda, w_out, router_w, router_b, w_gate_up, b_gate_up, w_down,
           b_down, final_norm_w):
    b, t, d = x.shape
    tcx = ctx.shape[1]
    n_exp = router_w.shape[-1]
    n_tok = b * t
    depth = w_in.shape[0]
    lb_all = jnp.cumsum(jax.nn.softmax(hg_lb_logits.astype(F32), axis=0), axis=0)

    for l in range(depth):
        assert l == depth - 1, "context stream update of non-final layers is not implemented"
        pad = (-(b + 1)) % 8
        c_all = jnp.concatenate([c, c_ctx[None], jnp.zeros((pad, d), F32)], axis=0)
        mod = _mod(c_all, w_ada[l], b_ada[l])
        sh1, sc1, g1, sh2, sc2, g2 = [m[:b, None, :] for m in jnp.split(mod, 6, axis=-1)]
        csh1, csc1 = [m[b:b + 1, None, :] for m in jnp.split(mod, 6, axis=-1)[:2]]

        w_in_b = w_in[l].astype(BF16)
        nw1 = norm1_w[l].reshape(1, d)
        tm = min(TOKEN_TILE, t)
        p_lat = _inproj(x, sh1, sc1, nw1, w_in_b, tm)
        p_ctx = _inproj(ctx, csh1, csc1, nw1, w_in_b, min(MXU_TILE, tcx))

        hg = _hgrn2(p_lat, p_ctx, lb_all[l], hg_norm_w[l].reshape(1, -1))
        ch = rg_conv_w.shape[-1]
        wg = _gate_weights(rg_wa[l], rg_wx[l])
        bg = jnp.concatenate([rg_ba[l, 0], rg_bx[l, 0], rg_ba[l, 1], rg_bx[l, 1]]).reshape(1, 4 * ch)
        rg = _rglru(p_lat, p_ctx, rg_conv_w[l], rg_conv_b[l].reshape(1, ch), wg, bg, rg_lambda[l])

        x1, h2, meta, cnt = _mix(hg, rg, x, g1, sh2, sc2, norm2_w[l].reshape(1, d), w_out[l].astype(BF16),
                                 router_w[l].astype(BF16), router_b[l].reshape(1, n_exp), min(ROUTER_TILE, t))

        bm = EXPERT_BLOCK
        meta2 = meta.reshape(n_tok, LANES)
        counts = cnt[0, :n_exp].astype(jnp.int32)
        padded = (counts + bm - 1) // bm * bm
        pad_end = jnp.cumsum(padded)
        pad_start = pad_end - padded
        idx = meta2[:, 0:TOP_K].astype(jnp.int32)
        rank = meta2[:, 2 * TOP_K:3 * TOP_K].astype(jnp.int32)
        dest = (pad_start[idx] + rank).reshape(-1)
        n_blocks = -(-n_tok * TOP_K // bm) + n_exp
        blk_start = jnp.arange(n_blocks, dtype=jnp.int32) * bm
        blk_expert = jnp.minimum(jnp.sum(blk_start[:, None] >= pad_end[None, :], axis=1), n_exp - 1).astype(jnp.int32)
        n_used = (pad_end[-1:] // bm).astype(jnp.int32)

        n_slots = n_blocks * bm
        tail = jnp.stack([pad_end[-1], (n_slots - pad_end[-1]) // (bm // 2)]).astype(jnp.int32)
        xs = _dispatch(h2.reshape(n_tok, d), dest, pad_start + counts, padded - counts, tail, n_slots,
                       min(DISPATCH_TILE, t), bm)
        y = _experts(xs, blk_expert, n_used, w_gate_up[l], b_gate_up[l], w_down[l], b_down[l], bm)
        out = _combine(x1.reshape(n_tok, d), meta2, g2, final_norm_w.reshape(1, d), dest, y, t,
                       min(COMBINE_TILE, t))
        return out.reshape(b, t, d)
```

```python
import functools

import jax
import jax.numpy as jnp
from jax import lax
from jax.experimental import pallas as pl
from jax.experimental.pallas import tpu as pltpu

GRID_W = 64
HG_HEADS = 4
HG_CHUNK = 32
RG_HEADS = 8
RG_CONV = 4
RG_C = 8.0
TOP_K = 4
SWIGLU_LIMIT = 7.0
SWIGLU_ALPHA = 1.702
EPS = 1e-6

LANES = 128
SUBLANES = 8
MXU_TILE = 256
VMEM_LIMIT = 56 * 1024 * 1024

ADALN_COLUMN_TILES = 4
TOKEN_TILE = 512
ROUTER_TILE = 512
EXPERT_BLOCK = 512
DISPATCH_TILE = 512
COMBINE_TILE = 512

F32 = jnp.float32
BF16 = jnp.bfloat16
HIGHEST = lax.Precision.HIGHEST


def _params(*sem):
    return pltpu.CompilerParams(dimension_semantics=sem, vmem_limit_bytes=VMEM_LIMIT)


def _sigmoid(x):
    return 0.5 * jnp.tanh(0.5 * x) + 0.5


def _silu(x):
    return x * _sigmoid(x)


def _rms(x, w):
    return x * lax.rsqrt(jnp.mean(x * x, axis=-1, keepdims=True) + EPS) * w


def _dot(a, b):
    return jnp.dot(a, b, preferred_element_type=F32)


def _dot_nt(a, b):
    return lax.dot_general(a, b, (((1,), (1,)), ((), ())), preferred_element_type=F32)


def _mod_kernel(c_ref, w_ref, b_ref, o_ref):
    o_ref[...] = jnp.dot(_silu(c_ref[...]), w_ref[...], preferred_element_type=F32,
                         precision=HIGHEST) + b_ref[...]


def _mod(c_all, w_ada, b_ada):
    r, d = c_all.shape
    n = w_ada.shape[1]
    assert n % (ADALN_COLUMN_TILES * LANES) == 0
    tn = n // ADALN_COLUMN_TILES
    return pl.pallas_call(
        _mod_kernel, grid=(n // tn,),
        in_specs=[pl.BlockSpec((r, d), lambda j: (0, 0)),
                  pl.BlockSpec((d, tn), lambda j: (0, j)),
                  pl.BlockSpec((1, tn), lambda j: (0, j))],
        out_specs=pl.BlockSpec((r, tn), lambda j: (0, j)),
        out_shape=jax.ShapeDtypeStruct((r, n), F32),
        compiler_params=_params("arbitrary"), name="adaln_mod",
    )(c_all, w_ada, b_ada.reshape(1, n))


def _inproj_kernel(x_ref, sh_ref, sc_ref, nw_ref, w_ref, o_ref):
    h = _rms(x_ref[0], nw_ref[...]) * (1.0 + sc_ref[0]) + sh_ref[0]
    p = _dot(h.astype(BF16), w_ref[...])
    for j in range(o_ref.shape[1]):
        o_ref[0, j] = p[:, j * LANES:(j + 1) * LANES]


def _inproj(x, shift, scale, norm_w, w_bf16, tm):
    b, t, d = x.shape
    n = w_bf16.shape[1]
    assert n % LANES == 0
    per_batch = shift.shape[0] == b
    mod_map = (lambda i, j: (i, 0, 0)) if per_batch else (lambda i, j: (0, 0, 0))
    return pl.pallas_call(
        _inproj_kernel, grid=(b, t // tm),
        in_specs=[pl.BlockSpec((1, tm, d), lambda i, j: (i, j, 0)),
                  pl.BlockSpec((1, 1, d), mod_map),
                  pl.BlockSpec((1, 1, d), mod_map),
                  pl.BlockSpec((1, d), lambda i, j: (0, 0)),
                  pl.BlockSpec((d, n), lambda i, j: (0, 0))],
        out_specs=pl.BlockSpec((1, n // LANES, tm, LANES), lambda i, j: (i, 0, j, 0)),
        out_shape=jax.ShapeDtypeStruct((b, n // LANES, t, LANES), F32),
        compiler_params=_params("arbitrary", "arbitrary"), name="norm_inproj",
    )(x, shift, scale, norm_w, w_bf16)


def _split_bf16(x):
    hi = x.astype(BF16)
    return hi, (x - hi.astype(F32)).astype(BF16)


def _hg_kernel(q_ref, v_ref, zf_ref, zb_ref, g_ref, cv_ref, czf_ref, czb_ref, lb_ref, nw_ref, o_ref,
               o_s, qd_s, u_s, dec_s, st_s, cu_s, cdec_s, tot_s, *, t_lat, t_ctx):
    c = HG_CHUNK
    dk = q_ref.shape[-1]
    lb = lb_ref[...]
    lbf, lbb = lb[0:1], lb[1:2]

    def prep(rb, zf, zb, q, v, u_out, dec_out, blk, r0):
        nc = rb // c
        ri = lax.broadcasted_iota(jnp.int32, (rb, rb), 0)
        ci = lax.broadcasted_iota(jnp.int32, (rb, rb), 1)
        same = (ri // c) == (ci // c)
        low = same & (ci <= ri)
        upp = same & (ci >= ri)
        ff = lbf + (1.0 - lbf) * _sigmoid(zf)
        fb = lbb + (1.0 - lbb) * _sigmoid(zb)
        lgf, lgb = jnp.log(ff), jnp.log(fb)
        rhs = jnp.concatenate([*_split_bf16(lgf), *_split_bf16(lgb)], axis=1)
        pre = _dot(low.astype(BF16), rhs)
        bcf = pre[:, :dk] + pre[:, dk:2 * dk]
        pfb = pre[:, 2 * dk:3 * dk] + pre[:, 3 * dk:]
        half = tot_s.shape[0] // 2
        tot_s[0:rb, :] = bcf
        tot_s[half:half + rb, :] = pfb
        totf = tot_s[pl.ds(c - 1, nc, stride=c), :]
        totb = tot_s[pl.ds(half + c - 1, nc, stride=c), :]

        def spread(tot):
            return jnp.broadcast_to(tot[:, None, :], (nc, c, dk)).reshape(rb, dk)

        remf = spread(totf) - bcf
        remb = pfb - lgb
        bcb = spread(totb) - remb
        kkf, kkb = 1.0 - ff, 1.0 - fb
        kef = (kkf * jnp.exp(remf)).astype(BF16)
        keb = (kkb * jnp.exp(remb)).astype(BF16)
        chunk_of_row = lax.broadcasted_iota(jnp.int32, (rb, dk), 0) // c
        zero = jnp.zeros((rb, dk), BF16)
        keys = jnp.concatenate([jnp.where(chunk_of_row == j, ke, zero) for ke in (kef, keb) for j in range(nc)],
                               axis=1)
        u_all = _dot(v.T.astype(BF16), keys)
        c0 = blk * nc
        for d in range(2):
            for j in range(nc):
                u_out[d, c0 + j] = u_all[:, (d * nc + j) * dk:(d * nc + j + 1) * dk]
        dec_out[0, pl.ds(pl.multiple_of(c0, nc), nc), :] = jnp.exp(totf)
        dec_out[1, pl.ds(pl.multiple_of(c0, nc), nc), :] = jnp.exp(totb)
        if q is None:
            return
        sq = _silu(q)
        qdf = (sq * jnp.exp(bcf)).astype(BF16)
        qdb = (sq * jnp.exp(bcb)).astype(BF16)
        kdf = (kkf * jnp.exp(-bcf)).astype(BF16)
        kdb = (kkb * jnp.exp(-bcb)).astype(BF16)
        p = jnp.where(low, _dot_nt(qdf, kdf), 0.0) + jnp.where(upp, _dot_nt(qdb, kdb), 0.0)
        o_s[pl.ds(r0, rb), :] = _dot(p.astype(BF16), v.astype(BF16))
        qd_s[pl.ds(r0, rb), 0:dk] = qdf
        qd_s[pl.ds(r0, rb), dk:2 * dk] = qdb

    def scan(n, ur, decr, keep, carry):
        def body(i, carry):
            sf, sb = carry
            j = n - 1 - i
            if keep:
                st_s[i, :, 0:dk] = sf.astype(BF16)
                st_s[j, :, dk:2 * dk] = sb.astype(BF16)
            return sf * decr[0, pl.ds(i, 1), :] + ur[0, i], sb * decr[1, pl.ds(j, 1), :] + ur[1, j]

        return lax.fori_loop(0, n, body, carry, unroll=2)

    rbc = min(MXU_TILE, t_ctx)
    for blk in range(t_ctx // rbc):
        r0 = blk * rbc
        prep(rbc, czf_ref[0, r0:r0 + rbc, :], czb_ref[0, r0:r0 + rbc, :], None, cv_ref[0, r0:r0 + rbc, :],
             cu_s, cdec_s, blk, r0)
    zero = jnp.zeros((dk, dk), F32)
    carry = scan(t_ctx // c, cu_s, cdec_s, False, (zero, zero))

    rbl = min(MXU_TILE, t_lat)

    def lat_prep(blk, _):
        r0 = pl.multiple_of(blk * rbl, rbl)
        rows = pl.ds(r0, rbl)
        prep(rbl, zf_ref[0, rows, :], zb_ref[0, rows, :], q_ref[0, rows, :], v_ref[0, rows, :], u_s, dec_s, blk, r0)
        return 0

    lax.fori_loop(0, t_lat // rbl, lat_prep, 0, unroll=8)
    scan(t_lat // c, u_s, dec_s, True, carry)

    def finish(blk, _):
        r0 = pl.multiple_of(blk * rbl, rbl)
        inter = [_dot_nt(qd_s[pl.ds(r0 + j * c, c), :], st_s[blk * (rbl // c) + j]) for j in range(rbl // c)]
        o = o_s[pl.ds(r0, rbl), :] + jnp.concatenate(inter, axis=0)
        o_ref[0, pl.ds(r0, rbl), :] = _rms(o, nw_ref[...]) * _silu(g_ref[0, pl.ds(r0, rbl), :])
        return 0

    lax.fori_loop(0, t_lat // rbl, finish, 0, unroll=4)


def _hgrn2(p_lat, p_ctx, lb, norm_w):
    b, _, t, _ = p_lat.shape
    tc = p_ctx.shape[2]
    hw = lb.shape[1]
    dk = hw // HG_HEADS
    nh = HG_HEADS
    c = HG_CHUNK
    assert dk == LANES

    def col(k, tt):
        return pl.BlockSpec((None, 1, tt, dk), lambda i, h, k=k: (i, k * nh + h, 0, 0))

    return pl.pallas_call(
        functools.partial(_hg_kernel, t_lat=t, t_ctx=tc), grid=(b, nh),
        in_specs=[col(0, t), col(1, t), col(2, t), col(3, t), col(4, t),
                  col(1, tc), col(2, tc), col(3, tc),
                  pl.BlockSpec((2, dk), lambda i, h: (0, h)),
                  pl.BlockSpec((1, dk), lambda i, h: (0, h))],
        out_specs=pl.BlockSpec((1, t, dk), lambda i, h: (i, 0, h)),
        out_shape=jax.ShapeDtypeStruct((b, t, hw), F32),
        scratch_shapes=[pltpu.VMEM((t, dk), F32),
                        pltpu.VMEM((t, 2 * dk), BF16),
                        pltpu.VMEM((2, t // c, dk, dk), F32),
                        pltpu.VMEM((2, t // c, dk), F32),
                        pltpu.VMEM((t // c, dk, 2 * dk), BF16),
                        pltpu.VMEM((2, tc // c, dk, dk), F32),
                        pltpu.VMEM((2, tc // c, dk), F32),
                        pltpu.VMEM((2 * min(MXU_TILE, max(t, tc)), dk), F32)],
        compiler_params=_params("arbitrary", "arbitrary"), name="hgrn2",
    )(p_lat, p_lat, p_lat, p_lat, p_lat, p_ctx, p_ctx, p_ctx, lb, norm_w)


def _shift_rows(x, k):
    n = x.shape[0]
    y = pltpu.roll(x, k % n, 0)
    r = lax.broadcasted_iota(jnp.int32, x.shape, 0)
    return jnp.where((r >= k) & (r < n + k), y, 0.0)


def _rg_kernel(rx_ref, rgate_ref, crx_ref, cw_ref, cb_ref, wg_ref, bg_ref, lam_ref, o_ref,
               xc_s, af_s, bf_s, ab_s, bb_s, hf_s, hb_s, caf_s, cbf_s, cab_s, cbb_s, *, t_lat, t_ctx):
    w = GRID_W
    rows = t_lat // w
    ch = rx_ref.shape[0] * rx_ref.shape[-1]
    half = ch // 2

    def wide(ref, rws):
        return jnp.concatenate([ref[j, rws, :] for j in range(ref.shape[0])], axis=1)

    cw = cw_ref[...]
    cb = cb_ref[...]
    bg = bg_ref[...]
    nl = -lam_ref[...]
    hdec = (-0.5 * RG_C) * (jnp.maximum(nl, 0.0) + jnp.log1p(jnp.exp(-jnp.abs(nl))))

    def conv(xm2, xm1, x0, xp1):
        return cb + cw[0:1] * xm2 + cw[1:2] * xm1 + cw[2:3] * x0 + cw[3:4] * xp1

    def gates(xc):
        xb = xc.astype(BF16)
        g0 = _dot(xb[:, :half], wg_ref[0])
        g1 = _dot(xb[:, half:], wg_ref[1])
        outs = []
        for d in range(2):
            pre = []
            for s in (2 * d, 2 * d + 1):
                pre.append(jnp.concatenate([g0[:, s * half:(s + 1) * half], g1[:, s * half:(s + 1) * half]],
                                           axis=1) + bg[:, s * ch:(s + 1) * ch])
            log_a = hdec[d:d + 1] * jnp.tanh(0.5 * pre[0]) + hdec[d:d + 1]
            a = jnp.exp(log_a)
            one_m_a2 = -jnp.tanh(log_a) * (a * a + 1.0)
            half_mult = 0.5 * jnp.where(one_m_a2 > 0.0, one_m_a2 * lax.rsqrt(one_m_a2), 0.0)
            m = half_mult * xc
            outs += [a, m * jnp.tanh(0.5 * pre[1]) + m]
        return outs

    xctx = wide(crx_ref, slice(None))
    xcc = conv(_shift_rows(xctx, 2), _shift_rows(xctx, 1), xctx, _shift_rows(xctx, -1))
    caf_s[...], cbf_s[...], cab_s[...], cbb_s[...] = gates(xcc)

    def cstep(i, carry):
        hf, hb = carry
        hf = caf_s[pl.ds(i, 1), :] * hf + cbf_s[pl.ds(i, 1), :]
        j = t_ctx - 1 - i
        hb = cab_s[pl.ds(j, 1), :] * hb + cbb_s[pl.ds(j, 1), :]
        return hf, hb

    zrow = jnp.zeros((1, ch), F32)
    hf0, hb0 = lax.fori_loop(0, t_ctx, cstep, (zrow, zrow), unroll=8)

    def slab(rr):
        if 0 <= rr < rows:
            return wide(rx_ref, slice(rr * w, (rr + 1) * w))
        if rr < 0:
            return _shift_rows(wide(rx_ref, slice((rr + rows) * w, (rr + rows + 1) * w)), 1)
        return _shift_rows(wide(rx_ref, slice((rr - rows) * w, (rr - rows + 1) * w)), -1)

    for r in range(rows):
        xc_s[r * w:(r + 1) * w, :] = conv(slab(r - 2), slab(r - 1), slab(r), slab(r + 1))

    mb = min(MXU_TILE, t_lat)

    def gbody(i, _):
        r0 = pl.multiple_of(i * mb, mb)
        a_f, b_f, a_b, b_b = gates(xc_s[pl.ds(r0, mb), :])
        af_s[pl.ds(r0, mb), :] = a_f
        bf_s[pl.ds(r0, mb), :] = b_f
        ab_s[pl.ds(r0, mb), :] = a_b
        bb_s[pl.ds(r0, mb), :] = b_b
        return 0

    lax.fori_loop(0, t_lat // mb, gbody, 0, unroll=2)

    def l1(i, _):
        pf = pl.multiple_of(i * w, w)
        qf = pl.multiple_of((i - 1) * w, w)
        a = af_s[pl.ds(pf, w), :]
        af_s[pl.ds(pf, w), :] = a * af_s[pl.ds(qf, w), :]
        bf_s[pl.ds(pf, w), :] = a * bf_s[pl.ds(qf, w), :] + bf_s[pl.ds(pf, w), :]
        pb = pl.multiple_of((rows - 1 - i) * w, w)
        qb = pl.multiple_of((rows - i) * w, w)
        a = ab_s[pl.ds(pb, w), :]
        ab_s[pl.ds(pb, w), :] = a * ab_s[pl.ds(qb, w), :]
        bb_s[pl.ds(pb, w), :] = a * bb_s[pl.ds(qb, w), :] + bb_s[pl.ds(pb, w), :]
        return 0

    lax.fori_loop(1, rows, l1, 0)

    last = (rows - 1) * w

    def l2(i, carry):
        hf, hb = carry
        hf_s[pl.ds(i, 1), :] = hf
        hf = af_s[pl.ds(last + i, 1), :] * hf + bf_s[pl.ds(last + i, 1), :]
        j = w - 1 - i
        hb_s[pl.ds(j, 1), :] = hb
        hb = ab_s[pl.ds(j, 1), :] * hb + bb_s[pl.ds(j, 1), :]
        return hf, hb

    lax.fori_loop(0, w, l2, (hf0, hb0), unroll=8)

    def l3(i, _):
        p = pl.multiple_of(i * w, w)
        h = (af_s[pl.ds(p, w), :] * hf_s[...] + bf_s[pl.ds(p, w), :]
             + ab_s[pl.ds(p, w), :] * hb_s[...] + bb_s[pl.ds(p, w), :])
        o_ref[0, pl.ds(p, w), :] = jax.nn.gelu(wide(rgate_ref, pl.ds(p, w))) * h
        return 0

    lax.fori_loop(0, rows, l3, 0, unroll=2)


def _rglru(p_lat, p_ctx, conv_w, conv_b, wg, bg, lam):
    b, n_slabs, t, _ = p_lat.shape
    tc = p_ctx.shape[2]
    ch = conv_w.shape[1]
    cs = ch // LANES
    rx_blk = n_slabs // cs - 2
    full = lambda shape: pl.BlockSpec(shape, lambda i: (0,) * len(shape))
    big = lambda: pltpu.VMEM((t, ch), F32)
    small = lambda: pltpu.VMEM((tc, ch), F32)
    return pl.pallas_call(
        functools.partial(_rg_kernel, t_lat=t, t_ctx=tc), grid=(b,),
        in_specs=[pl.BlockSpec((None, cs, t, LANES), lambda i: (i, rx_blk, 0, 0)),
                  pl.BlockSpec((None, cs, t, LANES), lambda i: (i, rx_blk + 1, 0, 0)),
                  pl.BlockSpec((None, cs, tc, LANES), lambda i: (i, rx_blk, 0, 0)),
                  full(conv_w.shape), full(conv_b.shape), full(wg.shape), full(bg.shape), full(lam.shape)],
        out_specs=pl.BlockSpec((1, t, ch), lambda i: (i, 0, 0)),
        out_shape=jax.ShapeDtypeStruct((b, t, ch), F32),
        scratch_shapes=[big(), big(), big(), big(), big(),
                        pltpu.VMEM((GRID_W, ch), F32), pltpu.VMEM((GRID_W, ch), F32),
                        small(), small(), small(), small()],
        compiler_params=_params("arbitrary"), name="rglru",
    )(p_lat, p_lat, p_ctx, conv_w, conv_b, wg, bg, lam)


def _mix_kernel(hg_ref, rg_ref, x_ref, g1_ref, sh_ref, sc_ref, nw_ref, wo_ref, rw_ref, rb_ref,
                x1_ref, h2_ref, meta_ref, cnt_ref, base_s, *, n_exp):
    tm = x_ref.shape[1]

    @pl.when((pl.program_id(0) == 0) & (pl.program_id(1) == 0))
    def _():
        base_s[...] = jnp.zeros_like(base_s)

    hcat = jnp.concatenate([hg_ref[0], rg_ref[0]], axis=1).astype(BF16)
    x1 = x_ref[0] + g1_ref[0] * _dot(hcat, wo_ref[...])
    x1_ref[0] = x1
    h2 = _rms(x1, nw_ref[...]) * (1.0 + sc_ref[0]) + sh_ref[0]
    h2_ref[0] = h2
    logits = _dot(h2.astype(BF16), rw_ref[...]) + rb_ref[...]

    lane_e = lax.broadcasted_iota(jnp.int32, (tm, n_exp), 1).astype(F32)
    vals, idxs = [], []
    cur = logits
    for _ in range(TOP_K):
        m = jnp.max(cur, axis=1, keepdims=True)
        ix = jnp.min(jnp.where(cur == m, lane_e, float(n_exp)), axis=1, keepdims=True)
        vals.append(m)
        idxs.append(ix)
        cur = jnp.where(lane_e == ix, -jnp.inf, cur)
    ex = [jnp.exp(v - vals[0]) for v in vals]
    den = ex[0] + ex[1] + ex[2] + ex[3]

    lane = lax.broadcasted_iota(jnp.int32, (tm, LANES), 1)
    lane_f = lane.astype(F32)
    onehot = jnp.zeros((tm, LANES), F32)
    for k in range(TOP_K):
        onehot = jnp.where(lane_f == idxs[k] + float(k * n_exp), 1.0, onehot)
    ri = lax.broadcasted_iota(jnp.int32, (tm, tm), 0)
    ci = lax.broadcasted_iota(jnp.int32, (tm, tm), 1)
    prefix = _dot((ci < ri).astype(BF16), onehot.astype(BF16))
    tot = jnp.broadcast_to(prefix[tm - 1:tm] + onehot[tm - 1:tm], (8, LANES))
    lane8 = lax.broadcasted_iota(jnp.int32, (8, LANES), 1)
    off = base_s[...]
    tot_all = tot
    for j in range(1, TOP_K):
        rolled = pltpu.roll(tot, j * n_exp, 1)
        off = off + jnp.where(lane8 >= j * n_exp, rolled, 0.0)
        tot_all = tot_all + rolled
    pos = onehot * (prefix + off[0:1])
    meta = jnp.zeros((tm, LANES), F32)
    for k in range(TOP_K):
        in_k = (lane >= k * n_exp) & (lane < (k + 1) * n_exp)
        rank = jnp.sum(jnp.where(in_k, pos, 0.0), axis=1, keepdims=True)
        meta = jnp.where(lane == k, idxs[k], meta)
        meta = jnp.where(lane == TOP_K + k, ex[k] / den, meta)
        meta = jnp.where(lane == 2 * TOP_K + k, rank, meta)
    meta_ref[0] = meta
    base_s[...] = base_s[...] + tot_all
    cnt_ref[...] = base_s[...]


def _mix(hg, rg, x, g1, sh2, sc2, norm_w, wo_bf16, router_w, router_b, tm):
    b, t, d = x.shape
    hw = hg.shape[2]
    n_exp = router_w.shape[1]
    assert TOP_K * n_exp == LANES
    tok = lambda last: pl.BlockSpec((1, tm, last), lambda i, j: (i, j, 0))
    per_b = pl.BlockSpec((1, 1, d), lambda i, j: (i, 0, 0))
    full = lambda shape: pl.BlockSpec(shape, lambda i, j: (0,) * len(shape))
    return pl.pallas_call(
        functools.partial(_mix_kernel, n_exp=n_exp), grid=(b, t // tm),
        in_specs=[tok(hw), tok(hw), tok(d), per_b, per_b, per_b, full((1, d)),
                  full(wo_bf16.shape), full(router_w.shape), full((1, n_exp))],
        out_specs=[tok(d), tok(d), tok(LANES), pl.BlockSpec((8, LANES), lambda i, j: (0, 0))],
        out_shape=[jax.ShapeDtypeStruct((b, t, d), F32), jax.ShapeDtypeStruct((b, t, d), F32),
                   jax.ShapeDtypeStruct((b, t, LANES), F32), jax.ShapeDtypeStruct((8, LANES), F32)],
        scratch_shapes=[pltpu.VMEM((8, LANES), F32)],
        compiler_params=_params("arbitrary", "arbitrary"), name="outproj_router",
    )(hg, rg, x, g1, sh2, sc2, norm_w, wo_bf16, router_w, router_b)


def _dispatch_kernel(fill_off, fill_n, tail, h_ref, dest_hbm, xs_out, idx0_s, idx1_s, zero_s, sem_i, sem_d, sem_z, *,
                     pad_bits):
    td = h_ref.shape[0] * SUBLANES
    n_idx = td * TOP_K
    n_exp = fill_n.shape[0]
    zrows = zero_s.shape[0]
    i = pl.program_id(0)
    slot = i % 2
    idx_s = (idx0_s, idx1_s)

    def idx_copy(j, sl):
        return pltpu.make_async_copy(dest_hbm.at[pl.ds(j * n_idx, n_idx)], idx_s[sl], sem_i.at[sl])

    @pl.when(i == 0)
    def _():
        idx_copy(0, 0).start()

    for sl in range(2):
        @pl.when((i + 1 < pl.num_programs(0)) & (slot != sl))
        def _():
            idx_copy(i + 1, sl).start()

    def fill(wait):
        def go(copy, cond):
            @pl.when(cond)
            def _():
                copy.wait() if wait else copy.start()

        def per_expert(e, _):
            off = fill_off[e]
            npad = fill_n[e]
            n_single = npad & (SUBLANES - 1)
            for r in range(SUBLANES - 1):
                go(pltpu.make_async_copy(zero_s.at[pl.ds(0, 1), :], xs_out.at[pl.ds(off + r, 1), :], sem_z),
                   r < n_single)
            off = pl.multiple_of(off + n_single, SUBLANES)
            for bit in reversed(range(SUBLANES.bit_length() - 1, pad_bits)):
                size = 1 << bit
                go(pltpu.make_async_copy(zero_s.at[pl.ds(0, size), :], xs_out.at[pl.ds(off, size), :], sem_z),
                   (npad & size) != 0)
                off = pl.multiple_of(off + (npad & size), SUBLANES)
            return 0

        lax.fori_loop(0, n_exp, per_expert, 0)

        def per_tail_chunk(j, _):
            off = pl.multiple_of(tail[0] + j * zrows, zrows)
            copy = pltpu.make_async_copy(zero_s, xs_out.at[pl.ds(off, zrows), :], sem_z)
            copy.wait() if wait else copy.start()
            return 0

        lax.fori_loop(0, tail[1], per_tail_chunk, 0)

    @pl.when(i == 0)
    def _():
        zero_s[...] = jnp.zeros_like(zero_s)
        fill(False)

    for sl in range(2):
        @pl.when(slot == sl)
        def _():
            idx_copy(i, sl).wait()

            for g in range(td // SUBLANES):
                for u in range(SUBLANES):
                    for k in range(TOP_K):
                        dst = idx_s[sl][(g * SUBLANES + u) * TOP_K + k]
                        pltpu.make_async_copy(h_ref.at[g, pl.ds(u, 1), :], xs_out.at[pl.ds(dst, 1), :],
                                              sem_d).start(priority=k % 2)

    for _ in range(TOP_K * td // zrows):
        pltpu.make_async_copy(zero_s, xs_out.at[pl.ds(0, zrows), :], sem_d).wait()

    @pl.when(i == 0)
    def _():
        fill(True)


def _dispatch(h2, dest_flat, fill_off, fill_n, tail, n_slots, td, bm):
    n, d = h2.shape
    pad_bits = (bm - 1).bit_length()
    assert (TOP_K * td) % (bm // 2) == 0
    grid_spec = pltpu.PrefetchScalarGridSpec(
        num_scalar_prefetch=3, grid=(n // td,),
        in_specs=[pl.BlockSpec((td // SUBLANES, SUBLANES, d), lambda i, fo, fn, tl: (i, 0, 0)),
                  pl.BlockSpec(memory_space=pl.ANY)],
        out_specs=pl.BlockSpec(memory_space=pl.ANY),
        scratch_shapes=[pltpu.SMEM((td * TOP_K,), jnp.int32), pltpu.SMEM((td * TOP_K,), jnp.int32),
                        pltpu.VMEM((bm // 2, d), F32),
                        pltpu.SemaphoreType.DMA((2,)), pltpu.SemaphoreType.DMA, pltpu.SemaphoreType.DMA])
    return pl.pallas_call(
        functools.partial(_dispatch_kernel, pad_bits=pad_bits), grid_spec=grid_spec,
        out_shape=jax.ShapeDtypeStruct((n_slots, d), F32),
        compiler_params=_params("arbitrary"), name="moe_dispatch",
    )(fill_off, fill_n, tail, h2.reshape(n // SUBLANES, SUBLANES, d), dest_flat)


def _expert_kernel(be_ref, nu_ref, x_ref, wgu_ref, bgu_ref, wd_ref, bd_ref, y_ref, wgu_s, wd_s):
    i = pl.program_id(0)
    d_ff = wd_ref.shape[1]

    @pl.when(i >= nu_ref[0])
    def _():
        y_ref[...] = jnp.zeros_like(y_ref)

    @pl.when(i < nu_ref[0])
    def _():
        @pl.when((i == 0) | (be_ref[i] != be_ref[jnp.maximum(i - 1, 0)]))
        def _():
            wgu_s[...] = wgu_ref[0].astype(BF16)
            wd_s[...] = wd_ref[0].astype(BF16)

        gu = _dot(x_ref[...].astype(BF16), wgu_s[...]) + bgu_ref[0]
        gate = jnp.minimum(gu[:, :d_ff], SWIGLU_LIMIT)
        up = jnp.clip(gu[:, d_ff:], -SWIGLU_LIMIT, SWIGLU_LIMIT)
        act = gate * _sigmoid(SWIGLU_ALPHA * gate) * (up + 1.0)
        y_ref[...] = _dot(act.astype(BF16), wd_s[...]) + bd_ref[0]


def _experts(xs, blk_expert, n_used, wgu, bgu, wd, bd, bm):
    n_slots, d = xs.shape
    n_exp, _, f2 = wgu.shape
    d_ff = wd.shape[1]
    n_blocks = n_slots // bm
    row = lambda i, be, nu: (jnp.minimum(i, nu[0] - 1), 0)
    grid_spec = pltpu.PrefetchScalarGridSpec(
        num_scalar_prefetch=2, grid=(n_blocks,),
        in_specs=[pl.BlockSpec((bm, d), row),
                  pl.BlockSpec((1, d, f2), lambda i, be, nu: (be[i], 0, 0)),
                  pl.BlockSpec((1, 1, f2), lambda i, be, nu: (be[i], 0, 0)),
                  pl.BlockSpec((1, d_ff, d), lambda i, be, nu: (be[i], 0, 0)),
                  pl.BlockSpec((1, 1, d), lambda i, be, nu: (be[i], 0, 0))],
        out_specs=pl.BlockSpec((bm, d), lambda i, be, nu: (i, 0)),
        scratch_shapes=[pltpu.VMEM((d, f2), BF16), pltpu.VMEM((d_ff, d), BF16)])
    return pl.pallas_call(
        _expert_kernel, grid_spec=grid_spec,
        out_shape=jax.ShapeDtypeStruct((n_slots, d), F32),
        compiler_params=_params("arbitrary"), name="moe_experts",
    )(blk_expert, n_used, xs, wgu, bgu.reshape(n_exp, 1, f2), wd, bd.reshape(n_exp, 1, d))


def _combine_kernel(x1_ref, meta_ref, g2_ref, fw_ref, dest_hbm, y_hbm, o_ref, rows0_s, rows1_s, idx0_s, idx1_s,
                    sem_i, sem_d):
    tc, d = x1_ref.shape
    n_idx = tc * TOP_K
    i = pl.program_id(0)
    n = pl.num_programs(0)
    slot = i % 2
    idx_s = (idx0_s, idx1_s)
    rows_s = (rows0_s, rows1_s)

    def idx_copy(j, sl):
        return pltpu.make_async_copy(dest_hbm.at[pl.ds(j * n_idx, n_idx)], idx_s[sl], sem_i.at[sl])

    def row_copy(src, sl, k, g, u):
        return pltpu.make_async_copy(y_hbm.at[pl.ds(src, 1), :], rows_s[sl].at[k, g, pl.ds(u, 1), :], sem_d.at[sl])

    def wait_rows(sl):
        for k in range(TOP_K):
            pltpu.make_async_copy(rows_s[1 - sl].at[k], rows_s[sl].at[k], sem_d.at[sl]).wait()

    def finish(sl):
        meta = meta_ref[...]
        moe = meta[:, TOP_K:TOP_K + 1] * rows_s[sl][0].reshape(tc, d)
        for k in range(1, TOP_K):
            moe = moe + meta[:, TOP_K + k:TOP_K + k + 1] * rows_s[sl][k].reshape(tc, d)
        o_ref[...] = _rms(x1_ref[...] + g2_ref[0] * moe, fw_ref[...])

    @pl.when(i == 0)
    def _():
        idx_copy(0, 0).start()

        @pl.when(n > 1)
        def _():
            idx_copy(1, 1).start()

        idx_copy(0, 0).wait()

        def first(g, _):
            for u in range(SUBLANES):
                for k in range(TOP_K):
                    row_copy(idx0_s[g * (SUBLANES * TOP_K) + u * TOP_K + k], 0, k, g, u).start(priority=k % 2)
            return 0

        lax.fori_loop(0, tc // SUBLANES, first, 0)

    for sl in range(2):
        @pl.when((slot == sl) & (i + 1 < n))
        def _():
            wait_rows(sl)
            idx_copy(i + 1, 1 - sl).wait()
            for g in range(tc // SUBLANES):
                for u in range(SUBLANES):
                    for k in range(TOP_K):
                        row_copy(idx_s[1 - sl][(g * SUBLANES + u) * TOP_K + k], 1 - sl, k, g, u).start(priority=k % 2)
            finish(sl)

        @pl.when((slot == sl) & (i + 1 == n))
        def _():
            wait_rows(sl)
            finish(sl)

        @pl.when((slot == sl) & (i + 2 < n))
        def _():
            idx_copy(i + 2, sl).start()


def _combine(x1, meta, g2, final_w, dest_flat, y, t_seq, tc):
    n, d = x1.shape
    return pl.pallas_call(
        _combine_kernel, grid=(n // tc,),
        in_specs=[pl.BlockSpec((tc, d), lambda i: (i, 0)),
                  pl.BlockSpec((tc, LANES), lambda i: (i, 0)),
                  pl.BlockSpec((1, 1, d), lambda i: (i * tc // t_seq, 0, 0)),
                  pl.BlockSpec((1, d), lambda i: (0, 0)),
                  pl.BlockSpec(memory_space=pl.ANY),
                  pl.BlockSpec(memory_space=pl.ANY)],
        out_specs=pl.BlockSpec((tc, d), lambda i: (i, 0)),
        out_shape=jax.ShapeDtypeStruct((n, d), F32),
        scratch_shapes=[pltpu.VMEM((TOP_K, tc // SUBLANES, SUBLANES, d), F32),
                        pltpu.VMEM((TOP_K, tc // SUBLANES, SUBLANES, d), F32),
                        pltpu.SMEM((tc * TOP_K,), jnp.int32), pltpu.SMEM((tc * TOP_K,), jnp.int32),
                        pltpu.SemaphoreType.DMA((2,)), pltpu.SemaphoreType.DMA((2,))],
        compiler_params=_params("arbitrary"), name="moe_combine",
    )(x1, meta, g2, final_w, dest_flat, y)


def _gate_weights(wa, wx):
    _, heads, hd, _ = wa.shape
    hh = heads // 2
    eye = jnp.eye(hh, dtype=wa.dtype)

    def blockdiag(wsel):
        return jnp.einsum('hij,hg->higj', wsel, eye).reshape(hh * hd, hh * hd)

    halves = []
    for s in range(2):
        sl = slice(s * hh, (s + 1) * hh)
        halves.append(jnp.concatenate([blockdiag(wa[0, sl]), blockdiag(wx[0, sl]),
                                       blockdiag(wa[1, sl]), blockdiag(wx[1, sl])], axis=1))
    return jnp.stack(halves).astype(BF16)


def kernel(x, c, ctx, c_ctx, norm1_w, norm2_w, w_ada, b_ada, w_in, hg_lb_logits, hg_norm_w, rg_conv_w, rg_conv_b,
           rg_wa, rg_ba, rg_wx, rg_bx, rg_lambda, w_out, router_w, router_b, w_gate_up, b_gate_up, w_down,
           b_down, final_norm_w):
    b, t, d = x.shape
    tcx = ctx.shape[1]
    n_exp = router_w.shape[-1]
    n_tok = b * t
    depth = w_in.shape[0]
    lb_all = jnp.cumsum(jax.nn.softmax(hg_lb_logits.astype(F32), axis=0), axis=0)

    for l in range(depth):
        assert l == depth - 1, "context stream update of non-final layers is not implemented"
        pad = (-(b + 1)) % 8
        c_all = jnp.concatenate([c, c_ctx[None], jnp.zeros((pad, d), F32)], axis=0)
        mod = _mod(c_all, w_ada[l], b_ada[l])
        sh1, sc1, g1, sh2, sc2, g2 = [m[:b, None, :] for m in jnp.split(mod, 6, axis=-1)]
        csh1, csc1 = [m[b:b + 1, None, :] for m in jnp.split(mod, 6, axis=-1)[:2]]

        w_in_b = w_in[l].astype(BF16)
        nw1 = norm1_w[l].reshape(1, d)
        tm = min(TOKEN_TILE, t)
        p_lat = _inproj(x, sh1, sc1, nw1, w_in_b, tm)
        p_ctx = _inproj(ctx, csh1, csc1, nw1, w_in_b, min(MXU_TILE, tcx))

        hg = _hgrn2(p_lat, p_ctx, lb_all[l], hg_norm_w[l].reshape(1, -1))
        ch = rg_conv_w.shape[-1]
        wg = _gate_weights(rg_wa[l], rg_wx[l])
        bg = jnp.concatenate([rg_ba[l, 0], rg_bx[l, 0], rg_ba[l, 1], rg_bx[l, 1]]).reshape(1, 4 * ch)
        rg = _rglru(p_lat, p_ctx, rg_conv_w[l], rg_conv_b[l].reshape(1, ch), wg, bg, rg_lambda[l])

        x1, h2, meta, cnt = _mix(hg, rg, x, g1, sh2, sc2, norm2_w[l].reshape(1, d), w_out[l].astype(BF16),
                                 router_w[l].astype(BF16), router_b[l].reshape(1, n_exp), min(ROUTER_TILE, t))

        bm = EXPERT_BLOCK
        meta2 = meta.reshape(n_tok, LANES)
        counts = cnt[0, :n_exp].astype(jnp.int32)
        padded = (counts + bm - 1) // bm * bm
        pad_end = jnp.cumsum(padded)
        pad_start = pad_end - padded
        idx = meta2[:, 0:TOP_K].astype(jnp.int32)
        rank = meta2[:, 2 * TOP_K:3 * TOP_K].astype(jnp.int32)
        dest = (pad_start[idx] + rank).reshape(-1)
        n_blocks = -(-n_tok * TOP_K // bm) + n_exp
        blk_start = jnp.arange(n_blocks, dtype=jnp.int32) * bm
        blk_expert = jnp.minimum(jnp.sum(blk_start[:, None] >= pad_end[None, :], axis=1), n_exp - 1).astype(jnp.int32)
        n_used = (pad_end[-1:] // bm).astype(jnp.int32)

        n_slots = n_blocks * bm
        tail = jnp.stack([pad_end[-1], (n_slots - pad_end[-1]) // (bm // 2)]).astype(jnp.int32)
        xs = _dispatch(h2.reshape(n_tok, d), dest, pad_start + counts, padded - counts, tail, n_slots,
                       min(DISPATCH_TILE, t), bm)
        y = _experts(xs, blk_expert, n_used, w_gate_up[l], b_gate_up[l], w_down[l], b_down[l], bm)
        out = _combine(x1.reshape(n_tok, d), meta2, g2, final_norm_w.reshape(1, d), dest, y, t,
                       min(COMBINE_TILE, t))
        return out.reshape(b, t, d)
```

```python
import functools

import jax
import jax.numpy as jnp
from jax import lax
from jax.experimental import pallas as pl
from jax.experimental.pallas import tpu as pltpu

GRID_W = 64
HG_HEADS = 4
HG_CHUNK = 32
RG_HEADS = 8
RG_CONV = 4
RG_C = 8.0
TOP_K = 4
SWIGLU_LIMIT = 7.0
SWIGLU_ALPHA = 1.702
EPS = 1e-6

LANES = 128
SUBLANES = 8
MXU_TILE = 256
VMEM_LIMIT = 56 * 1024 * 1024

ADALN_COLUMN_TILES = 4
TOKEN_TILE = 512
ROUTER_TILE = 512
EXPERT_BLOCK = 512
DISPATCH_TILE = 512
COMBINE_TILE = 512

F32 = jnp.float32
BF16 = jnp.bfloat16
HIGHEST = lax.Precision.HIGHEST


def _params(*sem):
    return pltpu.CompilerParams(dimension_semantics=sem, vmem_limit_bytes=VMEM_LIMIT)


def _sigmoid(x):
    return 0.5 * jnp.tanh(0.5 * x) + 0.5


def _silu(x):
    return x * _sigmoid(x)


def _rms(x, w):
    return x * lax.rsqrt(jnp.mean(x * x, axis=-1, keepdims=True) + EPS) * w


def _dot(a, b):
    return jnp.dot(a, b, preferred_element_type=F32)


def _dot_nt(a, b):
    return lax.dot_general(a, b, (((1,), (1,)), ((), ())), preferred_element_type=F32)


def _mod_kernel(c_ref, w_ref, b_ref, o_ref):
    o_ref[...] = jnp.dot(_silu(c_ref[...]), w_ref[...], preferred_element_type=F32,
                         precision=HIGHEST) + b_ref[...]


def _mod(c_all, w_ada, b_ada):
    r, d = c_all.shape
    n = w_ada.shape[1]
    assert n % (ADALN_COLUMN_TILES * LANES) == 0
    tn = n // ADALN_COLUMN_TILES
    return pl.pallas_call(
        _mod_kernel, grid=(n // tn,),
        in_specs=[pl.BlockSpec((r, d), lambda j: (0, 0)),
                  pl.BlockSpec((d, tn), lambda j: (0, j)),
                  pl.BlockSpec((1, tn), lambda j: (0, j))],
        out_specs=pl.BlockSpec((r, tn), lambda j: (0, j)),
        out_shape=jax.ShapeDtypeStruct((r, n), F32),
        compiler_params=_params("arbitrary"), name="adaln_mod",
    )(c_all, w_ada, b_ada.reshape(1, n))


def _inproj_kernel(x_ref, sh_ref, sc_ref, nw_ref, w_ref, o_ref):
    h = _rms(x_ref[0], nw_ref[...]) * (1.0 + sc_ref[0]) + sh_ref[0]
    p = _dot(h.astype(BF16), w_ref[...])
    for j in range(o_ref.shape[1]):
        o_ref[0, j] = p[:, j * LANES:(j + 1) * LANES]


def _inproj(x, shift, scale, norm_w, w_bf16, tm):
    b, t, d = x.shape
    n = w_bf16.shape[1]
    assert n % LANES == 0
    per_batch = shift.shape[0] == b
    mod_map = (lambda i, j: (i, 0, 0)) if per_batch else (lambda i, j: (0, 0, 0))
    return pl.pallas_call(
        _inproj_kernel, grid=(b, t // tm),
        in_specs=[pl.BlockSpec((1, tm, d), lambda i, j: (i, j, 0)),
                  pl.BlockSpec((1, 1, d), mod_map),
                  pl.BlockSpec((1, 1, d), mod_map),
                  pl.BlockSpec((1, d), lambda i, j: (0, 0)),
                  pl.BlockSpec((d, n), lambda i, j: (0, 0))],
        out_specs=pl.BlockSpec((1, n // LANES, tm, LANES), lambda i, j: (i, 0, j, 0)),
        out_shape=jax.ShapeDtypeStruct((b, n // LANES, t, LANES), F32),
        compiler_params=_params("arbitrary", "arbitrary"), name="norm_inproj",
    )(x, shift, scale, norm_w, w_bf16)


def _split_bf16(x):
    hi = x.astype(BF16)
    return hi, (x - hi.astype(F32)).astype(BF16)


def _hg_kernel(q_ref, v_ref, zf_ref, zb_ref, g_ref, cv_ref, czf_ref, czb_ref, lb_ref, nw_ref, o_ref,
               o_s, qd_s, u_s, dec_s, st_s, cu_s, cdec_s, tot_s, *, t_lat, t_ctx):
    c = HG_CHUNK
    dk = q_ref.shape[-1]
    lb = lb_ref[...]
    lbf, lbb = lb[0:1], lb[1:2]

    def prep(rb, zf, zb, q, v, u_out, dec_out, blk, r0):
        nc = rb // c
        ri = lax.broadcasted_iota(jnp.int32, (rb, rb), 0)
        ci = lax.broadcasted_iota(jnp.int32, (rb, rb), 1)
        same = (ri // c) == (ci // c)
        low = same & (ci <= ri)
        upp = same & (ci >= ri)
        ff = lbf + (1.0 - lbf) * _sigmoid(zf)
        fb = lbb + (1.0 - lbb) * _sigmoid(zb)
        lgf, lgb = jnp.log(ff), jnp.log(fb)
        rhs = jnp.concatenate([*_split_bf16(lgf), *_split_bf16(lgb)], axis=1)
        pre = _dot(low.astype(BF16), rhs)
        bcf = pre[:, :dk] + pre[:, dk:2 * dk]
        pfb = pre[:, 2 * dk:3 * dk] + pre[:, 3 * dk:]
        half = tot_s.shape[0] // 2
        tot_s[0:rb, :] = bcf
        tot_s[half:half + rb, :] = pfb
        totf = tot_s[pl.ds(c - 1, nc, stride=c), :]
        totb = tot_s[pl.ds(half + c - 1, nc, stride=c), :]

        def spread(tot):
            return jnp.broadcast_to(tot[:, None, :], (nc, c, dk)).reshape(rb, dk)

        remf = spread(totf) - bcf
        remb = pfb - lgb
        bcb = spread(totb) - remb
        kkf, kkb = 1.0 - ff, 1.0 - fb
        kef = (kkf * jnp.exp(remf)).astype(BF16)
        keb = (kkb * jnp.exp(remb)).astype(BF16)
        chunk_of_row = lax.broadcasted_iota(jnp.int32, (rb, dk), 0) // c
        zero = jnp.zeros((rb, dk), BF16)
        keys = jnp.concatenate([jnp.where(chunk_of_row == j, ke, zero) for ke in (kef, keb) for j in range(nc)],
                               axis=1)
        u_all = _dot(v.T.astype(BF16), keys)
        c0 = blk * nc
        for d in range(2):
            for j in range(nc):
                u_out[d, c0 + j] = u_all[:, (d * nc + j) * dk:(d * nc + j + 1) * dk]
        dec_out[0, pl.ds(pl.multiple_of(c0, nc), nc), :] = jnp.exp(totf)
        dec_out[1, pl.ds(pl.multiple_of(c0, nc), nc), :] = jnp.exp(totb)
        if q is None:
            return
        sq = _silu(q)
        qdf = (sq * jnp.exp(bcf)).astype(BF16)
        qdb = (sq * jnp.exp(bcb)).astype(BF16)
        kdf = (kkf * jnp.exp(-bcf)).astype(BF16)
        kdb = (kkb * jnp.exp(-bcb)).astype(BF16)
        p = jnp.where(low, _dot_nt(qdf, kdf), 0.0) + jnp.where(upp, _dot_nt(qdb, kdb), 0.0)
        o_s[pl.ds(r0, rb), :] = _dot(p.astype(BF16), v.astype(BF16))
        qd_s[pl.ds(r0, rb), 0:dk] = qdf
        qd_s[pl.ds(r0, rb), dk:2 * dk] = qdb

    def scan(n, ur, decr, keep, carry):
        def body(i, carry):
            sf, sb = carry
            j = n - 1 - i
            if keep:
                st_s[i, :, 0:dk] = sf.astype(BF16)
                st_s[j, :, dk:2 * dk] = sb.astype(BF16)
            return sf * decr[0, pl.ds(i, 1), :] + ur[0, i], sb * decr[1, pl.ds(j, 1), :] + ur[1, j]

        return lax.fori_loop(0, n, body, carry, unroll=2)

    rbc = min(MXU_TILE, t_ctx)
    for blk in range(t_ctx // rbc):
        r0 = blk * rbc
        prep(rbc, czf_ref[0, r0:r0 + rbc, :], czb_ref[0, r0:r0 + rbc, :], None, cv_ref[0, r0:r0 + rbc, :],
             cu_s, cdec_s, blk, r0)
    zero = jnp.zeros((dk, dk), F32)
    carry = scan(t_ctx // c, cu_s, cdec_s, False, (zero, zero))

    rbl = min(MXU_TILE, t_lat)

    def lat_prep(blk, _):
        r0 = pl.multiple_of(blk * rbl, rbl)
        rows = pl.ds(r0, rbl)
        prep(rbl, zf_ref[0, rows, :], zb_ref[0, rows, :], q_ref[0, rows, :], v_ref[0, rows, :], u_s, dec_s, blk, r0)
        return 0

    lax.fori_loop(0, t_lat // rbl, lat_prep, 0, unroll=8)
    scan(t_lat // c, u_s, dec_s, True, carry)

    def finish(blk, _):
        r0 = pl.multiple_of(blk * rbl, rbl)
        inter = [_dot_nt(qd_s[pl.ds(r0 + j * c, c), :], st_s[blk * (rbl // c) + j]) for j in range(rbl // c)]
        o = o_s[pl.ds(r0, rbl), :] + jnp.concatenate(inter, axis=0)
        o_ref[0, pl.ds(r0, rbl), :] = _rms(o, nw_ref[...]) * _silu(g_ref[0, pl.ds(r0, rbl), :])
        return 0

    lax.fori_loop(0, t_lat // rbl, finish, 0, unroll=8)


def _hgrn2(p_lat, p_ctx, lb, norm_w):
    b, _, t, _ = p_lat.shape
    tc = p_ctx.shape[2]
    hw = lb.shape[1]
    dk = hw // HG_HEADS
    nh = HG_HEADS
    c = HG_CHUNK
    assert dk == LANES

    def col(k, tt):
        return pl.BlockSpec((None, 1, tt, dk), lambda i, h, k=k: (i, k * nh + h, 0, 0))

    return pl.pallas_call(
        functools.partial(_hg_kernel, t_lat=t, t_ctx=tc), grid=(b, nh),
        in_specs=[col(0, t), col(1, t), col(2, t), col(3, t), col(4, t),
                  col(1, tc), col(2, tc), col(3, tc),
                  pl.BlockSpec((2, dk), lambda i, h: (0, h)),
                  pl.BlockSpec((1, dk), lambda i, h: (0, h))],
        out_specs=pl.BlockSpec((1, t, dk), lambda i, h: (i, 0, h)),
        out_shape=jax.ShapeDtypeStruct((b, t, hw), F32),
        scratch_shapes=[pltpu.VMEM((t, dk), F32),
                        pltpu.VMEM((t, 2 * dk), BF16),
                        pltpu.VMEM((2, t // c, dk, dk), F32),
                        pltpu.VMEM((2, t // c, dk), F32),
                        pltpu.VMEM((t // c, dk, 2 * dk), BF16),
                        pltpu.VMEM((2, tc // c, dk, dk), F32),
                        pltpu.VMEM((2, tc // c, dk), F32),
                        pltpu.VMEM((2 * min(MXU_TILE, max(t, tc)), dk), F32)],
        compiler_params=_params("arbitrary", "arbitrary"), name="hgrn2",
    )(p_lat, p_lat, p_lat, p_lat, p_lat, p_ctx, p_ctx, p_ctx, lb, norm_w)


def _shift_rows(x, k):
    n = x.shape[0]
    y = pltpu.roll(x, k % n, 0)
    r = lax.broadcasted_iota(jnp.int32, x.shape, 0)
    return jnp.where((r >= k) & (r < n + k), y, 0.0)


def _rg_kernel(rx_ref, rgate_ref, crx_ref, cw_ref, cb_ref, wg_ref, bg_ref, lam_ref, o_ref,
               xc_s, af_s, bf_s, ab_s, bb_s, hf_s, hb_s, caf_s, cbf_s, cab_s, cbb_s, *, t_lat, t_ctx):
    w = GRID_W
    rows = t_lat // w
    ch = rx_ref.shape[0] * rx_ref.shape[-1]
    half = ch // 2

    def wide(ref, rws):
        return jnp.concatenate([ref[j, rws, :] for j in range(ref.shape[0])], axis=1)

    cw = cw_ref[...]
    cb = cb_ref[...]
    bg = bg_ref[...]
    nl = -lam_ref[...]
    hdec = (-0.5 * RG_C) * (jnp.maximum(nl, 0.0) + jnp.log1p(jnp.exp(-jnp.abs(nl))))

    def conv(xm2, xm1, x0, xp1):
        return cb + cw[0:1] * xm2 + cw[1:2] * xm1 + cw[2:3] * x0 + cw[3:4] * xp1

    def gates(xc):
        xb = xc.astype(BF16)
        g0 = _dot(xb[:, :half], wg_ref[0])
        g1 = _dot(xb[:, half:], wg_ref[1])
        outs = []
        for d in range(2):
            pre = []
            for s in (2 * d, 2 * d + 1):
                pre.append(jnp.concatenate([g0[:, s * half:(s + 1) * half], g1[:, s * half:(s + 1) * half]],
                                           axis=1) + bg[:, s * ch:(s + 1) * ch])
            log_a = hdec[d:d + 1] * jnp.tanh(0.5 * pre[0]) + hdec[d:d + 1]
            a = jnp.exp(log_a)
            one_m_a2 = -jnp.tanh(log_a) * (a * a + 1.0)
            half_mult = 0.5 * jnp.where(one_m_a2 > 0.0, one_m_a2 * lax.rsqrt(one_m_a2), 0.0)
            m = half_mult * xc
            outs += [a, m * jnp.tanh(0.5 * pre[1]) + m]
        return outs

    xctx = wide(crx_ref, slice(None))
    xcc = conv(_shift_rows(xctx, 2), _shift_rows(xctx, 1), xctx, _shift_rows(xctx, -1))
    caf_s[...], cbf_s[...], cab_s[...], cbb_s[...] = gates(xcc)

    def cstep(i, carry):
        hf, hb = carry
        hf = caf_s[pl.ds(i, 1), :] * hf + cbf_s[pl.ds(i, 1), :]
        j = t_ctx - 1 - i
        hb = cab_s[pl.ds(j, 1), :] * hb + cbb_s[pl.ds(j, 1), :]
        return hf, hb

    zrow = jnp.zeros((1, ch), F32)
    hf0, hb0 = lax.fori_loop(0, t_ctx, cstep, (zrow, zrow), unroll=8)

    def slab(rr):
        if 0 <= rr < rows:
            return wide(rx_ref, slice(rr * w, (rr + 1) * w))
        if rr < 0:
            return _shift_rows(wide(rx_ref, slice((rr + rows) * w, (rr + rows + 1) * w)), 1)
        return _shift_rows(wide(rx_ref, slice((rr - rows) * w, (rr - rows + 1) * w)), -1)

    for r in range(rows):
        xc_s[r * w:(r + 1) * w, :] = conv(slab(r - 2), slab(r - 1), slab(r), slab(r + 1))

    mb = min(MXU_TILE, t_lat)

    def gbody(i, _):
        r0 = pl.multiple_of(i * mb, mb)
        a_f, b_f, a_b, b_b = gates(xc_s[pl.ds(r0, mb), :])
        af_s[pl.ds(r0, mb), :] = a_f
        bf_s[pl.ds(r0, mb), :] = b_f
        ab_s[pl.ds(r0, mb), :] = a_b
        bb_s[pl.ds(r0, mb), :] = b_b
        return 0

    lax.fori_loop(0, t_lat // mb, gbody, 0, unroll=2)

    def l1(i, _):
        pf = pl.multiple_of(i * w, w)
        qf = pl.multiple_of((i - 1) * w, w)
        a = af_s[pl.ds(pf, w), :]
        af_s[pl.ds(pf, w), :] = a * af_s[pl.ds(qf, w), :]
        bf_s[pl.ds(pf, w), :] = a * bf_s[pl.ds(qf, w), :] + bf_s[pl.ds(pf, w), :]
        pb = pl.multiple_of((rows - 1 - i) * w, w)
        qb = pl.multiple_of((rows - i) * w, w)
        a = ab_s[pl.ds(pb, w), :]
        ab_s[pl.ds(pb, w), :] = a * ab_s[pl.ds(qb, w), :]
        bb_s[pl.ds(pb, w), :] = a * bb_s[pl.ds(qb, w), :] + bb_s[pl.ds(pb, w), :]
        return 0

    lax.fori_loop(1, rows, l1, 0)

    last = (rows - 1) * w

    def l2(i, carry):
        hf, hb = carry
        hf_s[pl.ds(i, 1), :] = hf
        hf = af_s[pl.ds(last + i, 1), :] * hf + bf_s[pl.ds(last + i, 1), :]
        j = w - 1 - i
        hb_s[pl.ds(j, 1), :] = hb
        hb = ab_s[pl.ds(j, 1), :] * hb + bb_s[pl.ds(j, 1), :]
        return hf, hb

    lax.fori_loop(0, w, l2, (hf0, hb0), unroll=8)

    def l3(i, _):
        p = pl.multiple_of(i * w, w)
        h = (af_s[pl.ds(p, w), :] * hf_s[...] + bf_s[pl.ds(p, w), :]
             + ab_s[pl.ds(p, w), :] * hb_s[...] + bb_s[pl.ds(p, w), :])
        o_ref[0, pl.ds(p, w), :] = jax.nn.gelu(wide(rgate_ref, pl.ds(p, w))) * h
        return 0

    lax.fori_loop(0, rows, l3, 0, unroll=2)


def _rglru(p_lat, p_ctx, conv_w, conv_b, wg, bg, lam):
    b, n_slabs, t, _ = p_lat.shape
    tc = p_ctx.shape[2]
    ch = conv_w.shape[1]
    cs = ch // LANES
    rx_blk = n_slabs // cs - 2
    full = lambda shape: pl.BlockSpec(shape, lambda i: (0,) * len(shape))
    big = lambda: pltpu.VMEM((t, ch), F32)
    small = lambda: pltpu.VMEM((tc, ch), F32)
    return pl.pallas_call(
        functools.partial(_rg_kernel, t_lat=t, t_ctx=tc), grid=(b,),
        in_specs=[pl.BlockSpec((None, cs, t, LANES), lambda i: (i, rx_blk, 0, 0)),
                  pl.BlockSpec((None, cs, t, LANES), lambda i: (i, rx_blk + 1, 0, 0)),
                  pl.BlockSpec((None, cs, tc, LANES), lambda i: (i, rx_blk, 0, 0)),
                  full(conv_w.shape), full(conv_b.shape), full(wg.shape), full(bg.shape), full(lam.shape)],
        out_specs=pl.BlockSpec((1, t, ch), lambda i: (i, 0, 0)),
        out_shape=jax.ShapeDtypeStruct((b, t, ch), F32),
        scratch_shapes=[big(), big(), big(), big(), big(),
                        pltpu.VMEM((GRID_W, ch), F32), pltpu.VMEM((GRID_W, ch), F32),
                        small(), small(), small(), small()],
        compiler_params=_params("arbitrary"), name="rglru",
    )(p_lat, p_lat, p_ctx, conv_w, conv_b, wg, bg, lam)


def _mix_kernel(hg_ref, rg_ref, x_ref, g1_ref, sh_ref, sc_ref, nw_ref, wo_ref, rw_ref, rb_ref,
                x1_ref, h2_ref, meta_ref, cnt_ref, base_s, *, n_exp):
    tm = x_ref.shape[1]

    @pl.when((pl.program_id(0) == 0) & (pl.program_id(1) == 0))
    def _():
        base_s[...] = jnp.zeros_like(base_s)

    hcat = jnp.concatenate([hg_ref[0], rg_ref[0]], axis=1).astype(BF16)
    x1 = x_ref[0] + g1_ref[0] * _dot(hcat, wo_ref[...])
    x1_ref[0] = x1
    h2 = _rms(x1, nw_ref[...]) * (1.0 + sc_ref[0]) + sh_ref[0]
    h2_ref[0] = h2
    logits = _dot(h2.astype(BF16), rw_ref[...]) + rb_ref[...]

    lane_e = lax.broadcasted_iota(jnp.int32, (tm, n_exp), 1).astype(F32)
    vals, idxs = [], []
    cur = logits
    for _ in range(TOP_K):
        m = jnp.max(cur, axis=1, keepdims=True)
        ix = jnp.min(jnp.where(cur == m, lane_e, float(n_exp)), axis=1, keepdims=True)
        vals.append(m)
        idxs.append(ix)
        cur = jnp.where(lane_e == ix, -jnp.inf, cur)
    ex = [jnp.exp(v - vals[0]) for v in vals]
    den = ex[0] + ex[1] + ex[2] + ex[3]

    lane = lax.broadcasted_iota(jnp.int32, (tm, LANES), 1)
    lane_f = lane.astype(F32)
    onehot = jnp.zeros((tm, LANES), F32)
    for k in range(TOP_K):
        onehot = jnp.where(lane_f == idxs[k] + float(k * n_exp), 1.0, onehot)
    ri = lax.broadcasted_iota(jnp.int32, (tm, tm), 0)
    ci = lax.broadcasted_iota(jnp.int32, (tm, tm), 1)
    prefix = _dot((ci < ri).astype(BF16), onehot.astype(BF16))
    tot = jnp.broadcast_to(prefix[tm - 1:tm] + onehot[tm - 1:tm], (8, LANES))
    lane8 = lax.broadcasted_iota(jnp.int32, (8, LANES), 1)
    off = base_s[...]
    tot_all = tot
    for j in range(1, TOP_K):
        rolled = pltpu.roll(tot, j * n_exp, 1)
        off = off + jnp.where(lane8 >= j * n_exp, rolled, 0.0)
        tot_all = tot_all + rolled
    pos = onehot * (prefix + off[0:1])
    meta = jnp.zeros((tm, LANES), F32)
    for k in range(TOP_K):
        in_k = (lane >= k * n_exp) & (lane < (k + 1) * n_exp)
        rank = jnp.sum(jnp.where(in_k, pos, 0.0), axis=1, keepdims=True)
        meta = jnp.where(lane == k, idxs[k], meta)
        meta = jnp.where(lane == TOP_K + k, ex[k] / den, meta)
        meta = jnp.where(lane == 2 * TOP_K + k, rank, meta)
    meta_ref[0] = meta
    base_s[...] = base_s[...] + tot_all
    cnt_ref[...] = base_s[...]


def _mix(hg, rg, x, g1, sh2, sc2, norm_w, wo_bf16, router_w, router_b, tm):
    b, t, d = x.shape
    hw = hg.shape[2]
    n_exp = router_w.shape[1]
    assert TOP_K * n_exp == LANES
    tok = lambda last: pl.BlockSpec((1, tm, last), lambda i, j: (i, j, 0))
    per_b = pl.BlockSpec((1, 1, d), lambda i, j: (i, 0, 0))
    full = lambda shape: pl.BlockSpec(shape, lambda i, j: (0,) * len(shape))
    return pl.pallas_call(
        functools.partial(_mix_kernel, n_exp=n_exp), grid=(b, t // tm),
        in_specs=[tok(hw), tok(hw), tok(d), per_b, per_b, per_b, full((1, d)),
                  full(wo_bf16.shape), full(router_w.shape), full((1, n_exp))],
        out_specs=[tok(d), tok(d), tok(LANES), pl.BlockSpec((8, LANES), lambda i, j: (0, 0))],
        out_shape=[jax.ShapeDtypeStruct((b, t, d), F32), jax.ShapeDtypeStruct((b, t, d), F32),
                   jax.ShapeDtypeStruct((b, t, LANES), F32), jax.ShapeDtypeStruct((8, LANES), F32)],
        scratch_shapes=[pltpu.VMEM((8, LANES), F32)],
        compiler_params=_params("arbitrary", "arbitrary"), name="outproj_router",
    )(hg, rg, x, g1, sh2, sc2, norm_w, wo_bf16, router_w, router_b)


def _dispatch_kernel(fill_off, fill_n, tail, h_ref, dest_hbm, xs_out, idx0_s, idx1_s, zero_s, sem_i, sem_d, sem_z, *,
                     pad_bits):
    td = h_ref.shape[0] * SUBLANES
    n_idx = td * TOP_K
    n_exp = fill_n.shape[0]
    zrows = zero_s.shape[0]
    i = pl.program_id(0)
    slot = i % 2
    idx_s = (idx0_s, idx1_s)

    def idx_copy(j, sl):
        return pltpu.make_async_copy(dest_hbm.at[pl.ds(j * n_idx, n_idx)], idx_s[sl], sem_i.at[sl])

    @pl.when(i == 0)
    def _():
        idx_copy(0, 0).start()

    for sl in range(2):
        @pl.when((i + 1 < pl.num_programs(0)) & (slot != sl))
        def _():
            idx_copy(i + 1, sl).start()

    def fill(wait):
        def go(copy, cond):
            @pl.when(cond)
            def _():
                copy.wait() if wait else copy.start()

        def per_expert(e, _):
            off = fill_off[e]
            npad = fill_n[e]
            n_single = npad & (SUBLANES - 1)
            for r in range(SUBLANES - 1):
                go(pltpu.make_async_copy(zero_s.at[pl.ds(0, 1), :], xs_out.at[pl.ds(off + r, 1), :], sem_z),
                   r < n_single)
            off = pl.multiple_of(off + n_single, SUBLANES)
            for bit in reversed(range(SUBLANES.bit_length() - 1, pad_bits)):
                size = 1 << bit
                go(pltpu.make_async_copy(zero_s.at[pl.ds(0, size), :], xs_out.at[pl.ds(off, size), :], sem_z),
                   (npad & size) != 0)
                off = pl.multiple_of(off + (npad & size), SUBLANES)
            return 0

        lax.fori_loop(0, n_exp, per_expert, 0)

        def per_tail_chunk(j, _):
            off = pl.multiple_of(tail[0] + j * zrows, zrows)
            copy = pltpu.make_async_copy(zero_s, xs_out.at[pl.ds(off, zrows), :], sem_z)
            copy.wait() if wait else copy.start()
            return 0

        lax.fori_loop(0, tail[1], per_tail_chunk, 0)

    @pl.when(i == 0)
    def _():
        zero_s[...] = jnp.zeros_like(zero_s)
        fill(False)

    for sl in range(2):
        @pl.when(slot == sl)
        def _():
            idx_copy(i, sl).wait()

            for g in range(td // SUBLANES):
                for u in range(SUBLANES):
                    for k in range(TOP_K):
                        dst = idx_s[sl][(g * SUBLANES + u) * TOP_K + k]
                        pltpu.make_async_copy(h_ref.at[g, pl.ds(u, 1), :], xs_out.at[pl.ds(dst, 1), :],
                                              sem_d).start(priority=k % 2)

    for _ in range(TOP_K * td // zrows):
        pltpu.make_async_copy(zero_s, xs_out.at[pl.ds(0, zrows), :], sem_d).wait()

    @pl.when(i == 0)
    def _():
        fill(True)


def _dispatch(h2, dest_flat, fill_off, fill_n, tail, n_slots, td, bm):
    n, d = h2.shape
    pad_bits = (bm - 1).bit_length()
    assert (TOP_K * td) % (bm // 2) == 0
    grid_spec = pltpu.PrefetchScalarGridSpec(
        num_scalar_prefetch=3, grid=(n // td,),
        in_specs=[pl.BlockSpec((td // SUBLANES, SUBLANES, d), lambda i, fo, fn, tl: (i, 0, 0)),
                  pl.BlockSpec(memory_space=pl.ANY)],
        out_specs=pl.BlockSpec(memory_space=pl.ANY),
        scratch_shapes=[pltpu.SMEM((td * TOP_K,), jnp.int32), pltpu.SMEM((td * TOP_K,), jnp.int32),
                        pltpu.VMEM((bm // 2, d), F32),
                        pltpu.SemaphoreType.DMA((2,)), pltpu.SemaphoreType.DMA, pltpu.SemaphoreType.DMA])
    return pl.pallas_call(
        functools.partial(_dispatch_kernel, pad_bits=pad_bits), grid_spec=grid_spec,
        out_shape=jax.ShapeDtypeStruct((n_slots, d), F32),
        compiler_params=_params("arbitrary"), name="moe_dispatch",
    )(fill_off, fill_n, tail, h2.reshape(n // SUBLANES, SUBLANES, d), dest_flat)


def _expert_kernel(be_ref, nu_ref, x_ref, wgu_ref, bgu_ref, wd_ref, bd_ref, y_ref, wgu_s, wd_s):
    i = pl.program_id(0)
    d_ff = wd_ref.shape[1]

    @pl.when(i >= nu_ref[0])
    def _():
        y_ref[...] = jnp.zeros_like(y_ref)

    @pl.when(i < nu_ref[0])
    def _():
        @pl.when((i == 0) | (be_ref[i] != be_ref[jnp.maximum(i - 1, 0)]))
        def _():
            wgu_s[...] = wgu_ref[0].astype(BF16)
            wd_s[...] = wd_ref[0].astype(BF16)

        gu = _dot(x_ref[...].astype(BF16), wgu_s[...]) + bgu_ref[0]
        gate = jnp.minimum(gu[:, :d_ff], SWIGLU_LIMIT)
        up = jnp.clip(gu[:, d_ff:], -SWIGLU_LIMIT, SWIGLU_LIMIT)
        act = gate * _sigmoid(SWIGLU_ALPHA * gate) * (up + 1.0)
        y_ref[...] = _dot(act.astype(BF16), wd_s[...]) + bd_ref[0]


def _experts(xs, blk_expert, n_used, wgu, bgu, wd, bd, bm):
    n_slots, d = xs.shape
    n_exp, _, f2 = wgu.shape
    d_ff = wd.shape[1]
    n_blocks = n_slots // bm
    row = lambda i, be, nu: (jnp.minimum(i, nu[0] - 1), 0)
    grid_spec = pltpu.PrefetchScalarGridSpec(
        num_scalar_prefetch=2, grid=(n_blocks,),
        in_specs=[pl.BlockSpec((bm, d), row),
                  pl.BlockSpec((1, d, f2), lambda i, be, nu: (be[i], 0, 0)),
                  pl.BlockSpec((1, 1, f2), lambda i, be, nu: (be[i], 0, 0)),
                  pl.BlockSpec((1, d_ff, d), lambda i, be, nu: (be[i], 0, 0)),
                  pl.BlockSpec((1, 1, d), lambda i, be, nu: (be[i], 0, 0))],
        out_specs=pl.BlockSpec((bm, d), lambda i, be, nu: (i, 0)),
        scratch_shapes=[pltpu.VMEM((d, f2), BF16), pltpu.VMEM((d_ff, d), BF16)])
    return pl.pallas_call(
        _expert_kernel, grid_spec=grid_spec,
        out_shape=jax.ShapeDtypeStruct((n_slots, d), F32),
        compiler_params=_params("arbitrary"), name="moe_experts",
    )(blk_expert, n_used, xs, wgu, bgu.reshape(n_exp, 1, f2), wd, bd.reshape(n_exp, 1, d))


def _combine_kernel(x1_ref, meta_ref, g2_ref, fw_ref, dest_hbm, y_hbm, o_ref, rows0_s, rows1_s, idx0_s, idx1_s,
                    sem_i, sem_d):
    tc, d = x1_ref.shape
    n_idx = tc * TOP_K
    i = pl.program_id(0)
    n = pl.num_programs(0)
    slot = i % 2
    idx_s = (idx0_s, idx1_s)
    rows_s = (rows0_s, rows1_s)

    def idx_copy(j, sl):
        return pltpu.make_async_copy(dest_hbm.at[pl.ds(j * n_idx, n_idx)], idx_s[sl], sem_i.at[sl])

    def row_copy(src, sl, k, g, u):
        return pltpu.make_async_copy(y_hbm.at[pl.ds(src, 1), :], rows_s[sl].at[k, g, pl.ds(u, 1), :], sem_d.at[sl])

    def wait_rows(sl):
        for k in range(TOP_K):
            pltpu.make_async_copy(rows_s[1 - sl].at[k], rows_s[sl].at[k], sem_d.at[sl]).wait()

    def finish(sl):
        meta = meta_ref[...]
        moe = meta[:, TOP_K:TOP_K + 1] * rows_s[sl][0].reshape(tc, d)
        for k in range(1, TOP_K):
            moe = moe + meta[:, TOP_K + k:TOP_K + k + 1] * rows_s[sl][k].reshape(tc, d)
        o_ref[...] = _rms(x1_ref[...] + g2_ref[0] * moe, fw_ref[...])

    @pl.when(i == 0)
    def _():
        idx_copy(0, 0).start()

        @pl.when(n > 1)
        def _():
            idx_copy(1, 1).start()

        idx_copy(0, 0).wait()

        def first(g, _):
            for u in range(SUBLANES):
                for k in range(TOP_K):
                    row_copy(idx0_s[g * (SUBLANES * TOP_K) + u * TOP_K + k], 0, k, g, u).start(priority=k % 2)
            return 0

        lax.fori_loop(0, tc // SUBLANES, first, 0)

    for sl in range(2):
        @pl.when((slot == sl) & (i + 1 < n))
        def _():
            wait_rows(sl)
            idx_copy(i + 1, 1 - sl).wait()
            for g in range(tc // SUBLANES):
                for u in range(SUBLANES):
                    for k in range(TOP_K):
                        row_copy(idx_s[1 - sl][(g * SUBLANES + u) * TOP_K + k], 1 - sl, k, g, u).start(priority=k % 2)
            finish(sl)

        @pl.when((slot == sl) & (i + 1 == n))
        def _():
            wait_rows(sl)
            finish(sl)

        @pl.when((slot == sl) & (i + 2 < n))
        def _():
            idx_copy(i + 2, sl).start()


def _combine(x1, meta, g2, final_w, dest_flat, y, t_seq, tc):
    n, d = x1.shape
    return pl.pallas_call(
        _combine_kernel, grid=(n // tc,),
        in_specs=[pl.BlockSpec((tc, d), lambda i: (i, 0)),
                  pl.BlockSpec((tc, LANES), lambda i: (i, 0)),
                  pl.BlockSpec((1, 1, d), lambda i: (i * tc // t_seq, 0, 0)),
                  pl.BlockSpec((1, d), lambda i: (0, 0)),
                  pl.BlockSpec(memory_space=pl.ANY),
                  pl.BlockSpec(memory_space=pl.ANY)],
        out_specs=pl.BlockSpec((tc, d), lambda i: (i, 0)),
        out_shape=jax.ShapeDtypeStruct((n, d), F32),
        scratch_shapes=[pltpu.VMEM((TOP_K, tc // SUBLANES, SUBLANES, d), F32),
                        pltpu.VMEM((TOP_K, tc // SUBLANES, SUBLANES, d), F32),
                        pltpu.SMEM((tc * TOP_K,), jnp.int32), pltpu.SMEM((tc * TOP_K,), jnp.int32),
                        pltpu.SemaphoreType.DMA((2,)), pltpu.SemaphoreType.DMA((2,))],
        compiler_params=_params("arbitrary"), name="moe_combine",
    )(x1, meta, g2, final_w, dest_flat, y)


def _gate_weights(wa, wx):
    _, heads, hd, _ = wa.shape
    hh = heads // 2
    eye = jnp.eye(hh, dtype=wa.dtype)

    def blockdiag(wsel):
        return jnp.einsum('hij,hg->higj', wsel, eye).reshape(hh * hd, hh * hd)

    halves = []
    for s in range(2):
        sl = slice(s * hh, (s + 1) * hh)
        halves.append(jnp.concatenate([blockdiag(wa[0, sl]), blockdiag(wx[0, sl]),
                                       blockdiag(wa[1, sl]), blockdiag(wx[1, sl])], axis=1))
    return jnp.stack(halves).astype(BF16)


def kernel(x, c, ctx, c_ctx, norm1_w, norm2_w, w_ada, b_ada, w_in, hg_lb_logits, hg_norm_w, rg_conv_w, rg_conv_b,
           rg_wa, rg_ba, rg_wx, rg_bx, rg_lambda, w_out, router_w, router_b, w_gate_up, b_gate_up, w_down,
           b_down, final_norm_w):
    b, t, d = x.shape
    tcx = ctx.shape[1]
    n_exp = router_w.shape[-1]
    n_tok = b * t
    depth = w_in.shape[0]
    lb_all = jnp.cumsum(jax.nn.softmax(hg_lb_logits.astype(F32), axis=0), axis=0)

    for l in range(depth):
        assert l == depth - 1, "context stream update of non-final layers is not implemented"
        pad = (-(b + 1)) % 8
        c_all = jnp.concatenate([c, c_ctx[None], jnp.zeros((pad, d), F32)], axis=0)
        mod = _mod(c_all, w_ada[l], b_ada[l])
        sh1, sc1, g1, sh2, sc2, g2 = [m[:b, None, :] for m in jnp.split(mod, 6, axis=-1)]
        csh1, csc1 = [m[b:b + 1, None, :] for m in jnp.split(mod, 6, axis=-1)[:2]]

        w_in_b = w_in[l].astype(BF16)
        nw1 = norm1_w[l].reshape(1, d)
        tm = min(TOKEN_TILE, t)
        p_lat = _inproj(x, sh1, sc1, nw1, w_in_b, tm)
        p_ctx = _inproj(ctx, csh1, csc1, nw1, w_in_b, min(MXU_TILE, tcx))

        hg = _hgrn2(p_lat, p_ctx, lb_all[l], hg_norm_w[l].reshape(1, -1))
        ch = rg_conv_w.shape[-1]
        wg = _gate_weights(rg_wa[l], rg_wx[l])
        bg = jnp.concatenate([rg_ba[l, 0], rg_bx[l, 0], rg_ba[l, 1], rg_bx[l, 1]]).reshape(1, 4 * ch)
        rg = _rglru(p_lat, p_ctx, rg_conv_w[l], rg_conv_b[l].reshape(1, ch), wg, bg, rg_lambda[l])

        x1, h2, meta, cnt = _mix(hg, rg, x, g1, sh2, sc2, norm2_w[l].reshape(1, d), w_out[l].astype(BF16),
                                 router_w[l].astype(BF16), router_b[l].reshape(1, n_exp), min(ROUTER_TILE, t))

        bm = EXPERT_BLOCK
        meta2 = meta.reshape(n_tok, LANES)
        counts = cnt[0, :n_exp].astype(jnp.int32)
        padded = (counts + bm - 1) // bm * bm
        pad_end = jnp.cumsum(padded)
        pad_start = pad_end - padded
        idx = meta2[:, 0:TOP_K].astype(jnp.int32)
        rank = meta2[:, 2 * TOP_K:3 * TOP_K].astype(jnp.int32)
        dest = (pad_start[idx] + rank).reshape(-1)
        n_blocks = -(-n_tok * TOP_K // bm) + n_exp
        blk_start = jnp.arange(n_blocks, dtype=jnp.int32) * bm
        blk_expert = jnp.minimum(jnp.sum(blk_start[:, None] >= pad_end[None, :], axis=1), n_exp - 1).astype(jnp.int32)
        n_used = (pad_end[-1:] // bm).astype(jnp.int32)

        n_slots = n_blocks * bm
        tail = jnp.stack([pad_end[-1], (n_slots - pad_end[-1]) // (bm // 2)]).astype(jnp.int32)
        xs = _dispatch(h2.reshape(n_tok, d), dest, pad_start + counts, padded - counts, tail, n_slots,
                       min(DISPATCH_TILE, t), bm)
        y = _experts(xs, blk_expert, n_used, w_gate_up[l], b_gate_up[l], w_down[l], b_down[l], bm)
        out = _combine(x1.reshape(n_tok, d), meta2, g2, final_norm_w.reshape(1, d), dest, y, t,
                       min(COMBINE_TILE, t))
        return out.reshape(b, t, d)
```
